```python
import math
import jax, jax.numpy as jnp
from jax import lax
import numpy as np

D_MODEL = 1024
BATCH = 16
SEQ = 256
DEPTH = 4
DEC_BATCH = 4
DEC_SEQ = 1024
PAST_LEN = 256

GRID_W = 64
N_MIXERS = 2
N_DIFF_LAYERS = (DEPTH + 1) // 2
N_NA_LAYERS = DEPTH // 2
DIFF_DK = 64
DIFF_DV = 2 * DIFF_DK
DIFF_HEADS = D_MODEL // (2 * DIFF_DK)
NA_DH = 64
NA_HEADS = D_MODEL // NA_DH
NA_KH_MAX = 8
NA_KW = 16
NA_QB_W = 16
NA_KB_W = NA_QB_W + NA_KW
D_FF = 4 * D_MODEL
ROPE_BASE = 10000.0
EPS = 1e-6
NEG_INF = -1e30

kernel_name = "hybrid_diff_natten_prefix_dit_step"


def rms_norm(x, g):
    xf = x.astype(jnp.float32)
    y = xf * lax.rsqrt(jnp.mean(xf * xf, axis=-1, keepdims=True) + EPS)
    return (y * g.astype(jnp.float32)).astype(x.dtype)


def modulation(cond, w_ada_l, b_ada_l):
    m = (jax.nn.silu(cond) @ w_ada_l + b_ada_l)[:, None, :]
    return jnp.split(m, 6, axis=-1)


def mlp_residual(x, shift, scale, gate, g, w1, w2):
    h = rms_norm(x, g) * (1 + scale) + shift
    return x + gate * (jnp.square(jax.nn.relu(h @ w1)) @ w2)


def rope_half(x, ang):
    n = x.shape[-1] // 2
    cos = jnp.cos(ang).astype(x.dtype)
    sin = jnp.sin(ang).astype(x.dtype)
    x1, x2 = x[..., :n], x[..., n:]
    return jnp.concatenate([x1 * cos - x2 * sin, x2 * cos + x1 * sin], axis=-1)


def rope_2d(x, ang_row, ang_col):
    half = x.shape[-1] // 2
    return jnp.concatenate([rope_half(x[..., :half], ang_row), rope_half(x[..., half:], ang_col)], axis=-1)


def diff_qkv(h, w_qkv, q_g, k_g):
    B, T, _ = h.shape
    q, k, v = jnp.split(h @ w_qkv, 3, axis=-1)
    q = rms_norm(q.reshape(B, T, DIFF_HEADS, 2, DIFF_DK), q_g)
    k = rms_norm(k.reshape(B, T, DIFF_HEADS, 2, DIFF_DK), k_g)
    v = v.reshape(B, T, DIFF_HEADS, DIFF_DV)
    return q, k, v


def diff_lambda(lq1, lk1, lq2, lk2, lambda_init):
    e1 = jnp.exp(jnp.sum(lq1.astype(jnp.float32) * lk1.astype(jnp.float32)))
    e2 = jnp.exp(jnp.sum(lq2.astype(jnp.float32) * lk2.astype(jnp.float32)))
    return e1 - e2 + lambda_init


def diff_attend(q, k, v, lam, lambda_init, g_sub):
    B, Tq = q.shape[0], q.shape[1]
    s = jnp.einsum('bqhmd,bkhmd->bhmqk', q, k).astype(jnp.float32) * (DIFF_DK ** -0.5)
    p = jax.nn.softmax(s, axis=-1)
    a = p[:, :, 0] - lam * p[:, :, 1]
    o = jnp.einsum('bhqk,bkhe->bqhe', a.astype(v.dtype), v)
    o = rms_norm(o, g_sub) * (1.0 - lambda_init)
    return o.reshape(B, Tq, DIFF_HEADS * DIFF_DV)


def na_qkv(h, w_qkv, q_g, k_g):
    B, T, _ = h.shape
    q, k, v = jnp.split(h @ w_qkv, 3, axis=-1)
    q = rms_norm(q.reshape(B, T, NA_HEADS, NA_DH), q_g)
    k = rms_norm(k.reshape(B, T, NA_HEADS, NA_DH), k_g)
    v = v.reshape(B, T, NA_HEADS, NA_DH)
    return q, k, v


def dense_attend(q, k, v):
    B, Tq, H, Dh = q.shape
    s = jnp.einsum('bqhd,bkhd->bhqk', q, k).astype(jnp.float32) * (Dh ** -0.5)
    p = jax.nn.softmax(s, axis=-1)
    o = jnp.einsum('bhqk,bkhd->bqhd', p.astype(v.dtype), v)
    return o.reshape(B, Tq, H * Dh)


def na_latent_attend(q, k, v, kc, vc, rel_bias):
    B, T, H, Dh = q.shape
    rows = T // GRID_W
    kh = min(NA_KH_MAX, rows)
    ncb = GRID_W // NA_QB_W
    r = jnp.arange(rows)
    rs = jnp.clip(r - kh // 2, 0, rows - kh)
    row_idx = rs[:, None] + jnp.arange(kh)
    j = jnp.arange(ncb)
    kb = jnp.clip(j * NA_QB_W - NA_KW // 2, 0, GRID_W - NA_KB_W)
    col_idx = kb[:, None] + jnp.arange(NA_KB_W)
    qcol = j[:, None] * NA_QB_W + jnp.arange(NA_QB_W)
    cs = jnp.clip(qcol - NA_KW // 2, 0, GRID_W - NA_KW)
    kcol = col_idx[:, None, :]
    valid = (kcol >= cs[..., None]) & (kcol < cs[..., None] + NA_KW)
    nw = kh * NA_KB_W
    mask = jnp.broadcast_to(valid[:, :, None, :], (ncb, NA_QB_W, kh, NA_KB_W)).reshape(ncb, NA_QB_W, nw)
    dr = row_idx - r[:, None] + (NA_KH_MAX - 1)
    dc = jnp.clip(kcol - qcol[..., None], -(NA_KW - 1), NA_KW - 1) + (NA_KW - 1)
    bias = rel_bias[:, dr[:, None, None, :, None], dc[None, :, :, None, :]]
    bias = bias.reshape(H, rows, ncb, NA_QB_W, nw).astype(jnp.float32)
    qg = q.reshape(B, rows, ncb, NA_QB_W, H, Dh)
    ridx = row_idx[:, None, :, None]
    cidx = col_idx[None, :, None, :]
    kg = k.reshape(B, rows, GRID_W, H, Dh)[:, ridx, cidx].reshape(B, rows, ncb, nw, H, Dh)
    vg = v.reshape(B, rows, GRID_W, H, Dh)[:, ridx, cidx].reshape(B, rows, ncb, nw, H, Dh)
    scale = Dh ** -0.5
    s_win = jnp.einsum('brjqhd,brjkhd->bhrjqk', qg, kg).astype(jnp.float32) * scale + bias[None]
    s_win = jnp.where(mask[None, None, None], s_win, NEG_INF)
    s_ctx = jnp.einsum('brjqhd,bkhd->bhrjqk', qg, kc).astype(jnp.float32) * scale
    p = jax.nn.softmax(jnp.concatenate([s_win, s_ctx], axis=-1), axis=-1)
    o = (jnp.einsum('bhrjqk,brjkhd->brjqhd', p[..., :nw].astype(v.dtype), vg)
         + jnp.einsum('bhrjqk,bkhd->brjqhd', p[..., nw:].astype(v.dtype), vc))
    return o.reshape(B, T, H * Dh)


def setup_inputs(seed: int = 0) -> dict:
    key = jax.random.key(seed)
    ks = jax.random.split(key, 40)

    def nrm(k, shape, scale=1.0):
        return jax.random.normal(k, shape, jnp.float32) * scale

    D = D_MODEL
    return {
        "x_prompt": nrm(ks[0], (BATCH, SEQ, D)),
        "x_sample": nrm(ks[1], (DEC_BATCH, DEC_SEQ, D)),
        "cache_diff_k": nrm(ks[2], (DEC_BATCH, N_DIFF_LAYERS, PAST_LEN, DIFF_HEADS, 2 * DIFF_DK)),
        "cache_diff_v": nrm(ks[3], (DEC_BATCH, N_DIFF_LAYERS, PAST_LEN, DIFF_HEADS, DIFF_DV)),
        "cache_na_k": nrm(ks[4], (DEC_BATCH, N_NA_LAYERS, PAST_LEN, NA_HEADS, NA_DH)),
        "cache_na_v": nrm(ks[5], (DEC_BATCH, N_NA_LAYERS, PAST_LEN, NA_HEADS, NA_DH)),
        "c": nrm(ks[6], (DEC_BATCH, D)),
        "c_ctx": nrm(ks[7], (D,)),
        "w_ada": nrm(ks[8], (DEPTH, D, 6 * D), 0.5 * D ** -0.5),
        "b_ada": nrm(ks[9], (DEPTH, 6 * D), 0.01),
        "norm_mix_g": 1.0 + nrm(ks[10], (DEPTH, D), 0.02),
        "norm_mlp_g": 1.0 + nrm(ks[11], (DEPTH, D), 0.02),
        "w_fc1": nrm(ks[12], (DEPTH, D, D_FF), D ** -0.5),
        "w_fc2": nrm(ks[13], (DEPTH, D_FF, D), D_FF ** -0.5),
        "w_qkv_diff": nrm(ks[14], (N_DIFF_LAYERS, D, 3 * D), D ** -0.5),
        "w_o_diff": nrm(ks[15], (N_DIFF_LAYERS, DIFF_HEADS * DIFF_DV, D), (DIFF_HEADS * DIFF_DV) ** -0.5),
        "q_norm_diff_g": 1.0 + nrm(ks[16], (N_DIFF_LAYERS, DIFF_DK), 0.02),
        "k_norm_diff_g": 1.0 + nrm(ks[17], (N_DIFF_LAYERS, DIFF_DK), 0.02),
        "lambda_q1": nrm(ks[18], (N_DIFF_LAYERS, DIFF_DK), 0.1),
        "lambda_k1": nrm(ks[19], (N_DIFF_LAYERS, DIFF_DK), 0.1),
        "lambda_q2": nrm(ks[20], (N_DIFF_LAYERS, DIFF_DK), 0.1),
        "lambda_k2": nrm(ks[21], (N_DIFF_LAYERS, DIFF_DK), 0.1),
        "subln_g": 1.0 + nrm(ks[22], (N_DIFF_LAYERS, DIFF_DV), 0.02),
        "w_qkv_na": nrm(ks[23], (N_NA_LAYERS, D, 3 * D), D ** -0.5),
        "w_o_na": nrm(ks[24], (N_NA_LAYERS, NA_HEADS * NA_DH, D), (NA_HEADS * NA_DH) ** -0.5),
        "q_norm_na_g": 1.0 + nrm(ks[25], (N_NA_LAYERS, NA_DH), 0.02),
        "k_norm_na_g": 1.0 + nrm(ks[26], (N_NA_LAYERS, NA_DH), 0.02),
        "rel_bias_na": nrm(ks[27], (N_NA_LAYERS, NA_HEADS, 2 * NA_KH_MAX - 1, 2 * NA_KW - 1), 0.1),
    }


def reference(x_prompt, x_sample, cache_diff_k, cache_diff_v, cache_na_k, cache_na_v, c, c_ctx,
              w_ada, b_ada, norm_mix_g, norm_mlp_g, w_fc1, w_fc2,
              w_qkv_diff, w_o_diff, q_norm_diff_g, k_norm_diff_g,
              lambda_q1, lambda_k1, lambda_q2, lambda_k2, subln_g,
              w_qkv_na, w_o_na, q_norm_na_g, k_norm_na_g, rel_bias_na):
    Bd, T = x_sample.shape[0], x_sample.shape[1]
    L = cache_diff_k.shape[2]
    n_freq = DIFF_DK // 4
    freqs = ROPE_BASE ** (-jnp.arange(n_freq, dtype=jnp.float32) / n_freq)
    t = jnp.arange(T)
    ang_row = ((t // GRID_W).astype(jnp.float32)[:, None] * freqs).reshape(T, 1, 1, n_freq)
    ang_col = ((t % GRID_W).astype(jnp.float32)[:, None] * freqs).reshape(T, 1, 1, n_freq)
    cond_ctx = c_ctx[None, :]

    yp, ys = x_prompt, x_sample
    new_dk, new_dv, new_nk, new_nv = [], [], [], []
    for l in range(DEPTH):
        i = l // N_MIXERS
        mp = modulation(cond_ctx, w_ada[l], b_ada[l])
        ms = modulation(c, w_ada[l], b_ada[l])
        hp = rms_norm(yp, norm_mix_g[l]) * (1 + mp[1]) + mp[0]
        hs = rms_norm(ys, norm_mix_g[l]) * (1 + ms[1]) + ms[0]
        if l % N_MIXERS == 0:
            lambda_init = 0.8 - 0.6 * math.exp(-0.3 * l)
            lam = diff_lambda(lambda_q1[i], lambda_k1[i], lambda_q2[i], lambda_k2[i], lambda_init)
            qp, kp, vp = diff_qkv(hp, w_qkv_diff[i], q_norm_diff_g[i], k_norm_diff_g[i])
            op = diff_attend(qp, kp, vp, lam, lambda_init, subln_g[i])
            new_dk.append(kp.reshape(kp.shape[0], kp.shape[1], DIFF_HEADS, 2 * DIFF_DK))
            new_dv.append(vp)
            qs, kls, vls = diff_qkv(hs, w_qkv_diff[i], q_norm_diff_g[i], k_norm_diff_g[i])
            qs = rope_2d(qs, ang_row, ang_col)
            kls = rope_2d(kls, ang_row, ang_col)
            kc = cache_diff_k[:, i].reshape(Bd, L, DIFF_HEADS, 2, DIFF_DK)
            os_ = diff_attend(qs, jnp.concatenate([kls, kc], axis=1),
                              jnp.concatenate([vls, cache_diff_v[:, i]], axis=1),
                              lam, lambda_init, subln_g[i])
            yp = yp + mp[2] * (op @ w_o_diff[i])
            ys = ys + ms[2] * (os_ @ w_o_diff[i])
        else:
            qp, kp, vp = na_qkv(hp, w_qkv_na[i], q_norm_na_g[i], k_norm_na_g[i])
            op = dense_attend(qp, kp, vp)
            new_nk.append(kp)
            new_nv.append(vp)
            qs, kls, vls = na_qkv(hs, w_qkv_na[i], q_norm_na_g[i], k_norm_na_g[i])
            os_ = na_latent_attend(qs, kls, vls, cache_na_k[:, i], cache_na_v[:, i], rel_bias_na[i])
            yp = yp + mp[2] * (op @ w_o_na[i])
            ys = ys + ms[2] * (os_ @ w_o_na[i])
        yp = mlp_residual(yp, mp[3], mp[4], mp[5], norm_mlp_g[l], w_fc1[l], w_fc2[l])
        ys = mlp_residual(ys, ms[3], ms[4], ms[5], norm_mlp_g[l], w_fc1[l], w_fc2[l])

    return (yp, ys, jnp.stack(new_dk, axis=1), jnp.stack(new_dv, axis=1),
            jnp.stack(new_nk, axis=1), jnp.stack(new_nv, axis=1))
```

```python
import functools
import math

import jax
import jax.numpy as jnp
from jax import lax
from jax.experimental import pallas as pl
from jax.experimental.pallas import tpu as pltpu

D_MODEL = 1024
BATCH = 16
SEQ = 256
DEPTH = 4
DEC_BATCH = 4
DEC_SEQ = 1024
PAST_LEN = 256
GRID_W = 64
GRID_ROWS = DEC_SEQ // GRID_W
HEAD_DIM = 64
D_FF = 4 * D_MODEL
NA_HEADS = 16
NA_KH = 8
NA_KW = 16
ROPE_BASE = 10000.0
EPS = 1e-6
NEG_INF = -1e30

LANES = 128
N_PROMPT_ROWS = BATCH * SEQ
N_SAMPLE_ROWS = DEC_BATCH * DEC_SEQ
N_ROWS = N_PROMPT_ROWS + N_SAMPLE_ROWS
ROW_TILE = 512
N_ROW_TILES = N_ROWS // ROW_TILE
N_PROMPT_TILES = N_PROMPT_ROWS // ROW_TILE
TILES_PER_SAMPLE = DEC_SEQ // ROW_TILE
N_COND = 8
VMEM_LIMIT = 56 * 1024 * 1024

F32 = jnp.float32
BF16 = jnp.bfloat16
NT_DIMS = (((1,), (1,)), ((), ()))


def _dot(a, b):
    return jnp.dot(a, b, preferred_element_type=F32)


def _dot_nt(a, b):
    return lax.dot_general(a, b, NT_DIMS, preferred_element_type=F32)


def _params(n_axes):
    return pltpu.CompilerParams(dimension_semantics=("arbitrary",) * n_axes,
                                vmem_limit_bytes=VMEM_LIMIT)


def _resident(shape):
    return pl.BlockSpec(shape, lambda *_: (0,) * len(shape), pipeline_mode=pl.Buffered(1))


def _cond_row(i):
    return jnp.where(i < N_PROMPT_TILES, 0, 1 + (i - N_PROMPT_TILES) // TILES_PER_SAMPLE)


def _low_half_mask(shape):
    return lax.broadcasted_iota(jnp.int32, shape, len(shape) - 1) < HEAD_DIM


def _ada_kernel(cond_ref, w_ref, b_ref, out_ref):
    cond = cond_ref[...]
    act = (cond * jax.nn.sigmoid(cond)).astype(BF16)
    out_ref[...] = _dot(act, w_ref[...].astype(BF16)) + b_ref[...]


def _modulation(cond, w_ada, b_ada):
    tn = 1536
    return pl.pallas_call(
        _ada_kernel,
        grid=(DEPTH, 6 * D_MODEL // tn),
        in_specs=[
            pl.BlockSpec((N_COND, D_MODEL), lambda l, j: (0, 0)),
            pl.BlockSpec((None, D_MODEL, tn), lambda l, j: (l, 0, j)),
            pl.BlockSpec((None, 1, tn), lambda l, j: (l, 0, j)),
        ],
        out_specs=pl.BlockSpec((None, N_COND, tn), lambda l, j: (l, 0, j)),
        out_shape=jax.ShapeDtypeStruct((DEPTH, N_COND, 6 * D_MODEL), F32),
        compiler_params=_params(2),
        name="ada_modulation",
    )(cond, w_ada, b_ada.reshape(DEPTH, 1, 6 * D_MODEL))


def _group_rms(x, gain):
    low = _low_half_mask((1, LANES))
    cols = []
    for c in range(D_MODEL // LANES):
        xb = x[:, c * LANES:(c + 1) * LANES]
        sq = xb * xb
        s_lo = jnp.sum(jnp.where(low, sq, 0.0), axis=-1, keepdims=True)
        s_hi = jnp.sum(jnp.where(low, 0.0, sq), axis=-1, keepdims=True)
        ms = jnp.where(low, s_lo, s_hi) * (1.0 / HEAD_DIM)
        cols.append(xb * lax.rsqrt(ms + EPS) * gain)
    return cols


def _rope(cols, cos, sin_signed):
    lane = lax.broadcasted_iota(jnp.int32, (1, LANES), 1)
    first = (lane % 32) < 16
    out = []
    for xb in cols:
        partner = jnp.where(first, pltpu.roll(xb, LANES - 16, 1), pltpu.roll(xb, 16, 1))
        out.append(xb * cos + partner * sin_signed)
    return out


def _qkv_kernel(use_rope, x_ref, mod_ref, g_ref, w_ref, qg_ref, kg_ref, cos_ref, sin_ref,
                q_ref, k_ref, v_ref, kf_ref, vf_ref):
    i = pl.program_id(0)
    r = _cond_row(i)
    x = x_ref[...]
    ms = jnp.mean(x * x, axis=-1, keepdims=True)
    xn = x * lax.rsqrt(ms + EPS) * g_ref[...]
    shift = mod_ref[pl.ds(r, 1), 0:D_MODEL]
    scale = mod_ref[pl.ds(r, 1), D_MODEL:2 * D_MODEL]
    h = (xn * (1.0 + scale) + shift).astype(BF16)
    qkv = _dot(h, w_ref[...])
    qn = _group_rms(qkv[:, 0:D_MODEL], qg_ref[...])
    kn = _group_rms(qkv[:, D_MODEL:2 * D_MODEL], kg_ref[...])
    v = qkv[:, 2 * D_MODEL:3 * D_MODEL]
    v_ref[...] = v.astype(BF16)
    sm_scale = HEAD_DIM ** -0.5

    @pl.when(i < N_PROMPT_TILES)
    def _():
        for c in range(D_MODEL // LANES):
            sl = slice(c * LANES, (c + 1) * LANES)
            q_ref[:, sl] = (qn[c] * sm_scale).astype(BF16)
            k_ref[:, sl] = kn[c].astype(BF16)
            kf_ref[:, sl] = kn[c]
        vf_ref[...] = v

    @pl.when(i >= N_PROMPT_TILES)
    def _():
        if use_rope:
            cos = cos_ref[...]
            sin = sin_ref[...]
            qs = _rope(qn, cos, sin)
            ks = _rope(kn, cos, sin)
        else:
            qs, ks = qn, kn
        for c in range(D_MODEL // LANES):
            sl = slice(c * LANES, (c + 1) * LANES)
            q_ref[:, sl] = (qs[c] * sm_scale).astype(BF16)
            k_ref[:, sl] = ks[c].astype(BF16)


def _qkv(y, mod_l, g_mix, w_qkv, q_gain, k_gain, cos_tab, sin_tab, use_rope):
    def rope_idx(i):
        return (jnp.maximum(i - N_PROMPT_TILES, 0) % TILES_PER_SAMPLE, 0)

    def prompt_idx(i):
        return (jnp.minimum(i, N_PROMPT_TILES - 1), 0)

    row_spec = pl.BlockSpec((ROW_TILE, D_MODEL), lambda i: (i, 0))
    return pl.pallas_call(
        functools.partial(_qkv_kernel, use_rope),
        grid=(N_ROW_TILES,),
        in_specs=[
            row_spec,
            _resident((N_COND, 6 * D_MODEL)),
            _resident((1, D_MODEL)),
            _resident((D_MODEL, 3 * D_MODEL)),
            _resident((1, LANES)),
            _resident((1, LANES)),
            pl.BlockSpec((ROW_TILE, LANES), rope_idx),
            pl.BlockSpec((ROW_TILE, LANES), rope_idx),
        ],
        out_specs=[row_spec, row_spec, row_spec,
                   pl.BlockSpec((ROW_TILE, D_MODEL), prompt_idx),
                   pl.BlockSpec((ROW_TILE, D_MODEL), prompt_idx)],
        out_shape=[jax.ShapeDtypeStruct((N_ROWS, D_MODEL), BF16)] * 3
        + [jax.ShapeDtypeStruct((N_PROMPT_ROWS, D_MODEL), F32)] * 2,
        compiler_params=_params(1),
        name="qkv_proj",
    )(y, mod_l, g_mix, w_qkv, q_gain, k_gain, cos_tab, sin_tab)


def _softmax_pv(score_parts, value_parts):
    m = None
    for s in score_parts:
        pm = jnp.max(s, axis=-1, keepdims=True)
        m = pm if m is None else jnp.maximum(m, pm)
    total = None
    acc = None
    for s, v in zip(score_parts, value_parts):
        e = jnp.exp(s - m)
        ps = jnp.sum(e, axis=-1, keepdims=True)
        pv = _dot(e.astype(BF16), v)
        total = ps if total is None else total + ps
        acc = pv if acc is None else acc + pv
    return acc * (1.0 / total)


def _diff_lambda(lq1_ref, lk1_ref, lq2_ref, lk2_ref, lambda_init):
    e1 = jnp.exp(jnp.sum(lq1_ref[...] * lk1_ref[...], axis=-1, keepdims=True))
    e2 = jnp.exp(jnp.sum(lq2_ref[...] * lk2_ref[...], axis=-1, keepdims=True))
    return e1 - e2 + lambda_init


def _diff_head(q, key_parts, value_parts, lam, gsub, lambda_init):
    low = _low_half_mask(q.shape)
    zero = jnp.zeros_like(q)
    q1 = jnp.where(low, q, zero)
    q2 = jnp.where(low, zero, q)
    o1 = _softmax_pv([_dot_nt(q1, k) for k in key_parts], value_parts)
    o2 = _softmax_pv([_dot_nt(q2, k) for k in key_parts], value_parts)
    o = o1 - lam * o2
    ms = jnp.mean(o * o, axis=-1, keepdims=True)
    return o * lax.rsqrt(ms + EPS) * gsub * (1.0 - lambda_init)


def _pair_heads(q, key_parts_fn, value_parts):
    low = _low_half_mask(q.shape)
    zero = jnp.zeros_like(q)
    o_lo = _softmax_pv(key_parts_fn(jnp.where(low, q, zero), 0), value_parts)
    o_hi = _softmax_pv(key_parts_fn(jnp.where(low, zero, q), 1), value_parts)
    return jnp.where(_low_half_mask(o_lo.shape), o_lo, o_hi)


def _ctx_attn_kernel(is_diff, lambda_init, q_ref, k_ref, v_ref, lq1_ref, lk1_ref, lq2_ref, lk2_ref,
                     gsub_ref, o_ref):
    if is_diff:
        lam = _diff_lambda(lq1_ref, lk1_ref, lq2_ref, lk2_ref, lambda_init)
        gsub = gsub_ref[...]
    for h in range(D_MODEL // LANES):
        sl = slice(h * LANES, (h + 1) * LANES)
        q = q_ref[:, sl]
        k = k_ref[:, sl]
        v = v_ref[:, sl]
        if is_diff:
            o = _diff_head(q, [k], [v], lam, gsub, lambda_init)
        else:
            o = _pair_heads(q, lambda qm, _: [_dot_nt(qm, k)], [v])
        o_ref[:, sl] = o.astype(BF16)


def _ctx_attention(q, k, v, lam_params, gsub, is_diff, lambda_init):
    blk = pl.BlockSpec((SEQ, D_MODEL), lambda b: (b, 0))
    small = [_resident((1, HEAD_DIM))] * 4 + [_resident((1, LANES))]
    return pl.pallas_call(
        functools.partial(_ctx_attn_kernel, is_diff, lambda_init),
        grid=(BATCH,),
        in_specs=[blk, blk, blk] + small,
        out_specs=blk,
        out_shape=jax.ShapeDtypeStruct((N_ROWS, D_MODEL), BF16),
        compiler_params=_params(1),
        name="ctx_attention",
    )(q, k, v, *lam_params, gsub)


Q_CHUNK = 256


def _diff_latent_kernel(lambda_init, o_alias_ref, q_ref, k_ref, v_ref, ck_ref, cv_ref,
                        lq1_ref, lk1_ref, lq2_ref, lk2_ref, gsub_ref, o_ref):
    del o_alias_ref
    lam = _diff_lambda(lq1_ref, lk1_ref, lq2_ref, lk2_ref, lambda_init)
    gsub = gsub_ref[...]
    k = k_ref[...]
    v = v_ref[...]
    ck = ck_ref[...].astype(BF16)
    cv = cv_ref[...].astype(BF16)
    for c in range(DEC_SEQ // Q_CHUNK):
        rows = slice(c * Q_CHUNK, (c + 1) * Q_CHUNK)
        o = _diff_head(q_ref[rows, :], [k, ck], [v, cv], lam, gsub, lambda_init)
        o_ref[rows, :] = o.astype(BF16)


def _diff_latent_attention(o_all, q, k, v, cache_k, cache_v, layer_i, lam_params, gsub, lambda_init):
    n_heads = D_MODEL // LANES
    first = N_PROMPT_ROWS // DEC_SEQ
    blk = pl.BlockSpec((DEC_SEQ, LANES), lambda b, h: (first + b, h))
    cache = pl.BlockSpec((None, None, PAST_LEN, LANES), lambda b, h: (b, layer_i, 0, h))
    small = [_resident((1, HEAD_DIM))] * 4 + [_resident((1, LANES))]
    return pl.pallas_call(
        functools.partial(_diff_latent_kernel, lambda_init),
        grid=(DEC_BATCH, n_heads),
        in_specs=[pl.BlockSpec(memory_space=pl.ANY), blk, blk, blk, cache, cache] + small,
        out_specs=blk,
        out_shape=jax.ShapeDtypeStruct((N_ROWS, D_MODEL), BF16),
        input_output_aliases={0: 0},
        compiler_params=_params(2),
        name="diff_latent_attention",
    )(o_all, q, k, v, cache_k, cache_v, *lam_params, gsub)


NA_Q_ROWS = 8
NA_K_ROWS = 12
NA_Q_TOK = NA_Q_ROWS * GRID_W
NA_K_TOK = NA_K_ROWS * GRID_W
NA_GROUPS = GRID_ROWS // NA_Q_ROWS
NA_K_STEP = (GRID_ROWS - NA_K_ROWS) // (NA_GROUPS - 1)


def _na_latent_kernel(o_alias_ref, q_ref, k_ref, v_ref, ck_ref, cv_ref, bias_ref, o_ref):
    del o_alias_ref
    ck = ck_ref[...].astype(BF16)
    cv = cv_ref[...].astype(BF16)
    for g in range(NA_GROUPS):
        qrows = slice(g * NA_Q_TOK, (g + 1) * NA_Q_TOK)
        krows = slice(g * NA_K_STEP * GRID_W, g * NA_K_STEP * GRID_W + NA_K_TOK)
        kw = k_ref[krows, :]
        vw = v_ref[krows, :]

        def scores(qm, half, g=g, kw=kw):
            return [_dot_nt(qm, kw) + bias_ref[half, g], _dot_nt(qm, ck)]

        o_ref[qrows, :] = _pair_heads(q_ref[qrows, :], scores, [vw, cv]).astype(BF16)


def _na_latent_attention(o_all, q, k, v, cache_k, cache_v, layer_i, bias):
    n_pairs = D_MODEL // LANES
    first = N_PROMPT_ROWS // DEC_SEQ
    blk = pl.BlockSpec((DEC_SEQ, LANES), lambda p, b: (first + b, p))
    cache = pl.BlockSpec((None, None, PAST_LEN, LANES), lambda p, b: (b, layer_i, 0, p))
    bias_spec = pl.BlockSpec((None, None, 2, NA_GROUPS, NA_Q_TOK, NA_K_TOK),
                             lambda p, b: (layer_i, p, 0, 0, 0, 0))
    return pl.pallas_call(
        _na_latent_kernel,
        grid=(n_pairs, DEC_BATCH),
        in_specs=[pl.BlockSpec(memory_space=pl.ANY), blk, blk, blk, cache, cache, bias_spec],
        out_specs=blk,
        out_shape=jax.ShapeDtypeStruct((N_ROWS, D_MODEL), BF16),
        input_output_aliases={0: 0},
        compiler_params=_params(2),
        name="na_latent_attention",
    )(o_all, q, k, v, cache_k, cache_v, bias)


N_DR = 2 * NA_KH - 1
N_DC = 2 * NA_KW - 1


def _na_bias_kernel(rb_ref, out_ref, tile_ref):
    layer = pl.program_id(0)
    h = pl.program_id(1)
    qc = lax.broadcasted_iota(jnp.int32, (GRID_W, LANES), 0)
    lane = lax.broadcasted_iota(jnp.int32, (GRID_W, LANES), 1)
    kc = lane % GRID_W
    dc = jnp.clip(kc - qc, -(NA_KW - 1), NA_KW - 1) + (NA_KW - 1)
    cs = jnp.clip(qc - NA_KW // 2, 0, GRID_W - NA_KW)
    col_ok = (kc >= cs) & (kc < cs + NA_KW)
    neg = jnp.full((GRID_W, LANES), NEG_INF, F32)
    for dr in range(N_DR):
        acc = jnp.zeros((GRID_W, LANES), F32)
        for j in range(N_DC):
            acc = jnp.where(dc == j, rb_ref[layer, h, dr * N_DC + j], acc)
        tile_ref[dr] = jnp.where(col_ok, acc, neg)
    low = lane < GRID_W
    for g in range(NA_GROUPS):
        for rq in range(NA_Q_ROWS):
            r = g * NA_Q_ROWS + rq
            rs = min(max(r - NA_KH // 2, 0), GRID_ROWS - NA_KH)

            def tile(kr, r=r, rs=rs):
                if rs <= kr < rs + NA_KH:
                    return tile_ref[kr - r + NA_KH - 1]
                return neg

            for p in range(NA_K_ROWS // 2):
                kr = g * NA_K_STEP + 2 * p
                out_ref[g, rq * GRID_W:(rq + 1) * GRID_W, p * LANES:(p + 1) * LANES] = (
                    jnp.where(low, tile(kr), tile(kr + 1)))


def _na_bias(rel_bias_na):
    n_layers = rel_bias_na.shape[0]
    rb = rel_bias_na.reshape(n_layers, NA_HEADS, N_DR * N_DC)
    out = pl.pallas_call(
        _na_bias_kernel,
        grid=(n_layers, NA_HEADS),
        in_specs=[pl.BlockSpec(memory_space=pltpu.SMEM)],
        out_specs=pl.BlockSpec((None, None, NA_GROUPS, NA_Q_TOK, NA_K_TOK), lambda l, h: (l, h, 0, 0, 0)),
        out_shape=jax.ShapeDtypeStruct((n_layers, NA_HEADS, NA_GROUPS, NA_Q_TOK, NA_K_TOK), F32),
        scratch_shapes=[pltpu.VMEM((N_DR, GRID_W, LANES), F32)],
        compiler_params=_params(2),
        name="na_bias_expand",
    )(rb)
    return out.reshape(n_layers, NA_HEADS // 2, 2, NA_GROUPS, NA_Q_TOK, NA_K_TOK)


FF_CHUNK = 1024


def _post_kernel(x_ref, o_ref, mod_ref, g_ref, wo_ref, w1_ref, w2_ref, y_ref):
    i = pl.program_id(0)
    r = _cond_row(i)

    def mod(j):
        return mod_ref[pl.ds(r, 1), j * D_MODEL:(j + 1) * D_MODEL]

    y1 = x_ref[...] + mod(2) * _dot(o_ref[...], wo_ref[...])
    ms = jnp.mean(y1 * y1, axis=-1, keepdims=True)
    h = (y1 * lax.rsqrt(ms + EPS) * g_ref[...] * (1.0 + mod(4)) + mod(3)).astype(BF16)
    acc = jnp.zeros((ROW_TILE, D_MODEL), F32)
    for c in range(D_FF // FF_CHUNK):
        cols = slice(c * FF_CHUNK, (c + 1) * FF_CHUNK)
        hid = jnp.maximum(_dot(h, w1_ref[:, cols]), 0.0)
        acc = acc + _dot((hid * hid).astype(BF16), w2_ref[cols, :])
    y_ref[...] = y1 + mod(5) * acc


def _post(y, o, mod_l, g_mlp, w_o, w_fc1, w_fc2):
    row_spec = pl.BlockSpec((ROW_TILE, D_MODEL), lambda i: (i, 0))
    return pl.pallas_call(
        _post_kernel,
        grid=(N_ROW_TILES,),
        in_specs=[
            row_spec, row_spec,
            _resident((N_COND, 6 * D_MODEL)),
            _resident((1, D_MODEL)),
            _resident((D_MODEL, D_MODEL)),
            _resident((D_MODEL, D_FF)),
            _resident((D_FF, D_MODEL)),
        ],
        out_specs=row_spec,
        out_shape=jax.ShapeDtypeStruct((N_ROWS, D_MODEL), F32),
        compiler_params=_params(1),
        name="wo_mlp",
    )(y, o, mod_l, g_mlp, w_o, w_fc1, w_fc2)


def _rope_tables():
    n_freq = HEAD_DIM // 4
    freqs = ROPE_BASE ** (-jnp.arange(n_freq, dtype=F32) / n_freq)
    t = jnp.arange(DEC_SEQ)
    ang_row = (t // GRID_W).astype(F32)[:, None] * freqs
    ang_col = (t % GRID_W).astype(F32)[:, None] * freqs
    cos64 = jnp.concatenate([jnp.cos(ang_row)] * 2 + [jnp.cos(ang_col)] * 2, axis=-1)
    sin64 = jnp.concatenate([-jnp.sin(ang_row), jnp.sin(ang_row),
                             -jnp.sin(ang_col), jnp.sin(ang_col)], axis=-1)
    return jnp.tile(cos64, (1, 2)), jnp.tile(sin64, (1, 2))


def _pair_gain(g):
    return jnp.tile(g.reshape(1, HEAD_DIM), (1, LANES // HEAD_DIM))


def kernel(x_prompt, x_sample, cache_diff_k, cache_diff_v, cache_na_k, cache_na_v, c, c_ctx, w_ada, b_ada, norm_mix_g, norm_mlp_g, w_fc1, w_fc2, w_qkv_diff, w_o_diff, q_norm_diff_g, k_norm_diff_g, lambda_q1, lambda_k1, lambda_q2, lambda_k2, subln_g, w_qkv_na, w_o_na, q_norm_na_g, k_norm_na_g, rel_bias_na):
    cond = jnp.concatenate([c_ctx[None, :], c, jnp.zeros((N_COND - 1 - DEC_BATCH, D_MODEL), F32)], axis=0)
    mod = _modulation(cond, w_ada, b_ada)
    cos_tab, sin_tab = _rope_tables()
    na_bias = _na_bias(rel_bias_na)
    cdk = cache_diff_k.reshape(DEC_BATCH, -1, PAST_LEN, D_MODEL)
    cdv = cache_diff_v.reshape(DEC_BATCH, -1, PAST_LEN, D_MODEL)
    cnk = cache_na_k.reshape(DEC_BATCH, -1, PAST_LEN, D_MODEL)
    cnv = cache_na_v.reshape(DEC_BATCH, -1, PAST_LEN, D_MODEL)

    y = jnp.concatenate([x_prompt.reshape(N_PROMPT_ROWS, D_MODEL),
                         x_sample.reshape(N_SAMPLE_ROWS, D_MODEL)], axis=0)
    new_k = [[], []]
    new_v = [[], []]
    for l in range(DEPTH):
        i = l // 2
        is_diff = l % 2 == 0
        if is_diff:
            w_qkv, w_o = w_qkv_diff[i], w_o_diff[i]
            q_gain, k_gain = q_norm_diff_g[i], k_norm_diff_g[i]
        else:
            w_qkv, w_o = w_qkv_na[i], w_o_na[i]
            q_gain, k_gain = q_norm_na_g[i], k_norm_na_g[i]
        q, k, v, kf, vf = _qkv(y, mod[l], norm_mix_g[l].reshape(1, D_MODEL), w_qkv.astype(BF16),
                               _pair_gain(q_gain), _pair_gain(k_gain), cos_tab, sin_tab, is_diff)
        new_k[l % 2].append(kf)
        new_v[l % 2].append(vf)
        lambda_init = 0.8 - 0.6 * math.exp(-0.3 * l)
        lam_params = [p[i if is_diff else 0].reshape(1, HEAD_DIM)
                      for p in (lambda_q1, lambda_k1, lambda_q2, lambda_k2)]
        gsub = subln_g[i if is_diff else 0].reshape(1, LANES)
        o = _ctx_attention(q, k, v, lam_params, gsub, is_diff, lambda_init)
        if is_diff:
            o = _diff_latent_attention(o, q, k, v, cdk, cdv, i, lam_params, gsub, lambda_init)
        else:
            o = _na_latent_attention(o, q, k, v, cnk, cnv, i, na_bias)
        y = _post(y, o, mod[l], norm_mlp_g[l].reshape(1, D_MODEL), w_o.astype(BF16),
                  w_fc1[l].astype(BF16), w_fc2[l].astype(BF16))

    def stack(parts, heads, width):
        return jnp.stack([p.reshape(BATCH, SEQ, heads, width) for p in parts], axis=1)

    return (y[:N_PROMPT_ROWS].reshape(BATCH, SEQ, D_MODEL),
            y[N_PROMPT_ROWS:].reshape(DEC_BATCH, DEC_SEQ, D_MODEL),
            stack(new_k[0], D_MODEL // LANES, LANES), stack(new_v[0], D_MODEL // LANES, LANES),
            stack(new_k[1], NA_HEADS, HEAD_DIM), stack(new_v[1], NA_HEADS, HEAD_DIM))
```

```python
import functools
import math

import jax
import jax.numpy as jnp
from jax import lax
from jax.experimental import pallas as pl
from jax.experimental.pallas import tpu as pltpu

D_MODEL = 1024
BATCH = 16
SEQ = 256
DEPTH = 4
DEC_BATCH = 4
DEC_SEQ = 1024
PAST_LEN = 256
GRID_W = 64
GRID_ROWS = DEC_SEQ // GRID_W
HEAD_DIM = 64
D_FF = 4 * D_MODEL
NA_HEADS = 16
NA_KH = 8
NA_KW = 16
ROPE_BASE = 10000.0
EPS = 1e-6
NEG_INF = -1e30

LANES = 128
SUBLANES = 8
N_LANE_BLOCKS = D_MODEL // LANES
N_PROMPT_ROWS = BATCH * SEQ
N_SAMPLE_ROWS = DEC_BATCH * DEC_SEQ
N_ROWS = N_PROMPT_ROWS + N_SAMPLE_ROWS
ROW_TILE = 512
N_ROW_TILES = N_ROWS // ROW_TILE
N_PROMPT_TILES = N_PROMPT_ROWS // ROW_TILE
TILES_PER_SAMPLE = DEC_SEQ // ROW_TILE
REQ_PER_TILE = ROW_TILE // SEQ
N_COND = 8
VMEM_LIMIT = 56 * 1024 * 1024

F32 = jnp.float32
BF16 = jnp.bfloat16
NT_DIMS = (((1,), (1,)), ((), ()))


def _dot(a, b):
    return jnp.dot(a, b, preferred_element_type=F32)


def _dot_nt(a, b):
    return lax.dot_general(a, b, NT_DIMS, preferred_element_type=F32)


def _params(n_axes):
    return pltpu.CompilerParams(dimension_semantics=("arbitrary",) * n_axes,
                                vmem_limit_bytes=VMEM_LIMIT)


def _resident(shape):
    return pl.BlockSpec(shape, lambda *_: (0,) * len(shape), pipeline_mode=pl.Buffered(1))


def _layer_slice(shape, layer):
    return pl.BlockSpec((None,) + shape, lambda *_: (layer,) + (0,) * len(shape),
                        pipeline_mode=pl.Buffered(1))


def _prompt_tile(i):
    return jnp.minimum(i, N_PROMPT_TILES - 1)


def _sample_tile(i):
    return jnp.maximum(i - N_PROMPT_TILES, 0)


def _cond_row(i):
    return jnp.where(i < N_PROMPT_TILES, 0, 1 + (i - N_PROMPT_TILES) // TILES_PER_SAMPLE)


def _low_half_mask(shape):
    return lax.broadcasted_iota(jnp.int32, shape, len(shape) - 1) < HEAD_DIM


def _ada_kernel(cond_ref, w_ref, b_ref, out_ref):
    cond = cond_ref[...]
    act = (cond * jax.nn.sigmoid(cond)).astype(BF16)
    out_ref[...] = _dot(act, w_ref[...].astype(BF16)) + b_ref[...]


def _modulation(cond, w_ada, b_ada):
    tn = 1536
    return pl.pallas_call(
        _ada_kernel,
        grid=(DEPTH, 6 * D_MODEL // tn),
        in_specs=[
            pl.BlockSpec((N_COND, D_MODEL), lambda l, j: (0, 0)),
            pl.BlockSpec((None, D_MODEL, tn), lambda l, j: (l, 0, j)),
            pl.BlockSpec((None, 1, tn), lambda l, j: (l, 0, j)),
        ],
        out_specs=pl.BlockSpec((None, N_COND, tn), lambda l, j: (l, 0, j)),
        out_shape=jax.ShapeDtypeStruct((DEPTH, N_COND, 6 * D_MODEL), F32),
        compiler_params=_params(2),
        name="ada_modulation",
    )(cond, w_ada, b_ada.reshape(DEPTH, 1, 6 * D_MODEL))


def _group_rms(x, gain):
    low = _low_half_mask((1, LANES))
    cols = []
    for c in range(N_LANE_BLOCKS):
        xb = x[:, c * LANES:(c + 1) * LANES]
        sq = xb * xb
        s_lo = jnp.sum(jnp.where(low, sq, 0.0), axis=-1, keepdims=True)
        s_hi = jnp.sum(jnp.where(low, 0.0, sq), axis=-1, keepdims=True)
        ms = jnp.where(low, s_lo, s_hi) * (1.0 / HEAD_DIM)
        cols.append(xb * lax.rsqrt(ms + EPS) * gain)
    return cols


def _rope(cols, cos, sin_signed):
    lane = lax.broadcasted_iota(jnp.int32, (1, LANES), 1)
    first = (lane % 32) < 16
    out = []
    for xb in cols:
        partner = jnp.where(first, pltpu.roll(xb, LANES - 16, 1), pltpu.roll(xb, 16, 1))
        out.append(xb * cos + partner * sin_signed)
    return out


def _qkv_kernel(is_diff, has_prev, *refs):
    if has_prev:
        refs = refs[2:]
    (xp_ref, xs_ref, mod_ref, g_ref, w_ref, qg_ref, kg_ref, cos_ref, sin_ref,
     q_ref, k_ref, v_ref, kf_ref, vf_ref) = refs
    i = pl.program_id(0)
    r = _cond_row(i)
    x = jnp.where(i < N_PROMPT_TILES, xp_ref[...], xs_ref[...])
    ms = jnp.mean(x * x, axis=-1, keepdims=True)
    xn = x * lax.rsqrt(ms + EPS) * g_ref[...]
    shift = mod_ref[pl.ds(r, 1), 0:D_MODEL]
    scale = mod_ref[pl.ds(r, 1), D_MODEL:2 * D_MODEL]
    h = (xn * (1.0 + scale) + shift).astype(BF16)
    qkv = _dot(h, w_ref[...])
    qn = _group_rms(qkv[:, 0:D_MODEL], qg_ref[...])
    kn = _group_rms(qkv[:, D_MODEL:2 * D_MODEL], kg_ref[...])
    v = qkv[:, 2 * D_MODEL:3 * D_MODEL]
    v_ref[...] = v.astype(BF16)
    sm_scale = HEAD_DIM ** -0.5

    @pl.when(i < N_PROMPT_TILES)
    def _():
        for c in range(N_LANE_BLOCKS):
            sl = slice(c * LANES, (c + 1) * LANES)
            q_ref[:, sl] = (qn[c] * sm_scale).astype(BF16)
            k_ref[:, sl] = kn[c].astype(BF16)
            for bb in range(REQ_PER_TILE):
                rows = slice(bb * SEQ, (bb + 1) * SEQ)
                if is_diff:
                    dst = pl.ds(c, SEQ, stride=N_LANE_BLOCKS)
                    kf_ref[bb, dst, :] = kn[c][rows, :]
                    vf_ref[bb, dst, :] = v[rows, sl]
                else:
                    kf_ref[bb, sl, :] = kn[c][rows, :].T
                    vf_ref[bb, sl, :] = v[rows, sl].T

    @pl.when(i >= N_PROMPT_TILES)
    def _():
        if is_diff:
            cos = cos_ref[...]
            sin = sin_ref[...]
            qs = _rope(qn, cos, sin)
            ks = _rope(kn, cos, sin)
        else:
            qs, ks = qn, kn
        for c in range(N_LANE_BLOCKS):
            sl = slice(c * LANES, (c + 1) * LANES)
            q_ref[:, sl] = (qs[c] * sm_scale).astype(BF16)
            k_ref[:, sl] = ks[c].astype(BF16)


def _qkv(xp, xs, mod, layer, g_mix, w_qkv, kind_layer, q_gain, k_gain, cos_tab, sin_tab, is_diff, prev):
    def rope_idx(i):
        return (_sample_tile(i) % TILES_PER_SAMPLE, 0)

    if is_diff:
        cache_rows, cache_cols = SEQ * N_LANE_BLOCKS, LANES
    else:
        cache_rows, cache_cols = D_MODEL, SEQ
    cache_spec = pl.BlockSpec((REQ_PER_TILE, None, cache_rows, cache_cols),
                              lambda i: (_prompt_tile(i), kind_layer, 0, 0))
    cache_shape = jax.ShapeDtypeStruct((BATCH, DEPTH // 2, cache_rows, cache_cols), F32)
    row_spec = pl.BlockSpec((ROW_TILE, D_MODEL), lambda i: (i, 0))
    has_prev = prev is not None
    alias_specs = [pl.BlockSpec(memory_space=pl.ANY)] * 2 if has_prev else []
    alias_args = list(prev) if has_prev else []
    return pl.pallas_call(
        functools.partial(_qkv_kernel, is_diff, has_prev),
        grid=(N_ROW_TILES,),
        in_specs=alias_specs + [
            pl.BlockSpec((ROW_TILE, D_MODEL), lambda i: (_prompt_tile(i), 0)),
            pl.BlockSpec((ROW_TILE, D_MODEL), lambda i: (_sample_tile(i), 0)),
            _layer_slice((N_COND, 6 * D_MODEL), layer),
            _layer_slice((1, D_MODEL), layer),
            _layer_slice((D_MODEL, 3 * D_MODEL), kind_layer),
            _resident((1, LANES)),
            _resident((1, LANES)),
            pl.BlockSpec((ROW_TILE, LANES), rope_idx),
            pl.BlockSpec((ROW_TILE, LANES), rope_idx),
        ],
        out_specs=[row_spec, row_spec, row_spec, cache_spec, cache_spec],
        out_shape=[jax.ShapeDtypeStruct((N_ROWS, D_MODEL), BF16)] * 3 + [cache_shape] * 2,
        input_output_aliases={0: 3, 1: 4} if has_prev else {},
        compiler_params=_params(1),
        name="qkv_proj",
    )(*alias_args, xp, xs, mod, g_mix, w_qkv, q_gain, k_gain, cos_tab, sin_tab)


def _softmax_pv(score_parts, value_parts):
    m = None
    for s in score_parts:
        pm = jnp.max(s, axis=-1, keepdims=True)
        m = pm if m is None else jnp.maximum(m, pm)
    total = None
    acc = None
    for s, times_v in zip(score_parts, value_parts):
        e = jnp.exp(s - m)
        ps = jnp.sum(e, axis=-1, keepdims=True)
        pv = times_v(e.astype(BF16))
        total = ps if total is None else total + ps
        acc = pv if acc is None else acc + pv
    return acc * (1.0 / total)


def _times(v):
    return lambda e: _dot(e, v)


def _times_t(v_t):
    return lambda e: _dot_nt(e, v_t)


def _diff_lambda(lq1_ref, lk1_ref, lq2_ref, lk2_ref, lambda_init):
    e1 = jnp.exp(jnp.sum(lq1_ref[...] * lk1_ref[...], axis=-1, keepdims=True))
    e2 = jnp.exp(jnp.sum(lq2_ref[...] * lk2_ref[...], axis=-1, keepdims=True))
    return e1 - e2 + lambda_init


def _diff_head(q, key_parts, value_parts, lam, gsub, lambda_init):
    low = _low_half_mask(q.shape)
    zero = jnp.zeros_like(q)
    q1 = jnp.where(low, q, zero)
    q2 = jnp.where(low, zero, q)
    o1 = _softmax_pv([_dot_nt(q1, k) for k in key_parts], value_parts)
    o2 = _softmax_pv([_dot_nt(q2, k) for k in key_parts], value_parts)
    o = o1 - lam * o2
    ms = jnp.mean(o * o, axis=-1, keepdims=True)
    return o * lax.rsqrt(ms + EPS) * gsub * (1.0 - lambda_init)


def _pair_heads(q, scores_fn, value_parts):
    low = _low_half_mask(q.shape)
    zero = jnp.zeros_like(q)
    o_lo = _softmax_pv(scores_fn(jnp.where(low, q, zero), 0), value_parts)
    o_hi = _softmax_pv(scores_fn(jnp.where(low, zero, q), 1), value_parts)
    return jnp.where(_low_half_mask(o_lo.shape), o_lo, o_hi)


def _ctx_attn_kernel(is_diff, lambda_init, q_ref, k_ref, v_ref, lq1_ref, lk1_ref, lq2_ref, lk2_ref,
                     gsub_ref, o_ref):
    if is_diff:
        lam = _diff_lambda(lq1_ref, lk1_ref, lq2_ref, lk2_ref, lambda_init)
        gsub = gsub_ref[...]
    for h in range(N_LANE_BLOCKS):
        sl = slice(h * LANES, (h + 1) * LANES)
        q = q_ref[:, sl]
        k = k_ref[:, sl]
        v = v_ref[:, sl]
        if is_diff:
            o = _diff_head(q, [k], [_times(v)], lam, gsub, lambda_init)
        else:
            o = _pair_heads(q, lambda qm, _, k=k: [_dot_nt(qm, k)], [_times(v)])
        o_ref[:, sl] = o.astype(BF16)


def _ctx_attention(q, k, v, lam_params, gsub, is_diff, lambda_init):
    blk = pl.BlockSpec((SEQ, D_MODEL), lambda b: (b, 0))
    small = [_resident((1, HEAD_DIM))] * 4 + [_resident((1, LANES))]
    return pl.pallas_call(
        functools.partial(_ctx_attn_kernel, is_diff, lambda_init),
        grid=(BATCH,),
        in_specs=[blk, blk, blk] + small,
        out_specs=blk,
        out_shape=jax.ShapeDtypeStruct((N_ROWS, D_MODEL), BF16),
        compiler_params=_params(1),
        name="ctx_attention",
    )(q, k, v, *lam_params, gsub)


Q_CHUNK = 256


def _diff_latent_kernel(lambda_init, o_alias_ref, q_ref, k_ref, v_ref, ck_ref, cv_ref,
                        lq1_ref, lk1_ref, lq2_ref, lk2_ref, gsub_ref, o_ref):
    del o_alias_ref
    lam = _diff_lambda(lq1_ref, lk1_ref, lq2_ref, lk2_ref, lambda_init)
    gsub = gsub_ref[...]
    k = k_ref[...]
    v = v_ref[...]
    head_rows = pl.ds(pl.program_id(1), PAST_LEN, stride=N_LANE_BLOCKS)
    ck = ck_ref[head_rows, :].astype(BF16)
    cv = cv_ref[head_rows, :].astype(BF16)
    for c in range(DEC_SEQ // Q_CHUNK):
        rows = slice(c * Q_CHUNK, (c + 1) * Q_CHUNK)
        o = _diff_head(q_ref[rows, :], [k, ck], [_times(v), _times(cv)], lam, gsub, lambda_init)
        o_ref[rows, :] = o.astype(BF16)


def _diff_latent_attention(o_all, q, k, v, cache_k, cache_v, layer_i, lam_params, gsub, lambda_init):
    first = N_PROMPT_ROWS // DEC_SEQ
    blk = pl.BlockSpec((DEC_SEQ, LANES), lambda b, h: (first + b, h))
    cache = pl.BlockSpec((None, None, PAST_LEN * N_LANE_BLOCKS, LANES), lambda b, h: (b, layer_i, 0, 0))
    small = [_resident((1, HEAD_DIM))] * 4 + [_resident((1, LANES))]
    return pl.pallas_call(
        functools.partial(_diff_latent_kernel, lambda_init),
        grid=(DEC_BATCH, N_LANE_BLOCKS),
        in_specs=[pl.BlockSpec(memory_space=pl.ANY), blk, blk, blk, cache, cache] + small,
        out_specs=blk,
        out_shape=jax.ShapeDtypeStruct((N_ROWS, D_MODEL), BF16),
        input_output_aliases={0: 0},
        compiler_params=_params(2),
        name="diff_latent_attention",
    )(o_all, q, k, v, cache_k, cache_v, *lam_params, gsub)


NA_Q_ROWS = 8
NA_K_ROWS = 12
NA_Q_TOK = NA_Q_ROWS * GRID_W
NA_K_TOK = NA_K_ROWS * GRID_W
NA_GROUPS = GRID_ROWS // NA_Q_ROWS
NA_K_STEP = (GRID_ROWS - NA_K_ROWS) // (NA_GROUPS - 1)


def _na_latent_kernel(o_alias_ref, q_ref, k_ref, v_ref, ckt_ref, cvt_ref, bias_ref, o_ref):
    del o_alias_ref
    ck_t = ckt_ref[...].astype(BF16)
    cv_t = cvt_ref[...].astype(BF16)
    for g in range(NA_GROUPS):
        qrows = slice(g * NA_Q_TOK, (g + 1) * NA_Q_TOK)
        krows = slice(g * NA_K_STEP * GRID_W, g * NA_K_STEP * GRID_W + NA_K_TOK)
        kw = k_ref[krows, :]
        vw = v_ref[krows, :]

        def scores(qm, half, g=g, kw=kw):
            return [_dot_nt(qm, kw) + bias_ref[half, g], _dot(qm, ck_t)]

        o_ref[qrows, :] = _pair_heads(q_ref[qrows, :], scores, [_times(vw), _times_t(cv_t)]).astype(BF16)


def _na_latent_attention(o_all, q, k, v, cache_k_t, cache_v_t, layer_i, bias):
    first = N_PROMPT_ROWS // DEC_SEQ
    blk = pl.BlockSpec((DEC_SEQ, LANES), lambda p, b: (first + b, p))
    cache = pl.BlockSpec((None, None, LANES, PAST_LEN), lambda p, b: (b, layer_i, p, 0))
    bias_spec = pl.BlockSpec((None, None, 2, NA_GROUPS, NA_Q_TOK, NA_K_TOK),
                             lambda p, b: (layer_i, p, 0, 0, 0, 0))
    return pl.pallas_call(
        _na_latent_kernel,
        grid=(N_LANE_BLOCKS, DEC_BATCH),
        in_specs=[pl.BlockSpec(memory_space=pl.ANY), blk, blk, blk, cache, cache, bias_spec],
        out_specs=blk,
        out_shape=jax.ShapeDtypeStruct((N_ROWS, D_MODEL), BF16),
        input_output_aliases={0: 0},
        compiler_params=_params(2),
        name="na_latent_attention",
    )(o_all, q, k, v, cache_k_t, cache_v_t, bias)


N_DR = 2 * NA_KH - 1
N_DC = 2 * NA_KW - 1


def _na_bias_kernel(rb_ref, out_ref, tile_ref):
    layer = pl.program_id(0)
    h = pl.program_id(1)
    qc = lax.broadcasted_iota(jnp.int32, (GRID_W, LANES), 0)
    lane = lax.broadcasted_iota(jnp.int32, (GRID_W, LANES), 1)
    kc = lane % GRID_W
    dc = jnp.clip(kc - qc, -(NA_KW - 1), NA_KW - 1) + (NA_KW - 1)
    cs = jnp.clip(qc - NA_KW // 2, 0, GRID_W - NA_KW)
    col_ok = (kc >= cs) & (kc < cs + NA_KW)
    neg = jnp.full((GRID_W, LANES), NEG_INF, F32)
    for dr in range(N_DR):
        acc = jnp.zeros((GRID_W, LANES), F32)
        for j in range(N_DC):
            acc = jnp.where(dc == j, rb_ref[layer, h, dr * N_DC + j], acc)
        tile_ref[dr] = jnp.where(col_ok, acc, neg)
    low = lane < GRID_W
    for g in range(NA_GROUPS):
        for rq in range(NA_Q_ROWS):
            r = g * NA_Q_ROWS + rq
            rs = min(max(r - NA_KH // 2, 0), GRID_ROWS - NA_KH)

            def tile(kr, r=r, rs=rs):
                if rs <= kr < rs + NA_KH:
                    return tile_ref[kr - r + NA_KH - 1]
                return neg

            for p in range(NA_K_ROWS // 2):
                kr = g * NA_K_STEP + 2 * p
                out_ref[g, rq * GRID_W:(rq + 1) * GRID_W, p * LANES:(p + 1) * LANES] = (
                    jnp.where(low, tile(kr), tile(kr + 1)))


def _na_bias(rel_bias_na):
    n_layers = rel_bias_na.shape[0]
    rb = rel_bias_na.reshape(n_layers, NA_HEADS, N_DR * N_DC)
    out = pl.pallas_call(
        _na_bias_kernel,
        grid=(n_layers, NA_HEADS),
        in_specs=[pl.BlockSpec(memory_space=pltpu.SMEM)],
        out_specs=pl.BlockSpec((None, None, NA_GROUPS, NA_Q_TOK, NA_K_TOK), lambda l, h: (l, h, 0, 0, 0)),
        out_shape=jax.ShapeDtypeStruct((n_layers, NA_HEADS, NA_GROUPS, NA_Q_TOK, NA_K_TOK), F32),
        scratch_shapes=[pltpu.VMEM((N_DR, GRID_W, LANES), F32)],
        compiler_params=_params(2),
        name="na_bias_expand",
    )(rb)
    return out.reshape(n_layers, NA_HEADS // 2, 2, NA_GROUPS, NA_Q_TOK, NA_K_TOK)


FF_CHUNK = 1024


def _post_kernel(xp_ref, xs_ref, o_ref, mod_ref, g_ref, wo_ref, w1_ref, w2_ref, yp_ref, ys_ref):
    i = pl.program_id(0)
    r = _cond_row(i)

    def mod(j):
        return mod_ref[pl.ds(r, 1), j * D_MODEL:(j + 1) * D_MODEL]

    x = jnp.where(i < N_PROMPT_TILES, xp_ref[...], xs_ref[...])
    y1 = x + mod(2) * _dot(o_ref[...], wo_ref[...])
    ms = jnp.mean(y1 * y1, axis=-1, keepdims=True)
    h = (y1 * lax.rsqrt(ms + EPS) * g_ref[...] * (1.0 + mod(4)) + mod(3)).astype(BF16)
    acc = jnp.zeros((ROW_TILE, D_MODEL), F32)
    for c in range(D_FF // FF_CHUNK):
        cols = slice(c * FF_CHUNK, (c + 1) * FF_CHUNK)
        hid = jnp.maximum(_dot(h, w1_ref[:, cols]), 0.0)
        acc = acc + _dot((hid * hid).astype(BF16), w2_ref[cols, :])
    y = y1 + mod(5) * acc

    @pl.when(i < N_PROMPT_TILES)
    def _():
        yp_ref[...] = y

    @pl.when(i >= N_PROMPT_TILES)
    def _():
        ys_ref[...] = y


def _post(xp, xs, o, mod, layer, g_mlp, w_o, kind_layer, w_fc1, w_fc2):
    prompt_spec = pl.BlockSpec((ROW_TILE, D_MODEL), lambda i: (_prompt_tile(i), 0))
    sample_spec = pl.BlockSpec((ROW_TILE, D_MODEL), lambda i: (_sample_tile(i), 0))
    return pl.pallas_call(
        _post_kernel,
        grid=(N_ROW_TILES,),
        in_specs=[
            prompt_spec, sample_spec,
            pl.BlockSpec((ROW_TILE, D_MODEL), lambda i: (i, 0)),
            _layer_slice((N_COND, 6 * D_MODEL), layer),
            _layer_slice((1, D_MODEL), layer),
            _layer_slice((D_MODEL, D_MODEL), kind_layer),
            _layer_slice((D_MODEL, D_FF), layer),
            _layer_slice((D_FF, D_MODEL), layer),
        ],
        out_specs=[prompt_spec, sample_spec],
        out_shape=[jax.ShapeDtypeStruct((N_PROMPT_ROWS, D_MODEL), F32),
                   jax.ShapeDtypeStruct((N_SAMPLE_ROWS, D_MODEL), F32)],
        compiler_params=_params(1),
        name="wo_mlp",
    )(xp, xs, o, mod, g_mlp, w_o, w_fc1, w_fc2)


def _rope_tables():
    n_freq = HEAD_DIM // 4
    freqs = ROPE_BASE ** (-jnp.arange(n_freq, dtype=F32) / n_freq)
    t = jnp.arange(DEC_SEQ)
    ang_row = (t // GRID_W).astype(F32)[:, None] * freqs
    ang_col = (t % GRID_W).astype(F32)[:, None] * freqs
    cos64 = jnp.concatenate([jnp.cos(ang_row)] * 2 + [jnp.cos(ang_col)] * 2, axis=-1)
    sin64 = jnp.concatenate([-jnp.sin(ang_row), jnp.sin(ang_row),
                             -jnp.sin(ang_col), jnp.sin(ang_col)], axis=-1)
    return jnp.tile(cos64, (1, 2)), jnp.tile(sin64, (1, 2))


def _pair_gain(g):
    return jnp.tile(g.reshape(1, HEAD_DIM), (1, LANES // HEAD_DIM))


def _head_major_cache(cache):
    n_req, n_layers = cache.shape[:2]
    return jnp.transpose(cache, (0, 1, 3, 4, 2)).reshape(n_req, n_layers, D_MODEL, PAST_LEN)


def kernel(x_prompt, x_sample, cache_diff_k, cache_diff_v, cache_na_k, cache_na_v, c, c_ctx, w_ada, b_ada, norm_mix_g, norm_mlp_g, w_fc1, w_fc2, w_qkv_diff, w_o_diff, q_norm_diff_g, k_norm_diff_g, lambda_q1, lambda_k1, lambda_q2, lambda_k2, subln_g, w_qkv_na, w_o_na, q_norm_na_g, k_norm_na_g, rel_bias_na):
    cond = jnp.concatenate([c_ctx[None, :], c, jnp.zeros((N_COND - 1 - DEC_BATCH, D_MODEL), F32)], axis=0)
    mod = _modulation(cond, w_ada, b_ada)
    cos_tab, sin_tab = _rope_tables()
    na_bias = _na_bias(rel_bias_na)
    cdk = cache_diff_k.reshape(DEC_BATCH, DEPTH // 2, PAST_LEN * N_LANE_BLOCKS, LANES)
    cdv = cache_diff_v.reshape(DEC_BATCH, DEPTH // 2, PAST_LEN * N_LANE_BLOCKS, LANES)
    cnk_t = _head_major_cache(cache_na_k)
    cnv_t = _head_major_cache(cache_na_v)
    g_mix = norm_mix_g.reshape(DEPTH, 1, D_MODEL)
    g_mlp = norm_mlp_g.reshape(DEPTH, 1, D_MODEL)
    w1 = w_fc1.astype(BF16)
    w2 = w_fc2.astype(BF16)
    w_qkv = (w_qkv_diff.astype(BF16), w_qkv_na.astype(BF16))
    w_o = (w_o_diff.astype(BF16), w_o_na.astype(BF16))
    q_gain = (q_norm_diff_g, q_norm_na_g)
    k_gain = (k_norm_diff_g, k_norm_na_g)

    xp = x_prompt.reshape(N_PROMPT_ROWS, D_MODEL)
    xs = x_sample.reshape(N_SAMPLE_ROWS, D_MODEL)
    new_kv = [None, None]
    for l in range(DEPTH):
        i = l // 2
        kind = l % 2
        is_diff = kind == 0
        q, k, v, new_k, new_v = _qkv(xp, xs, mod, l, g_mix, w_qkv[kind], i,
                                     _pair_gain(q_gain[kind][i]), _pair_gain(k_gain[kind][i]),
                                     cos_tab, sin_tab, is_diff, new_kv[kind])
        new_kv[kind] = (new_k, new_v)
        lambda_init = 0.8 - 0.6 * math.exp(-0.3 * l)
        lam_params = [p[i if is_diff else 0].reshape(1, HEAD_DIM)
                      for p in (lambda_q1, lambda_k1, lambda_q2, lambda_k2)]
        gsub = subln_g[i if is_diff else 0].reshape(1, LANES)
        o = _ctx_attention(q, k, v, lam_params, gsub, is_diff, lambda_init)
        if is_diff:
            o = _diff_latent_attention(o, q, k, v, cdk, cdv, i, lam_params, gsub, lambda_init)
        else:
            o = _na_latent_attention(o, q, k, v, cnk_t, cnv_t, i, na_bias)
        xp, xs = _post(xp, xs, o, mod, l, g_mlp, w_o[kind], i, w1, w2)

    n_kind = DEPTH // 2
    diff_k, diff_v = (a.reshape(BATCH, n_kind, SEQ, N_LANE_BLOCKS, LANES) for a in new_kv[0])
    na_k, na_v = (jnp.transpose(a.reshape(BATCH, n_kind, NA_HEADS, HEAD_DIM, SEQ), (0, 1, 4, 2, 3))
                  for a in new_kv[1])
    return (xp.reshape(BATCH, SEQ, D_MODEL), xs.reshape(DEC_BATCH, DEC_SEQ, D_MODEL),
            diff_k, diff_v, na_k, na_v)
```

```python
import functools
import math

import jax
import jax.numpy as jnp
from jax import lax
from jax.experimental import pallas as pl
from jax.experimental.pallas import tpu as pltpu

D_MODEL = 1024
BATCH = 16
SEQ = 256
DEPTH = 4
DEC_BATCH = 4
DEC_SEQ = 1024
PAST_LEN = 256
GRID_W = 64
GRID_ROWS = DEC_SEQ // GRID_W
HEAD_DIM = 64
D_FF = 4 * D_MODEL
NA_HEADS = 16
NA_KH = 8
NA_KW = 16
ROPE_BASE = 10000.0
EPS = 1e-6
NEG_INF = -1e30

LANES = 128
SUBLANES = 8
N_LANE_BLOCKS = D_MODEL // LANES
N_PROMPT_ROWS = BATCH * SEQ
N_SAMPLE_ROWS = DEC_BATCH * DEC_SEQ
N_ROWS = N_PROMPT_ROWS + N_SAMPLE_ROWS
ROW_TILE = 512
N_ROW_TILES = N_ROWS // ROW_TILE
N_PROMPT_TILES = N_PROMPT_ROWS // ROW_TILE
TILES_PER_SAMPLE = DEC_SEQ // ROW_TILE
REQ_PER_TILE = ROW_TILE // SEQ
N_COND = 8
VMEM_LIMIT = 56 * 1024 * 1024

F32 = jnp.float32
BF16 = jnp.bfloat16
NT_DIMS = (((1,), (1,)), ((), ()))


def _dot(a, b):
    return jnp.dot(a, b, preferred_element_type=F32)


def _dot_nt(a, b):
    return lax.dot_general(a, b, NT_DIMS, preferred_element_type=F32)


def _params(n_axes):
    return pltpu.CompilerParams(dimension_semantics=("arbitrary",) * n_axes,
                                vmem_limit_bytes=VMEM_LIMIT)


def _resident(shape):
    return pl.BlockSpec(shape, lambda *_: (0,) * len(shape), pipeline_mode=pl.Buffered(1))


def _layer_slice(shape, layer):
    return pl.BlockSpec((None,) + shape, lambda *_: (layer,) + (0,) * len(shape),
                        pipeline_mode=pl.Buffered(1))


def _prompt_tile(i):
    return jnp.minimum(i, N_PROMPT_TILES - 1)


def _sample_tile(i):
    return jnp.maximum(i - N_PROMPT_TILES, 0)


def _cond_row(i):
    return jnp.where(i < N_PROMPT_TILES, 0, 1 + (i - N_PROMPT_TILES) // TILES_PER_SAMPLE)


def _low_half_mask(shape):
    return lax.broadcasted_iota(jnp.int32, shape, len(shape) - 1) < HEAD_DIM


def _ada_kernel(cond_ref, w_ref, b_ref, out_ref):
    cond = cond_ref[...]
    act = (cond * jax.nn.sigmoid(cond)).astype(BF16)
    out_ref[...] = _dot(act, w_ref[...].astype(BF16)) + b_ref[...]


def _modulation(cond, w_ada, b_ada):
    tn = 1536
    return pl.pallas_call(
        _ada_kernel,
        grid=(DEPTH, 6 * D_MODEL // tn),
        in_specs=[
            pl.BlockSpec((N_COND, D_MODEL), lambda l, j: (0, 0)),
            pl.BlockSpec((None, D_MODEL, tn), lambda l, j: (l, 0, j)),
            pl.BlockSpec((None, 1, tn), lambda l, j: (l, 0, j)),
        ],
        out_specs=pl.BlockSpec((None, N_COND, tn), lambda l, j: (l, 0, j)),
        out_shape=jax.ShapeDtypeStruct((DEPTH, N_COND, 6 * D_MODEL), F32),
        compiler_params=_params(2),
        name="ada_modulation",
    )(cond, w_ada, b_ada.reshape(DEPTH, 1, 6 * D_MODEL))


def _group_rms(x, gain):
    low = _low_half_mask((1, LANES))
    cols = []
    for c in range(N_LANE_BLOCKS):
        xb = x[:, c * LANES:(c + 1) * LANES]
        sq = xb * xb
        s_lo = jnp.sum(jnp.where(low, sq, 0.0), axis=-1, keepdims=True)
        s_hi = jnp.sum(jnp.where(low, 0.0, sq), axis=-1, keepdims=True)
        ms = jnp.where(low, s_lo, s_hi) * (1.0 / HEAD_DIM)
        cols.append(xb * lax.rsqrt(ms + EPS) * gain)
    return cols


def _rope(cols, cos, sin_signed):
    lane = lax.broadcasted_iota(jnp.int32, (1, LANES), 1)
    first = (lane % 32) < 16
    out = []
    for xb in cols:
        partner = jnp.where(first, pltpu.roll(xb, LANES - 16, 1), pltpu.roll(xb, 16, 1))
        out.append(xb * cos + partner * sin_signed)
    return out


def _qkv_kernel(is_diff, is_prompt, has_prev, *refs):
    if has_prev:
        refs = refs[2:]
    if is_prompt:
        x_ref, mod_ref, g_ref, w_ref, qg_ref, kg_ref, q_ref, k_ref, v_ref, kf_ref, vf_ref = refs
        r = 0
    else:
        x_ref, mod_ref, g_ref, w_ref, qg_ref, kg_ref, cos_ref, sin_ref, q_ref, k_ref, v_ref = refs
        r = 1 + pl.program_id(0) // TILES_PER_SAMPLE
    x = x_ref[...]
    ms = jnp.mean(x * x, axis=-1, keepdims=True)
    xn = x * lax.rsqrt(ms + EPS) * g_ref[...]
    shift = mod_ref[pl.ds(r, 1), 0:D_MODEL]
    scale = mod_ref[pl.ds(r, 1), D_MODEL:2 * D_MODEL]
    h = (xn * (1.0 + scale) + shift).astype(BF16)
    qkv = _dot(h, w_ref[...])
    qn = _group_rms(qkv[:, 0:D_MODEL], qg_ref[...])
    kn = _group_rms(qkv[:, D_MODEL:2 * D_MODEL], kg_ref[...])
    v = qkv[:, 2 * D_MODEL:3 * D_MODEL]
    v_ref[...] = v.astype(BF16)
    sm_scale = HEAD_DIM ** -0.5
    if is_diff and not is_prompt:
        cos = cos_ref[...]
        sin = sin_ref[...]
        qs = _rope(qn, cos, sin)
        ks = _rope(kn, cos, sin)
    else:
        qs, ks = qn, kn
    for c in range(N_LANE_BLOCKS):
        sl = slice(c * LANES, (c + 1) * LANES)
        q_ref[:, sl] = (qs[c] * sm_scale).astype(BF16)
        k_ref[:, sl] = ks[c].astype(BF16)
        if not is_prompt:
            continue
        for bb in range(REQ_PER_TILE):
            rows = slice(bb * SEQ, (bb + 1) * SEQ)
            if is_diff:
                dst = pl.ds(c, SEQ, stride=N_LANE_BLOCKS)
                kf_ref[bb, dst, :] = kn[c][rows, :]
                vf_ref[bb, dst, :] = v[rows, sl]
            else:
                kf_ref[bb, sl, :] = kn[c][rows, :].T
                vf_ref[bb, sl, :] = v[rows, sl].T


def _qkv(x, first_tile, is_prompt, mod, layer, g_mix, w_qkv, kind_layer, q_gain, k_gain, rope_tabs,
         is_diff, prev):
    n_tiles = N_PROMPT_TILES if is_prompt else N_ROW_TILES - N_PROMPT_TILES
    row_spec = pl.BlockSpec((ROW_TILE, D_MODEL), lambda i: (i, 0))
    in_specs = [
        pl.BlockSpec((ROW_TILE, D_MODEL), lambda i: (first_tile + i, 0)),
        _layer_slice((N_COND, 6 * D_MODEL), layer),
        _layer_slice((1, D_MODEL), layer),
        _layer_slice((D_MODEL, 3 * D_MODEL), kind_layer),
        _resident((1, LANES)),
        _resident((1, LANES)),
    ]
    args = [x, mod, g_mix, w_qkv, q_gain, k_gain]
    out_specs = [row_spec] * 3
    out_shape = [jax.ShapeDtypeStruct((n_tiles * ROW_TILE, D_MODEL), BF16)] * 3
    aliases = {}
    if is_prompt:
        if is_diff:
            cache_rows, cache_cols = SEQ * N_LANE_BLOCKS, LANES
        else:
            cache_rows, cache_cols = D_MODEL, SEQ
        cache_spec = pl.BlockSpec((REQ_PER_TILE, None, cache_rows, cache_cols),
                                  lambda i: (i, kind_layer, 0, 0))
        out_specs += [cache_spec] * 2
        out_shape += [jax.ShapeDtypeStruct((BATCH, DEPTH // 2, cache_rows, cache_cols), F32)] * 2
        if prev is not None:
            in_specs = [pl.BlockSpec(memory_space=pl.ANY)] * 2 + in_specs
            args = list(prev) + args
            aliases = {0: 3, 1: 4}
    else:
        rope_spec = pl.BlockSpec((ROW_TILE, LANES), lambda i: (i % TILES_PER_SAMPLE, 0))
        in_specs += [rope_spec] * 2
        args += list(rope_tabs)
    return pl.pallas_call(
        functools.partial(_qkv_kernel, is_diff, is_prompt, is_prompt and prev is not None),
        grid=(n_tiles,),
        in_specs=in_specs,
        out_specs=out_specs,
        out_shape=out_shape,
        input_output_aliases=aliases,
        compiler_params=_params(1),
        name="qkv_prompt" if is_prompt else "qkv_latent",
    )(*args)


def _softmax_pv(score_parts, value_parts):
    m = None
    for s in score_parts:
        pm = jnp.max(s, axis=-1, keepdims=True)
        m = pm if m is None else jnp.maximum(m, pm)
    total = None
    acc = None
    for s, times_v in zip(score_parts, value_parts):
        e = jnp.exp(s - m)
        ps = jnp.sum(e, axis=-1, keepdims=True)
        pv = times_v(e.astype(BF16))
        total = ps if total is None else total + ps
        acc = pv if acc is None else acc + pv
    return acc * (1.0 / total)


def _times(v):
    return lambda e: _dot(e, v)


def _times_t(v_t):
    return lambda e: _dot_nt(e, v_t)


def _diff_lambda(lq1_ref, lk1_ref, lq2_ref, lk2_ref, lambda_init):
    e1 = jnp.exp(jnp.sum(lq1_ref[...] * lk1_ref[...], axis=-1, keepdims=True))
    e2 = jnp.exp(jnp.sum(lq2_ref[...] * lk2_ref[...], axis=-1, keepdims=True))
    return e1 - e2 + lambda_init


def _diff_head(q, key_parts, value_parts, lam, gsub, lambda_init):
    low = _low_half_mask(q.shape)
    zero = jnp.zeros_like(q)
    q1 = jnp.where(low, q, zero)
    q2 = jnp.where(low, zero, q)
    o1 = _softmax_pv([_dot_nt(q1, k) for k in key_parts], value_parts)
    o2 = _softmax_pv([_dot_nt(q2, k) for k in key_parts], value_parts)
    o = o1 - lam * o2
    ms = jnp.mean(o * o, axis=-1, keepdims=True)
    return o * lax.rsqrt(ms + EPS) * gsub * (1.0 - lambda_init)


def _pair_heads(q, scores_fn, value_parts):
    low = _low_half_mask(q.shape)
    zero = jnp.zeros_like(q)
    o_lo = _softmax_pv(scores_fn(jnp.where(low, q, zero), 0), value_parts)
    o_hi = _softmax_pv(scores_fn(jnp.where(low, zero, q), 1), value_parts)
    return jnp.where(_low_half_mask(o_lo.shape), o_lo, o_hi)


def _ctx_attn_kernel(is_diff, lambda_init, q_ref, k_ref, v_ref, lq1_ref, lk1_ref, lq2_ref, lk2_ref,
                     gsub_ref, o_ref):
    if is_diff:
        lam = _diff_lambda(lq1_ref, lk1_ref, lq2_ref, lk2_ref, lambda_init)
        gsub = gsub_ref[...]
    for h in range(N_LANE_BLOCKS):
        sl = slice(h * LANES, (h + 1) * LANES)
        q = q_ref[:, sl]
        k = k_ref[:, sl]
        v = v_ref[:, sl]
        if is_diff:
            o = _diff_head(q, [k], [_times(v)], lam, gsub, lambda_init)
        else:
            o = _pair_heads(q, lambda qm, _, k=k: [_dot_nt(qm, k)], [_times(v)])
        o_ref[:, sl] = o.astype(BF16)


def _ctx_attention(q, k, v, lam_params, gsub, is_diff, lambda_init):
    blk = pl.BlockSpec((SEQ, D_MODEL), lambda b: (b, 0))
    small = [_resident((1, HEAD_DIM))] * 4 + [_resident((1, LANES))]
    return pl.pallas_call(
        functools.partial(_ctx_attn_kernel, is_diff, lambda_init),
        grid=(BATCH,),
        in_specs=[blk, blk, blk] + small,
        out_specs=blk,
        out_shape=jax.ShapeDtypeStruct((N_PROMPT_ROWS, D_MODEL), BF16),
        compiler_params=_params(1),
        name="ctx_attention",
    )(q, k, v, *lam_params, gsub)


Q_CHUNK = 256


def _diff_latent_kernel(lambda_init, q_ref, k_ref, v_ref, ck_ref, cv_ref,
                        lq1_ref, lk1_ref, lq2_ref, lk2_ref, gsub_ref, o_ref):
    lam = _diff_lambda(lq1_ref, lk1_ref, lq2_ref, lk2_ref, lambda_init)
    gsub = gsub_ref[...]
    k = k_ref[...]
    v = v_ref[...]
    head_rows = pl.ds(pl.program_id(1), PAST_LEN, stride=N_LANE_BLOCKS)
    ck = ck_ref[head_rows, :].astype(BF16)
    cv = cv_ref[head_rows, :].astype(BF16)
    for c in range(DEC_SEQ // Q_CHUNK):
        rows = slice(c * Q_CHUNK, (c + 1) * Q_CHUNK)
        o = _diff_head(q_ref[rows, :], [k, ck], [_times(v), _times(cv)], lam, gsub, lambda_init)
        o_ref[rows, :] = o.astype(BF16)


def _diff_latent_attention(q, k, v, cache_k, cache_v, layer_i, lam_params, gsub, lambda_init):
    blk = pl.BlockSpec((DEC_SEQ, LANES), lambda b, h: (b, h))
    cache = pl.BlockSpec((None, None, PAST_LEN * N_LANE_BLOCKS, LANES), lambda b, h: (b, layer_i, 0, 0))
    small = [_resident((1, HEAD_DIM))] * 4 + [_resident((1, LANES))]
    return pl.pallas_call(
        functools.partial(_diff_latent_kernel, lambda_init),
        grid=(DEC_BATCH, N_LANE_BLOCKS),
        in_specs=[blk, blk, blk, cache, cache] + small,
        out_specs=blk,
        out_shape=jax.ShapeDtypeStruct((N_SAMPLE_ROWS, D_MODEL), BF16),
        compiler_params=_params(2),
        name="diff_latent_attention",
    )(q, k, v, cache_k, cache_v, *lam_params, gsub)


NA_Q_ROWS = 8
NA_K_ROWS = 12
NA_Q_TOK = NA_Q_ROWS * GRID_W
NA_K_TOK = NA_K_ROWS * GRID_W
NA_GROUPS = GRID_ROWS // NA_Q_ROWS
NA_K_STEP = (GRID_ROWS - NA_K_ROWS) // (NA_GROUPS - 1)


def _na_latent_kernel(q_ref, k_ref, v_ref, ckt_ref, cvt_ref, bias_ref, o_ref):
    ck_t = ckt_ref[...].astype(BF16)
    cv_t = cvt_ref[...].astype(BF16)
    for g in range(NA_GROUPS):
        qrows = slice(g * NA_Q_TOK, (g + 1) * NA_Q_TOK)
        krows = slice(g * NA_K_STEP * GRID_W, g * NA_K_STEP * GRID_W + NA_K_TOK)
        kw = k_ref[krows, :]
        vw = v_ref[krows, :]

        def scores(qm, half, g=g, kw=kw):
            return [_dot_nt(qm, kw) + bias_ref[half, g], _dot(qm, ck_t)]

        o_ref[qrows, :] = _pair_heads(q_ref[qrows, :], scores, [_times(vw), _times_t(cv_t)]).astype(BF16)


def _na_latent_attention(q, k, v, cache_k_t, cache_v_t, layer_i, bias):
    blk = pl.BlockSpec((DEC_SEQ, LANES), lambda p, b: (b, p))
    cache = pl.BlockSpec((None, None, LANES, PAST_LEN), lambda p, b: (b, layer_i, p, 0))
    bias_spec = pl.BlockSpec((None, None, 2, NA_GROUPS, NA_Q_TOK, NA_K_TOK),
                             lambda p, b: (layer_i, p, 0, 0, 0, 0))
    return pl.pallas_call(
        _na_latent_kernel,
        grid=(N_LANE_BLOCKS, DEC_BATCH),
        in_specs=[blk, blk, blk, cache, cache, bias_spec],
        out_specs=blk,
        out_shape=jax.ShapeDtypeStruct((N_SAMPLE_ROWS, D_MODEL), BF16),
        compiler_params=_params(2),
        name="na_latent_attention",
    )(q, k, v, cache_k_t, cache_v_t, bias)


N_DR = 2 * NA_KH - 1
N_DC = 2 * NA_KW - 1


def _na_bias_kernel(rb_ref, out_ref, tile_ref):
    layer = pl.program_id(0)
    h = pl.program_id(1)
    qc = lax.broadcasted_iota(jnp.int32, (GRID_W, LANES), 0)
    lane = lax.broadcasted_iota(jnp.int32, (GRID_W, LANES), 1)
    kc = lane % GRID_W
    dc = jnp.clip(kc - qc, -(NA_KW - 1), NA_KW - 1) + (NA_KW - 1)
    cs = jnp.clip(qc - NA_KW // 2, 0, GRID_W - NA_KW)
    col_ok = (kc >= cs) & (kc < cs + NA_KW)
    neg = jnp.full((GRID_W, LANES), NEG_INF, F32)
    for dr in range(N_DR):
        acc = jnp.zeros((GRID_W, LANES), F32)
        for j in range(N_DC):
            acc = jnp.where(dc == j, rb_ref[layer, h, dr * N_DC + j], acc)
        tile_ref[dr] = jnp.where(col_ok, acc, neg)
    low = lane < GRID_W
    for g in range(NA_GROUPS):
        for rq in range(NA_Q_ROWS):
            r = g * NA_Q_ROWS + rq
            rs = min(max(r - NA_KH // 2, 0), GRID_ROWS - NA_KH)

            def tile(kr, r=r, rs=rs):
                if rs <= kr < rs + NA_KH:
                    return tile_ref[kr - r + NA_KH - 1]
                return neg

            for p in range(NA_K_ROWS // 2):
                kr = g * NA_K_STEP + 2 * p
                out_ref[g, rq * GRID_W:(rq + 1) * GRID_W, p * LANES:(p + 1) * LANES] = (
                    jnp.where(low, tile(kr), tile(kr + 1)))


def _na_bias(rel_bias_na):
    n_layers = rel_bias_na.shape[0]
    rb = rel_bias_na.reshape(n_layers, NA_HEADS, N_DR * N_DC)
    out = pl.pallas_call(
        _na_bias_kernel,
        grid=(n_layers, NA_HEADS),
        in_specs=[pl.BlockSpec(memory_space=pltpu.SMEM)],
        out_specs=pl.BlockSpec((None, None, NA_GROUPS, NA_Q_TOK, NA_K_TOK), lambda l, h: (l, h, 0, 0, 0)),
        out_shape=jax.ShapeDtypeStruct((n_layers, NA_HEADS, NA_GROUPS, NA_Q_TOK, NA_K_TOK), F32),
        scratch_shapes=[pltpu.VMEM((N_DR, GRID_W, LANES), F32)],
        compiler_params=_params(2),
        name="na_bias_expand",
    )(rb)
    return out.reshape(n_layers, NA_HEADS // 2, 2, NA_GROUPS, NA_Q_TOK, NA_K_TOK)


FF_CHUNK = 1024


def _post_kernel(split_in, split_out, *refs):
    n_x = 2 if split_in else 1
    x_refs = refs[:n_x]
    op_ref, os_ref, mod_ref, g_ref, wo_ref, w1_ref, w2_ref = refs[n_x:n_x + 7]
    y_refs = refs[n_x + 7:]
    i = pl.program_id(0)
    r = _cond_row(i)
    is_prompt = i < N_PROMPT_TILES

    def mod(j):
        return mod_ref[pl.ds(r, 1), j * D_MODEL:(j + 1) * D_MODEL]

    x = jnp.where(is_prompt, x_refs[0][...], x_refs[1][...]) if split_in else x_refs[0][...]
    o = jnp.where(is_prompt, op_ref[...], os_ref[...])
    y1 = x + mod(2) * _dot(o, wo_ref[...])
    ms = jnp.mean(y1 * y1, axis=-1, keepdims=True)
    h = (y1 * lax.rsqrt(ms + EPS) * g_ref[...] * (1.0 + mod(4)) + mod(3)).astype(BF16)
    acc = jnp.zeros((ROW_TILE, D_MODEL), F32)
    for c in range(D_FF // FF_CHUNK):
        cols = slice(c * FF_CHUNK, (c + 1) * FF_CHUNK)
        hid = jnp.maximum(_dot(h, w1_ref[:, cols]), 0.0)
        acc = acc + _dot((hid * hid).astype(BF16), w2_ref[cols, :])
    y = y1 + mod(5) * acc
    if split_out:
        @pl.when(is_prompt)
        def _():
            y_refs[0][...] = y

        @pl.when(jnp.logical_not(is_prompt))
        def _():
            y_refs[1][...] = y
    else:
        y_refs[0][...] = y


def _post(xs, o_p, o_s, mod, layer, g_mlp, w_o, kind_layer, w_fc1, w_fc2, split_out):
    prompt_spec = pl.BlockSpec((ROW_TILE, D_MODEL), lambda i: (_prompt_tile(i), 0))
    sample_spec = pl.BlockSpec((ROW_TILE, D_MODEL), lambda i: (_sample_tile(i), 0))
    row_spec = pl.BlockSpec((ROW_TILE, D_MODEL), lambda i: (i, 0))
    split_in = len(xs) == 2
    if split_out:
        out_specs = [prompt_spec, sample_spec]
        out_shape = [jax.ShapeDtypeStruct((N_PROMPT_ROWS, D_MODEL), F32),
                     jax.ShapeDtypeStruct((N_SAMPLE_ROWS, D_MODEL), F32)]
    else:
        out_specs = [row_spec]
        out_shape = [jax.ShapeDtypeStruct((N_ROWS, D_MODEL), F32)]
    return pl.pallas_call(
        functools.partial(_post_kernel, split_in, split_out),
        grid=(N_ROW_TILES,),
        in_specs=([prompt_spec, sample_spec] if split_in else [row_spec]) + [
            prompt_spec, sample_spec,
            _layer_slice((N_COND, 6 * D_MODEL), layer),
            _layer_slice((1, D_MODEL), layer),
            _layer_slice((D_MODEL, D_MODEL), kind_layer),
            _layer_slice((D_MODEL, D_FF), layer),
            _layer_slice((D_FF, D_MODEL), layer),
        ],
        out_specs=out_specs,
        out_shape=out_shape,
        compiler_params=_params(1),
        name="wo_mlp",
    )(*xs, o_p, o_s, mod, g_mlp, w_o, w_fc1, w_fc2)


def _rope_tables():
    n_freq = HEAD_DIM // 4
    freqs = ROPE_BASE ** (-jnp.arange(n_freq, dtype=F32) / n_freq)
    t = jnp.arange(DEC_SEQ)
    ang_row = (t // GRID_W).astype(F32)[:, None] * freqs
    ang_col = (t % GRID_W).astype(F32)[:, None] * freqs
    cos64 = jnp.concatenate([jnp.cos(ang_row)] * 2 + [jnp.cos(ang_col)] * 2, axis=-1)
    sin64 = jnp.concatenate([-jnp.sin(ang_row), jnp.sin(ang_row),
                             -jnp.sin(ang_col), jnp.sin(ang_col)], axis=-1)
    return jnp.tile(cos64, (1, 2)), jnp.tile(sin64, (1, 2))


def _pair_gain(g):
    return jnp.tile(g.reshape(1, HEAD_DIM), (1, LANES // HEAD_DIM))


def _head_major_cache(cache):
    n_req, n_layers = cache.shape[:2]
    return jnp.transpose(cache, (0, 1, 3, 4, 2)).reshape(n_req, n_layers, D_MODEL, PAST_LEN)


def kernel(x_prompt, x_sample, cache_diff_k, cache_diff_v, cache_na_k, cache_na_v, c, c_ctx, w_ada, b_ada, norm_mix_g, norm_mlp_g, w_fc1, w_fc2, w_qkv_diff, w_o_diff, q_norm_diff_g, k_norm_diff_g, lambda_q1, lambda_k1, lambda_q2, lambda_k2, subln_g, w_qkv_na, w_o_na, q_norm_na_g, k_norm_na_g, rel_bias_na):
    cond = jnp.concatenate([c_ctx[None, :], c, jnp.zeros((N_COND - 1 - DEC_BATCH, D_MODEL), F32)], axis=0)
    mod = _modulation(cond, w_ada, b_ada)
    cos_tab, sin_tab = _rope_tables()
    na_bias = _na_bias(rel_bias_na)
    cdk = cache_diff_k.reshape(DEC_BATCH, DEPTH // 2, PAST_LEN * N_LANE_BLOCKS, LANES)
    cdv = cache_diff_v.reshape(DEC_BATCH, DEPTH // 2, PAST_LEN * N_LANE_BLOCKS, LANES)
    cnk_t = _head_major_cache(cache_na_k)
    cnv_t = _head_major_cache(cache_na_v)
    g_mix = norm_mix_g.reshape(DEPTH, 1, D_MODEL)
    g_mlp = norm_mlp_g.reshape(DEPTH, 1, D_MODEL)
    w1 = w_fc1.astype(BF16)
    w2 = w_fc2.astype(BF16)
    w_qkv = (w_qkv_diff.astype(BF16), w_qkv_na.astype(BF16))
    w_o = (w_o_diff.astype(BF16), w_o_na.astype(BF16))
    q_gain = (q_norm_diff_g, q_norm_na_g)
    k_gain = (k_norm_diff_g, k_norm_na_g)

    xs = [x_prompt.reshape(N_PROMPT_ROWS, D_MODEL), x_sample.reshape(N_SAMPLE_ROWS, D_MODEL)]
    new_kv = [None, None]
    for l in range(DEPTH):
        i = l // 2
        kind = l % 2
        is_diff = kind == 0
        x_prompt_src, x_sample_src = (xs[0], xs[1]) if len(xs) == 2 else (xs[0], xs[0])
        sample_first_tile = 0 if len(xs) == 2 else N_PROMPT_TILES
        gains = (_pair_gain(q_gain[kind][i]), _pair_gain(k_gain[kind][i]))
        qp, kp, vp, new_k, new_v = _qkv(x_prompt_src, 0, True, mod, l, g_mix, w_qkv[kind], i, *gains,
                                        None, is_diff, new_kv[kind])
        new_kv[kind] = (new_k, new_v)
        qs, ks, vs = _qkv(x_sample_src, sample_first_tile, False, mod, l, g_mix, w_qkv[kind], i, *gains,
                          (cos_tab, sin_tab), is_diff, None)
        lambda_init = 0.8 - 0.6 * math.exp(-0.3 * l)
        lam_params = [p[i if is_diff else 0].reshape(1, HEAD_DIM)
                      for p in (lambda_q1, lambda_k1, lambda_q2, lambda_k2)]
        gsub = subln_g[i if is_diff else 0].reshape(1, LANES)
        o_p = _ctx_attention(qp, kp, vp, lam_params, gsub, is_diff, lambda_init)
        if is_diff:
            o_s = _diff_latent_attention(qs, ks, vs, cdk, cdv, i, lam_params, gsub, lambda_init)
        else:
            o_s = _na_latent_attention(qs, ks, vs, cnk_t, cnv_t, i, na_bias)
        xs = _post(xs, o_p, o_s, mod, l, g_mlp, w_o[kind], i, w1, w2, split_out=(l == DEPTH - 1))
    xp, xs = xs

    n_kind = DEPTH // 2
    diff_k, diff_v = (a.reshape(BATCH, n_kind, SEQ, N_LANE_BLOCKS, LANES) for a in new_kv[0])
    na_k, na_v = (jnp.transpose(a.reshape(BATCH, n_kind, NA_HEADS, HEAD_DIM, SEQ), (0, 1, 4, 2, 3))
                  for a in new_kv[1])
    return (xp.reshape(BATCH, SEQ, D_MODEL), xs.reshape(DEC_BATCH, DEC_SEQ, D_MODEL),
            diff_k, diff_v, na_k, na_v)
```

```python
import functools
import math

import jax
import jax.numpy as jnp
from jax import lax
from jax.experimental import pallas as pl
from jax.experimental.pallas import tpu as pltpu

D_MODEL = 1024
BATCH = 16
SEQ = 256
DEPTH = 4
DEC_BATCH = 4
DEC_SEQ = 1024
PAST_LEN = 256
GRID_W = 64
GRID_ROWS = DEC_SEQ // GRID_W
HEAD_DIM = 64
D_FF = 4 * D_MODEL
NA_HEADS = 16
NA_KH = 8
NA_KW = 16
ROPE_BASE = 10000.0
EPS = 1e-6
NEG_INF = -1e30

LANES = 128
SUBLANES = 8
N_LANE_BLOCKS = D_MODEL // LANES
N_PROMPT_ROWS = BATCH * SEQ
N_SAMPLE_ROWS = DEC_BATCH * DEC_SEQ
N_ROWS = N_PROMPT_ROWS + N_SAMPLE_ROWS
ROW_TILE = 512
N_ROW_TILES = N_ROWS // ROW_TILE
N_PROMPT_TILES = N_PROMPT_ROWS // ROW_TILE
TILES_PER_SAMPLE = DEC_SEQ // ROW_TILE
REQ_PER_TILE = ROW_TILE // SEQ
N_COND = 8
VMEM_LIMIT = 56 * 1024 * 1024

F32 = jnp.float32
BF16 = jnp.bfloat16
NT_DIMS = (((1,), (1,)), ((), ()))


def _dot(a, b):
    return jnp.dot(a, b, preferred_element_type=F32)


def _dot_nt(a, b):
    return lax.dot_general(a, b, NT_DIMS, preferred_element_type=F32)


def _params(n_axes):
    return pltpu.CompilerParams(dimension_semantics=("arbitrary",) * n_axes,
                                vmem_limit_bytes=VMEM_LIMIT)


def _resident(shape):
    return pl.BlockSpec(shape, lambda *_: (0,) * len(shape), pipeline_mode=pl.Buffered(1))


def _layer_slice(shape, layer):
    return pl.BlockSpec((None,) + shape, lambda *_: (layer,) + (0,) * len(shape),
                        pipeline_mode=pl.Buffered(1))


def _prompt_tile(i):
    return jnp.minimum(i, N_PROMPT_TILES - 1)


def _sample_tile(i):
    return jnp.maximum(i - N_PROMPT_TILES, 0)


def _cond_row(i):
    return jnp.where(i < N_PROMPT_TILES, 0, 1 + (i - N_PROMPT_TILES) // TILES_PER_SAMPLE)


def _low_half_mask(shape):
    return lax.broadcasted_iota(jnp.int32, shape, len(shape) - 1) < HEAD_DIM


def _ada_kernel(cond_ref, w_ref, b_ref, out_ref):
    cond = cond_ref[...]
    act = (cond * jax.nn.sigmoid(cond)).astype(BF16)
    out_ref[...] = _dot(act, w_ref[...].astype(BF16)) + b_ref[...]


def _modulation(cond, w_ada, b_ada):
    tn = 1536
    return pl.pallas_call(
        _ada_kernel,
        grid=(DEPTH, 6 * D_MODEL // tn),
        in_specs=[
            pl.BlockSpec((N_COND, D_MODEL), lambda l, j: (0, 0)),
            pl.BlockSpec((None, D_MODEL, tn), lambda l, j: (l, 0, j)),
            pl.BlockSpec((None, 1, tn), lambda l, j: (l, 0, j)),
        ],
        out_specs=pl.BlockSpec((None, N_COND, tn), lambda l, j: (l, 0, j)),
        out_shape=jax.ShapeDtypeStruct((DEPTH, N_COND, 6 * D_MODEL), F32),
        compiler_params=_params(2),
        name="ada_modulation",
    )(cond, w_ada, b_ada.reshape(DEPTH, 1, 6 * D_MODEL))


def _group_rms(x, gain):
    low = _low_half_mask((1, LANES))
    cols = []
    for c in range(N_LANE_BLOCKS):
        xb = x[:, c * LANES:(c + 1) * LANES]
        sq = xb * xb
        s_lo = jnp.sum(jnp.where(low, sq, 0.0), axis=-1, keepdims=True)
        s_hi = jnp.sum(jnp.where(low, 0.0, sq), axis=-1, keepdims=True)
        ms = jnp.where(low, s_lo, s_hi) * (1.0 / HEAD_DIM)
        cols.append(xb * lax.rsqrt(ms + EPS) * gain)
    return cols


def _rope(cols, cos, sin_signed):
    lane = lax.broadcasted_iota(jnp.int32, (1, LANES), 1)
    first = (lane % 32) < 16
    out = []
    for xb in cols:
        partner = jnp.where(first, pltpu.roll(xb, LANES - 16, 1), pltpu.roll(xb, 16, 1))
        out.append(xb * cos + partner * sin_signed)
    return out


def _qkv_kernel(is_diff, is_prompt, has_prev, *refs):
    if has_prev:
        refs = refs[2:]
    if is_prompt:
        x_ref, mod_ref, g_ref, w_ref, qg_ref, kg_ref, q_ref, k_ref, v_ref, kf_ref, vf_ref = refs
        r = 0
    else:
        x_ref, mod_ref, g_ref, w_ref, qg_ref, kg_ref, cos_ref, sin_ref, q_ref, k_ref, v_ref = refs
        r = 1 + pl.program_id(0) // TILES_PER_SAMPLE
    x = x_ref[...]
    ms = jnp.mean(x * x, axis=-1, keepdims=True)
    xn = x * lax.rsqrt(ms + EPS) * g_ref[...]
    shift = mod_ref[pl.ds(r, 1), 0:D_MODEL]
    scale = mod_ref[pl.ds(r, 1), D_MODEL:2 * D_MODEL]
    h = (xn * (1.0 + scale) + shift).astype(BF16)
    qkv = _dot(h, w_ref[...])
    qn = _group_rms(qkv[:, 0:D_MODEL], qg_ref[...])
    kn = _group_rms(qkv[:, D_MODEL:2 * D_MODEL], kg_ref[...])
    v = qkv[:, 2 * D_MODEL:3 * D_MODEL]
    v_ref[...] = v.astype(BF16)
    sm_scale = HEAD_DIM ** -0.5
    if is_diff and not is_prompt:
        cos = cos_ref[...]
        sin = sin_ref[...]
        qs = _rope(qn, cos, sin)
        ks = _rope(kn, cos, sin)
    else:
        qs, ks = qn, kn
    for c in range(N_LANE_BLOCKS):
        sl = slice(c * LANES, (c + 1) * LANES)
        q_ref[:, sl] = (qs[c] * sm_scale).astype(BF16)
        k_ref[:, sl] = ks[c].astype(BF16)
        if not is_prompt:
            continue
        for bb in range(REQ_PER_TILE):
            rows = slice(bb * SEQ, (bb + 1) * SEQ)
            if is_diff:
                dst = pl.ds(c, SEQ, stride=N_LANE_BLOCKS)
                kf_ref[bb, dst, :] = kn[c][rows, :]
                vf_ref[bb, dst, :] = v[rows, sl]
            else:
                kf_ref[bb, sl, :] = kn[c][rows, :].T
                vf_ref[bb, sl, :] = v[rows, sl].T


def _qkv(x, first_tile, is_prompt, mod, layer, g_mix, w_qkv, kind_layer, q_gain, k_gain, rope_tabs,
         is_diff, prev):
    n_tiles = N_PROMPT_TILES if is_prompt else N_ROW_TILES - N_PROMPT_TILES
    row_spec = pl.BlockSpec((ROW_TILE, D_MODEL), lambda i: (i, 0))
    in_specs = [
        pl.BlockSpec((ROW_TILE, D_MODEL), lambda i: (first_tile + i, 0)),
        _layer_slice((N_COND, 6 * D_MODEL), layer),
        _layer_slice((1, D_MODEL), layer),
        _layer_slice((D_MODEL, 3 * D_MODEL), kind_layer),
        _resident((1, LANES)),
        _resident((1, LANES)),
    ]
    args = [x, mod, g_mix, w_qkv, q_gain, k_gain]
    out_specs = [row_spec] * 3
    out_shape = [jax.ShapeDtypeStruct((n_tiles * ROW_TILE, D_MODEL), BF16)] * 3
    aliases = {}
    if is_prompt:
        if is_diff:
            cache_rows, cache_cols = SEQ * N_LANE_BLOCKS, LANES
        else:
            cache_rows, cache_cols = D_MODEL, SEQ
        cache_spec = pl.BlockSpec((REQ_PER_TILE, None, cache_rows, cache_cols),
                                  lambda i: (i, kind_layer, 0, 0))
        out_specs += [cache_spec] * 2
        out_shape += [jax.ShapeDtypeStruct((BATCH, DEPTH // 2, cache_rows, cache_cols), F32)] * 2
        if prev is not None:
            in_specs = [pl.BlockSpec(memory_space=pl.ANY)] * 2 + in_specs
            args = list(prev) + args
            aliases = {0: 3, 1: 4}
    else:
        rope_spec = pl.BlockSpec((ROW_TILE, LANES), lambda i: (i % TILES_PER_SAMPLE, 0))
        in_specs += [rope_spec] * 2
        args += list(rope_tabs)
    return pl.pallas_call(
        functools.partial(_qkv_kernel, is_diff, is_prompt, is_prompt and prev is not None),
        grid=(n_tiles,),
        in_specs=in_specs,
        out_specs=out_specs,
        out_shape=out_shape,
        input_output_aliases=aliases,
        compiler_params=_params(1),
        name="qkv_prompt" if is_prompt else "qkv_latent",
    )(*args)


def _softmax_pv(score_parts, value_parts):
    m = None
    for s in score_parts:
        pm = jnp.max(s, axis=-1, keepdims=True)
        m = pm if m is None else jnp.maximum(m, pm)
    total = None
    acc = None
    for s, times_v in zip(score_parts, value_parts):
        e = jnp.exp(s - m)
        ps = jnp.sum(e, axis=-1, keepdims=True)
        pv = times_v(e.astype(BF16))
        total = ps if total is None else total + ps
        acc = pv if acc is None else acc + pv
    return acc * (1.0 / total)


def _times(v):
    return lambda e: _dot(e, v)


def _times_t(v_t):
    return lambda e: _dot_nt(e, v_t)


def _diff_lambda(lq1_ref, lk1_ref, lq2_ref, lk2_ref, lambda_init):
    e1 = jnp.exp(jnp.sum(lq1_ref[...] * lk1_ref[...], axis=-1, keepdims=True))
    e2 = jnp.exp(jnp.sum(lq2_ref[...] * lk2_ref[...], axis=-1, keepdims=True))
    return e1 - e2 + lambda_init


def _attend_pairs(n_blocks, load_q, scores, values, combine, store):
    units = [(b, m) for b in range(n_blocks) for m in range(2)]

    def unit_scores(u):
        b, m = units[u]
        q = load_q(b)
        low = _low_half_mask(q.shape)
        zero = jnp.zeros_like(q)
        return scores(b, m, jnp.where(low, q, zero) if m == 0 else jnp.where(low, zero, q))

    s = unit_scores(0)
    first = None
    for u, (b, m) in enumerate(units):
        s_next = unit_scores(u + 1) if u + 1 < len(units) else None
        o = _softmax_pv(s, values(b))
        s = s_next
        if m == 0:
            first = o
        else:
            store(b, combine(first, o))


def _diff_combine(lam, gsub, lambda_init):
    def combine(o1, o2):
        o = o1 - lam * o2
        ms = jnp.mean(o * o, axis=-1, keepdims=True)
        return o * lax.rsqrt(ms + EPS) * gsub * (1.0 - lambda_init)
    return combine


def _head_pair_combine(o_lo, o_hi):
    return jnp.where(_low_half_mask(o_lo.shape), o_lo, o_hi)


def _ctx_attn_kernel(is_diff, lambda_init, q_ref, k_ref, v_ref, lq1_ref, lk1_ref, lq2_ref, lk2_ref,
                     gsub_ref, o_ref):
    if is_diff:
        lam = _diff_lambda(lq1_ref, lk1_ref, lq2_ref, lk2_ref, lambda_init)
        combine = _diff_combine(lam, gsub_ref[...], lambda_init)
    else:
        combine = _head_pair_combine

    def cols(h):
        return slice(h * LANES, (h + 1) * LANES)

    def store(h, o):
        o_ref[:, cols(h)] = o.astype(BF16)

    _attend_pairs(N_LANE_BLOCKS,
                  lambda h: q_ref[:, cols(h)],
                  lambda h, m, qm: [_dot_nt(qm, k_ref[:, cols(h)])],
                  lambda h: [_times(v_ref[:, cols(h)])],
                  combine, store)


def _ctx_attention(q, k, v, lam_params, gsub, is_diff, lambda_init):
    blk = pl.BlockSpec((SEQ, D_MODEL), lambda b: (b, 0))
    small = [_resident((1, HEAD_DIM))] * 4 + [_resident((1, LANES))]
    return pl.pallas_call(
        functools.partial(_ctx_attn_kernel, is_diff, lambda_init),
        grid=(BATCH,),
        in_specs=[blk, blk, blk] + small,
        out_specs=blk,
        out_shape=jax.ShapeDtypeStruct((N_PROMPT_ROWS, D_MODEL), BF16),
        compiler_params=_params(1),
        name="ctx_attention",
    )(q, k, v, *lam_params, gsub)


Q_CHUNK = 512


def _diff_latent_kernel(lambda_init, q_ref, k_ref, v_ref, ck_ref, cv_ref,
                        lq1_ref, lk1_ref, lq2_ref, lk2_ref, gsub_ref, o_ref):
    lam = _diff_lambda(lq1_ref, lk1_ref, lq2_ref, lk2_ref, lambda_init)
    k = k_ref[...]
    v = v_ref[...]
    head_rows = pl.ds(pl.program_id(1), PAST_LEN, stride=N_LANE_BLOCKS)
    ck = ck_ref[head_rows, :].astype(BF16)
    cv = cv_ref[head_rows, :].astype(BF16)
    value_parts = [_times(v), _times(cv)]

    def rows(c):
        return slice(c * Q_CHUNK, (c + 1) * Q_CHUNK)

    def store(c, o):
        o_ref[rows(c), :] = o.astype(BF16)

    _attend_pairs(DEC_SEQ // Q_CHUNK,
                  lambda c: q_ref[rows(c), :],
                  lambda c, m, qm: [_dot_nt(qm, k), _dot_nt(qm, ck)],
                  lambda c: value_parts,
                  _diff_combine(lam, gsub_ref[...], lambda_init), store)


def _diff_latent_attention(q, k, v, cache_k, cache_v, layer_i, lam_params, gsub, lambda_init):
    blk = pl.BlockSpec((DEC_SEQ, LANES), lambda b, h: (b, h))
    cache = pl.BlockSpec((None, None, PAST_LEN * N_LANE_BLOCKS, LANES), lambda b, h: (b, layer_i, 0, 0))
    small = [_resident((1, HEAD_DIM))] * 4 + [_resident((1, LANES))]
    return pl.pallas_call(
        functools.partial(_diff_latent_kernel, lambda_init),
        grid=(DEC_BATCH, N_LANE_BLOCKS),
        in_specs=[blk, blk, blk, cache, cache] + small,
        out_specs=blk,
        out_shape=jax.ShapeDtypeStruct((N_SAMPLE_ROWS, D_MODEL), BF16),
        compiler_params=_params(2),
        name="diff_latent_attention",
    )(q, k, v, cache_k, cache_v, *lam_params, gsub)


NA_Q_ROWS = 8
NA_K_ROWS = 12
NA_Q_TOK = NA_Q_ROWS * GRID_W
NA_K_TOK = NA_K_ROWS * GRID_W
NA_GROUPS = GRID_ROWS // NA_Q_ROWS
NA_K_STEP = (GRID_ROWS - NA_K_ROWS) // (NA_GROUPS - 1)


def _na_latent_kernel(q_ref, k_ref, v_ref, ckt_ref, cvt_ref, bias_ref, o_ref):
    ck_t = ckt_ref[...].astype(BF16)
    cv_t = cvt_ref[...].astype(BF16)

    def qrows(g):
        return slice(g * NA_Q_TOK, (g + 1) * NA_Q_TOK)

    def krows(g):
        return slice(g * NA_K_STEP * GRID_W, g * NA_K_STEP * GRID_W + NA_K_TOK)

    def store(g, o):
        o_ref[qrows(g), :] = o.astype(BF16)

    _attend_pairs(NA_GROUPS,
                  lambda g: q_ref[qrows(g), :],
                  lambda g, half, qm: [_dot_nt(qm, k_ref[krows(g), :]) + bias_ref[half, g], _dot(qm, ck_t)],
                  lambda g: [_times(v_ref[krows(g), :]), _times_t(cv_t)],
                  _head_pair_combine, store)


def _na_latent_attention(q, k, v, cache_k_t, cache_v_t, layer_i, bias):
    blk = pl.BlockSpec((DEC_SEQ, LANES), lambda p, b: (b, p))
    cache = pl.BlockSpec((None, None, LANES, PAST_LEN), lambda p, b: (b, layer_i, p, 0))
    bias_spec = pl.BlockSpec((None, None, 2, NA_GROUPS, NA_Q_TOK, NA_K_TOK),
                             lambda p, b: (layer_i, p, 0, 0, 0, 0))
    return pl.pallas_call(
        _na_latent_kernel,
        grid=(N_LANE_BLOCKS, DEC_BATCH),
        in_specs=[blk, blk, blk, cache, cache, bias_spec],
        out_specs=blk,
        out_shape=jax.ShapeDtypeStruct((N_SAMPLE_ROWS, D_MODEL), BF16),
        compiler_params=_params(2),
        name="na_latent_attention",
    )(q, k, v, cache_k_t, cache_v_t, bias)


N_DR = 2 * NA_KH - 1
N_DC = 2 * NA_KW - 1


def _na_bias_kernel(rb_ref, out_ref, tile_ref):
    layer = pl.program_id(0)
    h = pl.program_id(1)
    qc = lax.broadcasted_iota(jnp.int32, (GRID_W, LANES), 0)
    lane = lax.broadcasted_iota(jnp.int32, (GRID_W, LANES), 1)
    kc = lane % GRID_W
    dc = jnp.clip(kc - qc, -(NA_KW - 1), NA_KW - 1) + (NA_KW - 1)
    cs = jnp.clip(qc - NA_KW // 2, 0, GRID_W - NA_KW)
    col_ok = (kc >= cs) & (kc < cs + NA_KW)
    neg = jnp.full((GRID_W, LANES), NEG_INF, F32)
    for dr in range(N_DR):
        acc = jnp.zeros((GRID_W, LANES), F32)
        for j in range(N_DC):
            acc = jnp.where(dc == j, rb_ref[layer, h, dr * N_DC + j], acc)
        tile_ref[dr] = jnp.where(col_ok, acc, neg)
    low = lane < GRID_W
    for g in range(NA_GROUPS):
        for rq in range(NA_Q_ROWS):
            r = g * NA_Q_ROWS + rq
            rs = min(max(r - NA_KH // 2, 0), GRID_ROWS - NA_KH)

            def tile(kr, r=r, rs=rs):
                if rs <= kr < rs + NA_KH:
                    return tile_ref[kr - r + NA_KH - 1]
                return neg

            for p in range(NA_K_ROWS // 2):
                kr = g * NA_K_STEP + 2 * p
                out_ref[g, rq * GRID_W:(rq + 1) * GRID_W, p * LANES:(p + 1) * LANES] = (
                    jnp.where(low, tile(kr), tile(kr + 1)))


def _na_bias(rel_bias_na):
    n_layers = rel_bias_na.shape[0]
    rb = rel_bias_na.reshape(n_layers, NA_HEADS, N_DR * N_DC)
    out = pl.pallas_call(
        _na_bias_kernel,
        grid=(n_layers, NA_HEADS),
        in_specs=[pl.BlockSpec(memory_space=pltpu.SMEM)],
        out_specs=pl.BlockSpec((None, None, NA_GROUPS, NA_Q_TOK, NA_K_TOK), lambda l, h: (l, h, 0, 0, 0)),
        out_shape=jax.ShapeDtypeStruct((n_layers, NA_HEADS, NA_GROUPS, NA_Q_TOK, NA_K_TOK), F32),
        scratch_shapes=[pltpu.VMEM((N_DR, GRID_W, LANES), F32)],
        compiler_params=_params(2),
        name="na_bias_expand",
    )(rb)
    return out.reshape(n_layers, NA_HEADS // 2, 2, NA_GROUPS, NA_Q_TOK, NA_K_TOK)


FF_CHUNK = 1024


def _post_kernel(split_in, split_out, *refs):
    n_x = 2 if split_in else 1
    x_refs = refs[:n_x]
    op_ref, os_ref, mod_ref, g_ref, wo_ref, w1_ref, w2_ref = refs[n_x:n_x + 7]
    y_refs = refs[n_x + 7:]
    i = pl.program_id(0)
    r = _cond_row(i)
    is_prompt = i < N_PROMPT_TILES

    def mod(j):
        return mod_ref[pl.ds(r, 1), j * D_MODEL:(j + 1) * D_MODEL]

    x = jnp.where(is_prompt, x_refs[0][...], x_refs[1][...]) if split_in else x_refs[0][...]
    o = jnp.where(is_prompt, op_ref[...], os_ref[...])
    y1 = x + mod(2) * _dot(o, wo_ref[...])
    ms = jnp.mean(y1 * y1, axis=-1, keepdims=True)
    h = (y1 * lax.rsqrt(ms + EPS) * g_ref[...] * (1.0 + mod(4)) + mod(3)).astype(BF16)
    acc = jnp.zeros((ROW_TILE, D_MODEL), F32)
    for c in range(D_FF // FF_CHUNK):
        cols = slice(c * FF_CHUNK, (c + 1) * FF_CHUNK)
        hid = jnp.maximum(_dot(h, w1_ref[:, cols]), 0.0)
        acc = acc + _dot((hid * hid).astype(BF16), w2_ref[cols, :])
    y = y1 + mod(5) * acc
    if split_out:
        @pl.when(is_prompt)
        def _():
            y_refs[0][...] = y

        @pl.when(jnp.logical_not(is_prompt))
        def _():
            y_refs[1][...] = y
    else:
        y_refs[0][...] = y


def _post(xs, o_p, o_s, mod, layer, g_mlp, w_o, kind_layer, w_fc1, w_fc2, split_out):
    prompt_spec = pl.BlockSpec((ROW_TILE, D_MODEL), lambda i: (_prompt_tile(i), 0))
    sample_spec = pl.BlockSpec((ROW_TILE, D_MODEL), lambda i: (_sample_tile(i), 0))
    row_spec = pl.BlockSpec((ROW_TILE, D_MODEL), lambda i: (i, 0))
    split_in = len(xs) == 2
    if split_out:
        out_specs = [prompt_spec, sample_spec]
        out_shape = [jax.ShapeDtypeStruct((N_PROMPT_ROWS, D_MODEL), F32),
                     jax.ShapeDtypeStruct((N_SAMPLE_ROWS, D_MODEL), F32)]
    else:
        out_specs = [row_spec]
        out_shape = [jax.ShapeDtypeStruct((N_ROWS, D_MODEL), F32)]
    return pl.pallas_call(
        functools.partial(_post_kernel, split_in, split_out),
        grid=(N_ROW_TILES,),
        in_specs=([prompt_spec, sample_spec] if split_in else [row_spec]) + [
            prompt_spec, sample_spec,
            _layer_slice((N_COND, 6 * D_MODEL), layer),
            _layer_slice((1, D_MODEL), layer),
            _layer_slice((D_MODEL, D_MODEL), kind_layer),
            _layer_slice((D_MODEL, D_FF), layer),
            _layer_slice((D_FF, D_MODEL), layer),
        ],
        out_specs=out_specs,
        out_shape=out_shape,
        compiler_params=_params(1),
        name="wo_mlp",
    )(*xs, o_p, o_s, mod, g_mlp, w_o, w_fc1, w_fc2)


def _rope_tables():
    n_freq = HEAD_DIM // 4
    freqs = ROPE_BASE ** (-jnp.arange(n_freq, dtype=F32) / n_freq)
    t = jnp.arange(DEC_SEQ)
    ang_row = (t // GRID_W).astype(F32)[:, None] * freqs
    ang_col = (t % GRID_W).astype(F32)[:, None] * freqs
    cos64 = jnp.concatenate([jnp.cos(ang_row)] * 2 + [jnp.cos(ang_col)] * 2, axis=-1)
    sin64 = jnp.concatenate([-jnp.sin(ang_row), jnp.sin(ang_row),
                             -jnp.sin(ang_col), jnp.sin(ang_col)], axis=-1)
    return jnp.tile(cos64, (1, 2)), jnp.tile(sin64, (1, 2))


def _pair_gain(g):
    return jnp.tile(g.reshape(1, HEAD_DIM), (1, LANES // HEAD_DIM))


def _head_major_cache(cache):
    n_req, n_layers = cache.shape[:2]
    return jnp.transpose(cache, (0, 1, 3, 4, 2)).reshape(n_req, n_layers, D_MODEL, PAST_LEN)


def kernel(x_prompt, x_sample, cache_diff_k, cache_diff_v, cache_na_k, cache_na_v, c, c_ctx, w_ada, b_ada, norm_mix_g, norm_mlp_g, w_fc1, w_fc2, w_qkv_diff, w_o_diff, q_norm_diff_g, k_norm_diff_g, lambda_q1, lambda_k1, lambda_q2, lambda_k2, subln_g, w_qkv_na, w_o_na, q_norm_na_g, k_norm_na_g, rel_bias_na):
    cond = jnp.concatenate([c_ctx[None, :], c, jnp.zeros((N_COND - 1 - DEC_BATCH, D_MODEL), F32)], axis=0)
    mod = _modulation(cond, w_ada, b_ada)
    cos_tab, sin_tab = _rope_tables()
    na_bias = _na_bias(rel_bias_na)
    cdk = cache_diff_k.reshape(DEC_BATCH, DEPTH // 2, PAST_LEN * N_LANE_BLOCKS, LANES)
    cdv = cache_diff_v.reshape(DEC_BATCH, DEPTH // 2, PAST_LEN * N_LANE_BLOCKS, LANES)
    cnk_t = _head_major_cache(cache_na_k)
    cnv_t = _head_major_cache(cache_na_v)
    g_mix = norm_mix_g.reshape(DEPTH, 1, D_MODEL)
    g_mlp = norm_mlp_g.reshape(DEPTH, 1, D_MODEL)
    w1 = w_fc1.astype(BF16)
    w2 = w_fc2.astype(BF16)
    w_qkv = (w_qkv_diff.astype(BF16), w_qkv_na.astype(BF16))
    w_o = (w_o_diff.astype(BF16), w_o_na.astype(BF16))
    q_gain = (q_norm_diff_g, q_norm_na_g)
    k_gain = (k_norm_diff_g, k_norm_na_g)

    xs = [x_prompt.reshape(N_PROMPT_ROWS, D_MODEL), x_sample.reshape(N_SAMPLE_ROWS, D_MODEL)]
    new_kv = [None, None]
    for l in range(DEPTH):
        i = l // 2
        kind = l % 2
        is_diff = kind == 0
        x_prompt_src, x_sample_src = (xs[0], xs[1]) if len(xs) == 2 else (xs[0], xs[0])
        sample_first_tile = 0 if len(xs) == 2 else N_PROMPT_TILES
        gains = (_pair_gain(q_gain[kind][i]), _pair_gain(k_gain[kind][i]))
        qp, kp, vp, new_k, new_v = _qkv(x_prompt_src, 0, True, mod, l, g_mix, w_qkv[kind], i, *gains,
                                        None, is_diff, new_kv[kind])
        new_kv[kind] = (new_k, new_v)
        qs, ks, vs = _qkv(x_sample_src, sample_first_tile, False, mod, l, g_mix, w_qkv[kind], i, *gains,
                          (cos_tab, sin_tab), is_diff, None)
        lambda_init = 0.8 - 0.6 * math.exp(-0.3 * l)
        lam_params = [p[i if is_diff else 0].reshape(1, HEAD_DIM)
                      for p in (lambda_q1, lambda_k1, lambda_q2, lambda_k2)]
        gsub = subln_g[i if is_diff else 0].reshape(1, LANES)
        o_p = _ctx_attention(qp, kp, vp, lam_params, gsub, is_diff, lambda_init)
        if is_diff:
            o_s = _diff_latent_attention(qs, ks, vs, cdk, cdv, i, lam_params, gsub, lambda_init)
        else:
            o_s = _na_latent_attention(qs, ks, vs, cnk_t, cnv_t, i, na_bias)
        xs = _post(xs, o_p, o_s, mod, l, g_mlp, w_o[kind], i, w1, w2, split_out=(l == DEPTH - 1))
    xp, xs = xs

    n_kind = DEPTH // 2
    diff_k, diff_v = (a.reshape(BATCH, n_kind, SEQ, N_LANE_BLOCKS, LANES) for a in new_kv[0])
    na_k, na_v = (jnp.transpose(a.reshape(BATCH, n_kind, NA_HEADS, HEAD_DIM, SEQ), (0, 1, 4, 2, 3))
                  for a in new_kv[1])
    return (xp.reshape(BATCH, SEQ, D_MODEL), xs.reshape(DEC_BATCH, DEC_SEQ, D_MODEL),
            diff_k, diff_v, na_k, na_v)
```

```python
import functools
import math

import jax
import jax.numpy as jnp
from jax import lax
from jax.experimental import pallas as pl
from jax.experimental.pallas import tpu as pltpu

D_MODEL = 1024
BATCH = 16
SEQ = 256
DEPTH = 4
DEC_BATCH = 4
DEC_SEQ = 1024
PAST_LEN = 256
GRID_W = 64
GRID_ROWS = DEC_SEQ // GRID_W
HEAD_DIM = 64
D_FF = 4 * D_MODEL
NA_HEADS = 16
NA_KH = 8
NA_KW = 16
ROPE_BASE = 10000.0
EPS = 1e-6
NEG_INF = -1e30
LOG2_E = math.log2(math.e)
SCORE_SCALE = HEAD_DIM ** -0.5 * LOG2_E

LANES = 128
SUBLANES = 8
N_LANE_BLOCKS = D_MODEL // LANES
N_PROMPT_ROWS = BATCH * SEQ
N_SAMPLE_ROWS = DEC_BATCH * DEC_SEQ
N_ROWS = N_PROMPT_ROWS + N_SAMPLE_ROWS
ROW_TILE = 512
N_ROW_TILES = N_ROWS // ROW_TILE
N_PROMPT_TILES = N_PROMPT_ROWS // ROW_TILE
TILES_PER_SAMPLE = DEC_SEQ // ROW_TILE
REQ_PER_TILE = ROW_TILE // SEQ
N_COND = 8
VMEM_LIMIT = 56 * 1024 * 1024

F32 = jnp.float32
BF16 = jnp.bfloat16
NT_DIMS = (((1,), (1,)), ((), ()))


def _dot(a, b):
    return jnp.dot(a, b, preferred_element_type=F32)


def _dot_nt(a, b):
    return lax.dot_general(a, b, NT_DIMS, preferred_element_type=F32)


def _params(n_axes):
    return pltpu.CompilerParams(dimension_semantics=("arbitrary",) * n_axes,
                                vmem_limit_bytes=VMEM_LIMIT)


def _resident(shape):
    return pl.BlockSpec(shape, lambda *_: (0,) * len(shape), pipeline_mode=pl.Buffered(1))


def _layer_slice(shape, layer):
    return pl.BlockSpec((None,) + shape, lambda *_: (layer,) + (0,) * len(shape),
                        pipeline_mode=pl.Buffered(1))


def _prompt_tile(i):
    return jnp.minimum(i, N_PROMPT_TILES - 1)


def _sample_tile(i):
    return jnp.maximum(i - N_PROMPT_TILES, 0)


def _cond_row(i):
    return jnp.where(i < N_PROMPT_TILES, 0, 1 + (i - N_PROMPT_TILES) // TILES_PER_SAMPLE)


def _low_half_mask(shape):
    return lax.broadcasted_iota(jnp.int32, shape, len(shape) - 1) < HEAD_DIM


def _ada_kernel(cond_ref, w_ref, b_ref, out_ref):
    cond = cond_ref[...]
    act = (cond * jax.nn.sigmoid(cond)).astype(BF16)
    out_ref[...] = _dot(act, w_ref[...].astype(BF16)) + b_ref[...]


def _modulation(cond, w_ada, b_ada):
    tn = 1536
    return pl.pallas_call(
        _ada_kernel,
        grid=(DEPTH, 6 * D_MODEL // tn),
        in_specs=[
            pl.BlockSpec((N_COND, D_MODEL), lambda l, j: (0, 0)),
            pl.BlockSpec((None, D_MODEL, tn), lambda l, j: (l, 0, j)),
            pl.BlockSpec((None, 1, tn), lambda l, j: (l, 0, j)),
        ],
        out_specs=pl.BlockSpec((None, N_COND, tn), lambda l, j: (l, 0, j)),
        out_shape=jax.ShapeDtypeStruct((DEPTH, N_COND, 6 * D_MODEL), F32),
        compiler_params=_params(2),
        name="ada_modulation",
    )(cond, w_ada, b_ada.reshape(DEPTH, 1, 6 * D_MODEL))


def _group_rms(x, gain):
    low = _low_half_mask((1, LANES))
    cols = []
    for c in range(N_LANE_BLOCKS):
        xb = x[:, c * LANES:(c + 1) * LANES]
        sq = xb * xb
        s_lo = jnp.sum(jnp.where(low, sq, 0.0), axis=-1, keepdims=True)
        s_hi = jnp.sum(jnp.where(low, 0.0, sq), axis=-1, keepdims=True)
        ms = jnp.where(low, s_lo, s_hi) * (1.0 / HEAD_DIM)
        cols.append(xb * lax.rsqrt(ms + EPS) * gain)
    return cols


def _rope(cols, cos, sin_signed):
    lane = lax.broadcasted_iota(jnp.int32, (1, LANES), 1)
    first = (lane % 32) < 16
    out = []
    for xb in cols:
        partner = jnp.where(first, pltpu.roll(xb, LANES - 16, 1), pltpu.roll(xb, 16, 1))
        out.append(xb * cos + partner * sin_signed)
    return out


def _qkv_kernel(is_diff, is_prompt, has_prev, *refs):
    if has_prev:
        refs = refs[2:]
    if is_prompt:
        x_ref, mod_ref, g_ref, w_ref, qg_ref, kg_ref, q_ref, k_ref, v_ref, kf_ref, vf_ref = refs
        r = 0
    else:
        x_ref, mod_ref, g_ref, w_ref, qg_ref, kg_ref, cos_ref, sin_ref, q_ref, k_ref, v_ref = refs
        r = 1 + pl.program_id(0) // TILES_PER_SAMPLE
    x = x_ref[...]
    ms = jnp.mean(x * x, axis=-1, keepdims=True)
    xn = x * lax.rsqrt(ms + EPS) * g_ref[...]
    shift = mod_ref[pl.ds(r, 1), 0:D_MODEL]
    scale = mod_ref[pl.ds(r, 1), D_MODEL:2 * D_MODEL]
    h = (xn * (1.0 + scale) + shift).astype(BF16)
    qkv = _dot(h, w_ref[...])
    qn = _group_rms(qkv[:, 0:D_MODEL], qg_ref[...])
    kn = _group_rms(qkv[:, D_MODEL:2 * D_MODEL], kg_ref[...])
    v = qkv[:, 2 * D_MODEL:3 * D_MODEL]
    v_ref[...] = v.astype(BF16)
    sm_scale = SCORE_SCALE
    if is_diff and not is_prompt:
        cos = cos_ref[...]
        sin = sin_ref[...]
        qs = _rope(qn, cos, sin)
        ks = _rope(kn, cos, sin)
    else:
        qs, ks = qn, kn
    for c in range(N_LANE_BLOCKS):
        sl = slice(c * LANES, (c + 1) * LANES)
        q_ref[:, sl] = (qs[c] * sm_scale).astype(BF16)
        k_ref[:, sl] = ks[c].astype(BF16)
        if not is_prompt:
            continue
        for bb in range(REQ_PER_TILE):
            rows = slice(bb * SEQ, (bb + 1) * SEQ)
            if is_diff:
                dst = pl.ds(c, SEQ, stride=N_LANE_BLOCKS)
                kf_ref[bb, dst, :] = kn[c][rows, :]
                vf_ref[bb, dst, :] = v[rows, sl]
            else:
                kf_ref[bb, sl, :] = kn[c][rows, :].T
                vf_ref[bb, sl, :] = v[rows, sl].T


def _qkv(x, first_tile, is_prompt, mod, layer, g_mix, w_qkv, kind_layer, q_gain, k_gain, rope_tabs,
         is_diff, prev):
    n_tiles = N_PROMPT_TILES if is_prompt else N_ROW_TILES - N_PROMPT_TILES
    row_spec = pl.BlockSpec((ROW_TILE, D_MODEL), lambda i: (i, 0))
    in_specs = [
        pl.BlockSpec((ROW_TILE, D_MODEL), lambda i: (first_tile + i, 0)),
        _layer_slice((N_COND, 6 * D_MODEL), layer),
        _layer_slice((1, D_MODEL), layer),
        _layer_slice((D_MODEL, 3 * D_MODEL), kind_layer),
        _resident((1, LANES)),
        _resident((1, LANES)),
    ]
    args = [x, mod, g_mix, w_qkv, q_gain, k_gain]
    out_specs = [row_spec] * 3
    out_shape = [jax.ShapeDtypeStruct((n_tiles * ROW_TILE, D_MODEL), BF16)] * 3
    aliases = {}
    if is_prompt:
        if is_diff:
            cache_rows, cache_cols = SEQ * N_LANE_BLOCKS, LANES
        else:
            cache_rows, cache_cols = D_MODEL, SEQ
        cache_spec = pl.BlockSpec((REQ_PER_TILE, None, cache_rows, cache_cols),
                                  lambda i: (i, kind_layer, 0, 0))
        out_specs += [cache_spec] * 2
        out_shape += [jax.ShapeDtypeStruct((BATCH, DEPTH // 2, cache_rows, cache_cols), F32)] * 2
        if prev is not None:
            in_specs = [pl.BlockSpec(memory_space=pl.ANY)] * 2 + in_specs
            args = list(prev) + args
            aliases = {0: 3, 1: 4}
    else:
        rope_spec = pl.BlockSpec((ROW_TILE, LANES), lambda i: (i % TILES_PER_SAMPLE, 0))
        in_specs += [rope_spec] * 2
        args += list(rope_tabs)
    return pl.pallas_call(
        functools.partial(_qkv_kernel, is_diff, is_prompt, is_prompt and prev is not None),
        grid=(n_tiles,),
        in_specs=in_specs,
        out_specs=out_specs,
        out_shape=out_shape,
        input_output_aliases=aliases,
        compiler_params=_params(1),
        name="qkv_prompt" if is_prompt else "qkv_latent",
    )(*args)


def _softmax_pv(score_parts, value_parts):
    m = None
    for s in score_parts:
        pm = jnp.max(s, axis=-1, keepdims=True)
        m = pm if m is None else jnp.maximum(m, pm)
    acc = None
    for s, times_v in zip(score_parts, value_parts):
        pv = times_v(jnp.exp2(s - m).astype(BF16))
        acc = pv if acc is None else acc + pv
    return acc[:, :LANES] * (1.0 / acc[:, LANES:])


def _times(v):
    v1 = jnp.concatenate([v, jnp.ones_like(v)], axis=1)
    return lambda w: _dot(w, v1)


def _times_t(v_t):
    v1_t = jnp.concatenate([v_t, jnp.ones_like(v_t)], axis=0)
    return lambda w: _dot_nt(w, v1_t)


def _diff_lambda(lq1_ref, lk1_ref, lq2_ref, lk2_ref, lambda_init):
    e1 = jnp.exp(jnp.sum(lq1_ref[...] * lk1_ref[...], axis=-1, keepdims=True))
    e2 = jnp.exp(jnp.sum(lq2_ref[...] * lk2_ref[...], axis=-1, keepdims=True))
    return e1 - e2 + lambda_init


def _attend_pairs(n_blocks, load_q, scores, values, combine, store):
    units = [(b, m) for b in range(n_blocks) for m in range(2)]

    def unit_scores(u):
        b, m = units[u]
        q = load_q(b)
        low = _low_half_mask(q.shape)
        zero = jnp.zeros_like(q)
        return scores(b, m, jnp.where(low, q, zero) if m == 0 else jnp.where(low, zero, q))

    s = unit_scores(0)
    first = None
    for u, (b, m) in enumerate(units):
        s_next = unit_scores(u + 1) if u + 1 < len(units) else None
        o = _softmax_pv(s, values(b))
        s = s_next
        if m == 0:
            first = o
        else:
            store(b, combine(first, o))


def _diff_combine(lam, gsub, lambda_init):
    def combine(o1, o2):
        o = o1 - lam * o2
        ms = jnp.mean(o * o, axis=-1, keepdims=True)
        return o * lax.rsqrt(ms + EPS) * gsub * (1.0 - lambda_init)
    return combine


def _head_pair_combine(o_lo, o_hi):
    return jnp.where(_low_half_mask(o_lo.shape), o_lo, o_hi)


def _ctx_attn_kernel(is_diff, lambda_init, q_ref, k_ref, v_ref, lq1_ref, lk1_ref, lq2_ref, lk2_ref,
                     gsub_ref, o_ref):
    if is_diff:
        lam = _diff_lambda(lq1_ref, lk1_ref, lq2_ref, lk2_ref, lambda_init)
        combine = _diff_combine(lam, gsub_ref[...], lambda_init)
    else:
        combine = _head_pair_combine

    def cols(h):
        return slice(h * LANES, (h + 1) * LANES)

    def store(h, o):
        o_ref[:, cols(h)] = o.astype(BF16)

    _attend_pairs(N_LANE_BLOCKS,
                  lambda h: q_ref[:, cols(h)],
                  lambda h, m, qm: [_dot_nt(qm, k_ref[:, cols(h)])],
                  lambda h: [_times(v_ref[:, cols(h)])],
                  combine, store)


def _ctx_attention(q, k, v, lam_params, gsub, is_diff, lambda_init):
    blk = pl.BlockSpec((SEQ, D_MODEL), lambda b: (b, 0))
    small = [_resident((1, HEAD_DIM))] * 4 + [_resident((1, LANES))]
    return pl.pallas_call(
        functools.partial(_ctx_attn_kernel, is_diff, lambda_init),
        grid=(BATCH,),
        in_specs=[blk, blk, blk] + small,
        out_specs=blk,
        out_shape=jax.ShapeDtypeStruct((N_PROMPT_ROWS, D_MODEL), BF16),
        compiler_params=_params(1),
        name="ctx_attention",
    )(q, k, v, *lam_params, gsub)


Q_CHUNK = 512


def _diff_latent_kernel(lambda_init, q_ref, k_ref, v_ref, ck_ref, cv_ref,
                        lq1_ref, lk1_ref, lq2_ref, lk2_ref, gsub_ref, o_ref):
    lam = _diff_lambda(lq1_ref, lk1_ref, lq2_ref, lk2_ref, lambda_init)
    k = k_ref[...]
    v = v_ref[...]
    head_rows = pl.ds(pl.program_id(1), PAST_LEN, stride=N_LANE_BLOCKS)
    ck = ck_ref[head_rows, :].astype(BF16)
    cv = cv_ref[head_rows, :].astype(BF16)
    value_parts = [_times(v), _times(cv)]

    def rows(c):
        return slice(c * Q_CHUNK, (c + 1) * Q_CHUNK)

    def store(c, o):
        o_ref[rows(c), :] = o.astype(BF16)

    _attend_pairs(DEC_SEQ // Q_CHUNK,
                  lambda c: q_ref[rows(c), :],
                  lambda c, m, qm: [_dot_nt(qm, k), _dot_nt(qm, ck)],
                  lambda c: value_parts,
                  _diff_combine(lam, gsub_ref[...], lambda_init), store)


def _diff_latent_attention(q, k, v, cache_k, cache_v, layer_i, lam_params, gsub, lambda_init):
    blk = pl.BlockSpec((DEC_SEQ, LANES), lambda b, h: (b, h))
    cache = pl.BlockSpec((None, None, PAST_LEN * N_LANE_BLOCKS, LANES), lambda b, h: (b, layer_i, 0, 0))
    small = [_resident((1, HEAD_DIM))] * 4 + [_resident((1, LANES))]
    return pl.pallas_call(
        functools.partial(_diff_latent_kernel, lambda_init),
        grid=(DEC_BATCH, N_LANE_BLOCKS),
        in_specs=[blk, blk, blk, cache, cache] + small,
        out_specs=blk,
        out_shape=jax.ShapeDtypeStruct((N_SAMPLE_ROWS, D_MODEL), BF16),
        compiler_params=_params(2),
        name="diff_latent_attention",
    )(q, k, v, cache_k, cache_v, *lam_params, gsub)


NA_Q_ROWS = 8
NA_K_ROWS = 12
NA_Q_TOK = NA_Q_ROWS * GRID_W
NA_K_TOK = NA_K_ROWS * GRID_W
NA_GROUPS = GRID_ROWS // NA_Q_ROWS
NA_K_STEP = (GRID_ROWS - NA_K_ROWS) // (NA_GROUPS - 1)


def _na_latent_kernel(q_ref, k_ref, v_ref, ckt_ref, cvt_ref, bias_ref, o_ref):
    ck_t = ckt_ref[...].astype(BF16)
    cv_t = cvt_ref[...].astype(BF16)

    def qrows(g):
        return slice(g * NA_Q_TOK, (g + 1) * NA_Q_TOK)

    def krows(g):
        return slice(g * NA_K_STEP * GRID_W, g * NA_K_STEP * GRID_W + NA_K_TOK)

    def store(g, o):
        o_ref[qrows(g), :] = o.astype(BF16)

    _attend_pairs(NA_GROUPS,
                  lambda g: q_ref[qrows(g), :],
                  lambda g, half, qm: [_dot_nt(qm, k_ref[krows(g), :]) + bias_ref[half, g], _dot(qm, ck_t)],
                  lambda g: [_times(v_ref[krows(g), :]), _times_t(cv_t)],
                  _head_pair_combine, store)


def _na_latent_attention(q, k, v, cache_k_t, cache_v_t, layer_i, bias):
    blk = pl.BlockSpec((DEC_SEQ, LANES), lambda p, b: (b, p))
    cache = pl.BlockSpec((None, None, LANES, PAST_LEN), lambda p, b: (b, layer_i, p, 0))
    bias_spec = pl.BlockSpec((None, None, 2, NA_GROUPS, NA_Q_TOK, NA_K_TOK),
                             lambda p, b: (layer_i, p, 0, 0, 0, 0))
    return pl.pallas_call(
        _na_latent_kernel,
        grid=(N_LANE_BLOCKS, DEC_BATCH),
        in_specs=[blk, blk, blk, cache, cache, bias_spec],
        out_specs=blk,
        out_shape=jax.ShapeDtypeStruct((N_SAMPLE_ROWS, D_MODEL), BF16),
        compiler_params=_params(2),
        name="na_latent_attention",
    )(q, k, v, cache_k_t, cache_v_t, bias)


N_DR = 2 * NA_KH - 1
N_DC = 2 * NA_KW - 1


def _na_bias_kernel(rb_ref, out_ref, tile_ref):
    layer = pl.program_id(0)
    h = pl.program_id(1)
    qc = lax.broadcasted_iota(jnp.int32, (GRID_W, LANES), 0)
    lane = lax.broadcasted_iota(jnp.int32, (GRID_W, LANES), 1)
    kc = lane % GRID_W
    dc = jnp.clip(kc - qc, -(NA_KW - 1), NA_KW - 1) + (NA_KW - 1)
    cs = jnp.clip(qc - NA_KW // 2, 0, GRID_W - NA_KW)
    col_ok = (kc >= cs) & (kc < cs + NA_KW)
    neg = jnp.full((GRID_W, LANES), NEG_INF, F32)
    for dr in range(N_DR):
        acc = jnp.zeros((GRID_W, LANES), F32)
        for j in range(N_DC):
            acc = jnp.where(dc == j, rb_ref[layer, h, dr * N_DC + j], acc)
        tile_ref[dr] = jnp.where(col_ok, acc * LOG2_E, neg)
    low = lane < GRID_W
    for g in range(NA_GROUPS):
        for rq in range(NA_Q_ROWS):
            r = g * NA_Q_ROWS + rq
            rs = min(max(r - NA_KH // 2, 0), GRID_ROWS - NA_KH)

            def tile(kr, r=r, rs=rs):
                if rs <= kr < rs + NA_KH:
                    return tile_ref[kr - r + NA_KH - 1]
                return neg

            for p in range(NA_K_ROWS // 2):
                kr = g * NA_K_STEP + 2 * p
                out_ref[g, rq * GRID_W:(rq + 1) * GRID_W, p * LANES:(p + 1) * LANES] = (
                    jnp.where(low, tile(kr), tile(kr + 1)))


def _na_bias(rel_bias_na):
    n_layers = rel_bias_na.shape[0]
    rb = rel_bias_na.reshape(n_layers, NA_HEADS, N_DR * N_DC)
    out = pl.pallas_call(
        _na_bias_kernel,
        grid=(n_layers, NA_HEADS),
        in_specs=[pl.BlockSpec(memory_space=pltpu.SMEM)],
        out_specs=pl.BlockSpec((None, None, NA_GROUPS, NA_Q_TOK, NA_K_TOK), lambda l, h: (l, h, 0, 0, 0)),
        out_shape=jax.ShapeDtypeStruct((n_layers, NA_HEADS, NA_GROUPS, NA_Q_TOK, NA_K_TOK), F32),
        scratch_shapes=[pltpu.VMEM((N_DR, GRID_W, LANES), F32)],
        compiler_params=_params(2),
        name="na_bias_expand",
    )(rb)
    return out.reshape(n_layers, NA_HEADS // 2, 2, NA_GROUPS, NA_Q_TOK, NA_K_TOK)


FF_CHUNK = 1024


def _post_kernel(split_in, split_out, *refs):
    n_x = 2 if split_in else 1
    x_refs = refs[:n_x]
    op_ref, os_ref, mod_ref, g_ref, wo_ref, w1_ref, w2_ref = refs[n_x:n_x + 7]
    y_refs = refs[n_x + 7:]
    i = pl.program_id(0)
    r = _cond_row(i)
    is_prompt = i < N_PROMPT_TILES

    def mod(j):
        return mod_ref[pl.ds(r, 1), j * D_MODEL:(j + 1) * D_MODEL]

    x = jnp.where(is_prompt, x_refs[0][...], x_refs[1][...]) if split_in else x_refs[0][...]
    o = jnp.where(is_prompt, op_ref[...], os_ref[...])
    y1 = x + mod(2) * _dot(o, wo_ref[...])
    ms = jnp.mean(y1 * y1, axis=-1, keepdims=True)
    h = (y1 * lax.rsqrt(ms + EPS) * g_ref[...] * (1.0 + mod(4)) + mod(3)).astype(BF16)
    acc = jnp.zeros((ROW_TILE, D_MODEL), F32)
    for c in range(D_FF // FF_CHUNK):
        cols = slice(c * FF_CHUNK, (c + 1) * FF_CHUNK)
        hid = jnp.maximum(_dot(h, w1_ref[:, cols]), 0.0)
        acc = acc + _dot((hid * hid).astype(BF16), w2_ref[cols, :])
    y = y1 + mod(5) * acc
    if split_out:
        @pl.when(is_prompt)
        def _():
            y_refs[0][...] = y

        @pl.when(jnp.logical_not(is_prompt))
        def _():
            y_refs[1][...] = y
    else:
        y_refs[0][...] = y


def _post(xs, o_p, o_s, mod, layer, g_mlp, w_o, kind_layer, w_fc1, w_fc2, split_out):
    prompt_spec = pl.BlockSpec((ROW_TILE, D_MODEL), lambda i: (_prompt_tile(i), 0))
    sample_spec = pl.BlockSpec((ROW_TILE, D_MODEL), lambda i: (_sample_tile(i), 0))
    row_spec = pl.BlockSpec((ROW_TILE, D_MODEL), lambda i: (i, 0))
    split_in = len(xs) == 2
    if split_out:
        out_specs = [prompt_spec, sample_spec]
        out_shape = [jax.ShapeDtypeStruct((N_PROMPT_ROWS, D_MODEL), F32),
                     jax.ShapeDtypeStruct((N_SAMPLE_ROWS, D_MODEL), F32)]
    else:
        out_specs = [row_spec]
        out_shape = [jax.ShapeDtypeStruct((N_ROWS, D_MODEL), F32)]
    return pl.pallas_call(
        functools.partial(_post_kernel, split_in, split_out),
        grid=(N_ROW_TILES,),
        in_specs=([prompt_spec, sample_spec] if split_in else [row_spec]) + [
            prompt_spec, sample_spec,
            _layer_slice((N_COND, 6 * D_MODEL), layer),
            _layer_slice((1, D_MODEL), layer),
            _layer_slice((D_MODEL, D_MODEL), kind_layer),
            _layer_slice((D_MODEL, D_FF), layer),
            _layer_slice((D_FF, D_MODEL), layer),
        ],
        out_specs=out_specs,
        out_shape=out_shape,
        compiler_params=_params(1),
        name="wo_mlp",
    )(*xs, o_p, o_s, mod, g_mlp, w_o, w_fc1, w_fc2)


def _rope_tables():
    n_freq = HEAD_DIM // 4
    freqs = ROPE_BASE ** (-jnp.arange(n_freq, dtype=F32) / n_freq)
    t = jnp.arange(DEC_SEQ)
    ang_row = (t // GRID_W).astype(F32)[:, None] * freqs
    ang_col = (t % GRID_W).astype(F32)[:, None] * freqs
    cos64 = jnp.concatenate([jnp.cos(ang_row)] * 2 + [jnp.cos(ang_col)] * 2, axis=-1)
    sin64 = jnp.concatenate([-jnp.sin(ang_row), jnp.sin(ang_row),
                             -jnp.sin(ang_col), jnp.sin(ang_col)], axis=-1)
    return jnp.tile(cos64, (1, 2)), jnp.tile(sin64, (1, 2))


def _pair_gain(g):
    return jnp.tile(g.reshape(1, HEAD_DIM), (1, LANES // HEAD_DIM))


def _head_major_cache(cache):
    n_req, n_layers = cache.shape[:2]
    return jnp.transpose(cache, (0, 1, 3, 4, 2)).reshape(n_req, n_layers, D_MODEL, PAST_LEN)


def kernel(x_prompt, x_sample, cache_diff_k, cache_diff_v, cache_na_k, cache_na_v, c, c_ctx, w_ada, b_ada, norm_mix_g, norm_mlp_g, w_fc1, w_fc2, w_qkv_diff, w_o_diff, q_norm_diff_g, k_norm_diff_g, lambda_q1, lambda_k1, lambda_q2, lambda_k2, subln_g, w_qkv_na, w_o_na, q_norm_na_g, k_norm_na_g, rel_bias_na):
    cond = jnp.concatenate([c_ctx[None, :], c, jnp.zeros((N_COND - 1 - DEC_BATCH, D_MODEL), F32)], axis=0)
    mod = _modulation(cond, w_ada, b_ada)
    cos_tab, sin_tab = _rope_tables()
    na_bias = _na_bias(rel_bias_na)
    cdk = cache_diff_k.reshape(DEC_BATCH, DEPTH // 2, PAST_LEN * N_LANE_BLOCKS, LANES)
    cdv = cache_diff_v.reshape(DEC_BATCH, DEPTH // 2, PAST_LEN * N_LANE_BLOCKS, LANES)
    cnk_t = _head_major_cache(cache_na_k)
    cnv_t = _head_major_cache(cache_na_v)
    g_mix = norm_mix_g.reshape(DEPTH, 1, D_MODEL)
    g_mlp = norm_mlp_g.reshape(DEPTH, 1, D_MODEL)
    w1 = w_fc1.astype(BF16)
    w2 = w_fc2.astype(BF16)
    w_qkv = (w_qkv_diff.astype(BF16), w_qkv_na.astype(BF16))
    w_o = (w_o_diff.astype(BF16), w_o_na.astype(BF16))
    q_gain = (q_norm_diff_g, q_norm_na_g)
    k_gain = (k_norm_diff_g, k_norm_na_g)

    xs = [x_prompt.reshape(N_PROMPT_ROWS, D_MODEL), x_sample.reshape(N_SAMPLE_ROWS, D_MODEL)]
    new_kv = [None, None]
    for l in range(DEPTH):
        i = l // 2
        kind = l % 2
        is_diff = kind == 0
        x_prompt_src, x_sample_src = (xs[0], xs[1]) if len(xs) == 2 else (xs[0], xs[0])
        sample_first_tile = 0 if len(xs) == 2 else N_PROMPT_TILES
        gains = (_pair_gain(q_gain[kind][i]), _pair_gain(k_gain[kind][i]))
        qp, kp, vp, new_k, new_v = _qkv(x_prompt_src, 0, True, mod, l, g_mix, w_qkv[kind], i, *gains,
                                        None, is_diff, new_kv[kind])
        new_kv[kind] = (new_k, new_v)
        qs, ks, vs = _qkv(x_sample_src, sample_first_tile, False, mod, l, g_mix, w_qkv[kind], i, *gains,
                          (cos_tab, sin_tab), is_diff, None)
        lambda_init = 0.8 - 0.6 * math.exp(-0.3 * l)
        lam_params = [p[i if is_diff else 0].reshape(1, HEAD_DIM)
                      for p in (lambda_q1, lambda_k1, lambda_q2, lambda_k2)]
        gsub = subln_g[i if is_diff else 0].reshape(1, LANES)
        o_p = _ctx_attention(qp, kp, vp, lam_params, gsub, is_diff, lambda_init)
        if is_diff:
            o_s = _diff_latent_attention(qs, ks, vs, cdk, cdv, i, lam_params, gsub, lambda_init)
        else:
            o_s = _na_latent_attention(qs, ks, vs, cnk_t, cnv_t, i, na_bias)
        xs = _post(xs, o_p, o_s, mod, l, g_mlp, w_o[kind], i, w1, w2, split_out=(l == DEPTH - 1))
    xp, xs = xs

    n_kind = DEPTH // 2
    diff_k, diff_v = (a.reshape(BATCH, n_kind, SEQ, N_LANE_BLOCKS, LANES) for a in new_kv[0])
    na_k, na_v = (jnp.transpose(a.reshape(BATCH, n_kind, NA_HEADS, HEAD_DIM, SEQ), (0, 1, 4, 2, 3))
                  for a in new_kv[1])
    return (xp.reshape(BATCH, SEQ, D_MODEL), xs.reshape(DEC_BATCH, DEC_SEQ, D_MODEL),
            diff_k, diff_v, na_k, na_v)
```

```python
import functools
import math
from typing import NamedTuple

import jax
import jax.numpy as jnp
from jax import lax
from jax.experimental import pallas as pl
from jax.experimental.pallas import tpu as pltpu

D_MODEL = 1024
BATCH = 16
SEQ = 256
DEPTH = 4
DEC_BATCH = 4
DEC_SEQ = 1024
PAST_LEN = 256
GRID_W = 64
GRID_ROWS = DEC_SEQ // GRID_W
HEAD_DIM = 64
D_FF = 4 * D_MODEL
NA_HEADS = 16
NA_KH = 8
NA_KW = 16
ROPE_BASE = 10000.0
EPS = 1e-6
NEG_INF = -1e30
LOG2_E = math.log2(math.e)
SCORE_SCALE = HEAD_DIM ** -0.5 * LOG2_E

LANES = 128
SUBLANES = 8
N_LANE_BLOCKS = D_MODEL // LANES
N_PROMPT_ROWS = BATCH * SEQ
N_SAMPLE_ROWS = DEC_BATCH * DEC_SEQ
N_ROWS = N_PROMPT_ROWS + N_SAMPLE_ROWS
ROW_TILE = 512
N_ROW_TILES = N_ROWS // ROW_TILE
N_PROMPT_TILES = N_PROMPT_ROWS // ROW_TILE
TILES_PER_SAMPLE = DEC_SEQ // ROW_TILE
REQ_PER_TILE = ROW_TILE // SEQ
N_COND = 8
VMEM_LIMIT = 56 * 1024 * 1024

F32 = jnp.float32
BF16 = jnp.bfloat16
NT_DIMS = (((1,), (1,)), ((), ()))


def _dot(a, b):
    return jnp.dot(a, b, preferred_element_type=F32)


def _dot_nt(a, b):
    return lax.dot_general(a, b, NT_DIMS, preferred_element_type=F32)


def _params(n_axes):
    return pltpu.CompilerParams(dimension_semantics=("arbitrary",) * n_axes,
                                vmem_limit_bytes=VMEM_LIMIT)


def _resident(shape):
    return pl.BlockSpec(shape, lambda *_: (0,) * len(shape), pipeline_mode=pl.Buffered(1))


def _layer_slice(shape, layer):
    return pl.BlockSpec((None,) + shape, lambda *_: (layer,) + (0,) * len(shape),
                        pipeline_mode=pl.Buffered(1))


class _CastJob(NamedTuple):
    weights: jax.Array
    layer: int
    split_axis: int


def _pallas_call_with_cast(kernel_fn, job, *, grid, in_specs, out_specs, out_shape, args, name,
                           input_output_aliases=None):
    if job is None:
        return pl.pallas_call(kernel_fn, grid=grid, in_specs=in_specs, out_specs=out_specs,
                              out_shape=out_shape, input_output_aliases=input_output_aliases or {},
                              compiler_params=_params(1), name=name)(*args)
    n_in, n_out = len(in_specs), len(out_specs)
    (n_steps,) = grid
    _, rows, cols = job.weights.shape
    if job.split_axis == 0:
        band = (rows // n_steps, cols)
        src_spec = pl.BlockSpec((None,) + band, lambda i: (job.layer, i, 0))
        dst_spec = pl.BlockSpec(band, lambda i: (i, 0))
    else:
        band = (rows, cols // n_steps)
        src_spec = pl.BlockSpec((None,) + band, lambda i: (job.layer, 0, i))
        dst_spec = pl.BlockSpec(band, lambda i: (0, i))

    def kernel_and_cast(*refs):
        src_ref, dst_ref = refs[n_in], refs[n_in + 1 + n_out]
        dst_ref[...] = src_ref[...].astype(BF16)
        kernel_fn(*refs[:n_in], *refs[n_in + 1:n_in + 1 + n_out], *refs[n_in + 2 + n_out:])

    return pl.pallas_call(
        kernel_and_cast,
        grid=grid,
        in_specs=list(in_specs) + [src_spec],
        out_specs=list(out_specs) + [dst_spec],
        out_shape=list(out_shape) + [jax.ShapeDtypeStruct((rows, cols), BF16)],
        input_output_aliases=input_output_aliases or {},
        compiler_params=_params(1),
        name=name,
    )(*args, job.weights)


def _prompt_tile(i):
    return jnp.minimum(i, N_PROMPT_TILES - 1)


def _sample_tile(i):
    return jnp.maximum(i - N_PROMPT_TILES, 0)


def _cond_row(i):
    return jnp.where(i < N_PROMPT_TILES, 0, 1 + (i - N_PROMPT_TILES) // TILES_PER_SAMPLE)


def _low_half_mask(shape):
    return lax.broadcasted_iota(jnp.int32, shape, len(shape) - 1) < HEAD_DIM


def _ada_kernel(cond_ref, w_ref, b_ref, out_ref):
    cond = cond_ref[...]
    act = (cond * jax.nn.sigmoid(cond)).astype(BF16)
    out_ref[...] = _dot(act, w_ref[...].astype(BF16)) + b_ref[...]


def _modulation(cond, w_ada, b_ada):
    tn = 1536
    return pl.pallas_call(
        _ada_kernel,
        grid=(DEPTH, 6 * D_MODEL // tn),
        in_specs=[
            pl.BlockSpec((N_COND, D_MODEL), lambda l, j: (0, 0)),
            pl.BlockSpec((None, D_MODEL, tn), lambda l, j: (l, 0, j)),
            pl.BlockSpec((None, 1, tn), lambda l, j: (l, 0, j)),
        ],
        out_specs=pl.BlockSpec((None, N_COND, tn), lambda l, j: (l, 0, j)),
        out_shape=jax.ShapeDtypeStruct((DEPTH, N_COND, 6 * D_MODEL), F32),
        compiler_params=_params(2),
        name="ada_modulation",
    )(cond, w_ada, b_ada.reshape(DEPTH, 1, 6 * D_MODEL))


def _group_rms(x, gain):
    low = _low_half_mask((1, LANES))
    cols = []
    for c in range(N_LANE_BLOCKS):
        xb = x[:, c * LANES:(c + 1) * LANES]
        sq = xb * xb
        s_lo = jnp.sum(jnp.where(low, sq, 0.0), axis=-1, keepdims=True)
        s_hi = jnp.sum(jnp.where(low, 0.0, sq), axis=-1, keepdims=True)
        ms = jnp.where(low, s_lo, s_hi) * (1.0 / HEAD_DIM)
        cols.append(xb * lax.rsqrt(ms + EPS) * gain)
    return cols


def _rope(cols, cos, sin_signed):
    lane = lax.broadcasted_iota(jnp.int32, (1, LANES), 1)
    first = (lane % 32) < 16
    out = []
    for xb in cols:
        partner = jnp.where(first, pltpu.roll(xb, LANES - 16, 1), pltpu.roll(xb, 16, 1))
        out.append(xb * cos + partner * sin_signed)
    return out


def _qkv_kernel(is_diff, is_prompt, has_prev, *refs):
    if has_prev:
        refs = refs[2:]
    if is_prompt:
        x_ref, mod_ref, g_ref, w_ref, qg_ref, kg_ref, q_ref, k_ref, v_ref, kf_ref, vf_ref = refs
        r = 0
    else:
        x_ref, mod_ref, g_ref, w_ref, qg_ref, kg_ref, cos_ref, sin_ref, q_ref, k_ref, v_ref = refs
        r = 1 + pl.program_id(0) // TILES_PER_SAMPLE
    x = x_ref[...]
    ms = jnp.mean(x * x, axis=-1, keepdims=True)
    xn = x * lax.rsqrt(ms + EPS) * g_ref[...]
    shift = mod_ref[pl.ds(r, 1), 0:D_MODEL]
    scale = mod_ref[pl.ds(r, 1), D_MODEL:2 * D_MODEL]
    h = (xn * (1.0 + scale) + shift).astype(BF16)
    qkv = _dot(h, w_ref[...])
    qn = _group_rms(qkv[:, 0:D_MODEL], qg_ref[...])
    kn = _group_rms(qkv[:, D_MODEL:2 * D_MODEL], kg_ref[...])
    v = qkv[:, 2 * D_MODEL:3 * D_MODEL]
    v_ref[...] = v.astype(BF16)
    sm_scale = SCORE_SCALE
    if is_diff and not is_prompt:
        cos = cos_ref[...]
        sin = sin_ref[...]
        qs = _rope(qn, cos, sin)
        ks = _rope(kn, cos, sin)
    else:
        qs, ks = qn, kn
    for c in range(N_LANE_BLOCKS):
        sl = slice(c * LANES, (c + 1) * LANES)
        q_ref[:, sl] = (qs[c] * sm_scale).astype(BF16)
        k_ref[:, sl] = ks[c].astype(BF16)
        if not is_prompt:
            continue
        for bb in range(REQ_PER_TILE):
            rows = slice(bb * SEQ, (bb + 1) * SEQ)
            if is_diff:
                dst = pl.ds(c, SEQ, stride=N_LANE_BLOCKS)
                kf_ref[bb, dst, :] = kn[c][rows, :]
                vf_ref[bb, dst, :] = v[rows, sl]
            else:
                kf_ref[bb, sl, :] = kn[c][rows, :].T
                vf_ref[bb, sl, :] = v[rows, sl].T


def _qkv(x, first_tile, is_prompt, mod, layer, g_mix, w_qkv, kind_layer, q_gain, k_gain, rope_tabs,
         is_diff, prev, cast_job):
    n_tiles = N_PROMPT_TILES if is_prompt else N_ROW_TILES - N_PROMPT_TILES
    row_spec = pl.BlockSpec((ROW_TILE, D_MODEL), lambda i: (i, 0))
    in_specs = [
        pl.BlockSpec((ROW_TILE, D_MODEL), lambda i: (first_tile + i, 0)),
        _layer_slice((N_COND, 6 * D_MODEL), layer),
        _layer_slice((1, D_MODEL), layer),
        _resident((D_MODEL, 3 * D_MODEL)),
        _resident((1, LANES)),
        _resident((1, LANES)),
    ]
    args = [x, mod, g_mix, w_qkv, q_gain, k_gain]
    out_specs = [row_spec] * 3
    out_shape = [jax.ShapeDtypeStruct((n_tiles * ROW_TILE, D_MODEL), BF16)] * 3
    aliases = {}
    if is_prompt:
        if is_diff:
            cache_rows, cache_cols = SEQ * N_LANE_BLOCKS, LANES
        else:
            cache_rows, cache_cols = D_MODEL, SEQ
        cache_spec = pl.BlockSpec((REQ_PER_TILE, None, cache_rows, cache_cols),
                                  lambda i: (i, kind_layer, 0, 0))
        out_specs += [cache_spec] * 2
        out_shape += [jax.ShapeDtypeStruct((BATCH, DEPTH // 2, cache_rows, cache_cols), F32)] * 2
        if prev is not None:
            in_specs = [pl.BlockSpec(memory_space=pl.ANY)] * 2 + in_specs
            args = list(prev) + args
            aliases = {0: 3, 1: 4}
    else:
        rope_spec = pl.BlockSpec((ROW_TILE, LANES), lambda i: (i % TILES_PER_SAMPLE, 0))
        in_specs += [rope_spec] * 2
        args += list(rope_tabs)
    return _pallas_call_with_cast(
        functools.partial(_qkv_kernel, is_diff, is_prompt, is_prompt and prev is not None), cast_job,
        grid=(n_tiles,),
        in_specs=in_specs,
        out_specs=out_specs,
        out_shape=out_shape,
        args=args,
        input_output_aliases=aliases,
        name="qkv_prompt" if is_prompt else "qkv_latent",
    )


def _softmax_pv(score_parts, value_parts):
    m = None
    for s in score_parts:
        pm = jnp.max(s, axis=-1, keepdims=True)
        m = pm if m is None else jnp.maximum(m, pm)
    acc = None
    for s, times_v in zip(score_parts, value_parts):
        pv = times_v(jnp.exp2(s - m).astype(BF16))
        acc = pv if acc is None else acc + pv
    return acc[:, :LANES] * (1.0 / acc[:, LANES:])


def _times(v):
    v1 = jnp.concatenate([v, jnp.ones_like(v)], axis=1)
    return lambda w: _dot(w, v1)


def _times_t(v_t):
    v1_t = jnp.concatenate([v_t, jnp.ones_like(v_t)], axis=0)
    return lambda w: _dot_nt(w, v1_t)


def _diff_lambda(lq1_ref, lk1_ref, lq2_ref, lk2_ref, lambda_init):
    e1 = jnp.exp(jnp.sum(lq1_ref[...] * lk1_ref[...], axis=-1, keepdims=True))
    e2 = jnp.exp(jnp.sum(lq2_ref[...] * lk2_ref[...], axis=-1, keepdims=True))
    return e1 - e2 + lambda_init


def _attend_pairs(n_blocks, load_q, scores, values, combine, store):
    units = [(b, m) for b in range(n_blocks) for m in range(2)]

    def unit_scores(u):
        b, m = units[u]
        q = load_q(b)
        low = _low_half_mask(q.shape)
        zero = jnp.zeros_like(q)
        return scores(b, m, jnp.where(low, q, zero) if m == 0 else jnp.where(low, zero, q))

    s = unit_scores(0)
    first = None
    for u, (b, m) in enumerate(units):
        s_next = unit_scores(u + 1) if u + 1 < len(units) else None
        o = _softmax_pv(s, values(b))
        s = s_next
        if m == 0:
            first = o
        else:
            store(b, combine(first, o))


def _diff_combine(lam, gsub, lambda_init):
    def combine(o1, o2):
        o = o1 - lam * o2
        ms = jnp.mean(o * o, axis=-1, keepdims=True)
        return o * lax.rsqrt(ms + EPS) * gsub * (1.0 - lambda_init)
    return combine


def _head_pair_combine(o_lo, o_hi):
    return jnp.where(_low_half_mask(o_lo.shape), o_lo, o_hi)


def _ctx_attn_kernel(is_diff, lambda_init, q_ref, k_ref, v_ref, lq1_ref, lk1_ref, lq2_ref, lk2_ref,
                     gsub_ref, o_ref):
    if is_diff:
        lam = _diff_lambda(lq1_ref, lk1_ref, lq2_ref, lk2_ref, lambda_init)
        combine = _diff_combine(lam, gsub_ref[...], lambda_init)
    else:
        combine = _head_pair_combine

    def cols(h):
        return slice(h * LANES, (h + 1) * LANES)

    def store(h, o):
        o_ref[:, cols(h)] = o.astype(BF16)

    _attend_pairs(N_LANE_BLOCKS,
                  lambda h: q_ref[:, cols(h)],
                  lambda h, m, qm: [_dot_nt(qm, k_ref[:, cols(h)])],
                  lambda h: [_times(v_ref[:, cols(h)])],
                  combine, store)


def _ctx_attention(q, k, v, lam_params, gsub, is_diff, lambda_init, cast_job):
    blk = pl.BlockSpec((SEQ, D_MODEL), lambda b: (b, 0))
    small = [_resident((1, HEAD_DIM))] * 4 + [_resident((1, LANES))]
    return _pallas_call_with_cast(
        functools.partial(_ctx_attn_kernel, is_diff, lambda_init), cast_job,
        grid=(BATCH,),
        in_specs=[blk, blk, blk] + small,
        out_specs=[blk],
        out_shape=[jax.ShapeDtypeStruct((N_PROMPT_ROWS, D_MODEL), BF16)],
        args=[q, k, v, *lam_params, gsub],
        name="ctx_attention",
    )


Q_CHUNK = 512


def _diff_latent_kernel(lambda_init, q_ref, k_ref, v_ref, ck_ref, cv_ref,
                        lq1_ref, lk1_ref, lq2_ref, lk2_ref, gsub_ref, o_ref):
    lam = _diff_lambda(lq1_ref, lk1_ref, lq2_ref, lk2_ref, lambda_init)
    k = k_ref[...]
    v = v_ref[...]
    head_rows = pl.ds(pl.program_id(1), PAST_LEN, stride=N_LANE_BLOCKS)
    ck = ck_ref[head_rows, :].astype(BF16)
    cv = cv_ref[head_rows, :].astype(BF16)
    value_parts = [_times(v), _times(cv)]

    def rows(c):
        return slice(c * Q_CHUNK, (c + 1) * Q_CHUNK)

    def store(c, o):
        o_ref[rows(c), :] = o.astype(BF16)

    _attend_pairs(DEC_SEQ // Q_CHUNK,
                  lambda c: q_ref[rows(c), :],
                  lambda c, m, qm: [_dot_nt(qm, k), _dot_nt(qm, ck)],
                  lambda c: value_parts,
                  _diff_combine(lam, gsub_ref[...], lambda_init), store)


def _diff_latent_attention(q, k, v, cache_k, cache_v, layer_i, lam_params, gsub, lambda_init):
    blk = pl.BlockSpec((DEC_SEQ, LANES), lambda b, h: (b, h))
    cache = pl.BlockSpec((None, None, PAST_LEN * N_LANE_BLOCKS, LANES), lambda b, h: (b, layer_i, 0, 0))
    small = [_resident((1, HEAD_DIM))] * 4 + [_resident((1, LANES))]
    return pl.pallas_call(
        functools.partial(_diff_latent_kernel, lambda_init),
        grid=(DEC_BATCH, N_LANE_BLOCKS),
        in_specs=[blk, blk, blk, cache, cache] + small,
        out_specs=blk,
        out_shape=jax.ShapeDtypeStruct((N_SAMPLE_ROWS, D_MODEL), BF16),
        compiler_params=_params(2),
        name="diff_latent_attention",
    )(q, k, v, cache_k, cache_v, *lam_params, gsub)


NA_Q_ROWS = 8
NA_K_ROWS = 12
NA_Q_TOK = NA_Q_ROWS * GRID_W
NA_K_TOK = NA_K_ROWS * GRID_W
NA_GROUPS = GRID_ROWS // NA_Q_ROWS
NA_K_STEP = (GRID_ROWS - NA_K_ROWS) // (NA_GROUPS - 1)


def _na_latent_kernel(q_ref, k_ref, v_ref, ckt_ref, cvt_ref, bias_ref, o_ref):
    ck_t = ckt_ref[...].astype(BF16)
    cv_t = cvt_ref[...].astype(BF16)

    def qrows(g):
        return slice(g * NA_Q_TOK, (g + 1) * NA_Q_TOK)

    def krows(g):
        return slice(g * NA_K_STEP * GRID_W, g * NA_K_STEP * GRID_W + NA_K_TOK)

    def store(g, o):
        o_ref[qrows(g), :] = o.astype(BF16)

    _attend_pairs(NA_GROUPS,
                  lambda g: q_ref[qrows(g), :],
                  lambda g, half, qm: [_dot_nt(qm, k_ref[krows(g), :]) + bias_ref[half, g], _dot(qm, ck_t)],
                  lambda g: [_times(v_ref[krows(g), :]), _times_t(cv_t)],
                  _head_pair_combine, store)


def _na_latent_attention(q, k, v, cache_k_t, cache_v_t, layer_i, bias):
    blk = pl.BlockSpec((DEC_SEQ, LANES), lambda p, b: (b, p))
    cache = pl.BlockSpec((None, None, LANES, PAST_LEN), lambda p, b: (b, layer_i, p, 0))
    bias_spec = pl.BlockSpec((None, None, 2, NA_GROUPS, NA_Q_TOK, NA_K_TOK),
                             lambda p, b: (layer_i, p, 0, 0, 0, 0))
    return pl.pallas_call(
        _na_latent_kernel,
        grid=(N_LANE_BLOCKS, DEC_BATCH),
        in_specs=[blk, blk, blk, cache, cache, bias_spec],
        out_specs=blk,
        out_shape=jax.ShapeDtypeStruct((N_SAMPLE_ROWS, D_MODEL), BF16),
        compiler_params=_params(2),
        name="na_latent_attention",
    )(q, k, v, cache_k_t, cache_v_t, bias)


N_DR = 2 * NA_KH - 1
N_DC = 2 * NA_KW - 1


def _na_bias_kernel(rb_ref, out_ref, tile_ref):
    layer = pl.program_id(0)
    h = pl.program_id(1)
    qc = lax.broadcasted_iota(jnp.int32, (GRID_W, LANES), 0)
    lane = lax.broadcasted_iota(jnp.int32, (GRID_W, LANES), 1)
    kc = lane % GRID_W
    dc = jnp.clip(kc - qc, -(NA_KW - 1), NA_KW - 1) + (NA_KW - 1)
    cs = jnp.clip(qc - NA_KW // 2, 0, GRID_W - NA_KW)
    col_ok = (kc >= cs) & (kc < cs + NA_KW)
    neg = jnp.full((GRID_W, LANES), NEG_INF, F32)
    for dr in range(N_DR):
        acc = jnp.zeros((GRID_W, LANES), F32)
        for j in range(N_DC):
            acc = jnp.where(dc == j, rb_ref[layer, h, dr * N_DC + j], acc)
        tile_ref[dr] = jnp.where(col_ok, acc * LOG2_E, neg)
    low = lane < GRID_W
    for g in range(NA_GROUPS):
        for rq in range(NA_Q_ROWS):
            r = g * NA_Q_ROWS + rq
            rs = min(max(r - NA_KH // 2, 0), GRID_ROWS - NA_KH)

            def tile(kr, r=r, rs=rs):
                if rs <= kr < rs + NA_KH:
                    return tile_ref[kr - r + NA_KH - 1]
                return neg

            for p in range(NA_K_ROWS // 2):
                kr = g * NA_K_STEP + 2 * p
                out_ref[g, rq * GRID_W:(rq + 1) * GRID_W, p * LANES:(p + 1) * LANES] = (
                    jnp.where(low, tile(kr), tile(kr + 1)))


def _na_bias(rel_bias_na):
    n_layers = rel_bias_na.shape[0]
    rb = rel_bias_na.reshape(n_layers, NA_HEADS, N_DR * N_DC)
    out = pl.pallas_call(
        _na_bias_kernel,
        grid=(n_layers, NA_HEADS),
        in_specs=[pl.BlockSpec(memory_space=pltpu.SMEM)],
        out_specs=pl.BlockSpec((None, None, NA_GROUPS, NA_Q_TOK, NA_K_TOK), lambda l, h: (l, h, 0, 0, 0)),
        out_shape=jax.ShapeDtypeStruct((n_layers, NA_HEADS, NA_GROUPS, NA_Q_TOK, NA_K_TOK), F32),
        scratch_shapes=[pltpu.VMEM((N_DR, GRID_W, LANES), F32)],
        compiler_params=_params(2),
        name="na_bias_expand",
    )(rb)
    return out.reshape(n_layers, NA_HEADS // 2, 2, NA_GROUPS, NA_Q_TOK, NA_K_TOK)


FF_CHUNK = 1024


def _post_kernel(split_in, split_out, *refs):
    n_x = 2 if split_in else 1
    x_refs = refs[:n_x]
    op_ref, os_ref, mod_ref, g_ref, wo_ref, w1_ref, w2_ref = refs[n_x:n_x + 7]
    y_refs = refs[n_x + 7:]
    i = pl.program_id(0)
    r = _cond_row(i)
    is_prompt = i < N_PROMPT_TILES

    def mod(j):
        return mod_ref[pl.ds(r, 1), j * D_MODEL:(j + 1) * D_MODEL]

    x = jnp.where(is_prompt, x_refs[0][...], x_refs[1][...]) if split_in else x_refs[0][...]
    o = jnp.where(is_prompt, op_ref[...], os_ref[...])
    y1 = x + mod(2) * _dot(o, wo_ref[...])
    ms = jnp.mean(y1 * y1, axis=-1, keepdims=True)
    h = (y1 * lax.rsqrt(ms + EPS) * g_ref[...] * (1.0 + mod(4)) + mod(3)).astype(BF16)
    acc = jnp.zeros((ROW_TILE, D_MODEL), F32)
    for c in range(D_FF // FF_CHUNK):
        cols = slice(c * FF_CHUNK, (c + 1) * FF_CHUNK)
        hid = jnp.maximum(_dot(h, w1_ref[:, cols]), 0.0)
        acc = acc + _dot((hid * hid).astype(BF16), w2_ref[cols, :])
    y = y1 + mod(5) * acc
    if split_out:
        @pl.when(is_prompt)
        def _():
            y_refs[0][...] = y

        @pl.when(jnp.logical_not(is_prompt))
        def _():
            y_refs[1][...] = y
    else:
        y_refs[0][...] = y


def _post(xs, o_p, o_s, mod, layer, g_mlp, w_o, w_fc1, w_fc2, split_out, cast_job):
    prompt_spec = pl.BlockSpec((ROW_TILE, D_MODEL), lambda i: (_prompt_tile(i), 0))
    sample_spec = pl.BlockSpec((ROW_TILE, D_MODEL), lambda i: (_sample_tile(i), 0))
    row_spec = pl.BlockSpec((ROW_TILE, D_MODEL), lambda i: (i, 0))
    split_in = len(xs) == 2
    if split_out:
        out_specs = [prompt_spec, sample_spec]
        out_shape = [jax.ShapeDtypeStruct((N_PROMPT_ROWS, D_MODEL), F32),
                     jax.ShapeDtypeStruct((N_SAMPLE_ROWS, D_MODEL), F32)]
    else:
        out_specs = [row_spec]
        out_shape = [jax.ShapeDtypeStruct((N_ROWS, D_MODEL), F32)]
    return _pallas_call_with_cast(
        functools.partial(_post_kernel, split_in, split_out), cast_job,
        grid=(N_ROW_TILES,),
        in_specs=([prompt_spec, sample_spec] if split_in else [row_spec]) + [
            prompt_spec, sample_spec,
            _layer_slice((N_COND, 6 * D_MODEL), layer),
            _layer_slice((1, D_MODEL), layer),
            _resident((D_MODEL, D_MODEL)),
            _resident((D_MODEL, D_FF)),
            _resident((D_FF, D_MODEL)),
        ],
        out_specs=out_specs,
        out_shape=out_shape,
        args=[*xs, o_p, o_s, mod, g_mlp, w_o, w_fc1, w_fc2],
        name="wo_mlp",
    )


def _rope_tables():
    n_freq = HEAD_DIM // 4
    freqs = ROPE_BASE ** (-jnp.arange(n_freq, dtype=F32) / n_freq)
    t = jnp.arange(DEC_SEQ)
    ang_row = (t // GRID_W).astype(F32)[:, None] * freqs
    ang_col = (t % GRID_W).astype(F32)[:, None] * freqs
    cos64 = jnp.concatenate([jnp.cos(ang_row)] * 2 + [jnp.cos(ang_col)] * 2, axis=-1)
    sin64 = jnp.concatenate([-jnp.sin(ang_row), jnp.sin(ang_row),
                             -jnp.sin(ang_col), jnp.sin(ang_col)], axis=-1)
    return jnp.tile(cos64, (1, 2)), jnp.tile(sin64, (1, 2))


def _pair_gain(g):
    return jnp.tile(g.reshape(1, HEAD_DIM), (1, LANES // HEAD_DIM))


def _head_major_cache(cache):
    n_req, n_layers = cache.shape[:2]
    return jnp.transpose(cache, (0, 1, 3, 4, 2)).reshape(n_req, n_layers, D_MODEL, PAST_LEN)


def kernel(x_prompt, x_sample, cache_diff_k, cache_diff_v, cache_na_k, cache_na_v, c, c_ctx, w_ada, b_ada, norm_mix_g, norm_mlp_g, w_fc1, w_fc2, w_qkv_diff, w_o_diff, q_norm_diff_g, k_norm_diff_g, lambda_q1, lambda_k1, lambda_q2, lambda_k2, subln_g, w_qkv_na, w_o_na, q_norm_na_g, k_norm_na_g, rel_bias_na):
    cond = jnp.concatenate([c_ctx[None, :], c, jnp.zeros((N_COND - 1 - DEC_BATCH, D_MODEL), F32)], axis=0)
    mod = _modulation(cond, w_ada, b_ada)
    cos_tab, sin_tab = _rope_tables()
    na_bias = _na_bias(rel_bias_na)
    cdk = cache_diff_k.reshape(DEC_BATCH, DEPTH // 2, PAST_LEN * N_LANE_BLOCKS, LANES)
    cdv = cache_diff_v.reshape(DEC_BATCH, DEPTH // 2, PAST_LEN * N_LANE_BLOCKS, LANES)
    cnk_t = _head_major_cache(cache_na_k)
    cnv_t = _head_major_cache(cache_na_v)
    g_mix = norm_mix_g.reshape(DEPTH, 1, D_MODEL)
    g_mlp = norm_mlp_g.reshape(DEPTH, 1, D_MODEL)
    w_qkv_f32 = (w_qkv_diff, w_qkv_na)
    w_o_f32 = (w_o_diff, w_o_na)
    q_gain = (q_norm_diff_g, q_norm_na_g)
    k_gain = (k_norm_diff_g, k_norm_na_g)

    xs = [x_prompt.reshape(N_PROMPT_ROWS, D_MODEL), x_sample.reshape(N_SAMPLE_ROWS, D_MODEL)]
    new_kv = [None, None]
    w_qkv = w_qkv_diff[0].astype(BF16)
    for l in range(DEPTH):
        i = l // 2
        kind = l % 2
        is_diff = kind == 0
        x_prompt_src, x_sample_src = (xs[0], xs[1]) if len(xs) == 2 else (xs[0], xs[0])
        sample_first_tile = 0 if len(xs) == 2 else N_PROMPT_TILES
        gains = (_pair_gain(q_gain[kind][i]), _pair_gain(k_gain[kind][i]))
        qp, kp, vp, new_k, new_v, w1 = _qkv(x_prompt_src, 0, True, mod, l, g_mix, w_qkv, i, *gains,
                                            None, is_diff, new_kv[kind], _CastJob(w_fc1, l, 1))
        new_kv[kind] = (new_k, new_v)
        qs, ks, vs, w2 = _qkv(x_sample_src, sample_first_tile, False, mod, l, g_mix, w_qkv, i, *gains,
                              (cos_tab, sin_tab), is_diff, None, _CastJob(w_fc2, l, 0))
        lambda_init = 0.8 - 0.6 * math.exp(-0.3 * l)
        lam_params = [p[i if is_diff else 0].reshape(1, HEAD_DIM)
                      for p in (lambda_q1, lambda_k1, lambda_q2, lambda_k2)]
        gsub = subln_g[i if is_diff else 0].reshape(1, LANES)
        o_p, w_o = _ctx_attention(qp, kp, vp, lam_params, gsub, is_diff, lambda_init,
                                  _CastJob(w_o_f32[kind], i, 0))
        if is_diff:
            o_s = _diff_latent_attention(qs, ks, vs, cdk, cdv, i, lam_params, gsub, lambda_init)
        else:
            o_s = _na_latent_attention(qs, ks, vs, cnk_t, cnv_t, i, na_bias)
        last = l == DEPTH - 1
        next_qkv = None if last else _CastJob(w_qkv_f32[(l + 1) % 2], (l + 1) // 2, 0)
        outs = list(_post(xs, o_p, o_s, mod, l, g_mlp, w_o, w1, w2, last, next_qkv))
        if last:
            xp, xs = outs
        else:
            *xs, w_qkv = outs

    n_kind = DEPTH // 2
    diff_k, diff_v = (a.reshape(BATCH, n_kind, SEQ, N_LANE_BLOCKS, LANES) for a in new_kv[0])
    na_k, na_v = (jnp.transpose(a.reshape(BATCH, n_kind, NA_HEADS, HEAD_DIM, SEQ), (0, 1, 4, 2, 3))
                  for a in new_kv[1])
    return (xp.reshape(BATCH, SEQ, D_MODEL), xs.reshape(DEC_BATCH, DEC_SEQ, D_MODEL),
            diff_k, diff_v, na_k, na_v)
```

```python
import functools
import math
from typing import NamedTuple

import jax
import jax.numpy as jnp
from jax import lax
from jax.experimental import pallas as pl
from jax.experimental.pallas import tpu as pltpu

D_MODEL = 1024
BATCH = 16
SEQ = 256
DEPTH = 4
DEC_BATCH = 4
DEC_SEQ = 1024
PAST_LEN = 256
GRID_W = 64
GRID_ROWS = DEC_SEQ // GRID_W
HEAD_DIM = 64
D_FF = 4 * D_MODEL
NA_HEADS = 16
NA_KH = 8
NA_KW = 16
ROPE_BASE = 10000.0
EPS = 1e-6
NEG_INF = -1e30
LOG2_E = math.log2(math.e)
SCORE_SCALE = HEAD_DIM ** -0.5 * LOG2_E

LANES = 128
SUBLANES = 8
N_LANE_BLOCKS = D_MODEL // LANES
N_PROMPT_ROWS = BATCH * SEQ
N_SAMPLE_ROWS = DEC_BATCH * DEC_SEQ
N_ROWS = N_PROMPT_ROWS + N_SAMPLE_ROWS
ROW_TILE = 512
N_ROW_TILES = N_ROWS // ROW_TILE
N_PROMPT_TILES = N_PROMPT_ROWS // ROW_TILE
TILES_PER_SAMPLE = DEC_SEQ // ROW_TILE
REQ_PER_TILE = ROW_TILE // SEQ
N_COND = 8
VMEM_LIMIT = 56 * 1024 * 1024

F32 = jnp.float32
BF16 = jnp.bfloat16
NT_DIMS = (((1,), (1,)), ((), ()))


def _dot(a, b):
    return jnp.dot(a, b, preferred_element_type=F32)


def _dot_nt(a, b):
    return lax.dot_general(a, b, NT_DIMS, preferred_element_type=F32)


def _params(n_axes):
    return pltpu.CompilerParams(dimension_semantics=("arbitrary",) * n_axes,
                                vmem_limit_bytes=VMEM_LIMIT)


def _resident(shape):
    return pl.BlockSpec(shape, lambda *_: (0,) * len(shape), pipeline_mode=pl.Buffered(1))


def _layer_slice(shape, layer):
    return pl.BlockSpec((None,) + shape, lambda *_: (layer,) + (0,) * len(shape),
                        pipeline_mode=pl.Buffered(1))


class _CastJob(NamedTuple):
    weights: jax.Array
    layer: int
    split_axis: int


def _pallas_call_with_casts(kernel_fn, jobs, *, grid, in_specs, out_specs, out_shape, args, name,
                            input_output_aliases=None):
    n_in, n_out, n_jobs = len(in_specs), len(out_specs), len(jobs)
    (n_steps,) = grid
    src_specs, dst_specs, dst_shapes = [], [], []
    for job in jobs:
        _, rows, cols = job.weights.shape
        if job.split_axis == 0:
            band = (rows // n_steps, cols)
            src_specs.append(pl.BlockSpec((None,) + band, lambda i, layer=job.layer: (layer, i, 0)))
            dst_specs.append(pl.BlockSpec(band, lambda i: (i, 0)))
        else:
            band = (rows, cols // n_steps)
            src_specs.append(pl.BlockSpec((None,) + band, lambda i, layer=job.layer: (layer, 0, i)))
            dst_specs.append(pl.BlockSpec(band, lambda i: (0, i)))
        dst_shapes.append(jax.ShapeDtypeStruct((rows, cols), BF16))

    def kernel_and_casts(*refs):
        ins, refs = refs[:n_in], refs[n_in:]
        srcs, refs = refs[:n_jobs], refs[n_jobs:]
        outs, refs = refs[:n_out], refs[n_out:]
        dsts, scratch = refs[:n_jobs], refs[n_jobs:]
        for src_ref, dst_ref in zip(srcs, dsts):
            dst_ref[...] = src_ref[...].astype(BF16)
        kernel_fn(*ins, *outs, *scratch)

    return pl.pallas_call(
        kernel_and_casts,
        grid=grid,
        in_specs=list(in_specs) + src_specs,
        out_specs=list(out_specs) + dst_specs,
        out_shape=list(out_shape) + dst_shapes,
        input_output_aliases=input_output_aliases or {},
        compiler_params=_params(1),
        name=name,
    )(*args, *(job.weights for job in jobs))


def _prompt_tile(i):
    return jnp.minimum(i, N_PROMPT_TILES - 1)


def _sample_tile(i):
    return jnp.maximum(i - N_PROMPT_TILES, 0)


def _cond_row(i):
    return jnp.where(i < N_PROMPT_TILES, 0, 1 + (i - N_PROMPT_TILES) // TILES_PER_SAMPLE)


def _low_half_mask(shape):
    return lax.broadcasted_iota(jnp.int32, shape, len(shape) - 1) < HEAD_DIM


def _ada_kernel(cond_ref, w_ref, b_ref, out_ref):
    cond = cond_ref[...]
    act = (cond * jax.nn.sigmoid(cond)).astype(BF16)
    out_ref[...] = _dot(act, w_ref[...].astype(BF16)) + b_ref[...]


def _modulation(cond, w_ada, b_ada):
    tn = 1536
    return pl.pallas_call(
        _ada_kernel,
        grid=(DEPTH, 6 * D_MODEL // tn),
        in_specs=[
            pl.BlockSpec((N_COND, D_MODEL), lambda l, j: (0, 0)),
            pl.BlockSpec((None, D_MODEL, tn), lambda l, j: (l, 0, j)),
            pl.BlockSpec((None, 1, tn), lambda l, j: (l, 0, j)),
        ],
        out_specs=pl.BlockSpec((None, N_COND, tn), lambda l, j: (l, 0, j)),
        out_shape=jax.ShapeDtypeStruct((DEPTH, N_COND, 6 * D_MODEL), F32),
        compiler_params=_params(2),
        name="ada_modulation",
    )(cond, w_ada, b_ada.reshape(DEPTH, 1, 6 * D_MODEL))


def _group_rms(x, gain):
    low = _low_half_mask((1, LANES))
    cols = []
    for c in range(N_LANE_BLOCKS):
        xb = x[:, c * LANES:(c + 1) * LANES]
        sq = xb * xb
        s_lo = jnp.sum(jnp.where(low, sq, 0.0), axis=-1, keepdims=True)
        s_hi = jnp.sum(jnp.where(low, 0.0, sq), axis=-1, keepdims=True)
        ms = jnp.where(low, s_lo, s_hi) * (1.0 / HEAD_DIM)
        cols.append(xb * lax.rsqrt(ms + EPS) * gain)
    return cols


def _rope(cols, cos, sin_signed):
    lane = lax.broadcasted_iota(jnp.int32, (1, LANES), 1)
    first = (lane % 32) < 16
    out = []
    for xb in cols:
        partner = jnp.where(first, pltpu.roll(xb, LANES - 16, 1), pltpu.roll(xb, 16, 1))
        out.append(xb * cos + partner * sin_signed)
    return out


def _qkv_kernel(is_diff, is_prompt, has_prev, *refs):
    if has_prev:
        refs = refs[2:]
    if is_prompt:
        x_ref, mod_ref, g_ref, w_ref, qg_ref, kg_ref, q_ref, k_ref, v_ref, kf_ref, vf_ref = refs
        r = 0
    else:
        x_ref, mod_ref, g_ref, w_ref, qg_ref, kg_ref, cos_ref, sin_ref, q_ref, k_ref, v_ref = refs
        r = 1 + pl.program_id(0) // TILES_PER_SAMPLE
    x = x_ref[...]
    ms = jnp.mean(x * x, axis=-1, keepdims=True)
    xn = x * lax.rsqrt(ms + EPS) * g_ref[...]
    shift = mod_ref[pl.ds(r, 1), 0:D_MODEL]
    scale = mod_ref[pl.ds(r, 1), D_MODEL:2 * D_MODEL]
    h = (xn * (1.0 + scale) + shift).astype(BF16)
    qkv = _dot(h, w_ref[...])
    qn = _group_rms(qkv[:, 0:D_MODEL], qg_ref[...])
    kn = _group_rms(qkv[:, D_MODEL:2 * D_MODEL], kg_ref[...])
    v = qkv[:, 2 * D_MODEL:3 * D_MODEL]
    v_ref[...] = v.astype(BF16)
    sm_scale = SCORE_SCALE
    if is_diff and not is_prompt:
        cos = cos_ref[...]
        sin = sin_ref[...]
        qs = _rope(qn, cos, sin)
        ks = _rope(kn, cos, sin)
    else:
        qs, ks = qn, kn
    for c in range(N_LANE_BLOCKS):
        sl = slice(c * LANES, (c + 1) * LANES)
        q_ref[:, sl] = (qs[c] * sm_scale).astype(BF16)
        k_ref[:, sl] = ks[c].astype(BF16)
        if not is_prompt:
            continue
        for bb in range(REQ_PER_TILE):
            rows = slice(bb * SEQ, (bb + 1) * SEQ)
            if is_diff:
                dst = pl.ds(c, SEQ, stride=N_LANE_BLOCKS)
                kf_ref[bb, dst, :] = kn[c][rows, :]
                vf_ref[bb, dst, :] = v[rows, sl]
            else:
                kf_ref[bb, sl, :] = kn[c][rows, :].T
                vf_ref[bb, sl, :] = v[rows, sl].T


def _qkv(x, first_tile, is_prompt, mod, layer, g_mix, w_qkv, kind_layer, q_gain, k_gain, rope_tabs,
         is_diff, prev, cast_jobs):
    n_tiles = N_PROMPT_TILES if is_prompt else N_ROW_TILES - N_PROMPT_TILES
    row_spec = pl.BlockSpec((ROW_TILE, D_MODEL), lambda i: (i, 0))
    in_specs = [
        pl.BlockSpec((ROW_TILE, D_MODEL), lambda i: (first_tile + i, 0)),
        _layer_slice((N_COND, 6 * D_MODEL), layer),
        _layer_slice((1, D_MODEL), layer),
        _resident((D_MODEL, 3 * D_MODEL)),
        _resident((1, LANES)),
        _resident((1, LANES)),
    ]
    args = [x, mod, g_mix, w_qkv, q_gain, k_gain]
    out_specs = [row_spec] * 3
    out_shape = [jax.ShapeDtypeStruct((n_tiles * ROW_TILE, D_MODEL), BF16)] * 3
    aliases = {}
    if is_prompt:
        if is_diff:
            cache_rows, cache_cols = SEQ * N_LANE_BLOCKS, LANES
        else:
            cache_rows, cache_cols = D_MODEL, SEQ
        cache_spec = pl.BlockSpec((REQ_PER_TILE, None, cache_rows, cache_cols),
                                  lambda i: (i, kind_layer, 0, 0))
        out_specs += [cache_spec] * 2
        out_shape += [jax.ShapeDtypeStruct((BATCH, DEPTH // 2, cache_rows, cache_cols), F32)] * 2
        if prev is not None:
            in_specs = [pl.BlockSpec(memory_space=pl.ANY)] * 2 + in_specs
            args = list(prev) + args
            aliases = {0: 3, 1: 4}
    else:
        rope_spec = pl.BlockSpec((ROW_TILE, LANES), lambda i: (i % TILES_PER_SAMPLE, 0))
        in_specs += [rope_spec] * 2
        args += list(rope_tabs)
    return _pallas_call_with_casts(
        functools.partial(_qkv_kernel, is_diff, is_prompt, is_prompt and prev is not None), cast_jobs,
        grid=(n_tiles,),
        in_specs=in_specs,
        out_specs=out_specs,
        out_shape=out_shape,
        args=args,
        input_output_aliases=aliases,
        name="qkv_prompt" if is_prompt else "qkv_latent",
    )


def _softmax_pv(score_parts, value_parts):
    m = None
    for s in score_parts:
        pm = jnp.max(s, axis=-1, keepdims=True)
        m = pm if m is None else jnp.maximum(m, pm)
    acc = None
    for s, times_v in zip(score_parts, value_parts):
        pv = times_v(jnp.exp2(s - m).astype(BF16))
        acc = pv if acc is None else acc + pv
    return acc[:, :LANES] * (1.0 / acc[:, LANES:])


def _times(v):
    v1 = jnp.concatenate([v, jnp.ones_like(v)], axis=1)
    return lambda w: _dot(w, v1)


def _times_t(v_t):
    v1_t = jnp.concatenate([v_t, jnp.ones_like(v_t)], axis=0)
    return lambda w: _dot_nt(w, v1_t)


def _diff_lambda(lq1_ref, lk1_ref, lq2_ref, lk2_ref, lambda_init):
    e1 = jnp.exp(jnp.sum(lq1_ref[...] * lk1_ref[...], axis=-1, keepdims=True))
    e2 = jnp.exp(jnp.sum(lq2_ref[...] * lk2_ref[...], axis=-1, keepdims=True))
    return e1 - e2 + lambda_init


def _attend_pairs(n_blocks, load_q, scores, values, combine, store):
    units = [(b, m) for b in range(n_blocks) for m in range(2)]

    def unit_scores(u):
        b, m = units[u]
        q = load_q(b)
        low = _low_half_mask(q.shape)
        zero = jnp.zeros_like(q)
        return scores(b, m, jnp.where(low, q, zero) if m == 0 else jnp.where(low, zero, q))

    s = unit_scores(0)
    first = None
    for u, (b, m) in enumerate(units):
        s_next = unit_scores(u + 1) if u + 1 < len(units) else None
        o = _softmax_pv(s, values(b))
        s = s_next
        if m == 0:
            first = o
        else:
            store(b, combine(first, o))


def _diff_combine(lam, gsub, lambda_init):
    def combine(o1, o2):
        o = o1 - lam * o2
        ms = jnp.mean(o * o, axis=-1, keepdims=True)
        return o * lax.rsqrt(ms + EPS) * gsub * (1.0 - lambda_init)
    return combine


def _head_pair_combine(o_lo, o_hi):
    return jnp.where(_low_half_mask(o_lo.shape), o_lo, o_hi)


def _ctx_attn_kernel(is_diff, lambda_init, q_ref, k_ref, v_ref, lq1_ref, lk1_ref, lq2_ref, lk2_ref,
                     gsub_ref, o_ref):
    if is_diff:
        lam = _diff_lambda(lq1_ref, lk1_ref, lq2_ref, lk2_ref, lambda_init)
        combine = _diff_combine(lam, gsub_ref[...], lambda_init)
    else:
        combine = _head_pair_combine

    def cols(h):
        return slice(h * LANES, (h + 1) * LANES)

    def store(h, o):
        o_ref[:, cols(h)] = o.astype(BF16)

    _attend_pairs(N_LANE_BLOCKS,
                  lambda h: q_ref[:, cols(h)],
                  lambda h, m, qm: [_dot_nt(qm, k_ref[:, cols(h)])],
                  lambda h: [_times(v_ref[:, cols(h)])],
                  combine, store)


def _ctx_attention(q, k, v, lam_params, gsub, is_diff, lambda_init, cast_jobs):
    blk = pl.BlockSpec((SEQ, D_MODEL), lambda b: (b, 0))
    small = [_resident((1, HEAD_DIM))] * 4 + [_resident((1, LANES))]
    return _pallas_call_with_casts(
        functools.partial(_ctx_attn_kernel, is_diff, lambda_init), cast_jobs,
        grid=(BATCH,),
        in_specs=[blk, blk, blk] + small,
        out_specs=[blk],
        out_shape=[jax.ShapeDtypeStruct((N_PROMPT_ROWS, D_MODEL), BF16)],
        args=[q, k, v, *lam_params, gsub],
        name="ctx_attention",
    )


Q_CHUNK = 512


def _diff_latent_kernel(lambda_init, q_ref, k_ref, v_ref, ck_ref, cv_ref,
                        lq1_ref, lk1_ref, lq2_ref, lk2_ref, gsub_ref, o_ref):
    lam = _diff_lambda(lq1_ref, lk1_ref, lq2_ref, lk2_ref, lambda_init)
    k = k_ref[...]
    v = v_ref[...]
    head_rows = pl.ds(pl.program_id(1), PAST_LEN, stride=N_LANE_BLOCKS)
    ck = ck_ref[head_rows, :].astype(BF16)
    cv = cv_ref[head_rows, :].astype(BF16)
    value_parts = [_times(v), _times(cv)]

    def rows(c):
        return slice(c * Q_CHUNK, (c + 1) * Q_CHUNK)

    def store(c, o):
        o_ref[rows(c), :] = o.astype(BF16)

    _attend_pairs(DEC_SEQ // Q_CHUNK,
                  lambda c: q_ref[rows(c), :],
                  lambda c, m, qm: [_dot_nt(qm, k), _dot_nt(qm, ck)],
                  lambda c: value_parts,
                  _diff_combine(lam, gsub_ref[...], lambda_init), store)


def _diff_latent_attention(q, k, v, cache_k, cache_v, layer_i, lam_params, gsub, lambda_init):
    blk = pl.BlockSpec((DEC_SEQ, LANES), lambda b, h: (b, h))
    cache = pl.BlockSpec((None, None, PAST_LEN * N_LANE_BLOCKS, LANES), lambda b, h: (b, layer_i, 0, 0))
    small = [_resident((1, HEAD_DIM))] * 4 + [_resident((1, LANES))]
    return pl.pallas_call(
        functools.partial(_diff_latent_kernel, lambda_init),
        grid=(DEC_BATCH, N_LANE_BLOCKS),
        in_specs=[blk, blk, blk, cache, cache] + small,
        out_specs=blk,
        out_shape=jax.ShapeDtypeStruct((N_SAMPLE_ROWS, D_MODEL), BF16),
        compiler_params=_params(2),
        name="diff_latent_attention",
    )(q, k, v, cache_k, cache_v, *lam_params, gsub)


NA_Q_ROWS = 8
NA_K_ROWS = 12
NA_Q_TOK = NA_Q_ROWS * GRID_W
NA_K_TOK = NA_K_ROWS * GRID_W
NA_GROUPS = GRID_ROWS // NA_Q_ROWS
NA_K_STEP = (GRID_ROWS - NA_K_ROWS) // (NA_GROUPS - 1)


def _na_latent_kernel(q_ref, k_ref, v_ref, ckt_ref, cvt_ref, bias_ref, o_ref):
    ck_t = ckt_ref[...].astype(BF16)
    cv_t = cvt_ref[...].astype(BF16)

    def qrows(g):
        return slice(g * NA_Q_TOK, (g + 1) * NA_Q_TOK)

    def krows(g):
        return slice(g * NA_K_STEP * GRID_W, g * NA_K_STEP * GRID_W + NA_K_TOK)

    def store(g, o):
        o_ref[qrows(g), :] = o.astype(BF16)

    _attend_pairs(NA_GROUPS,
                  lambda g: q_ref[qrows(g), :],
                  lambda g, half, qm: [_dot_nt(qm, k_ref[krows(g), :]) + bias_ref[half, g], _dot(qm, ck_t)],
                  lambda g: [_times(v_ref[krows(g), :]), _times_t(cv_t)],
                  _head_pair_combine, store)


def _na_latent_attention(q, k, v, cache_k_t, cache_v_t, layer_i, bias):
    blk = pl.BlockSpec((DEC_SEQ, LANES), lambda p, b: (b, p))
    cache = pl.BlockSpec((None, None, LANES, PAST_LEN), lambda p, b: (b, layer_i, p, 0))
    bias_spec = pl.BlockSpec((None, None, 2, NA_GROUPS, NA_Q_TOK, NA_K_TOK),
                             lambda p, b: (layer_i, p, 0, 0, 0, 0))
    return pl.pallas_call(
        _na_latent_kernel,
        grid=(N_LANE_BLOCKS, DEC_BATCH),
        in_specs=[blk, blk, blk, cache, cache, bias_spec],
        out_specs=blk,
        out_shape=jax.ShapeDtypeStruct((N_SAMPLE_ROWS, D_MODEL), BF16),
        compiler_params=_params(2),
        name="na_latent_attention",
    )(q, k, v, cache_k_t, cache_v_t, bias)


N_DR = 2 * NA_KH - 1
N_DC = 2 * NA_KW - 1


def _na_bias_kernel(rb_ref, out_ref, tile_ref):
    layer = pl.program_id(0)
    h = pl.program_id(1)
    qc = lax.broadcasted_iota(jnp.int32, (GRID_W, LANES), 0)
    lane = lax.broadcasted_iota(jnp.int32, (GRID_W, LANES), 1)
    kc = lane % GRID_W
    dc = jnp.clip(kc - qc, -(NA_KW - 1), NA_KW - 1) + (NA_KW - 1)
    cs = jnp.clip(qc - NA_KW // 2, 0, GRID_W - NA_KW)
    col_ok = (kc >= cs) & (kc < cs + NA_KW)
    neg = jnp.full((GRID_W, LANES), NEG_INF, F32)
    for dr in range(N_DR):
        acc = jnp.zeros((GRID_W, LANES), F32)
        for j in range(N_DC):
            acc = jnp.where(dc == j, rb_ref[layer, h, dr * N_DC + j], acc)
        tile_ref[dr] = jnp.where(col_ok, acc * LOG2_E, neg)
    low = lane < GRID_W
    for g in range(NA_GROUPS):
        for rq in range(NA_Q_ROWS):
            r = g * NA_Q_ROWS + rq
            rs = min(max(r - NA_KH // 2, 0), GRID_ROWS - NA_KH)

            def tile(kr, r=r, rs=rs):
                if rs <= kr < rs + NA_KH:
                    return tile_ref[kr - r + NA_KH - 1]
                return neg

            for p in range(NA_K_ROWS // 2):
                kr = g * NA_K_STEP + 2 * p
                out_ref[g, rq * GRID_W:(rq + 1) * GRID_W, p * LANES:(p + 1) * LANES] = (
                    jnp.where(low, tile(kr), tile(kr + 1)))


def _na_bias(rel_bias_na):
    n_layers = rel_bias_na.shape[0]
    rb = rel_bias_na.reshape(n_layers, NA_HEADS, N_DR * N_DC)
    out = pl.pallas_call(
        _na_bias_kernel,
        grid=(n_layers, NA_HEADS),
        in_specs=[pl.BlockSpec(memory_space=pltpu.SMEM)],
        out_specs=pl.BlockSpec((None, None, NA_GROUPS, NA_Q_TOK, NA_K_TOK), lambda l, h: (l, h, 0, 0, 0)),
        out_shape=jax.ShapeDtypeStruct((n_layers, NA_HEADS, NA_GROUPS, NA_Q_TOK, NA_K_TOK), F32),
        scratch_shapes=[pltpu.VMEM((N_DR, GRID_W, LANES), F32)],
        compiler_params=_params(2),
        name="na_bias_expand",
    )(rb)
    return out.reshape(n_layers, NA_HEADS // 2, 2, NA_GROUPS, NA_Q_TOK, NA_K_TOK)


FF_CHUNK = 1024


def _post_kernel(split_in, split_out, *refs):
    n_x = 2 if split_in else 1
    x_refs = refs[:n_x]
    op_ref, os_ref, mod_ref, g_ref, wo_ref, w1_ref, w2_ref = refs[n_x:n_x + 7]
    y_refs = refs[n_x + 7:]
    i = pl.program_id(0)
    r = _cond_row(i)
    is_prompt = i < N_PROMPT_TILES

    def mod(j):
        return mod_ref[pl.ds(r, 1), j * D_MODEL:(j + 1) * D_MODEL]

    x = jnp.where(is_prompt, x_refs[0][...], x_refs[1][...]) if split_in else x_refs[0][...]
    o = jnp.where(is_prompt, op_ref[...], os_ref[...])
    y1 = x + mod(2) * _dot(o, wo_ref[...])
    ms = jnp.mean(y1 * y1, axis=-1, keepdims=True)
    h = (y1 * lax.rsqrt(ms + EPS) * g_ref[...] * (1.0 + mod(4)) + mod(3)).astype(BF16)
    acc = jnp.zeros((ROW_TILE, D_MODEL), F32)
    for c in range(D_FF // FF_CHUNK):
        cols = slice(c * FF_CHUNK, (c + 1) * FF_CHUNK)
        hid = jnp.maximum(_dot(h, w1_ref[:, cols]), 0.0)
        acc = acc + _dot((hid * hid).astype(BF16), w2_ref[cols, :])
    y = y1 + mod(5) * acc
    if split_out:
        @pl.when(is_prompt)
        def _():
            y_refs[0][...] = y

        @pl.when(jnp.logical_not(is_prompt))
        def _():
            y_refs[1][...] = y
    else:
        y_refs[0][...] = y


def _post(xs, o_p, o_s, mod, layer, g_mlp, w_o, w_fc1, w_fc2, split_out, cast_jobs):
    prompt_spec = pl.BlockSpec((ROW_TILE, D_MODEL), lambda i: (_prompt_tile(i), 0))
    sample_spec = pl.BlockSpec((ROW_TILE, D_MODEL), lambda i: (_sample_tile(i), 0))
    row_spec = pl.BlockSpec((ROW_TILE, D_MODEL), lambda i: (i, 0))
    split_in = len(xs) == 2
    if split_out:
        out_specs = [prompt_spec, sample_spec]
        out_shape = [jax.ShapeDtypeStruct((N_PROMPT_ROWS, D_MODEL), F32),
                     jax.ShapeDtypeStruct((N_SAMPLE_ROWS, D_MODEL), F32)]
    else:
        out_specs = [row_spec]
        out_shape = [jax.ShapeDtypeStruct((N_ROWS, D_MODEL), F32)]
    return _pallas_call_with_casts(
        functools.partial(_post_kernel, split_in, split_out), cast_jobs,
        grid=(N_ROW_TILES,),
        in_specs=([prompt_spec, sample_spec] if split_in else [row_spec]) + [
            prompt_spec, sample_spec,
            _layer_slice((N_COND, 6 * D_MODEL), layer),
            _layer_slice((1, D_MODEL), layer),
            _resident((D_MODEL, D_MODEL)),
            _resident((D_MODEL, D_FF)),
            _resident((D_FF, D_MODEL)),
        ],
        out_specs=out_specs,
        out_shape=out_shape,
        args=[*xs, o_p, o_s, mod, g_mlp, w_o, w_fc1, w_fc2],
        name="wo_mlp",
    )


def _rope_tables():
    n_freq = HEAD_DIM // 4
    freqs = ROPE_BASE ** (-jnp.arange(n_freq, dtype=F32) / n_freq)
    t = jnp.arange(DEC_SEQ)
    ang_row = (t // GRID_W).astype(F32)[:, None] * freqs
    ang_col = (t % GRID_W).astype(F32)[:, None] * freqs
    cos64 = jnp.concatenate([jnp.cos(ang_row)] * 2 + [jnp.cos(ang_col)] * 2, axis=-1)
    sin64 = jnp.concatenate([-jnp.sin(ang_row), jnp.sin(ang_row),
                             -jnp.sin(ang_col), jnp.sin(ang_col)], axis=-1)
    return jnp.tile(cos64, (1, 2)), jnp.tile(sin64, (1, 2))


def _pair_gain(g):
    return jnp.tile(g.reshape(1, HEAD_DIM), (1, LANES // HEAD_DIM))


def _head_major_cache(cache):
    n_req, n_layers = cache.shape[:2]
    return jnp.transpose(cache, (0, 1, 3, 4, 2)).reshape(n_req, n_layers, D_MODEL, PAST_LEN)


def kernel(x_prompt, x_sample, cache_diff_k, cache_diff_v, cache_na_k, cache_na_v, c, c_ctx, w_ada, b_ada, norm_mix_g, norm_mlp_g, w_fc1, w_fc2, w_qkv_diff, w_o_diff, q_norm_diff_g, k_norm_diff_g, lambda_q1, lambda_k1, lambda_q2, lambda_k2, subln_g, w_qkv_na, w_o_na, q_norm_na_g, k_norm_na_g, rel_bias_na):
    cond = jnp.concatenate([c_ctx[None, :], c, jnp.zeros((N_COND - 1 - DEC_BATCH, D_MODEL), F32)], axis=0)
    mod = _modulation(cond, w_ada, b_ada)
    cos_tab, sin_tab = _rope_tables()
    na_bias = _na_bias(rel_bias_na)
    cdk = cache_diff_k.reshape(DEC_BATCH, DEPTH // 2, PAST_LEN * N_LANE_BLOCKS, LANES)
    cdv = cache_diff_v.reshape(DEC_BATCH, DEPTH // 2, PAST_LEN * N_LANE_BLOCKS, LANES)
    cnk_t = _head_major_cache(cache_na_k)
    cnv_t = _head_major_cache(cache_na_v)
    g_mix = norm_mix_g.reshape(DEPTH, 1, D_MODEL)
    g_mlp = norm_mlp_g.reshape(DEPTH, 1, D_MODEL)
    w_qkv_f32 = (w_qkv_diff, w_qkv_na)
    w_o_f32 = (w_o_diff, w_o_na)
    q_gain = (q_norm_diff_g, q_norm_na_g)
    k_gain = (k_norm_diff_g, k_norm_na_g)

    xs = [x_prompt.reshape(N_PROMPT_ROWS, D_MODEL), x_sample.reshape(N_SAMPLE_ROWS, D_MODEL)]
    new_kv = [None, None]
    w_qkv = w_qkv_diff[0].astype(BF16)
    w_o = w1 = w2 = None
    for l in range(DEPTH):
        i = l // 2
        kind = l % 2
        is_diff = kind == 0
        first = l == 0
        last = l == DEPTH - 1
        x_prompt_src, x_sample_src = (xs[0], xs[1]) if len(xs) == 2 else (xs[0], xs[0])
        sample_first_tile = 0 if len(xs) == 2 else N_PROMPT_TILES
        gains = (_pair_gain(q_gain[kind][i]), _pair_gain(k_gain[kind][i]))
        qp, kp, vp, new_k, new_v, *cast = _qkv(x_prompt_src, 0, True, mod, l, g_mix, w_qkv, i, *gains, None,
                                               is_diff, new_kv[kind], [_CastJob(w_fc1, l, 1)] if first else [])
        w1 = cast[0] if first else w1
        new_kv[kind] = (new_k, new_v)
        qs, ks, vs, *cast = _qkv(x_sample_src, sample_first_tile, False, mod, l, g_mix, w_qkv, i, *gains,
                                 (cos_tab, sin_tab), is_diff, None, [_CastJob(w_fc2, l, 0)] if first else [])
        w2 = cast[0] if first else w2
        lambda_init = 0.8 - 0.6 * math.exp(-0.3 * l)
        lam_params = [p[i if is_diff else 0].reshape(1, HEAD_DIM)
                      for p in (lambda_q1, lambda_k1, lambda_q2, lambda_k2)]
        gsub = subln_g[i if is_diff else 0].reshape(1, LANES)
        o_p, *cast = _ctx_attention(qp, kp, vp, lam_params, gsub, is_diff, lambda_init,
                                    [_CastJob(w_o_f32[kind], i, 0)] if first else [])
        w_o = cast[0] if first else w_o
        if is_diff:
            o_s = _diff_latent_attention(qs, ks, vs, cdk, cdv, i, lam_params, gsub, lambda_init)
        else:
            o_s = _na_latent_attention(qs, ks, vs, cnk_t, cnv_t, i, na_bias)
        next_jobs = [] if last else [_CastJob(w_qkv_f32[(l + 1) % 2], (l + 1) // 2, 0),
                                     _CastJob(w_o_f32[(l + 1) % 2], (l + 1) // 2, 0),
                                     _CastJob(w_fc1, l + 1, 1), _CastJob(w_fc2, l + 1, 0)]
        outs = list(_post(xs, o_p, o_s, mod, l, g_mlp, w_o, w1, w2, last, next_jobs))
        if last:
            xp, xs = outs
        else:
            *xs, w_qkv, w_o, w1, w2 = outs

    n_kind = DEPTH // 2
    diff_k, diff_v = (a.reshape(BATCH, n_kind, SEQ, N_LANE_BLOCKS, LANES) for a in new_kv[0])
    na_k, na_v = (jnp.transpose(a.reshape(BATCH, n_kind, NA_HEADS, HEAD_DIM, SEQ), (0, 1, 4, 2, 3))
                  for a in new_kv[1])
    return (xp.reshape(BATCH, SEQ, D_MODEL), xs.reshape(DEC_BATCH, DEC_SEQ, D_MODEL),
            diff_k, diff_v, na_k, na_v)
```

```python
import functools
import math
from typing import NamedTuple

import jax
import jax.numpy as jnp
from jax import lax
from jax.experimental import pallas as pl
from jax.experimental.pallas import tpu as pltpu

D_MODEL = 1024
BATCH = 16
SEQ = 256
DEPTH = 4
DEC_BATCH = 4
DEC_SEQ = 1024
PAST_LEN = 256
GRID_W = 64
GRID_ROWS = DEC_SEQ // GRID_W
HEAD_DIM = 64
D_FF = 4 * D_MODEL
NA_HEADS = 16
NA_KH = 8
NA_KW = 16
ROPE_BASE = 10000.0
EPS = 1e-6
NEG_INF = -1e30
LOG2_E = math.log2(math.e)
SCORE_SCALE = HEAD_DIM ** -0.5 * LOG2_E

LANES = 128
SUBLANES = 8
N_LANE_BLOCKS = D_MODEL // LANES
N_PROMPT_ROWS = BATCH * SEQ
N_SAMPLE_ROWS = DEC_BATCH * DEC_SEQ
N_ROWS = N_PROMPT_ROWS + N_SAMPLE_ROWS
ROW_TILE = 512
N_ROW_TILES = N_ROWS // ROW_TILE
N_PROMPT_TILES = N_PROMPT_ROWS // ROW_TILE
TILES_PER_SAMPLE = DEC_SEQ // ROW_TILE
REQ_PER_TILE = ROW_TILE // SEQ
N_COND = 8
VMEM_LIMIT = 56 * 1024 * 1024

F32 = jnp.float32
BF16 = jnp.bfloat16
NT_DIMS = (((1,), (1,)), ((), ()))


def _dot(a, b):
    return jnp.dot(a, b, preferred_element_type=F32)


def _dot_nt(a, b):
    return lax.dot_general(a, b, NT_DIMS, preferred_element_type=F32)


def _params(n_axes):
    return pltpu.CompilerParams(dimension_semantics=("arbitrary",) * n_axes,
                                vmem_limit_bytes=VMEM_LIMIT)


def _resident(shape):
    return pl.BlockSpec(shape, lambda *_: (0,) * len(shape), pipeline_mode=pl.Buffered(1))


def _layer_slice(shape, layer):
    return pl.BlockSpec((None,) + shape, lambda *_: (layer,) + (0,) * len(shape),
                        pipeline_mode=pl.Buffered(1))


class _CastJob(NamedTuple):
    weights: jax.Array
    layer: int
    split_axis: int


def _pallas_call_with_casts(kernel_fn, jobs, *, grid, in_specs, out_specs, out_shape, args, name,
                            input_output_aliases=None):
    n_in, n_out, n_jobs = len(in_specs), len(out_specs), len(jobs)
    (n_steps,) = grid
    src_specs, dst_specs, dst_shapes = [], [], []
    for job in jobs:
        _, rows, cols = job.weights.shape
        if job.split_axis == 0:
            band = (rows // n_steps, cols)
            src_specs.append(pl.BlockSpec((None,) + band, lambda i, layer=job.layer: (layer, i, 0)))
            dst_specs.append(pl.BlockSpec(band, lambda i: (i, 0)))
        else:
            band = (rows, cols // n_steps)
            src_specs.append(pl.BlockSpec((None,) + band, lambda i, layer=job.layer: (layer, 0, i)))
            dst_specs.append(pl.BlockSpec(band, lambda i: (0, i)))
        dst_shapes.append(jax.ShapeDtypeStruct((rows, cols), BF16))

    def kernel_and_casts(*refs):
        ins, refs = refs[:n_in], refs[n_in:]
        srcs, refs = refs[:n_jobs], refs[n_jobs:]
        outs, refs = refs[:n_out], refs[n_out:]
        dsts, scratch = refs[:n_jobs], refs[n_jobs:]
        for src_ref, dst_ref in zip(srcs, dsts):
            dst_ref[...] = src_ref[...].astype(BF16)
        kernel_fn(*ins, *outs, *scratch)

    return pl.pallas_call(
        kernel_and_casts,
        grid=grid,
        in_specs=list(in_specs) + src_specs,
        out_specs=list(out_specs) + dst_specs,
        out_shape=list(out_shape) + dst_shapes,
        input_output_aliases=input_output_aliases or {},
        compiler_params=_params(1),
        name=name,
    )(*args, *(job.weights for job in jobs))


def _prompt_tile(i):
    return jnp.minimum(i, N_PROMPT_TILES - 1)


def _sample_tile(i):
    return jnp.maximum(i - N_PROMPT_TILES, 0)


def _cond_row(i):
    return jnp.where(i < N_PROMPT_TILES, 0, 1 + (i - N_PROMPT_TILES) // TILES_PER_SAMPLE)


def _low_half_mask(shape):
    return lax.broadcasted_iota(jnp.int32, shape, len(shape) - 1) < HEAD_DIM


def _ada_kernel(cond_ref, w_ref, b_ref, out_ref):
    cond = cond_ref[...]
    act = (cond * jax.nn.sigmoid(cond)).astype(BF16)
    out_ref[...] = _dot(act, w_ref[...].astype(BF16)) + b_ref[...]


def _modulation(cond, w_ada, b_ada):
    tn = 1536
    return pl.pallas_call(
        _ada_kernel,
        grid=(DEPTH, 6 * D_MODEL // tn),
        in_specs=[
            pl.BlockSpec((N_COND, D_MODEL), lambda l, j: (0, 0)),
            pl.BlockSpec((None, D_MODEL, tn), lambda l, j: (l, 0, j)),
            pl.BlockSpec((None, 1, tn), lambda l, j: (l, 0, j)),
        ],
        out_specs=pl.BlockSpec((None, N_COND, tn), lambda l, j: (l, 0, j)),
        out_shape=jax.ShapeDtypeStruct((DEPTH, N_COND, 6 * D_MODEL), F32),
        compiler_params=_params(2),
        name="ada_modulation",
    )(cond, w_ada, b_ada.reshape(DEPTH, 1, 6 * D_MODEL))


def _group_rms(x, gain):
    low = _low_half_mask((1, LANES))
    cols = []
    for c in range(N_LANE_BLOCKS):
        xb = x[:, c * LANES:(c + 1) * LANES]
        sq = xb * xb
        s_lo = jnp.sum(jnp.where(low, sq, 0.0), axis=-1, keepdims=True)
        s_hi = jnp.sum(jnp.where(low, 0.0, sq), axis=-1, keepdims=True)
        ms = jnp.where(low, s_lo, s_hi) * (1.0 / HEAD_DIM)
        cols.append(xb * lax.rsqrt(ms + EPS) * gain)
    return cols


def _rope(cols, cos, sin_signed):
    lane = lax.broadcasted_iota(jnp.int32, (1, LANES), 1)
    first = (lane % 32) < 16
    out = []
    for xb in cols:
        partner = jnp.where(first, pltpu.roll(xb, LANES - 16, 1), pltpu.roll(xb, 16, 1))
        out.append(xb * cos + partner * sin_signed)
    return out


def _qkv_kernel(is_diff, is_prompt, has_prev, *refs):
    if has_prev:
        refs = refs[2:]
    if is_prompt:
        x_ref, mod_ref, g_ref, w_ref, qg_ref, kg_ref, q_ref, k_ref, v_ref, kf_ref, vf_ref = refs
        r = 0
    else:
        x_ref, mod_ref, g_ref, w_ref, qg_ref, kg_ref, cos_ref, sin_ref, q_ref, k_ref, v_ref = refs
        r = 1 + pl.program_id(0) // TILES_PER_SAMPLE
    x = x_ref[...]
    ms = jnp.mean(x * x, axis=-1, keepdims=True)
    xn = x * lax.rsqrt(ms + EPS) * g_ref[...]
    shift = mod_ref[pl.ds(r, 1), 0:D_MODEL]
    scale = mod_ref[pl.ds(r, 1), D_MODEL:2 * D_MODEL]
    h = (xn * (1.0 + scale) + shift).astype(BF16)
    qkv = _dot(h, w_ref[...])
    qn = _group_rms(qkv[:, 0:D_MODEL], qg_ref[...])
    kn = _group_rms(qkv[:, D_MODEL:2 * D_MODEL], kg_ref[...])
    v = qkv[:, 2 * D_MODEL:3 * D_MODEL]
    v_ref[...] = v.astype(BF16)
    sm_scale = SCORE_SCALE
    if is_diff and not is_prompt:
        cos = cos_ref[...]
        sin = sin_ref[...]
        qs = _rope(qn, cos, sin)
        ks = _rope(kn, cos, sin)
    else:
        qs, ks = qn, kn
    for c in range(N_LANE_BLOCKS):
        sl = slice(c * LANES, (c + 1) * LANES)
        q_ref[:, sl] = (qs[c] * sm_scale).astype(BF16)
        k_ref[:, sl] = ks[c].astype(BF16)
        if not is_prompt:
            continue
        for bb in range(REQ_PER_TILE):
            rows = slice(bb * SEQ, (bb + 1) * SEQ)
            if is_diff:
                dst = pl.ds(c, SEQ, stride=N_LANE_BLOCKS)
                kf_ref[bb, dst, :] = kn[c][rows, :]
                vf_ref[bb, dst, :] = v[rows, sl]
            else:
                kf_ref[bb, sl, :] = kn[c][rows, :].T
                vf_ref[bb, sl, :] = v[rows, sl].T


def _qkv(x, first_tile, is_prompt, mod, layer, g_mix, w_qkv, kind_layer, q_gain, k_gain, rope_tabs,
         is_diff, prev, cast_jobs):
    n_tiles = N_PROMPT_TILES if is_prompt else N_ROW_TILES - N_PROMPT_TILES
    row_spec = pl.BlockSpec((ROW_TILE, D_MODEL), lambda i: (i, 0))
    in_specs = [
        pl.BlockSpec((ROW_TILE, D_MODEL), lambda i: (first_tile + i, 0)),
        _layer_slice((N_COND, 6 * D_MODEL), layer),
        _layer_slice((1, D_MODEL), layer),
        _resident((D_MODEL, 3 * D_MODEL)),
        _resident((1, LANES)),
        _resident((1, LANES)),
    ]
    args = [x, mod, g_mix, w_qkv, q_gain, k_gain]
    out_specs = [row_spec] * 3
    out_shape = [jax.ShapeDtypeStruct((n_tiles * ROW_TILE, D_MODEL), BF16)] * 3
    aliases = {}
    if is_prompt:
        if is_diff:
            cache_rows, cache_cols = SEQ * N_LANE_BLOCKS, LANES
        else:
            cache_rows, cache_cols = D_MODEL, SEQ
        cache_spec = pl.BlockSpec((REQ_PER_TILE, None, cache_rows, cache_cols),
                                  lambda i: (i, kind_layer, 0, 0))
        out_specs += [cache_spec] * 2
        out_shape += [jax.ShapeDtypeStruct((BATCH, DEPTH // 2, cache_rows, cache_cols), F32)] * 2
        if prev is not None:
            in_specs = [pl.BlockSpec(memory_space=pl.ANY)] * 2 + in_specs
            args = list(prev) + args
            aliases = {0: 3, 1: 4}
    else:
        rope_spec = pl.BlockSpec((ROW_TILE, LANES), lambda i: (i % TILES_PER_SAMPLE, 0))
        in_specs += [rope_spec] * 2
        args += list(rope_tabs)
    return _pallas_call_with_casts(
        functools.partial(_qkv_kernel, is_diff, is_prompt, is_prompt and prev is not None), cast_jobs,
        grid=(n_tiles,),
        in_specs=in_specs,
        out_specs=out_specs,
        out_shape=out_shape,
        args=args,
        input_output_aliases=aliases,
        name="qkv_prompt" if is_prompt else "qkv_latent",
    )


def _softmax_pv(score_parts, value_parts):
    m = None
    for s in score_parts:
        pm = jnp.max(s, axis=-1, keepdims=True)
        m = pm if m is None else jnp.maximum(m, pm)
    acc = None
    for s, times_v in zip(score_parts, value_parts):
        pv = times_v(jnp.exp2(s - m).astype(BF16))
        acc = pv if acc is None else acc + pv
    return acc[:, :LANES] * (1.0 / acc[:, LANES:])


def _times(v):
    v1 = jnp.concatenate([v, jnp.ones_like(v)], axis=1)
    return lambda w: _dot(w, v1)


def _times_t(v_t):
    v1_t = jnp.concatenate([v_t, jnp.ones_like(v_t)], axis=0)
    return lambda w: _dot_nt(w, v1_t)


def _diff_lambda(lq1_ref, lk1_ref, lq2_ref, lk2_ref, lambda_init):
    e1 = jnp.exp(jnp.sum(lq1_ref[...] * lk1_ref[...], axis=-1, keepdims=True))
    e2 = jnp.exp(jnp.sum(lq2_ref[...] * lk2_ref[...], axis=-1, keepdims=True))
    return e1 - e2 + lambda_init


def _attend_pairs(n_blocks, load_q, scores, values, combine, store):
    units = [(b, m) for b in range(n_blocks) for m in range(2)]

    def unit_scores(u):
        b, m = units[u]
        q = load_q(b)
        low = _low_half_mask(q.shape)
        zero = jnp.zeros_like(q)
        return scores(b, m, jnp.where(low, q, zero) if m == 0 else jnp.where(low, zero, q))

    s = unit_scores(0)
    first = None
    for u, (b, m) in enumerate(units):
        s_next = unit_scores(u + 1) if u + 1 < len(units) else None
        o = _softmax_pv(s, values(b))
        s = s_next
        if m == 0:
            first = o
        else:
            store(b, combine(first, o))


def _diff_combine(lam, gsub, lambda_init):
    def combine(o1, o2):
        o = o1 - lam * o2
        ms = jnp.mean(o * o, axis=-1, keepdims=True)
        return o * lax.rsqrt(ms + EPS) * gsub * (1.0 - lambda_init)
    return combine


def _head_pair_combine(o_lo, o_hi):
    return jnp.where(_low_half_mask(o_lo.shape), o_lo, o_hi)


def _ctx_attn_kernel(is_diff, lambda_init, q_ref, k_ref, v_ref, lq1_ref, lk1_ref, lq2_ref, lk2_ref,
                     gsub_ref, o_ref):
    if is_diff:
        lam = _diff_lambda(lq1_ref, lk1_ref, lq2_ref, lk2_ref, lambda_init)
        combine = _diff_combine(lam, gsub_ref[...], lambda_init)
    else:
        combine = _head_pair_combine

    def cols(h):
        return slice(h * LANES, (h + 1) * LANES)

    def store(h, o):
        o_ref[:, cols(h)] = o.astype(BF16)

    _attend_pairs(N_LANE_BLOCKS,
                  lambda h: q_ref[:, cols(h)],
                  lambda h, m, qm: [_dot_nt(qm, k_ref[:, cols(h)])],
                  lambda h: [_times(v_ref[:, cols(h)])],
                  combine, store)


def _ctx_attention(q, k, v, lam_params, gsub, is_diff, lambda_init, cast_jobs):
    blk = pl.BlockSpec((SEQ, D_MODEL), lambda b: (b, 0))
    small = [_resident((1, HEAD_DIM))] * 4 + [_resident((1, LANES))]
    return _pallas_call_with_casts(
        functools.partial(_ctx_attn_kernel, is_diff, lambda_init), cast_jobs,
        grid=(BATCH,),
        in_specs=[blk, blk, blk] + small,
        out_specs=[blk],
        out_shape=[jax.ShapeDtypeStruct((N_PROMPT_ROWS, D_MODEL), BF16)],
        args=[q, k, v, *lam_params, gsub],
        name="ctx_attention",
    )


Q_CHUNK = 512


def _diff_latent_kernel(lambda_init, q_ref, k_ref, v_ref, ck_ref, cv_ref,
                        lq1_ref, lk1_ref, lq2_ref, lk2_ref, gsub_ref, o_ref):
    lam = _diff_lambda(lq1_ref, lk1_ref, lq2_ref, lk2_ref, lambda_init)
    k = k_ref[...]
    v = v_ref[...]
    head_rows = pl.ds(pl.program_id(1), PAST_LEN, stride=N_LANE_BLOCKS)
    ck = ck_ref[head_rows, :].astype(BF16)
    cv = cv_ref[head_rows, :].astype(BF16)
    value_parts = [_times(v), _times(cv)]

    def rows(c):
        return slice(c * Q_CHUNK, (c + 1) * Q_CHUNK)

    def store(c, o):
        o_ref[rows(c), :] = o.astype(BF16)

    _attend_pairs(DEC_SEQ // Q_CHUNK,
                  lambda c: q_ref[rows(c), :],
                  lambda c, m, qm: [_dot_nt(qm, k), _dot_nt(qm, ck)],
                  lambda c: value_parts,
                  _diff_combine(lam, gsub_ref[...], lambda_init), store)


def _diff_latent_attention(q, k, v, cache_k, cache_v, layer_i, lam_params, gsub, lambda_init):
    blk = pl.BlockSpec((DEC_SEQ, LANES), lambda b, h: (b, h))
    cache = pl.BlockSpec((None, None, PAST_LEN * N_LANE_BLOCKS, LANES), lambda b, h: (b, layer_i, 0, 0))
    small = [_resident((1, HEAD_DIM))] * 4 + [_resident((1, LANES))]
    return pl.pallas_call(
        functools.partial(_diff_latent_kernel, lambda_init),
        grid=(DEC_BATCH, N_LANE_BLOCKS),
        in_specs=[blk, blk, blk, cache, cache] + small,
        out_specs=blk,
        out_shape=jax.ShapeDtypeStruct((N_SAMPLE_ROWS, D_MODEL), BF16),
        compiler_params=_params(2),
        name="diff_latent_attention",
    )(q, k, v, cache_k, cache_v, *lam_params, gsub)


NA_Q_ROWS = 8
NA_K_ROWS = 12
NA_Q_TOK = NA_Q_ROWS * GRID_W
NA_K_TOK = NA_K_ROWS * GRID_W
NA_GROUPS = GRID_ROWS // NA_Q_ROWS
NA_K_STEP = (GRID_ROWS - NA_K_ROWS) // (NA_GROUPS - 1)


N_DR = 2 * NA_KH - 1
N_DC = 2 * NA_KW - 1
N_DR_PAD = 16


def _expand_na_bias(rb_ref, tile_ref, bias_ref):
    qc = lax.broadcasted_iota(jnp.int32, (GRID_W, LANES), 0)
    lane = lax.broadcasted_iota(jnp.int32, (GRID_W, LANES), 1)
    kc = lane % GRID_W
    cs = jnp.clip(qc - NA_KW // 2, 0, GRID_W - NA_KW)
    col_ok = (kc >= cs) & (kc < cs + NA_KW)
    low = lane < GRID_W
    neg = jnp.full((GRID_W, LANES), NEG_INF, F32)
    for head in range(2):
        for dr in range(N_DR):
            row = jnp.broadcast_to(rb_ref[head, dr:dr + 1, :], (GRID_W, LANES))
            shifted = pltpu.roll(row, LANES - (NA_KW - 1), 1, stride=1, stride_axis=0)
            tile_ref[dr] = jnp.where(col_ok, shifted * LOG2_E, neg)
        for g in range(NA_GROUPS):
            for rq in range(NA_Q_ROWS):
                r = g * NA_Q_ROWS + rq
                rs = min(max(r - NA_KH // 2, 0), GRID_ROWS - NA_KH)

                def tile(kr, r=r, rs=rs):
                    if rs <= kr < rs + NA_KH:
                        return tile_ref[kr - r + NA_KH - 1]
                    return neg

                for p in range(NA_K_ROWS // 2):
                    kr = g * NA_K_STEP + 2 * p
                    bias_ref[head, g, rq * GRID_W:(rq + 1) * GRID_W, p * LANES:(p + 1) * LANES] = (
                        jnp.where(low, tile(kr), tile(kr + 1)))


def _na_latent_kernel(q_ref, k_ref, v_ref, ckt_ref, cvt_ref, rb_ref, o_ref, tile_ref, bias_ref):
    @pl.when(pl.program_id(1) == 0)
    def _():
        _expand_na_bias(rb_ref, tile_ref, bias_ref)

    ck_t = ckt_ref[...].astype(BF16)
    cv_t = cvt_ref[...].astype(BF16)

    def qrows(g):
        return slice(g * NA_Q_TOK, (g + 1) * NA_Q_TOK)

    def krows(g):
        return slice(g * NA_K_STEP * GRID_W, g * NA_K_STEP * GRID_W + NA_K_TOK)

    def store(g, o):
        o_ref[qrows(g), :] = o.astype(BF16)

    _attend_pairs(NA_GROUPS,
                  lambda g: q_ref[qrows(g), :],
                  lambda g, half, qm: [_dot_nt(qm, k_ref[krows(g), :]) + bias_ref[half, g], _dot(qm, ck_t)],
                  lambda g: [_times(v_ref[krows(g), :]), _times_t(cv_t)],
                  _head_pair_combine, store)


def _na_latent_attention(q, k, v, cache_k_t, cache_v_t, layer_i, rel_bias_rows):
    blk = pl.BlockSpec((DEC_SEQ, LANES), lambda p, b: (b, p))
    cache = pl.BlockSpec((None, None, LANES, PAST_LEN), lambda p, b: (b, layer_i, p, 0))
    rb_spec = pl.BlockSpec((None, 2, N_DR_PAD, LANES), lambda p, b: (layer_i, p, 0, 0))
    return pl.pallas_call(
        _na_latent_kernel,
        grid=(N_LANE_BLOCKS, DEC_BATCH),
        in_specs=[blk, blk, blk, cache, cache, rb_spec],
        out_specs=blk,
        out_shape=jax.ShapeDtypeStruct((N_SAMPLE_ROWS, D_MODEL), BF16),
        scratch_shapes=[pltpu.VMEM((N_DR, GRID_W, LANES), F32),
                        pltpu.VMEM((2, NA_GROUPS, NA_Q_TOK, NA_K_TOK), F32)],
        compiler_params=_params(2),
        name="na_latent_attention",
    )(q, k, v, cache_k_t, cache_v_t, rel_bias_rows)


def _padded_rel_bias(rel_bias_na):
    half = jnp.pad(rel_bias_na, ((0, 0), (0, 0), (0, N_DR_PAD - N_DR), (0, GRID_W - N_DC)))
    return jnp.concatenate([half, half], axis=-1)


FF_CHUNK = 1024


def _post_kernel(split_in, split_out, *refs):
    n_x = 2 if split_in else 1
    x_refs = refs[:n_x]
    op_ref, os_ref, mod_ref, g_ref, wo_ref, w1_ref, w2_ref = refs[n_x:n_x + 7]
    y_refs = refs[n_x + 7:]
    i = pl.program_id(0)
    r = _cond_row(i)
    is_prompt = i < N_PROMPT_TILES

    def mod(j):
        return mod_ref[pl.ds(r, 1), j * D_MODEL:(j + 1) * D_MODEL]

    x = jnp.where(is_prompt, x_refs[0][...], x_refs[1][...]) if split_in else x_refs[0][...]
    o = jnp.where(is_prompt, op_ref[...], os_ref[...])
    y1 = x + mod(2) * _dot(o, wo_ref[...])
    ms = jnp.mean(y1 * y1, axis=-1, keepdims=True)
    h = (y1 * lax.rsqrt(ms + EPS) * g_ref[...] * (1.0 + mod(4)) + mod(3)).astype(BF16)
    acc = jnp.zeros((ROW_TILE, D_MODEL), F32)
    for c in range(D_FF // FF_CHUNK):
        cols = slice(c * FF_CHUNK, (c + 1) * FF_CHUNK)
        hid = jnp.maximum(_dot(h, w1_ref[:, cols]), 0.0)
        acc = acc + _dot((hid * hid).astype(BF16), w2_ref[cols, :])
    y = y1 + mod(5) * acc
    if split_out:
        @pl.when(is_prompt)
        def _():
            y_refs[0][...] = y

        @pl.when(jnp.logical_not(is_prompt))
        def _():
            y_refs[1][...] = y
    else:
        y_refs[0][...] = y


def _post(xs, o_p, o_s, mod, layer, g_mlp, w_o, w_fc1, w_fc2, split_out, cast_jobs):
    prompt_spec = pl.BlockSpec((ROW_TILE, D_MODEL), lambda i: (_prompt_tile(i), 0))
    sample_spec = pl.BlockSpec((ROW_TILE, D_MODEL), lambda i: (_sample_tile(i), 0))
    row_spec = pl.BlockSpec((ROW_TILE, D_MODEL), lambda i: (i, 0))
    split_in = len(xs) == 2
    if split_out:
        out_specs = [prompt_spec, sample_spec]
        out_shape = [jax.ShapeDtypeStruct((N_PROMPT_ROWS, D_MODEL), F32),
                     jax.ShapeDtypeStruct((N_SAMPLE_ROWS, D_MODEL), F32)]
    else:
        out_specs = [row_spec]
        out_shape = [jax.ShapeDtypeStruct((N_ROWS, D_MODEL), F32)]
    return _pallas_call_with_casts(
        functools.partial(_post_kernel, split_in, split_out), cast_jobs,
        grid=(N_ROW_TILES,),
        in_specs=([prompt_spec, sample_spec] if split_in else [row_spec]) + [
            prompt_spec, sample_spec,
            _layer_slice((N_COND, 6 * D_MODEL), layer),
            _layer_slice((1, D_MODEL), layer),
            _resident((D_MODEL, D_MODEL)),
            _resident((D_MODEL, D_FF)),
            _resident((D_FF, D_MODEL)),
        ],
        out_specs=out_specs,
        out_shape=out_shape,
        args=[*xs, o_p, o_s, mod, g_mlp, w_o, w_fc1, w_fc2],
        name="wo_mlp",
    )


def _rope_tables():
    n_freq = HEAD_DIM // 4
    freqs = ROPE_BASE ** (-jnp.arange(n_freq, dtype=F32) / n_freq)
    t = jnp.arange(DEC_SEQ)
    ang_row = (t // GRID_W).astype(F32)[:, None] * freqs
    ang_col = (t % GRID_W).astype(F32)[:, None] * freqs
    cos64 = jnp.concatenate([jnp.cos(ang_row)] * 2 + [jnp.cos(ang_col)] * 2, axis=-1)
    sin64 = jnp.concatenate([-jnp.sin(ang_row), jnp.sin(ang_row),
                             -jnp.sin(ang_col), jnp.sin(ang_col)], axis=-1)
    return jnp.tile(cos64, (1, 2)), jnp.tile(sin64, (1, 2))


def _pair_gain(g):
    return jnp.tile(g.reshape(1, HEAD_DIM), (1, LANES // HEAD_DIM))


def _head_major_cache(cache):
    n_req, n_layers = cache.shape[:2]
    return jnp.transpose(cache, (0, 1, 3, 4, 2)).reshape(n_req, n_layers, D_MODEL, PAST_LEN)


def kernel(x_prompt, x_sample, cache_diff_k, cache_diff_v, cache_na_k, cache_na_v, c, c_ctx, w_ada, b_ada, norm_mix_g, norm_mlp_g, w_fc1, w_fc2, w_qkv_diff, w_o_diff, q_norm_diff_g, k_norm_diff_g, lambda_q1, lambda_k1, lambda_q2, lambda_k2, subln_g, w_qkv_na, w_o_na, q_norm_na_g, k_norm_na_g, rel_bias_na):
    cond = jnp.concatenate([c_ctx[None, :], c, jnp.zeros((N_COND - 1 - DEC_BATCH, D_MODEL), F32)], axis=0)
    mod = _modulation(cond, w_ada, b_ada)
    cos_tab, sin_tab = _rope_tables()
    na_bias = _padded_rel_bias(rel_bias_na)
    cdk = cache_diff_k.reshape(DEC_BATCH, DEPTH // 2, PAST_LEN * N_LANE_BLOCKS, LANES)
    cdv = cache_diff_v.reshape(DEC_BATCH, DEPTH // 2, PAST_LEN * N_LANE_BLOCKS, LANES)
    cnk_t = _head_major_cache(cache_na_k)
    cnv_t = _head_major_cache(cache_na_v)
    g_mix = norm_mix_g.reshape(DEPTH, 1, D_MODEL)
    g_mlp = norm_mlp_g.reshape(DEPTH, 1, D_MODEL)
    w_qkv_f32 = (w_qkv_diff, w_qkv_na)
    w_o_f32 = (w_o_diff, w_o_na)
    q_gain = (q_norm_diff_g, q_norm_na_g)
    k_gain = (k_norm_diff_g, k_norm_na_g)

    xs = [x_prompt.reshape(N_PROMPT_ROWS, D_MODEL), x_sample.reshape(N_SAMPLE_ROWS, D_MODEL)]
    new_kv = [None, None]
    w_qkv = w_qkv_diff[0].astype(BF16)
    w_o = w1 = w2 = None
    for l in range(DEPTH):
        i = l // 2
        kind = l % 2
        is_diff = kind == 0
        first = l == 0
        last = l == DEPTH - 1
        x_prompt_src, x_sample_src = (xs[0], xs[1]) if len(xs) == 2 else (xs[0], xs[0])
        sample_first_tile = 0 if len(xs) == 2 else N_PROMPT_TILES
        gains = (_pair_gain(q_gain[kind][i]), _pair_gain(k_gain[kind][i]))
        qp, kp, vp, new_k, new_v, *cast = _qkv(x_prompt_src, 0, True, mod, l, g_mix, w_qkv, i, *gains, None,
                                               is_diff, new_kv[kind], [_CastJob(w_fc1, l, 1)] if first else [])
        w1 = cast[0] if first else w1
        new_kv[kind] = (new_k, new_v)
        qs, ks, vs, *cast = _qkv(x_sample_src, sample_first_tile, False, mod, l, g_mix, w_qkv, i, *gains,
                                 (cos_tab, sin_tab), is_diff, None, [_CastJob(w_fc2, l, 0)] if first else [])
        w2 = cast[0] if first else w2
        lambda_init = 0.8 - 0.6 * math.exp(-0.3 * l)
        lam_params = [p[i if is_diff else 0].reshape(1, HEAD_DIM)
                      for p in (lambda_q1, lambda_k1, lambda_q2, lambda_k2)]
        gsub = subln_g[i if is_diff else 0].reshape(1, LANES)
        o_p, *cast = _ctx_attention(qp, kp, vp, lam_params, gsub, is_diff, lambda_init,
                                    [_CastJob(w_o_f32[kind], i, 0)] if first else [])
        w_o = cast[0] if first else w_o
        if is_diff:
            o_s = _diff_latent_attention(qs, ks, vs, cdk, cdv, i, lam_params, gsub, lambda_init)
        else:
            o_s = _na_latent_attention(qs, ks, vs, cnk_t, cnv_t, i, na_bias)
        next_jobs = [] if last else [_CastJob(w_qkv_f32[(l + 1) % 2], (l + 1) // 2, 0),
                                     _CastJob(w_o_f32[(l + 1) % 2], (l + 1) // 2, 0),
                                     _CastJob(w_fc1, l + 1, 1), _CastJob(w_fc2, l + 1, 0)]
        outs = list(_post(xs, o_p, o_s, mod, l, g_mlp, w_o, w1, w2, last, next_jobs))
        if last:
            xp, xs = outs
        else:
            *xs, w_qkv, w_o, w1, w2 = outs

    n_kind = DEPTH // 2
    diff_k, diff_v = (a.reshape(BATCH, n_kind, SEQ, N_LANE_BLOCKS, LANES) for a in new_kv[0])
    na_k, na_v = (jnp.transpose(a.reshape(BATCH, n_kind, NA_HEADS, HEAD_DIM, SEQ), (0, 1, 4, 2, 3))
                  for a in new_kv[1])
    return (xp.reshape(BATCH, SEQ, D_MODEL), xs.reshape(DEC_BATCH, DEC_SEQ, D_MODEL),
            diff_k, diff_v, na_k, na_v)
```

```python
import functools
import math
from typing import NamedTuple

import jax
import jax.numpy as jnp
from jax import lax
from jax.experimental import pallas as pl
from jax.experimental.pallas import tpu as pltpu

D_MODEL = 1024
BATCH = 16
SEQ = 256
DEPTH = 4
DEC_BATCH = 4
DEC_SEQ = 1024
PAST_LEN = 256
GRID_W = 64
GRID_ROWS = DEC_SEQ // GRID_W
HEAD_DIM = 64
D_FF = 4 * D_MODEL
NA_HEADS = 16
NA_KH = 8
NA_KW = 16
ROPE_BASE = 10000.0
EPS = 1e-6
NEG_INF = -1e30
LOG2_E = math.log2(math.e)
SCORE_SCALE = HEAD_DIM ** -0.5 * LOG2_E

LANES = 128
SUBLANES = 8
N_LANE_BLOCKS = D_MODEL // LANES
N_PROMPT_ROWS = BATCH * SEQ
N_SAMPLE_ROWS = DEC_BATCH * DEC_SEQ
N_ROWS = N_PROMPT_ROWS + N_SAMPLE_ROWS
ROW_TILE = 512
N_ROW_TILES = N_ROWS // ROW_TILE
N_PROMPT_TILES = N_PROMPT_ROWS // ROW_TILE
TILES_PER_SAMPLE = DEC_SEQ // ROW_TILE
REQ_PER_TILE = ROW_TILE // SEQ
N_COND = 8
VMEM_LIMIT = 56 * 1024 * 1024

F32 = jnp.float32
BF16 = jnp.bfloat16
NT_DIMS = (((1,), (1,)), ((), ()))


def _dot(a, b):
    return jnp.dot(a, b, preferred_element_type=F32)


def _dot_nt(a, b):
    return lax.dot_general(a, b, NT_DIMS, preferred_element_type=F32)


def _params(n_axes):
    return pltpu.CompilerParams(dimension_semantics=("arbitrary",) * n_axes,
                                vmem_limit_bytes=VMEM_LIMIT)


def _resident(shape):
    return pl.BlockSpec(shape, lambda *_: (0,) * len(shape), pipeline_mode=pl.Buffered(1))


def _layer_slice(shape, layer):
    return pl.BlockSpec((None,) + shape, lambda *_: (layer,) + (0,) * len(shape),
                        pipeline_mode=pl.Buffered(1))


class _CastJob(NamedTuple):
    weights: jax.Array
    layer: int
    split_axis: int


def _pallas_call_with_casts(kernel_fn, jobs, *, grid, in_specs, out_specs, out_shape, args, name,
                            input_output_aliases=None):
    n_in, n_out, n_jobs = len(in_specs), len(out_specs), len(jobs)
    (n_steps,) = grid
    src_specs, dst_specs, dst_shapes = [], [], []
    for job in jobs:
        _, rows, cols = job.weights.shape
        if job.split_axis == 0:
            band = (rows // n_steps, cols)
            src_specs.append(pl.BlockSpec((None,) + band, lambda i, layer=job.layer: (layer, i, 0)))
            dst_specs.append(pl.BlockSpec(band, lambda i: (i, 0)))
        else:
            band = (rows, cols // n_steps)
            src_specs.append(pl.BlockSpec((None,) + band, lambda i, layer=job.layer: (layer, 0, i)))
            dst_specs.append(pl.BlockSpec(band, lambda i: (0, i)))
        dst_shapes.append(jax.ShapeDtypeStruct((rows, cols), BF16))

    def kernel_and_casts(*refs):
        ins, refs = refs[:n_in], refs[n_in:]
        srcs, refs = refs[:n_jobs], refs[n_jobs:]
        outs, refs = refs[:n_out], refs[n_out:]
        dsts, scratch = refs[:n_jobs], refs[n_jobs:]
        for src_ref, dst_ref in zip(srcs, dsts):
            dst_ref[...] = src_ref[...].astype(BF16)
        kernel_fn(*ins, *outs, *scratch)

    return pl.pallas_call(
        kernel_and_casts,
        grid=grid,
        in_specs=list(in_specs) + src_specs,
        out_specs=list(out_specs) + dst_specs,
        out_shape=list(out_shape) + dst_shapes,
        input_output_aliases=input_output_aliases or {},
        compiler_params=_params(1),
        name=name,
    )(*args, *(job.weights for job in jobs))


def _prompt_tile(i):
    return jnp.minimum(i, N_PROMPT_TILES - 1)


def _sample_tile(i):
    return jnp.maximum(i - N_PROMPT_TILES, 0)


def _cond_row(i):
    return jnp.where(i < N_PROMPT_TILES, 0, 1 + (i - N_PROMPT_TILES) // TILES_PER_SAMPLE)


def _low_half_mask(shape):
    return lax.broadcasted_iota(jnp.int32, shape, len(shape) - 1) < HEAD_DIM


def _ada_kernel(cond_ref, w_ref, b_ref, out_ref):
    cond = cond_ref[...]
    act = (cond * jax.nn.sigmoid(cond)).astype(BF16)
    out_ref[...] = _dot(act, w_ref[...].astype(BF16)) + b_ref[...]


def _modulation(cond, w_ada, b_ada):
    tn = 1536
    return pl.pallas_call(
        _ada_kernel,
        grid=(DEPTH, 6 * D_MODEL // tn),
        in_specs=[
            pl.BlockSpec((N_COND, D_MODEL), lambda l, j: (0, 0)),
            pl.BlockSpec((None, D_MODEL, tn), lambda l, j: (l, 0, j)),
            pl.BlockSpec((None, 1, tn), lambda l, j: (l, 0, j)),
        ],
        out_specs=pl.BlockSpec((None, N_COND, tn), lambda l, j: (l, 0, j)),
        out_shape=jax.ShapeDtypeStruct((DEPTH, N_COND, 6 * D_MODEL), F32),
        compiler_params=_params(2),
        name="ada_modulation",
    )(cond, w_ada, b_ada.reshape(DEPTH, 1, 6 * D_MODEL))


def _group_rms(x, gain):
    low = _low_half_mask((1, LANES))
    cols = []
    for c in range(N_LANE_BLOCKS):
        xb = x[:, c * LANES:(c + 1) * LANES]
        sq = xb * xb
        s_lo = jnp.sum(jnp.where(low, sq, 0.0), axis=-1, keepdims=True)
        s_hi = jnp.sum(jnp.where(low, 0.0, sq), axis=-1, keepdims=True)
        ms = jnp.where(low, s_lo, s_hi) * (1.0 / HEAD_DIM)
        cols.append(xb * lax.rsqrt(ms + EPS) * gain)
    return cols


def _rope(cols, cos, sin_signed):
    lane = lax.broadcasted_iota(jnp.int32, (1, LANES), 1)
    first = (lane % 32) < 16
    out = []
    for xb in cols:
        partner = jnp.where(first, pltpu.roll(xb, LANES - 16, 1), pltpu.roll(xb, 16, 1))
        out.append(xb * cos + partner * sin_signed)
    return out


def _qkv_kernel(is_diff, is_prompt, has_prev, *refs):
    if has_prev:
        refs = refs[2:]
    if is_prompt:
        x_ref, mod_ref, g_ref, w_ref, qg_ref, kg_ref, q_ref, k_ref, v_ref, kf_ref, vf_ref = refs
        r = 0
    else:
        x_ref, mod_ref, g_ref, w_ref, qg_ref, kg_ref, cos_ref, sin_ref, q_ref, k_ref, v_ref = refs
        r = 1 + pl.program_id(0) // TILES_PER_SAMPLE
    x = x_ref[...]
    ms = jnp.mean(x * x, axis=-1, keepdims=True)
    xn = x * lax.rsqrt(ms + EPS) * g_ref[...]
    shift = mod_ref[pl.ds(r, 1), 0:D_MODEL]
    scale = mod_ref[pl.ds(r, 1), D_MODEL:2 * D_MODEL]
    h = (xn * (1.0 + scale) + shift).astype(BF16)
    qkv = _dot(h, w_ref[...])
    qn = _group_rms(qkv[:, 0:D_MODEL], qg_ref[...])
    kn = _group_rms(qkv[:, D_MODEL:2 * D_MODEL], kg_ref[...])
    v = qkv[:, 2 * D_MODEL:3 * D_MODEL]
    v_ref[...] = v.astype(BF16)
    sm_scale = SCORE_SCALE
    if is_diff and not is_prompt:
        cos = cos_ref[...]
        sin = sin_ref[...]
        qs = _rope(qn, cos, sin)
        ks = _rope(kn, cos, sin)
    else:
        qs, ks = qn, kn
    for c in range(N_LANE_BLOCKS):
        sl = slice(c * LANES, (c + 1) * LANES)
        q_ref[:, sl] = (qs[c] * sm_scale).astype(BF16)
        k_ref[:, sl] = ks[c].astype(BF16)
        if not is_prompt:
            continue
        for bb in range(REQ_PER_TILE):
            rows = slice(bb * SEQ, (bb + 1) * SEQ)
            if is_diff:
                dst = pl.ds(c, SEQ, stride=N_LANE_BLOCKS)
                kf_ref[bb, dst, :] = kn[c][rows, :]
                vf_ref[bb, dst, :] = v[rows, sl]
            else:
                kf_ref[bb, sl, :] = kn[c][rows, :].T
                vf_ref[bb, sl, :] = v[rows, sl].T


def _qkv(x, first_tile, is_prompt, mod, layer, g_mix, w_qkv, kind_layer, q_gain, k_gain, rope_tabs,
         is_diff, prev, cast_jobs):
    n_tiles = N_PROMPT_TILES if is_prompt else N_ROW_TILES - N_PROMPT_TILES
    row_spec = pl.BlockSpec((ROW_TILE, D_MODEL), lambda i: (i, 0))
    in_specs = [
        pl.BlockSpec((ROW_TILE, D_MODEL), lambda i: (first_tile + i, 0)),
        _layer_slice((N_COND, 6 * D_MODEL), layer),
        _layer_slice((1, D_MODEL), layer),
        _resident((D_MODEL, 3 * D_MODEL)),
        _resident((1, LANES)),
        _resident((1, LANES)),
    ]
    args = [x, mod, g_mix, w_qkv, q_gain, k_gain]
    out_specs = [row_spec] * 3
    out_shape = [jax.ShapeDtypeStruct((n_tiles * ROW_TILE, D_MODEL), BF16)] * 3
    aliases = {}
    if is_prompt:
        if is_diff:
            cache_rows, cache_cols = SEQ * N_LANE_BLOCKS, LANES
        else:
            cache_rows, cache_cols = D_MODEL, SEQ
        cache_spec = pl.BlockSpec((REQ_PER_TILE, None, cache_rows, cache_cols),
                                  lambda i: (i, kind_layer, 0, 0))
        out_specs += [cache_spec] * 2
        out_shape += [jax.ShapeDtypeStruct((BATCH, DEPTH // 2, cache_rows, cache_cols), F32)] * 2
        if prev is not None:
            in_specs = [pl.BlockSpec(memory_space=pl.ANY)] * 2 + in_specs
            args = list(prev) + args
            aliases = {0: 3, 1: 4}
    else:
        rope_spec = pl.BlockSpec((ROW_TILE, LANES), lambda i: (i % TILES_PER_SAMPLE, 0))
        in_specs += [rope_spec] * 2
        args += list(rope_tabs)
    return _pallas_call_with_casts(
        functools.partial(_qkv_kernel, is_diff, is_prompt, is_prompt and prev is not None), cast_jobs,
        grid=(n_tiles,),
        in_specs=in_specs,
        out_specs=out_specs,
        out_shape=out_shape,
        args=args,
        input_output_aliases=aliases,
        name="qkv_prompt" if is_prompt else "qkv_latent",
    )


def _softmax_pv(score_parts, value_parts):
    m = None
    for s in score_parts:
        pm = jnp.max(s, axis=-1, keepdims=True)
        m = pm if m is None else jnp.maximum(m, pm)
    acc = None
    for s, times_v in zip(score_parts, value_parts):
        pv = times_v(jnp.exp2(s - m).astype(BF16))
        acc = pv if acc is None else acc + pv
    return acc[:, :LANES] * (1.0 / acc[:, LANES:])


def _times(v):
    v1 = jnp.concatenate([v, jnp.ones_like(v)], axis=1)
    return lambda w: _dot(w, v1)


def _times_t(v_t):
    v1_t = jnp.concatenate([v_t, jnp.ones_like(v_t)], axis=0)
    return lambda w: _dot_nt(w, v1_t)


def _diff_lambda(lq1_ref, lk1_ref, lq2_ref, lk2_ref, lambda_init):
    e1 = jnp.exp(jnp.sum(lq1_ref[...] * lk1_ref[...], axis=-1, keepdims=True))
    e2 = jnp.exp(jnp.sum(lq2_ref[...] * lk2_ref[...], axis=-1, keepdims=True))
    return e1 - e2 + lambda_init


def _attend_pairs(n_blocks, load_q, scores, values, combine, store):
    units = [(b, m) for b in range(n_blocks) for m in range(2)]

    def unit_scores(u):
        b, m = units[u]
        q = load_q(b)
        low = _low_half_mask(q.shape)
        zero = jnp.zeros_like(q)
        return scores(b, m, jnp.where(low, q, zero) if m == 0 else jnp.where(low, zero, q))

    s = unit_scores(0)
    first = None
    for u, (b, m) in enumerate(units):
        s_next = unit_scores(u + 1) if u + 1 < len(units) else None
        o = _softmax_pv(s, values(b))
        s = s_next
        if m == 0:
            first = o
        else:
            store(b, combine(first, o))


def _diff_combine(lam, gsub, lambda_init):
    def combine(o1, o2):
        o = o1 - lam * o2
        ms = jnp.mean(o * o, axis=-1, keepdims=True)
        return o * lax.rsqrt(ms + EPS) * gsub * (1.0 - lambda_init)
    return combine


def _head_pair_combine(o_lo, o_hi):
    return jnp.where(_low_half_mask(o_lo.shape), o_lo, o_hi)


def _ctx_attn_kernel(is_diff, lambda_init, q_ref, k_ref, v_ref, lq1_ref, lk1_ref, lq2_ref, lk2_ref,
                     gsub_ref, o_ref):
    if is_diff:
        lam = _diff_lambda(lq1_ref, lk1_ref, lq2_ref, lk2_ref, lambda_init)
        combine = _diff_combine(lam, gsub_ref[...], lambda_init)
    else:
        combine = _head_pair_combine

    def cols(h):
        return slice(h * LANES, (h + 1) * LANES)

    def store(h, o):
        o_ref[:, cols(h)] = o.astype(BF16)

    _attend_pairs(N_LANE_BLOCKS,
                  lambda h: q_ref[:, cols(h)],
                  lambda h, m, qm: [_dot_nt(qm, k_ref[:, cols(h)])],
                  lambda h: [_times(v_ref[:, cols(h)])],
                  combine, store)


def _ctx_attention(q, k, v, lam_params, gsub, is_diff, lambda_init, cast_jobs):
    blk = pl.BlockSpec((SEQ, D_MODEL), lambda b: (b, 0))
    small = [_resident((1, HEAD_DIM))] * 4 + [_resident((1, LANES))]
    return _pallas_call_with_casts(
        functools.partial(_ctx_attn_kernel, is_diff, lambda_init), cast_jobs,
        grid=(BATCH,),
        in_specs=[blk, blk, blk] + small,
        out_specs=[blk],
        out_shape=[jax.ShapeDtypeStruct((N_PROMPT_ROWS, D_MODEL), BF16)],
        args=[q, k, v, *lam_params, gsub],
        name="ctx_attention",
    )


Q_CHUNK = 512
LATENT_HEADS_PER_STEP = 2


def _diff_latent_kernel(lambda_init, q_ref, k_ref, v_ref, ck_ref, cv_ref,
                        lq1_ref, lk1_ref, lq2_ref, lk2_ref, gsub_ref, o_ref):
    lam = _diff_lambda(lq1_ref, lk1_ref, lq2_ref, lk2_ref, lambda_init)
    n_chunks = DEC_SEQ // Q_CHUNK
    keys, values = [], []
    for hh in range(LATENT_HEADS_PER_STEP):
        cols = slice(hh * LANES, (hh + 1) * LANES)
        head = pl.program_id(1) * LATENT_HEADS_PER_STEP + hh
        head_rows = pl.ds(head, PAST_LEN, stride=N_LANE_BLOCKS)
        keys.append([k_ref[:, cols], ck_ref[head_rows, :].astype(BF16)])
        values.append([_times(v_ref[:, cols]), _times(cv_ref[head_rows, :].astype(BF16))])

    def block(b):
        hh, c = divmod(b, n_chunks)
        return slice(c * Q_CHUNK, (c + 1) * Q_CHUNK), slice(hh * LANES, (hh + 1) * LANES)

    def store(b, o):
        o_ref[block(b)] = o.astype(BF16)

    _attend_pairs(LATENT_HEADS_PER_STEP * n_chunks,
                  lambda b: q_ref[block(b)],
                  lambda b, m, qm: [_dot_nt(qm, kpart) for kpart in keys[b // n_chunks]],
                  lambda b: values[b // n_chunks],
                  _diff_combine(lam, gsub_ref[...], lambda_init), store)


def _diff_latent_attention(q, k, v, cache_k, cache_v, layer_i, lam_params, gsub, lambda_init):
    blk = pl.BlockSpec((DEC_SEQ, LATENT_HEADS_PER_STEP * LANES), lambda b, h: (b, h))
    cache = pl.BlockSpec((None, None, PAST_LEN * N_LANE_BLOCKS, LANES), lambda b, h: (b, layer_i, 0, 0))
    small = [_resident((1, HEAD_DIM))] * 4 + [_resident((1, LANES))]
    return pl.pallas_call(
        functools.partial(_diff_latent_kernel, lambda_init),
        grid=(DEC_BATCH, N_LANE_BLOCKS // LATENT_HEADS_PER_STEP),
        in_specs=[blk, blk, blk, cache, cache] + small,
        out_specs=blk,
        out_shape=jax.ShapeDtypeStruct((N_SAMPLE_ROWS, D_MODEL), BF16),
        compiler_params=_params(2),
        name="diff_latent_attention",
    )(q, k, v, cache_k, cache_v, *lam_params, gsub)


NA_Q_ROWS = 8
NA_K_ROWS = 12
NA_Q_TOK = NA_Q_ROWS * GRID_W
NA_K_TOK = NA_K_ROWS * GRID_W
NA_GROUPS = GRID_ROWS // NA_Q_ROWS
NA_K_STEP = (GRID_ROWS - NA_K_ROWS) // (NA_GROUPS - 1)


N_DR = 2 * NA_KH - 1
N_DC = 2 * NA_KW - 1
N_DR_PAD = 16


def _expand_na_bias(rb_ref, tile_ref, bias_ref):
    qc = lax.broadcasted_iota(jnp.int32, (GRID_W, LANES), 0)
    lane = lax.broadcasted_iota(jnp.int32, (GRID_W, LANES), 1)
    kc = lane % GRID_W
    cs = jnp.clip(qc - NA_KW // 2, 0, GRID_W - NA_KW)
    col_ok = (kc >= cs) & (kc < cs + NA_KW)
    low = lane < GRID_W
    neg = jnp.full((GRID_W, LANES), NEG_INF, F32)
    for head in range(2 * LATENT_HEADS_PER_STEP):
        for dr in range(N_DR):
            row = jnp.broadcast_to(rb_ref[head, dr:dr + 1, :], (GRID_W, LANES))
            shifted = pltpu.roll(row, LANES - (NA_KW - 1), 1, stride=1, stride_axis=0)
            tile_ref[dr] = jnp.where(col_ok, shifted * LOG2_E, neg)
        for g in range(NA_GROUPS):
            for rq in range(NA_Q_ROWS):
                r = g * NA_Q_ROWS + rq
                rs = min(max(r - NA_KH // 2, 0), GRID_ROWS - NA_KH)

                def tile(kr, r=r, rs=rs):
                    if rs <= kr < rs + NA_KH:
                        return tile_ref[kr - r + NA_KH - 1]
                    return neg

                for p in range(NA_K_ROWS // 2):
                    kr = g * NA_K_STEP + 2 * p
                    bias_ref[head, g, rq * GRID_W:(rq + 1) * GRID_W, p * LANES:(p + 1) * LANES] = (
                        jnp.where(low, tile(kr), tile(kr + 1)))


def _na_latent_kernel(q_ref, k_ref, v_ref, ckt_ref, cvt_ref, rb_ref, o_ref, tile_ref, bias_ref):
    @pl.when(pl.program_id(1) == 0)
    def _():
        _expand_na_bias(rb_ref, tile_ref, bias_ref)

    ck_t = [ckt_ref[hh * LANES:(hh + 1) * LANES, :].astype(BF16) for hh in range(LATENT_HEADS_PER_STEP)]
    cv_t = [cvt_ref[hh * LANES:(hh + 1) * LANES, :].astype(BF16) for hh in range(LATENT_HEADS_PER_STEP)]

    def block(b):
        return divmod(b, NA_GROUPS)

    def cols(hh):
        return slice(hh * LANES, (hh + 1) * LANES)

    def qrows(g):
        return slice(g * NA_Q_TOK, (g + 1) * NA_Q_TOK)

    def krows(g):
        return slice(g * NA_K_STEP * GRID_W, g * NA_K_STEP * GRID_W + NA_K_TOK)

    def load_q(b):
        hh, g = block(b)
        return q_ref[qrows(g), cols(hh)]

    def scores(b, half, qm):
        hh, g = block(b)
        return [_dot_nt(qm, k_ref[krows(g), cols(hh)]) + bias_ref[2 * hh + half, g], _dot(qm, ck_t[hh])]

    def values(b):
        hh, g = block(b)
        return [_times(v_ref[krows(g), cols(hh)]), _times_t(cv_t[hh])]

    def store(b, o):
        hh, g = block(b)
        o_ref[qrows(g), cols(hh)] = o.astype(BF16)

    _attend_pairs(LATENT_HEADS_PER_STEP * NA_GROUPS, load_q, scores, values, _head_pair_combine, store)


def _na_latent_attention(q, k, v, cache_k_t, cache_v_t, layer_i, rel_bias_rows):
    width = LATENT_HEADS_PER_STEP * LANES
    blk = pl.BlockSpec((DEC_SEQ, width), lambda p, b: (b, p))
    cache = pl.BlockSpec((None, None, width, PAST_LEN), lambda p, b: (b, layer_i, p, 0))
    rb_spec = pl.BlockSpec((None, 2 * LATENT_HEADS_PER_STEP, N_DR_PAD, LANES), lambda p, b: (layer_i, p, 0, 0))
    return pl.pallas_call(
        _na_latent_kernel,
        grid=(N_LANE_BLOCKS // LATENT_HEADS_PER_STEP, DEC_BATCH),
        in_specs=[blk, blk, blk, cache, cache, rb_spec],
        out_specs=blk,
        out_shape=jax.ShapeDtypeStruct((N_SAMPLE_ROWS, D_MODEL), BF16),
        scratch_shapes=[pltpu.VMEM((N_DR, GRID_W, LANES), F32),
                        pltpu.VMEM((2 * LATENT_HEADS_PER_STEP, NA_GROUPS, NA_Q_TOK, NA_K_TOK), F32)],
        compiler_params=_params(2),
        name="na_latent_attention",
    )(q, k, v, cache_k_t, cache_v_t, rel_bias_rows)


def _padded_rel_bias(rel_bias_na):
    half = jnp.pad(rel_bias_na, ((0, 0), (0, 0), (0, N_DR_PAD - N_DR), (0, GRID_W - N_DC)))
    return jnp.concatenate([half, half], axis=-1)


FF_CHUNK = 1024


def _post_kernel(split_in, split_out, *refs):
    n_x = 2 if split_in else 1
    x_refs = refs[:n_x]
    op_ref, os_ref, mod_ref, g_ref, wo_ref, w1_ref, w2_ref = refs[n_x:n_x + 7]
    y_refs = refs[n_x + 7:]
    i = pl.program_id(0)
    r = _cond_row(i)
    is_prompt = i < N_PROMPT_TILES

    def mod(j):
        return mod_ref[pl.ds(r, 1), j * D_MODEL:(j + 1) * D_MODEL]

    x = jnp.where(is_prompt, x_refs[0][...], x_refs[1][...]) if split_in else x_refs[0][...]
    o = jnp.where(is_prompt, op_ref[...], os_ref[...])
    y1 = x + mod(2) * _dot(o, wo_ref[...])
    ms = jnp.mean(y1 * y1, axis=-1, keepdims=True)
    h = (y1 * lax.rsqrt(ms + EPS) * g_ref[...] * (1.0 + mod(4)) + mod(3)).astype(BF16)
    acc = jnp.zeros((ROW_TILE, D_MODEL), F32)
    for c in range(D_FF // FF_CHUNK):
        cols = slice(c * FF_CHUNK, (c + 1) * FF_CHUNK)
        hid = jnp.maximum(_dot(h, w1_ref[:, cols]), 0.0)
        acc = acc + _dot((hid * hid).astype(BF16), w2_ref[cols, :])
    y = y1 + mod(5) * acc
    if split_out:
        @pl.when(is_prompt)
        def _():
            y_refs[0][...] = y

        @pl.when(jnp.logical_not(is_prompt))
        def _():
            y_refs[1][...] = y
    else:
        y_refs[0][...] = y


def _post(xs, o_p, o_s, mod, layer, g_mlp, w_o, w_fc1, w_fc2, split_out, cast_jobs):
    prompt_spec = pl.BlockSpec((ROW_TILE, D_MODEL), lambda i: (_prompt_tile(i), 0))
    sample_spec = pl.BlockSpec((ROW_TILE, D_MODEL), lambda i: (_sample_tile(i), 0))
    row_spec = pl.BlockSpec((ROW_TILE, D_MODEL), lambda i: (i, 0))
    split_in = len(xs) == 2
    if split_out:
        out_specs = [prompt_spec, sample_spec]
        out_shape = [jax.ShapeDtypeStruct((N_PROMPT_ROWS, D_MODEL), F32),
                     jax.ShapeDtypeStruct((N_SAMPLE_ROWS, D_MODEL), F32)]
    else:
        out_specs = [row_spec]
        out_shape = [jax.ShapeDtypeStruct((N_ROWS, D_MODEL), F32)]
    return _pallas_call_with_casts(
        functools.partial(_post_kernel, split_in, split_out), cast_jobs,
        grid=(N_ROW_TILES,),
        in_specs=([prompt_spec, sample_spec] if split_in else [row_spec]) + [
            prompt_spec, sample_spec,
            _layer_slice((N_COND, 6 * D_MODEL), layer),
            _layer_slice((1, D_MODEL), layer),
            _resident((D_MODEL, D_MODEL)),
            _resident((D_MODEL, D_FF)),
            _resident((D_FF, D_MODEL)),
        ],
        out_specs=out_specs,
        out_shape=out_shape,
        args=[*xs, o_p, o_s, mod, g_mlp, w_o, w_fc1, w_fc2],
        name="wo_mlp",
    )


def _rope_tables():
    n_freq = HEAD_DIM // 4
    freqs = ROPE_BASE ** (-jnp.arange(n_freq, dtype=F32) / n_freq)
    t = jnp.arange(DEC_SEQ)
    ang_row = (t // GRID_W).astype(F32)[:, None] * freqs
    ang_col = (t % GRID_W).astype(F32)[:, None] * freqs
    cos64 = jnp.concatenate([jnp.cos(ang_row)] * 2 + [jnp.cos(ang_col)] * 2, axis=-1)
    sin64 = jnp.concatenate([-jnp.sin(ang_row), jnp.sin(ang_row),
                             -jnp.sin(ang_col), jnp.sin(ang_col)], axis=-1)
    return jnp.tile(cos64, (1, 2)), jnp.tile(sin64, (1, 2))


def _pair_gain(g):
    return jnp.tile(g.reshape(1, HEAD_DIM), (1, LANES // HEAD_DIM))


def _head_major_cache(cache):
    n_req, n_layers = cache.shape[:2]
    return jnp.transpose(cache, (0, 1, 3, 4, 2)).reshape(n_req, n_layers, D_MODEL, PAST_LEN)


def kernel(x_prompt, x_sample, cache_diff_k, cache_diff_v, cache_na_k, cache_na_v, c, c_ctx, w_ada, b_ada, norm_mix_g, norm_mlp_g, w_fc1, w_fc2, w_qkv_diff, w_o_diff, q_norm_diff_g, k_norm_diff_g, lambda_q1, lambda_k1, lambda_q2, lambda_k2, subln_g, w_qkv_na, w_o_na, q_norm_na_g, k_norm_na_g, rel_bias_na):
    cond = jnp.concatenate([c_ctx[None, :], c, jnp.zeros((N_COND - 1 - DEC_BATCH, D_MODEL), F32)], axis=0)
    mod = _modulation(cond, w_ada, b_ada)
    cos_tab, sin_tab = _rope_tables()
    na_bias = _padded_rel_bias(rel_bias_na)
    cdk = cache_diff_k.reshape(DEC_BATCH, DEPTH // 2, PAST_LEN * N_LANE_BLOCKS, LANES)
    cdv = cache_diff_v.reshape(DEC_BATCH, DEPTH // 2, PAST_LEN * N_LANE_BLOCKS, LANES)
    cnk_t = _head_major_cache(cache_na_k)
    cnv_t = _head_major_cache(cache_na_v)
    g_mix = norm_mix_g.reshape(DEPTH, 1, D_MODEL)
    g_mlp = norm_mlp_g.reshape(DEPTH, 1, D_MODEL)
    w_qkv_f32 = (w_qkv_diff, w_qkv_na)
    w_o_f32 = (w_o_diff, w_o_na)
    q_gain = (q_norm_diff_g, q_norm_na_g)
    k_gain = (k_norm_diff_g, k_norm_na_g)

    xs = [x_prompt.reshape(N_PROMPT_ROWS, D_MODEL), x_sample.reshape(N_SAMPLE_ROWS, D_MODEL)]
    new_kv = [None, None]
    w_qkv = w_qkv_diff[0].astype(BF16)
    w_o = w1 = w2 = None
    for l in range(DEPTH):
        i = l // 2
        kind = l % 2
        is_diff = kind == 0
        first = l == 0
        last = l == DEPTH - 1
        x_prompt_src, x_sample_src = (xs[0], xs[1]) if len(xs) == 2 else (xs[0], xs[0])
        sample_first_tile = 0 if len(xs) == 2 else N_PROMPT_TILES
        gains = (_pair_gain(q_gain[kind][i]), _pair_gain(k_gain[kind][i]))
        qp, kp, vp, new_k, new_v, *cast = _qkv(x_prompt_src, 0, True, mod, l, g_mix, w_qkv, i, *gains, None,
                                               is_diff, new_kv[kind], [_CastJob(w_fc1, l, 1)] if first else [])
        w1 = cast[0] if first else w1
        new_kv[kind] = (new_k, new_v)
        qs, ks, vs, *cast = _qkv(x_sample_src, sample_first_tile, False, mod, l, g_mix, w_qkv, i, *gains,
                                 (cos_tab, sin_tab), is_diff, None, [_CastJob(w_fc2, l, 0)] if first else [])
        w2 = cast[0] if first else w2
        lambda_init = 0.8 - 0.6 * math.exp(-0.3 * l)
        lam_params = [p[i if is_diff else 0].reshape(1, HEAD_DIM)
                      for p in (lambda_q1, lambda_k1, lambda_q2, lambda_k2)]
        gsub = subln_g[i if is_diff else 0].reshape(1, LANES)
        o_p, *cast = _ctx_attention(qp, kp, vp, lam_params, gsub, is_diff, lambda_init,
                                    [_CastJob(w_o_f32[kind], i, 0)] if first else [])
        w_o = cast[0] if first else w_o
        if is_diff:
            o_s = _diff_latent_attention(qs, ks, vs, cdk, cdv, i, lam_params, gsub, lambda_init)
        else:
            o_s = _na_latent_attention(qs, ks, vs, cnk_t, cnv_t, i, na_bias)
        next_jobs = [] if last else [_CastJob(w_qkv_f32[(l + 1) % 2], (l + 1) // 2, 0),
                                     _CastJob(w_o_f32[(l + 1) % 2], (l + 1) // 2, 0),
                                     _CastJob(w_fc1, l + 1, 1), _CastJob(w_fc2, l + 1, 0)]
        outs = list(_post(xs, o_p, o_s, mod, l, g_mlp, w_o, w1, w2, last, next_jobs))
        if last:
            xp, xs = outs
        else:
            *xs, w_qkv, w_o, w1, w2 = outs

    n_kind = DEPTH // 2
    diff_k, diff_v = (a.reshape(BATCH, n_kind, SEQ, N_LANE_BLOCKS, LANES) for a in new_kv[0])
    na_k, na_v = (jnp.transpose(a.reshape(BATCH, n_kind, NA_HEADS, HEAD_DIM, SEQ), (0, 1, 4, 2, 3))
                  for a in new_kv[1])
    return (xp.reshape(BATCH, SEQ, D_MODEL), xs.reshape(DEC_BATCH, DEC_SEQ, D_MODEL),
            diff_k, diff_v, na_k, na_v)
```

```python
import functools
import math
from typing import Callable, NamedTuple

import jax
import jax.numpy as jnp
from jax import lax
from jax.experimental import pallas as pl
from jax.experimental.pallas import tpu as pltpu

D_MODEL = 1024
BATCH = 16
SEQ = 256
DEPTH = 4
DEC_BATCH = 4
DEC_SEQ = 1024
PAST_LEN = 256
GRID_W = 64
GRID_ROWS = DEC_SEQ // GRID_W
HEAD_DIM = 64
D_FF = 4 * D_MODEL
NA_HEADS = 16
NA_KH = 8
NA_KW = 16
ROPE_BASE = 10000.0
EPS = 1e-6
NEG_INF = -1e30
LOG2_E = math.log2(math.e)
SCORE_SCALE = HEAD_DIM ** -0.5 * LOG2_E

LANES = 128
SUBLANES = 8
N_LANE_BLOCKS = D_MODEL // LANES
N_PROMPT_ROWS = BATCH * SEQ
N_SAMPLE_ROWS = DEC_BATCH * DEC_SEQ
N_ROWS = N_PROMPT_ROWS + N_SAMPLE_ROWS
ROW_TILE = 512
N_ROW_TILES = N_ROWS // ROW_TILE
N_PROMPT_TILES = N_PROMPT_ROWS // ROW_TILE
TILES_PER_SAMPLE = DEC_SEQ // ROW_TILE
REQ_PER_TILE = ROW_TILE // SEQ
N_COND = 8
VMEM_LIMIT = 56 * 1024 * 1024

F32 = jnp.float32
BF16 = jnp.bfloat16
NT_DIMS = (((1,), (1,)), ((), ()))


def _dot(a, b):
    return jnp.dot(a, b, preferred_element_type=F32)


def _dot_nt(a, b):
    return lax.dot_general(a, b, NT_DIMS, preferred_element_type=F32)


def _params(n_axes):
    return pltpu.CompilerParams(dimension_semantics=("arbitrary",) * n_axes,
                                vmem_limit_bytes=VMEM_LIMIT)


def _resident(shape):
    return pl.BlockSpec(shape, lambda *_: (0,) * len(shape), pipeline_mode=pl.Buffered(1))


def _layer_slice(shape, layer):
    return pl.BlockSpec((None,) + shape, lambda *_: (layer,) + (0,) * len(shape),
                        pipeline_mode=pl.Buffered(1))


class _SideJob(NamedTuple):
    args: list
    in_specs: list
    out_spec: pl.BlockSpec
    out_shape: jax.ShapeDtypeStruct
    body: Callable


def _cast_body(src_ref, dst_ref):
    dst_ref[...] = src_ref[...].astype(BF16)


def _cast_job(weights, layer, split_axis, n_steps):
    _, rows, cols = weights.shape
    if split_axis == 0:
        band = (rows // n_steps, cols)
        src_spec = pl.BlockSpec((None,) + band, lambda i: (layer, i, 0))
        dst_spec = pl.BlockSpec(band, lambda i: (i, 0))
    else:
        band = (rows, cols // n_steps)
        src_spec = pl.BlockSpec((None,) + band, lambda i: (layer, 0, i))
        dst_spec = pl.BlockSpec(band, lambda i: (0, i))
    return _SideJob([weights], [src_spec], dst_spec, jax.ShapeDtypeStruct((rows, cols), BF16), _cast_body)


def _pallas_call_with_side_jobs(kernel_fn, jobs, *, grid, in_specs, out_specs, out_shape, args, name,
                                input_output_aliases=None):
    n_in, n_out = len(in_specs), len(out_specs)
    n_job_in = [len(job.args) for job in jobs]

    def kernel_and_jobs(*refs):
        ins, refs = refs[:n_in], refs[n_in:]
        job_ins, refs = refs[:sum(n_job_in)], refs[sum(n_job_in):]
        outs, refs = refs[:n_out], refs[n_out:]
        job_outs, scratch = refs[:len(jobs)], refs[len(jobs):]
        for job, n, out_ref in zip(jobs, n_job_in, job_outs):
            job.body(*job_ins[:n], out_ref)
            job_ins = job_ins[n:]
        kernel_fn(*ins, *outs, *scratch)

    return pl.pallas_call(
        kernel_and_jobs,
        grid=grid,
        in_specs=list(in_specs) + [spec for job in jobs for spec in job.in_specs],
        out_specs=list(out_specs) + [job.out_spec for job in jobs],
        out_shape=list(out_shape) + [job.out_shape for job in jobs],
        input_output_aliases=input_output_aliases or {},
        compiler_params=_params(1),
        name=name,
    )(*args, *(arg for job in jobs for arg in job.args))


def _prompt_tile(i):
    return jnp.minimum(i, N_PROMPT_TILES - 1)


def _sample_tile(i):
    return jnp.maximum(i - N_PROMPT_TILES, 0)


def _cond_row(i):
    return jnp.where(i < N_PROMPT_TILES, 0, 1 + (i - N_PROMPT_TILES) // TILES_PER_SAMPLE)


def _low_half_mask(shape):
    return lax.broadcasted_iota(jnp.int32, shape, len(shape) - 1) < HEAD_DIM


def _ada_kernel(cond_ref, w_ref, b_ref, out_ref):
    cond = cond_ref[...]
    act = (cond * jax.nn.sigmoid(cond)).astype(BF16)
    out_ref[...] = _dot(act, w_ref[...].astype(BF16)) + b_ref[...]


def _ada_specs(layer, n_steps):
    tn = 6 * D_MODEL // n_steps
    in_specs = [pl.BlockSpec((N_COND, D_MODEL), lambda j: (0, 0)),
                pl.BlockSpec((None, D_MODEL, tn), lambda j: (layer, 0, j)),
                pl.BlockSpec((None, 1, tn), lambda j: (layer, 0, j))]
    return in_specs, pl.BlockSpec((N_COND, tn), lambda j: (0, j)), jax.ShapeDtypeStruct((N_COND, 6 * D_MODEL), F32)


def _modulation(cond, w_ada, b_ada, layer):
    n_steps = 4
    in_specs, out_spec, out_shape = _ada_specs(layer, n_steps)
    return pl.pallas_call(_ada_kernel, grid=(n_steps,), in_specs=in_specs, out_specs=out_spec,
                          out_shape=out_shape, compiler_params=_params(1),
                          name="ada_modulation")(cond, w_ada, b_ada)


def _ada_job(cond, w_ada, b_ada, layer, n_steps):
    in_specs, out_spec, out_shape = _ada_specs(layer, n_steps)
    return _SideJob([cond, w_ada, b_ada], in_specs, out_spec, out_shape, _ada_kernel)


def _group_rms(x, gain):
    low = _low_half_mask((1, LANES))
    cols = []
    for c in range(N_LANE_BLOCKS):
        xb = x[:, c * LANES:(c + 1) * LANES]
        sq = xb * xb
        s_lo = jnp.sum(jnp.where(low, sq, 0.0), axis=-1, keepdims=True)
        s_hi = jnp.sum(jnp.where(low, 0.0, sq), axis=-1, keepdims=True)
        ms = jnp.where(low, s_lo, s_hi) * (1.0 / HEAD_DIM)
        cols.append(xb * lax.rsqrt(ms + EPS) * gain)
    return cols


def _rope(cols, cos, sin_signed):
    lane = lax.broadcasted_iota(jnp.int32, (1, LANES), 1)
    first = (lane % 32) < 16
    out = []
    for xb in cols:
        partner = jnp.where(first, pltpu.roll(xb, LANES - 16, 1), pltpu.roll(xb, 16, 1))
        out.append(xb * cos + partner * sin_signed)
    return out


def _qkv_kernel(is_diff, is_prompt, has_prev, *refs):
    if has_prev:
        refs = refs[2:]
    if is_prompt:
        x_ref, mod_ref, g_ref, w_ref, qg_ref, kg_ref, q_ref, k_ref, v_ref, kf_ref, vf_ref = refs
        r = 0
    else:
        x_ref, mod_ref, g_ref, w_ref, qg_ref, kg_ref, cos_ref, sin_ref, q_ref, k_ref, v_ref = refs
        r = 1 + pl.program_id(0) // TILES_PER_SAMPLE
    x = x_ref[...]
    ms = jnp.mean(x * x, axis=-1, keepdims=True)
    xn = x * lax.rsqrt(ms + EPS) * g_ref[...]
    shift = mod_ref[pl.ds(r, 1), 0:D_MODEL]
    scale = mod_ref[pl.ds(r, 1), D_MODEL:2 * D_MODEL]
    h = (xn * (1.0 + scale) + shift).astype(BF16)
    qkv = _dot(h, w_ref[...])
    qn = _group_rms(qkv[:, 0:D_MODEL], qg_ref[...])
    kn = _group_rms(qkv[:, D_MODEL:2 * D_MODEL], kg_ref[...])
    v = qkv[:, 2 * D_MODEL:3 * D_MODEL]
    v_ref[...] = v.astype(BF16)
    sm_scale = SCORE_SCALE
    if is_diff and not is_prompt:
        cos = cos_ref[...]
        sin = sin_ref[...]
        qs = _rope(qn, cos, sin)
        ks = _rope(kn, cos, sin)
    else:
        qs, ks = qn, kn
    for c in range(N_LANE_BLOCKS):
        sl = slice(c * LANES, (c + 1) * LANES)
        q_ref[:, sl] = (qs[c] * sm_scale).astype(BF16)
        k_ref[:, sl] = ks[c].astype(BF16)
        if not is_prompt:
            continue
        for bb in range(REQ_PER_TILE):
            rows = slice(bb * SEQ, (bb + 1) * SEQ)
            if is_diff:
                dst = pl.ds(c, SEQ, stride=N_LANE_BLOCKS)
                kf_ref[bb, dst, :] = kn[c][rows, :]
                vf_ref[bb, dst, :] = v[rows, sl]
            else:
                kf_ref[bb, sl, :] = kn[c][rows, :].T
                vf_ref[bb, sl, :] = v[rows, sl].T


def _qkv(x, first_tile, is_prompt, mod, layer, g_mix, w_qkv, kind_layer, q_gain, k_gain, rope_tabs,
         is_diff, prev, side_jobs):
    n_tiles = N_PROMPT_TILES if is_prompt else N_ROW_TILES - N_PROMPT_TILES
    row_spec = pl.BlockSpec((ROW_TILE, D_MODEL), lambda i: (i, 0))
    in_specs = [
        pl.BlockSpec((ROW_TILE, D_MODEL), lambda i: (first_tile + i, 0)),
        _resident((N_COND, 6 * D_MODEL)),
        _layer_slice((1, D_MODEL), layer),
        _resident((D_MODEL, 3 * D_MODEL)),
        _resident((1, LANES)),
        _resident((1, LANES)),
    ]
    args = [x, mod, g_mix, w_qkv, q_gain, k_gain]
    out_specs = [row_spec] * 3
    out_shape = [jax.ShapeDtypeStruct((n_tiles * ROW_TILE, D_MODEL), BF16)] * 3
    aliases = {}
    if is_prompt:
        if is_diff:
            cache_rows, cache_cols = SEQ * N_LANE_BLOCKS, LANES
        else:
            cache_rows, cache_cols = D_MODEL, SEQ
        cache_spec = pl.BlockSpec((REQ_PER_TILE, None, cache_rows, cache_cols),
                                  lambda i: (i, kind_layer, 0, 0))
        out_specs += [cache_spec] * 2
        out_shape += [jax.ShapeDtypeStruct((BATCH, DEPTH // 2, cache_rows, cache_cols), F32)] * 2
        if prev is not None:
            in_specs = [pl.BlockSpec(memory_space=pl.ANY)] * 2 + in_specs
            args = list(prev) + args
            aliases = {0: 3, 1: 4}
    else:
        rope_spec = pl.BlockSpec((ROW_TILE, LANES), lambda i: (i % TILES_PER_SAMPLE, 0))
        in_specs += [rope_spec] * 2
        args += list(rope_tabs)
    return _pallas_call_with_side_jobs(
        functools.partial(_qkv_kernel, is_diff, is_prompt, is_prompt and prev is not None), side_jobs,
        grid=(n_tiles,),
        in_specs=in_specs,
        out_specs=out_specs,
        out_shape=out_shape,
        args=args,
        input_output_aliases=aliases,
        name="qkv_prompt" if is_prompt else "qkv_latent",
    )


def _softmax_pv(score_parts, value_parts):
    m = None
    for s in score_parts:
        pm = jnp.max(s, axis=-1, keepdims=True)
        m = pm if m is None else jnp.maximum(m, pm)
    acc = None
    for s, times_v in zip(score_parts, value_parts):
        pv = times_v(jnp.exp2(s - m).astype(BF16))
        acc = pv if acc is None else acc + pv
    return acc[:, :LANES] * (1.0 / acc[:, LANES:])


def _times(v):
    v1 = jnp.concatenate([v, jnp.ones_like(v)], axis=1)
    return lambda w: _dot(w, v1)


def _times_t(v_t):
    v1_t = jnp.concatenate([v_t, jnp.ones_like(v_t)], axis=0)
    return lambda w: _dot_nt(w, v1_t)


def _diff_lambda(lq1_ref, lk1_ref, lq2_ref, lk2_ref, lambda_init):
    e1 = jnp.exp(jnp.sum(lq1_ref[...] * lk1_ref[...], axis=-1, keepdims=True))
    e2 = jnp.exp(jnp.sum(lq2_ref[...] * lk2_ref[...], axis=-1, keepdims=True))
    return e1 - e2 + lambda_init


def _attend_pairs(n_blocks, load_q, scores, values, combine, store):
    units = [(b, m) for b in range(n_blocks) for m in range(2)]

    def unit_scores(u):
        b, m = units[u]
        q = load_q(b)
        low = _low_half_mask(q.shape)
        zero = jnp.zeros_like(q)
        return scores(b, m, jnp.where(low, q, zero) if m == 0 else jnp.where(low, zero, q))

    s = unit_scores(0)
    first = None
    for u, (b, m) in enumerate(units):
        s_next = unit_scores(u + 1) if u + 1 < len(units) else None
        o = _softmax_pv(s, values(b))
        s = s_next
        if m == 0:
            first = o
        else:
            store(b, combine(first, o))


def _diff_combine(lam, gsub, lambda_init):
    def combine(o1, o2):
        o = o1 - lam * o2
        ms = jnp.mean(o * o, axis=-1, keepdims=True)
        return o * lax.rsqrt(ms + EPS) * gsub * (1.0 - lambda_init)
    return combine


def _head_pair_combine(o_lo, o_hi):
    return jnp.where(_low_half_mask(o_lo.shape), o_lo, o_hi)


def _ctx_attn_kernel(is_diff, lambda_init, q_ref, k_ref, v_ref, lq1_ref, lk1_ref, lq2_ref, lk2_ref,
                     gsub_ref, o_ref):
    if is_diff:
        lam = _diff_lambda(lq1_ref, lk1_ref, lq2_ref, lk2_ref, lambda_init)
        combine = _diff_combine(lam, gsub_ref[...], lambda_init)
    else:
        combine = _head_pair_combine

    def cols(h):
        return slice(h * LANES, (h + 1) * LANES)

    def store(h, o):
        o_ref[:, cols(h)] = o.astype(BF16)

    _attend_pairs(N_LANE_BLOCKS,
                  lambda h: q_ref[:, cols(h)],
                  lambda h, m, qm: [_dot_nt(qm, k_ref[:, cols(h)])],
                  lambda h: [_times(v_ref[:, cols(h)])],
                  combine, store)


def _ctx_attention(q, k, v, lam_params, gsub, is_diff, lambda_init, side_jobs):
    blk = pl.BlockSpec((SEQ, D_MODEL), lambda b: (b, 0))
    small = [_resident((1, HEAD_DIM))] * 4 + [_resident((1, LANES))]
    return _pallas_call_with_side_jobs(
        functools.partial(_ctx_attn_kernel, is_diff, lambda_init), side_jobs,
        grid=(BATCH,),
        in_specs=[blk, blk, blk] + small,
        out_specs=[blk],
        out_shape=[jax.ShapeDtypeStruct((N_PROMPT_ROWS, D_MODEL), BF16)],
        args=[q, k, v, *lam_params, gsub],
        name="ctx_attention",
    )


Q_CHUNK = 512
LATENT_HEADS_PER_STEP = 2


def _diff_latent_kernel(lambda_init, q_ref, k_ref, v_ref, ck_ref, cv_ref,
                        lq1_ref, lk1_ref, lq2_ref, lk2_ref, gsub_ref, o_ref):
    lam = _diff_lambda(lq1_ref, lk1_ref, lq2_ref, lk2_ref, lambda_init)
    n_chunks = DEC_SEQ // Q_CHUNK
    keys, values = [], []
    for hh in range(LATENT_HEADS_PER_STEP):
        cols = slice(hh * LANES, (hh + 1) * LANES)
        head = pl.program_id(1) * LATENT_HEADS_PER_STEP + hh
        head_rows = pl.ds(head, PAST_LEN, stride=N_LANE_BLOCKS)
        keys.append([k_ref[:, cols], ck_ref[head_rows, :].astype(BF16)])
        values.append([_times(v_ref[:, cols]), _times(cv_ref[head_rows, :].astype(BF16))])

    def block(b):
        hh, c = divmod(b, n_chunks)
        return slice(c * Q_CHUNK, (c + 1) * Q_CHUNK), slice(hh * LANES, (hh + 1) * LANES)

    def store(b, o):
        o_ref[block(b)] = o.astype(BF16)

    _attend_pairs(LATENT_HEADS_PER_STEP * n_chunks,
                  lambda b: q_ref[block(b)],
                  lambda b, m, qm: [_dot_nt(qm, kpart) for kpart in keys[b // n_chunks]],
                  lambda b: values[b // n_chunks],
                  _diff_combine(lam, gsub_ref[...], lambda_init), store)


def _diff_latent_attention(q, k, v, cache_k, cache_v, layer_i, lam_params, gsub, lambda_init):
    blk = pl.BlockSpec((DEC_SEQ, LATENT_HEADS_PER_STEP * LANES), lambda b, h: (b, h))
    cache = pl.BlockSpec((None, None, PAST_LEN * N_LANE_BLOCKS, LANES), lambda b, h: (b, layer_i, 0, 0))
    small = [_resident((1, HEAD_DIM))] * 4 + [_resident((1, LANES))]
    return pl.pallas_call(
        functools.partial(_diff_latent_kernel, lambda_init),
        grid=(DEC_BATCH, N_LANE_BLOCKS // LATENT_HEADS_PER_STEP),
        in_specs=[blk, blk, blk, cache, cache] + small,
        out_specs=blk,
        out_shape=jax.ShapeDtypeStruct((N_SAMPLE_ROWS, D_MODEL), BF16),
        compiler_params=_params(2),
        name="diff_latent_attention",
    )(q, k, v, cache_k, cache_v, *lam_params, gsub)


NA_Q_ROWS = 8
NA_K_ROWS = 12
NA_Q_TOK = NA_Q_ROWS * GRID_W
NA_K_TOK = NA_K_ROWS * GRID_W
NA_GROUPS = GRID_ROWS // NA_Q_ROWS
NA_K_STEP = (GRID_ROWS - NA_K_ROWS) // (NA_GROUPS - 1)


N_DR = 2 * NA_KH - 1
N_DC = 2 * NA_KW - 1
N_DR_PAD = 16


def _expand_na_bias(rb_ref, tile_ref, bias_ref):
    qc = lax.broadcasted_iota(jnp.int32, (GRID_W, LANES), 0)
    lane = lax.broadcasted_iota(jnp.int32, (GRID_W, LANES), 1)
    kc = lane % GRID_W
    cs = jnp.clip(qc - NA_KW // 2, 0, GRID_W - NA_KW)
    col_ok = (kc >= cs) & (kc < cs + NA_KW)
    low = lane < GRID_W
    neg = jnp.full((GRID_W, LANES), NEG_INF, F32)
    for head in range(2 * LATENT_HEADS_PER_STEP):
        for dr in range(N_DR):
            row = jnp.broadcast_to(rb_ref[head, dr:dr + 1, :], (GRID_W, LANES))
            shifted = pltpu.roll(row, LANES - (NA_KW - 1), 1, stride=1, stride_axis=0)
            tile_ref[dr] = jnp.where(col_ok, shifted * LOG2_E, neg)
        for g in range(NA_GROUPS):
            for rq in range(NA_Q_ROWS):
                r = g * NA_Q_ROWS + rq
                rs = min(max(r - NA_KH // 2, 0), GRID_ROWS - NA_KH)

                def tile(kr, r=r, rs=rs):
                    if rs <= kr < rs + NA_KH:
                        return tile_ref[kr - r + NA_KH - 1]
                    return neg

                for p in range(NA_K_ROWS // 2):
                    kr = g * NA_K_STEP + 2 * p
                    bias_ref[head, g, rq * GRID_W:(rq + 1) * GRID_W, p * LANES:(p + 1) * LANES] = (
                        jnp.where(low, tile(kr), tile(kr + 1)))


def _na_latent_kernel(q_ref, k_ref, v_ref, ckt_ref, cvt_ref, rb_ref, o_ref, tile_ref, bias_ref):
    @pl.when(pl.program_id(1) == 0)
    def _():
        _expand_na_bias(rb_ref, tile_ref, bias_ref)

    ck_t = [ckt_ref[hh * LANES:(hh + 1) * LANES, :].astype(BF16) for hh in range(LATENT_HEADS_PER_STEP)]
    cv_t = [cvt_ref[hh * LANES:(hh + 1) * LANES, :].astype(BF16) for hh in range(LATENT_HEADS_PER_STEP)]

    def block(b):
        return divmod(b, NA_GROUPS)

    def cols(hh):
        return slice(hh * LANES, (hh + 1) * LANES)

    def qrows(g):
        return slice(g * NA_Q_TOK, (g + 1) * NA_Q_TOK)

    def krows(g):
        return slice(g * NA_K_STEP * GRID_W, g * NA_K_STEP * GRID_W + NA_K_TOK)

    def load_q(b):
        hh, g = block(b)
        return q_ref[qrows(g), cols(hh)]

    def scores(b, half, qm):
        hh, g = block(b)
        return [_dot_nt(qm, k_ref[krows(g), cols(hh)]) + bias_ref[2 * hh + half, g], _dot(qm, ck_t[hh])]

    def values(b):
        hh, g = block(b)
        return [_times(v_ref[krows(g), cols(hh)]), _times_t(cv_t[hh])]

    def store(b, o):
        hh, g = block(b)
        o_ref[qrows(g), cols(hh)] = o.astype(BF16)

    _attend_pairs(LATENT_HEADS_PER_STEP * NA_GROUPS, load_q, scores, values, _head_pair_combine, store)


def _na_latent_attention(q, k, v, cache_k_t, cache_v_t, layer_i, rel_bias_rows):
    width = LATENT_HEADS_PER_STEP * LANES
    blk = pl.BlockSpec((DEC_SEQ, width), lambda p, b: (b, p))
    cache = pl.BlockSpec((None, None, width, PAST_LEN), lambda p, b: (b, layer_i, p, 0))
    rb_spec = pl.BlockSpec((None, 2 * LATENT_HEADS_PER_STEP, N_DR_PAD, LANES), lambda p, b: (layer_i, p, 0, 0))
    return pl.pallas_call(
        _na_latent_kernel,
        grid=(N_LANE_BLOCKS // LATENT_HEADS_PER_STEP, DEC_BATCH),
        in_specs=[blk, blk, blk, cache, cache, rb_spec],
        out_specs=blk,
        out_shape=jax.ShapeDtypeStruct((N_SAMPLE_ROWS, D_MODEL), BF16),
        scratch_shapes=[pltpu.VMEM((N_DR, GRID_W, LANES), F32),
                        pltpu.VMEM((2 * LATENT_HEADS_PER_STEP, NA_GROUPS, NA_Q_TOK, NA_K_TOK), F32)],
        compiler_params=_params(2),
        name="na_latent_attention",
    )(q, k, v, cache_k_t, cache_v_t, rel_bias_rows)


def _padded_rel_bias(rel_bias_na):
    half = jnp.pad(rel_bias_na, ((0, 0), (0, 0), (0, N_DR_PAD - N_DR), (0, GRID_W - N_DC)))
    return jnp.concatenate([half, half], axis=-1)


FF_CHUNK = 1024


def _post_kernel(split_in, split_out, *refs):
    n_x = 2 if split_in else 1
    x_refs = refs[:n_x]
    op_ref, os_ref, mod_ref, g_ref, wo_ref, w1_ref, w2_ref = refs[n_x:n_x + 7]
    y_refs = refs[n_x + 7:]
    i = pl.program_id(0)
    r = _cond_row(i)
    is_prompt = i < N_PROMPT_TILES

    def mod(j):
        return mod_ref[pl.ds(r, 1), j * D_MODEL:(j + 1) * D_MODEL]

    x = jnp.where(is_prompt, x_refs[0][...], x_refs[1][...]) if split_in else x_refs[0][...]
    o = jnp.where(is_prompt, op_ref[...], os_ref[...])
    y1 = x + mod(2) * _dot(o, wo_ref[...])
    ms = jnp.mean(y1 * y1, axis=-1, keepdims=True)
    h = (y1 * lax.rsqrt(ms + EPS) * g_ref[...] * (1.0 + mod(4)) + mod(3)).astype(BF16)
    acc = jnp.zeros((ROW_TILE, D_MODEL), F32)
    for c in range(D_FF // FF_CHUNK):
        cols = slice(c * FF_CHUNK, (c + 1) * FF_CHUNK)
        hid = jnp.maximum(_dot(h, w1_ref[:, cols]), 0.0)
        acc = acc + _dot((hid * hid).astype(BF16), w2_ref[cols, :])
    y = y1 + mod(5) * acc
    if split_out:
        @pl.when(is_prompt)
        def _():
            y_refs[0][...] = y

        @pl.when(jnp.logical_not(is_prompt))
        def _():
            y_refs[1][...] = y
    else:
        y_refs[0][...] = y


def _post(xs, o_p, o_s, mod, layer, g_mlp, w_o, w_fc1, w_fc2, split_out, side_jobs):
    prompt_spec = pl.BlockSpec((ROW_TILE, D_MODEL), lambda i: (_prompt_tile(i), 0))
    sample_spec = pl.BlockSpec((ROW_TILE, D_MODEL), lambda i: (_sample_tile(i), 0))
    row_spec = pl.BlockSpec((ROW_TILE, D_MODEL), lambda i: (i, 0))
    split_in = len(xs) == 2
    if split_out:
        out_specs = [prompt_spec, sample_spec]
        out_shape = [jax.ShapeDtypeStruct((N_PROMPT_ROWS, D_MODEL), F32),
                     jax.ShapeDtypeStruct((N_SAMPLE_ROWS, D_MODEL), F32)]
    else:
        out_specs = [row_spec]
        out_shape = [jax.ShapeDtypeStruct((N_ROWS, D_MODEL), F32)]
    return _pallas_call_with_side_jobs(
        functools.partial(_post_kernel, split_in, split_out), side_jobs,
        grid=(N_ROW_TILES,),
        in_specs=([prompt_spec, sample_spec] if split_in else [row_spec]) + [
            prompt_spec, sample_spec,
            _resident((N_COND, 6 * D_MODEL)),
            _layer_slice((1, D_MODEL), layer),
            _resident((D_MODEL, D_MODEL)),
            _resident((D_MODEL, D_FF)),
            _resident((D_FF, D_MODEL)),
        ],
        out_specs=out_specs,
        out_shape=out_shape,
        args=[*xs, o_p, o_s, mod, g_mlp, w_o, w_fc1, w_fc2],
        name="wo_mlp",
    )


def _rope_tables():
    n_freq = HEAD_DIM // 4
    freqs = ROPE_BASE ** (-jnp.arange(n_freq, dtype=F32) / n_freq)
    t = jnp.arange(DEC_SEQ)
    ang_row = (t // GRID_W).astype(F32)[:, None] * freqs
    ang_col = (t % GRID_W).astype(F32)[:, None] * freqs
    cos64 = jnp.concatenate([jnp.cos(ang_row)] * 2 + [jnp.cos(ang_col)] * 2, axis=-1)
    sin64 = jnp.concatenate([-jnp.sin(ang_row), jnp.sin(ang_row),
                             -jnp.sin(ang_col), jnp.sin(ang_col)], axis=-1)
    return jnp.tile(cos64, (1, 2)), jnp.tile(sin64, (1, 2))


def _pair_gain(g):
    return jnp.tile(g.reshape(1, HEAD_DIM), (1, LANES // HEAD_DIM))


def _head_major_cache(cache):
    n_req, n_layers = cache.shape[:2]
    return jnp.transpose(cache, (0, 1, 3, 4, 2)).reshape(n_req, n_layers, D_MODEL, PAST_LEN)


def kernel(x_prompt, x_sample, cache_diff_k, cache_diff_v, cache_na_k, cache_na_v, c, c_ctx, w_ada, b_ada, norm_mix_g, norm_mlp_g, w_fc1, w_fc2, w_qkv_diff, w_o_diff, q_norm_diff_g, k_norm_diff_g, lambda_q1, lambda_k1, lambda_q2, lambda_k2, subln_g, w_qkv_na, w_o_na, q_norm_na_g, k_norm_na_g, rel_bias_na):
    cond = jnp.concatenate([c_ctx[None, :], c, jnp.zeros((N_COND - 1 - DEC_BATCH, D_MODEL), F32)], axis=0)
    b_ada = b_ada.reshape(DEPTH, 1, 6 * D_MODEL)
    mod = _modulation(cond, w_ada, b_ada, 0)
    cos_tab, sin_tab = _rope_tables()
    na_bias = _padded_rel_bias(rel_bias_na)
    cdk = cache_diff_k.reshape(DEC_BATCH, DEPTH // 2, PAST_LEN * N_LANE_BLOCKS, LANES)
    cdv = cache_diff_v.reshape(DEC_BATCH, DEPTH // 2, PAST_LEN * N_LANE_BLOCKS, LANES)
    cnk_t = _head_major_cache(cache_na_k)
    cnv_t = _head_major_cache(cache_na_v)
    g_mix = norm_mix_g.reshape(DEPTH, 1, D_MODEL)
    g_mlp = norm_mlp_g.reshape(DEPTH, 1, D_MODEL)
    w_qkv_f32 = (w_qkv_diff, w_qkv_na)
    w_o_f32 = (w_o_diff, w_o_na)
    q_gain = (q_norm_diff_g, q_norm_na_g)
    k_gain = (k_norm_diff_g, k_norm_na_g)

    xs = [x_prompt.reshape(N_PROMPT_ROWS, D_MODEL), x_sample.reshape(N_SAMPLE_ROWS, D_MODEL)]
    new_kv = [None, None]
    w_qkv = w_qkv_diff[0].astype(BF16)
    w_o = w1 = w2 = None
    n_qkv_tiles = N_PROMPT_TILES
    for l in range(DEPTH):
        i = l // 2
        kind = l % 2
        is_diff = kind == 0
        first = l == 0
        last = l == DEPTH - 1
        x_prompt_src, x_sample_src = (xs[0], xs[1]) if len(xs) == 2 else (xs[0], xs[0])
        sample_first_tile = 0 if len(xs) == 2 else N_PROMPT_TILES
        gains = (_pair_gain(q_gain[kind][i]), _pair_gain(k_gain[kind][i]))
        qp, kp, vp, new_k, new_v, *cast = _qkv(
            x_prompt_src, 0, True, mod, l, g_mix, w_qkv, i, *gains, None, is_diff, new_kv[kind],
            [_cast_job(w_fc1, l, 1, n_qkv_tiles)] if first else [])
        w1 = cast[0] if first else w1
        new_kv[kind] = (new_k, new_v)
        qs, ks, vs, *cast = _qkv(
            x_sample_src, sample_first_tile, False, mod, l, g_mix, w_qkv, i, *gains, (cos_tab, sin_tab),
            is_diff, None, [_cast_job(w_fc2, l, 0, n_qkv_tiles)] if first else [])
        w2 = cast[0] if first else w2
        lambda_init = 0.8 - 0.6 * math.exp(-0.3 * l)
        lam_params = [p[i if is_diff else 0].reshape(1, HEAD_DIM)
                      for p in (lambda_q1, lambda_k1, lambda_q2, lambda_k2)]
        gsub = subln_g[i if is_diff else 0].reshape(1, LANES)
        o_p, *cast = _ctx_attention(qp, kp, vp, lam_params, gsub, is_diff, lambda_init,
                                    [_cast_job(w_o_f32[kind], i, 0, BATCH)] if first else [])
        w_o = cast[0] if first else w_o
        if is_diff:
            o_s = _diff_latent_attention(qs, ks, vs, cdk, cdv, i, lam_params, gsub, lambda_init)
        else:
            o_s = _na_latent_attention(qs, ks, vs, cnk_t, cnv_t, i, na_bias)
        nxt, nxt_kind, nxt_i = l + 1, (l + 1) % 2, (l + 1) // 2
        next_jobs = [] if last else [_cast_job(w_qkv_f32[nxt_kind], nxt_i, 0, N_ROW_TILES),
                                     _cast_job(w_o_f32[nxt_kind], nxt_i, 0, N_ROW_TILES),
                                     _cast_job(w_fc1, nxt, 1, N_ROW_TILES),
                                     _cast_job(w_fc2, nxt, 0, N_ROW_TILES),
                                     _ada_job(cond, w_ada, b_ada, nxt, N_ROW_TILES)]
        outs = list(_post(xs, o_p, o_s, mod, l, g_mlp, w_o, w1, w2, last, next_jobs))
        if last:
            xp, xs = outs
        else:
            *xs, w_qkv, w_o, w1, w2, mod = outs

    n_kind = DEPTH // 2
    diff_k, diff_v = (a.reshape(BATCH, n_kind, SEQ, N_LANE_BLOCKS, LANES) for a in new_kv[0])
    na_k, na_v = (jnp.transpose(a.reshape(BATCH, n_kind, NA_HEADS, HEAD_DIM, SEQ), (0, 1, 4, 2, 3))
                  for a in new_kv[1])
    return (xp.reshape(BATCH, SEQ, D_MODEL), xs.reshape(DEC_BATCH, DEC_SEQ, D_MODEL),
            diff_k, diff_v, na_k, na_v)
```

```python
import functools
import math
from typing import Callable, NamedTuple

import jax
import jax.numpy as jnp
from jax import lax
from jax.experimental import pallas as pl
from jax.experimental.pallas import tpu as pltpu

D_MODEL = 1024
BATCH = 16
SEQ = 256
DEPTH = 4
DEC_BATCH = 4
DEC_SEQ = 1024
PAST_LEN = 256
GRID_W = 64
GRID_ROWS = DEC_SEQ // GRID_W
HEAD_DIM = 64
D_FF = 4 * D_MODEL
NA_HEADS = 16
NA_KH = 8
NA_KW = 16
ROPE_BASE = 10000.0
EPS = 1e-6
NEG_INF = -1e30
LOG2_E = math.log2(math.e)
SCORE_SCALE = HEAD_DIM ** -0.5 * LOG2_E

LANES = 128
SUBLANES = 8
N_LANE_BLOCKS = D_MODEL // LANES
N_PROMPT_ROWS = BATCH * SEQ
N_SAMPLE_ROWS = DEC_BATCH * DEC_SEQ
N_ROWS = N_PROMPT_ROWS + N_SAMPLE_ROWS
ROW_TILE = 512
N_ROW_TILES = N_ROWS // ROW_TILE
N_PROMPT_TILES = N_PROMPT_ROWS // ROW_TILE
TILES_PER_SAMPLE = DEC_SEQ // ROW_TILE
REQ_PER_TILE = ROW_TILE // SEQ
N_COND = 8
VMEM_LIMIT = 56 * 1024 * 1024

F32 = jnp.float32
BF16 = jnp.bfloat16
NT_DIMS = (((1,), (1,)), ((), ()))


def _dot(a, b):
    return jnp.dot(a, b, preferred_element_type=F32)


def _dot_nt(a, b):
    return lax.dot_general(a, b, NT_DIMS, preferred_element_type=F32)


def _params(n_axes):
    return pltpu.CompilerParams(dimension_semantics=("arbitrary",) * n_axes,
                                vmem_limit_bytes=VMEM_LIMIT)


def _resident(shape):
    return pl.BlockSpec(shape, lambda *_: (0,) * len(shape), pipeline_mode=pl.Buffered(1))


def _layer_slice(shape, layer):
    return pl.BlockSpec((None,) + shape, lambda *_: (layer,) + (0,) * len(shape),
                        pipeline_mode=pl.Buffered(1))


class _SideJob(NamedTuple):
    args: list
    in_specs: list
    out_spec: pl.BlockSpec
    out_shape: jax.ShapeDtypeStruct
    body: Callable


def _cast_body(src_ref, dst_ref):
    dst_ref[...] = src_ref[...].astype(BF16)


def _cast_job(weights, layer, split_axis, n_steps):
    _, rows, cols = weights.shape
    if split_axis == 0:
        band = (rows // n_steps, cols)
        src_spec = pl.BlockSpec((None,) + band, lambda i: (layer, i, 0))
        dst_spec = pl.BlockSpec(band, lambda i: (i, 0))
    else:
        band = (rows, cols // n_steps)
        src_spec = pl.BlockSpec((None,) + band, lambda i: (layer, 0, i))
        dst_spec = pl.BlockSpec(band, lambda i: (0, i))
    return _SideJob([weights], [src_spec], dst_spec, jax.ShapeDtypeStruct((rows, cols), BF16), _cast_body)


def _pallas_call_with_side_jobs(kernel_fn, jobs, *, grid, in_specs, out_specs, out_shape, args, name,
                                input_output_aliases=None):
    n_in, n_out = len(in_specs), len(out_specs)
    n_job_in = [len(job.args) for job in jobs]

    def kernel_and_jobs(*refs):
        ins, refs = refs[:n_in], refs[n_in:]
        job_ins, refs = refs[:sum(n_job_in)], refs[sum(n_job_in):]
        outs, refs = refs[:n_out], refs[n_out:]
        job_outs, scratch = refs[:len(jobs)], refs[len(jobs):]
        for job, n, out_ref in zip(jobs, n_job_in, job_outs):
            job.body(*job_ins[:n], out_ref)
            job_ins = job_ins[n:]
        kernel_fn(*ins, *outs, *scratch)

    return pl.pallas_call(
        kernel_and_jobs,
        grid=grid,
        in_specs=list(in_specs) + [spec for job in jobs for spec in job.in_specs],
        out_specs=list(out_specs) + [job.out_spec for job in jobs],
        out_shape=list(out_shape) + [job.out_shape for job in jobs],
        input_output_aliases=input_output_aliases or {},
        compiler_params=_params(1),
        name=name,
    )(*args, *(arg for job in jobs for arg in job.args))


def _prompt_tile(i):
    return jnp.minimum(i, N_PROMPT_TILES - 1)


def _sample_tile(i):
    return jnp.maximum(i - N_PROMPT_TILES, 0)


def _cond_row(i):
    return jnp.where(i < N_PROMPT_TILES, 0, 1 + (i - N_PROMPT_TILES) // TILES_PER_SAMPLE)


def _low_half_mask(shape):
    return lax.broadcasted_iota(jnp.int32, shape, len(shape) - 1) < HEAD_DIM


def _ada_kernel(cond_ref, w_ref, b_ref, out_ref):
    cond = cond_ref[...]
    act = (cond * jax.nn.sigmoid(cond)).astype(BF16)
    out_ref[...] = _dot(act, w_ref[...].astype(BF16)) + b_ref[...]


def _ada_specs(layer, n_steps):
    tn = 6 * D_MODEL // n_steps
    in_specs = [pl.BlockSpec((N_COND, D_MODEL), lambda j: (0, 0)),
                pl.BlockSpec((None, D_MODEL, tn), lambda j: (layer, 0, j)),
                pl.BlockSpec((None, 1, tn), lambda j: (layer, 0, j))]
    return in_specs, pl.BlockSpec((N_COND, tn), lambda j: (0, j)), jax.ShapeDtypeStruct((N_COND, 6 * D_MODEL), F32)


def _modulation(cond, w_ada, b_ada, layer):
    n_steps = 4
    in_specs, out_spec, out_shape = _ada_specs(layer, n_steps)
    return pl.pallas_call(_ada_kernel, grid=(n_steps,), in_specs=in_specs, out_specs=out_spec,
                          out_shape=out_shape, compiler_params=_params(1),
                          name="ada_modulation")(cond, w_ada, b_ada)


def _ada_job(cond, w_ada, b_ada, layer, n_steps):
    in_specs, out_spec, out_shape = _ada_specs(layer, n_steps)
    return _SideJob([cond, w_ada, b_ada], in_specs, out_spec, out_shape, _ada_kernel)


def _group_rms(x, gain):
    low = _low_half_mask((1, LANES))
    cols = []
    for c in range(N_LANE_BLOCKS):
        xb = x[:, c * LANES:(c + 1) * LANES]
        sq = xb * xb
        s_lo = jnp.sum(jnp.where(low, sq, 0.0), axis=-1, keepdims=True)
        s_hi = jnp.sum(jnp.where(low, 0.0, sq), axis=-1, keepdims=True)
        ms = jnp.where(low, s_lo, s_hi) * (1.0 / HEAD_DIM)
        cols.append(xb * lax.rsqrt(ms + EPS) * gain)
    return cols


def _rope(cols, cos, sin_signed):
    lane = lax.broadcasted_iota(jnp.int32, (1, LANES), 1)
    first = (lane % 32) < 16
    out = []
    for xb in cols:
        partner = jnp.where(first, pltpu.roll(xb, LANES - 16, 1), pltpu.roll(xb, 16, 1))
        out.append(xb * cos + partner * sin_signed)
    return out


def _project_qkv(x_ref, mod_ref, g_ref, w_ref, qg_ref, kg_ref, r):
    x = x_ref[...]
    ms = jnp.mean(x * x, axis=-1, keepdims=True)
    xn = x * lax.rsqrt(ms + EPS) * g_ref[...]
    shift = mod_ref[pl.ds(r, 1), 0:D_MODEL]
    scale = mod_ref[pl.ds(r, 1), D_MODEL:2 * D_MODEL]
    h = (xn * (1.0 + scale) + shift).astype(BF16)
    qkv = _dot(h, w_ref[...])
    qn = _group_rms(qkv[:, 0:D_MODEL], qg_ref[...])
    kn = _group_rms(qkv[:, D_MODEL:2 * D_MODEL], kg_ref[...])
    return qn, kn, qkv[:, 2 * D_MODEL:3 * D_MODEL]


def _lane_block(c):
    return slice(c * LANES, (c + 1) * LANES)


def _qkv_latent_kernel(is_diff, x_ref, mod_ref, g_ref, w_ref, qg_ref, kg_ref, cos_ref, sin_ref,
                       q_ref, k_ref, v_ref):
    r = 1 + pl.program_id(0) // TILES_PER_SAMPLE
    qn, kn, v = _project_qkv(x_ref, mod_ref, g_ref, w_ref, qg_ref, kg_ref, r)
    v_ref[...] = v.astype(BF16)
    if is_diff:
        cos = cos_ref[...]
        sin = sin_ref[...]
        qn = _rope(qn, cos, sin)
        kn = _rope(kn, cos, sin)
    for c in range(N_LANE_BLOCKS):
        q_ref[:, _lane_block(c)] = (qn[c] * SCORE_SCALE).astype(BF16)
        k_ref[:, _lane_block(c)] = kn[c].astype(BF16)


def _ctx_layer_kernel(is_diff, lambda_init, has_prev, *refs):
    if has_prev:
        refs = refs[2:]
    (x_ref, mod_ref, g_ref, w_ref, qg_ref, kg_ref, lq1_ref, lk1_ref, lq2_ref, lk2_ref, gsub_ref,
     o_ref, kf_ref, vf_ref) = refs
    qn, kn, v = _project_qkv(x_ref, mod_ref, g_ref, w_ref, qg_ref, kg_ref, 0)
    for c in range(N_LANE_BLOCKS):
        for bb in range(REQ_PER_TILE):
            rows = slice(bb * SEQ, (bb + 1) * SEQ)
            if is_diff:
                dst = pl.ds(c, SEQ, stride=N_LANE_BLOCKS)
                kf_ref[bb, dst, :] = kn[c][rows, :]
                vf_ref[bb, dst, :] = v[rows, _lane_block(c)]
            else:
                kf_ref[bb, _lane_block(c), :] = kn[c][rows, :].T
                vf_ref[bb, _lane_block(c), :] = v[rows, _lane_block(c)].T
    q = [(qn[c] * SCORE_SCALE).astype(BF16) for c in range(N_LANE_BLOCKS)]
    k = [kn[c].astype(BF16) for c in range(N_LANE_BLOCKS)]
    vb = [v[:, _lane_block(c)].astype(BF16) for c in range(N_LANE_BLOCKS)]
    if is_diff:
        lam = _diff_lambda(lq1_ref, lk1_ref, lq2_ref, lk2_ref, lambda_init)
        combine = _diff_combine(lam, gsub_ref[...], lambda_init)
    else:
        combine = _head_pair_combine

    def block(b):
        bb, c = divmod(b, N_LANE_BLOCKS)
        return slice(bb * SEQ, (bb + 1) * SEQ), c

    def load_q(b):
        rows, c = block(b)
        return q[c][rows, :]

    def scores(b, m, qm):
        rows, c = block(b)
        return [_dot_nt(qm, k[c][rows, :])]

    def values(b):
        rows, c = block(b)
        return [_times(vb[c][rows, :])]

    def store(b, o):
        rows, c = block(b)
        o_ref[rows, _lane_block(c)] = o.astype(BF16)

    _attend_pairs(REQ_PER_TILE * N_LANE_BLOCKS, load_q, scores, values, combine, store)


def _qkv_latent(x, first_tile, mod, layer, g_mix, w_qkv, q_gain, k_gain, rope_tabs, is_diff, side_jobs):
    n_tiles = N_ROW_TILES - N_PROMPT_TILES
    row_spec = pl.BlockSpec((ROW_TILE, D_MODEL), lambda i: (i, 0))
    rope_spec = pl.BlockSpec((ROW_TILE, LANES), lambda i: (i % TILES_PER_SAMPLE, 0))
    return _pallas_call_with_side_jobs(
        functools.partial(_qkv_latent_kernel, is_diff), side_jobs,
        grid=(n_tiles,),
        in_specs=[pl.BlockSpec((ROW_TILE, D_MODEL), lambda i: (first_tile + i, 0)),
                  _resident((N_COND, 6 * D_MODEL)),
                  _layer_slice((1, D_MODEL), layer),
                  _resident((D_MODEL, 3 * D_MODEL)),
                  _resident((1, LANES)), _resident((1, LANES)), rope_spec, rope_spec],
        out_specs=[row_spec] * 3,
        out_shape=[jax.ShapeDtypeStruct((N_SAMPLE_ROWS, D_MODEL), BF16)] * 3,
        args=[x, mod, g_mix, w_qkv, q_gain, k_gain, *rope_tabs],
        name="qkv_latent",
    )


def _ctx_layer(x, mod, layer, g_mix, w_qkv, kind_layer, q_gain, k_gain, lam_params, gsub, is_diff,
               lambda_init, prev, side_jobs):
    row_spec = pl.BlockSpec((ROW_TILE, D_MODEL), lambda i: (i, 0))
    if is_diff:
        cache_rows, cache_cols = SEQ * N_LANE_BLOCKS, LANES
    else:
        cache_rows, cache_cols = D_MODEL, SEQ
    cache_spec = pl.BlockSpec((REQ_PER_TILE, None, cache_rows, cache_cols), lambda i: (i, kind_layer, 0, 0))
    cache_shape = jax.ShapeDtypeStruct((BATCH, DEPTH // 2, cache_rows, cache_cols), F32)
    in_specs = ([row_spec, _resident((N_COND, 6 * D_MODEL)), _layer_slice((1, D_MODEL), layer),
                 _resident((D_MODEL, 3 * D_MODEL)), _resident((1, LANES)), _resident((1, LANES))]
                + [_resident((1, HEAD_DIM))] * 4 + [_resident((1, LANES))])
    args = [x, mod, g_mix, w_qkv, q_gain, k_gain, *lam_params, gsub]
    aliases = {}
    if prev is not None:
        in_specs = [pl.BlockSpec(memory_space=pl.ANY)] * 2 + in_specs
        args = list(prev) + args
        aliases = {0: 1, 1: 2}
    return _pallas_call_with_side_jobs(
        functools.partial(_ctx_layer_kernel, is_diff, lambda_init, prev is not None), side_jobs,
        grid=(N_PROMPT_TILES,),
        in_specs=in_specs,
        out_specs=[row_spec, cache_spec, cache_spec],
        out_shape=[jax.ShapeDtypeStruct((N_PROMPT_ROWS, D_MODEL), BF16), cache_shape, cache_shape],
        args=args,
        input_output_aliases=aliases,
        name="ctx_layer",
    )


def _softmax_pv(score_parts, value_parts):
    m = None
    for s in score_parts:
        pm = jnp.max(s, axis=-1, keepdims=True)
        m = pm if m is None else jnp.maximum(m, pm)
    acc = None
    for s, times_v in zip(score_parts, value_parts):
        pv = times_v(jnp.exp2(s - m).astype(BF16))
        acc = pv if acc is None else acc + pv
    return acc[:, :LANES] * (1.0 / acc[:, LANES:])


def _times(v):
    v1 = jnp.concatenate([v, jnp.ones_like(v)], axis=1)
    return lambda w: _dot(w, v1)


def _times_t(v_t):
    v1_t = jnp.concatenate([v_t, jnp.ones_like(v_t)], axis=0)
    return lambda w: _dot_nt(w, v1_t)


def _diff_lambda(lq1_ref, lk1_ref, lq2_ref, lk2_ref, lambda_init):
    e1 = jnp.exp(jnp.sum(lq1_ref[...] * lk1_ref[...], axis=-1, keepdims=True))
    e2 = jnp.exp(jnp.sum(lq2_ref[...] * lk2_ref[...], axis=-1, keepdims=True))
    return e1 - e2 + lambda_init


def _attend_pairs(n_blocks, load_q, scores, values, combine, store):
    units = [(b, m) for b in range(n_blocks) for m in range(2)]

    def unit_scores(u):
        b, m = units[u]
        q = load_q(b)
        low = _low_half_mask(q.shape)
        zero = jnp.zeros_like(q)
        return scores(b, m, jnp.where(low, q, zero) if m == 0 else jnp.where(low, zero, q))

    s = unit_scores(0)
    first = None
    for u, (b, m) in enumerate(units):
        s_next = unit_scores(u + 1) if u + 1 < len(units) else None
        o = _softmax_pv(s, values(b))
        s = s_next
        if m == 0:
            first = o
        else:
            store(b, combine(first, o))


def _diff_combine(lam, gsub, lambda_init):
    def combine(o1, o2):
        o = o1 - lam * o2
        ms = jnp.mean(o * o, axis=-1, keepdims=True)
        return o * lax.rsqrt(ms + EPS) * gsub * (1.0 - lambda_init)
    return combine


def _head_pair_combine(o_lo, o_hi):
    return jnp.where(_low_half_mask(o_lo.shape), o_lo, o_hi)


Q_CHUNK = 512
LATENT_HEADS_PER_STEP = 2


def _diff_latent_kernel(lambda_init, q_ref, k_ref, v_ref, ck_ref, cv_ref,
                        lq1_ref, lk1_ref, lq2_ref, lk2_ref, gsub_ref, o_ref):
    lam = _diff_lambda(lq1_ref, lk1_ref, lq2_ref, lk2_ref, lambda_init)
    n_chunks = DEC_SEQ // Q_CHUNK
    keys, values = [], []
    for hh in range(LATENT_HEADS_PER_STEP):
        cols = slice(hh * LANES, (hh + 1) * LANES)
        head = pl.program_id(1) * LATENT_HEADS_PER_STEP + hh
        head_rows = pl.ds(head, PAST_LEN, stride=N_LANE_BLOCKS)
        keys.append([k_ref[:, cols], ck_ref[head_rows, :].astype(BF16)])
        values.append([_times(v_ref[:, cols]), _times(cv_ref[head_rows, :].astype(BF16))])

    def block(b):
        hh, c = divmod(b, n_chunks)
        return slice(c * Q_CHUNK, (c + 1) * Q_CHUNK), slice(hh * LANES, (hh + 1) * LANES)

    def store(b, o):
        o_ref[block(b)] = o.astype(BF16)

    _attend_pairs(LATENT_HEADS_PER_STEP * n_chunks,
                  lambda b: q_ref[block(b)],
                  lambda b, m, qm: [_dot_nt(qm, kpart) for kpart in keys[b // n_chunks]],
                  lambda b: values[b // n_chunks],
                  _diff_combine(lam, gsub_ref[...], lambda_init), store)


def _diff_latent_attention(q, k, v, cache_k, cache_v, layer_i, lam_params, gsub, lambda_init):
    blk = pl.BlockSpec((DEC_SEQ, LATENT_HEADS_PER_STEP * LANES), lambda b, h: (b, h))
    cache = pl.BlockSpec((None, None, PAST_LEN * N_LANE_BLOCKS, LANES), lambda b, h: (b, layer_i, 0, 0))
    small = [_resident((1, HEAD_DIM))] * 4 + [_resident((1, LANES))]
    return pl.pallas_call(
        functools.partial(_diff_latent_kernel, lambda_init),
        grid=(DEC_BATCH, N_LANE_BLOCKS // LATENT_HEADS_PER_STEP),
        in_specs=[blk, blk, blk, cache, cache] + small,
        out_specs=blk,
        out_shape=jax.ShapeDtypeStruct((N_SAMPLE_ROWS, D_MODEL), BF16),
        compiler_params=_params(2),
        name="diff_latent_attention",
    )(q, k, v, cache_k, cache_v, *lam_params, gsub)


NA_Q_ROWS = 8
NA_K_ROWS = 12
NA_Q_TOK = NA_Q_ROWS * GRID_W
NA_K_TOK = NA_K_ROWS * GRID_W
NA_GROUPS = GRID_ROWS // NA_Q_ROWS
NA_K_STEP = (GRID_ROWS - NA_K_ROWS) // (NA_GROUPS - 1)


N_DR = 2 * NA_KH - 1
N_DC = 2 * NA_KW - 1
N_DR_PAD = 16


def _expand_na_bias(rb_ref, tile_ref, bias_ref):
    qc = lax.broadcasted_iota(jnp.int32, (GRID_W, LANES), 0)
    lane = lax.broadcasted_iota(jnp.int32, (GRID_W, LANES), 1)
    kc = lane % GRID_W
    cs = jnp.clip(qc - NA_KW // 2, 0, GRID_W - NA_KW)
    col_ok = (kc >= cs) & (kc < cs + NA_KW)
    low = lane < GRID_W
    neg = jnp.full((GRID_W, LANES), NEG_INF, F32)
    for head in range(2 * LATENT_HEADS_PER_STEP):
        for dr in range(N_DR):
            row = jnp.broadcast_to(rb_ref[head, dr:dr + 1, :], (GRID_W, LANES))
            shifted = pltpu.roll(row, LANES - (NA_KW - 1), 1, stride=1, stride_axis=0)
            tile_ref[dr] = jnp.where(col_ok, shifted * LOG2_E, neg)
        for g in range(NA_GROUPS):
            for rq in range(NA_Q_ROWS):
                r = g * NA_Q_ROWS + rq
                rs = min(max(r - NA_KH // 2, 0), GRID_ROWS - NA_KH)

                def tile(kr, r=r, rs=rs):
                    if rs <= kr < rs + NA_KH:
                        return tile_ref[kr - r + NA_KH - 1]
                    return neg

                for p in range(NA_K_ROWS // 2):
                    kr = g * NA_K_STEP + 2 * p
                    bias_ref[head, g, rq * GRID_W:(rq + 1) * GRID_W, p * LANES:(p + 1) * LANES] = (
                        jnp.where(low, tile(kr), tile(kr + 1)))


def _na_latent_kernel(q_ref, k_ref, v_ref, ckt_ref, cvt_ref, rb_ref, o_ref, tile_ref, bias_ref):
    @pl.when(pl.program_id(1) == 0)
    def _():
        _expand_na_bias(rb_ref, tile_ref, bias_ref)

    ck_t = [ckt_ref[hh * LANES:(hh + 1) * LANES, :].astype(BF16) for hh in range(LATENT_HEADS_PER_STEP)]
    cv_t = [cvt_ref[hh * LANES:(hh + 1) * LANES, :].astype(BF16) for hh in range(LATENT_HEADS_PER_STEP)]

    def block(b):
        return divmod(b, NA_GROUPS)

    def cols(hh):
        return slice(hh * LANES, (hh + 1) * LANES)

    def qrows(g):
        return slice(g * NA_Q_TOK, (g + 1) * NA_Q_TOK)

    def krows(g):
        return slice(g * NA_K_STEP * GRID_W, g * NA_K_STEP * GRID_W + NA_K_TOK)

    def load_q(b):
        hh, g = block(b)
        return q_ref[qrows(g), cols(hh)]

    def scores(b, half, qm):
        hh, g = block(b)
        return [_dot_nt(qm, k_ref[krows(g), cols(hh)]) + bias_ref[2 * hh + half, g], _dot(qm, ck_t[hh])]

    def values(b):
        hh, g = block(b)
        return [_times(v_ref[krows(g), cols(hh)]), _times_t(cv_t[hh])]

    def store(b, o):
        hh, g = block(b)
        o_ref[qrows(g), cols(hh)] = o.astype(BF16)

    _attend_pairs(LATENT_HEADS_PER_STEP * NA_GROUPS, load_q, scores, values, _head_pair_combine, store)


def _na_latent_attention(q, k, v, cache_k_t, cache_v_t, layer_i, rel_bias_rows):
    width = LATENT_HEADS_PER_STEP * LANES
    blk = pl.BlockSpec((DEC_SEQ, width), lambda p, b: (b, p))
    cache = pl.BlockSpec((None, None, width, PAST_LEN), lambda p, b: (b, layer_i, p, 0))
    rb_spec = pl.BlockSpec((None, 2 * LATENT_HEADS_PER_STEP, N_DR_PAD, LANES), lambda p, b: (layer_i, p, 0, 0))
    return pl.pallas_call(
        _na_latent_kernel,
        grid=(N_LANE_BLOCKS // LATENT_HEADS_PER_STEP, DEC_BATCH),
        in_specs=[blk, blk, blk, cache, cache, rb_spec],
        out_specs=blk,
        out_shape=jax.ShapeDtypeStruct((N_SAMPLE_ROWS, D_MODEL), BF16),
        scratch_shapes=[pltpu.VMEM((N_DR, GRID_W, LANES), F32),
                        pltpu.VMEM((2 * LATENT_HEADS_PER_STEP, NA_GROUPS, NA_Q_TOK, NA_K_TOK), F32)],
        compiler_params=_params(2),
        name="na_latent_attention",
    )(q, k, v, cache_k_t, cache_v_t, rel_bias_rows)


def _padded_rel_bias(rel_bias_na):
    half = jnp.pad(rel_bias_na, ((0, 0), (0, 0), (0, N_DR_PAD - N_DR), (0, GRID_W - N_DC)))
    return jnp.concatenate([half, half], axis=-1)


FF_CHUNK = 1024


def _post_kernel(split_in, split_out, *refs):
    n_x = 2 if split_in else 1
    x_refs = refs[:n_x]
    op_ref, os_ref, mod_ref, g_ref, wo_ref, w1_ref, w2_ref = refs[n_x:n_x + 7]
    y_refs = refs[n_x + 7:]
    i = pl.program_id(0)
    r = _cond_row(i)
    is_prompt = i < N_PROMPT_TILES

    def mod(j):
        return mod_ref[pl.ds(r, 1), j * D_MODEL:(j + 1) * D_MODEL]

    x = jnp.where(is_prompt, x_refs[0][...], x_refs[1][...]) if split_in else x_refs[0][...]
    o = jnp.where(is_prompt, op_ref[...], os_ref[...])
    y1 = x + mod(2) * _dot(o, wo_ref[...])
    ms = jnp.mean(y1 * y1, axis=-1, keepdims=True)
    h = (y1 * lax.rsqrt(ms + EPS) * g_ref[...] * (1.0 + mod(4)) + mod(3)).astype(BF16)
    acc = jnp.zeros((ROW_TILE, D_MODEL), F32)
    for c in range(D_FF // FF_CHUNK):
        cols = slice(c * FF_CHUNK, (c + 1) * FF_CHUNK)
        hid = jnp.maximum(_dot(h, w1_ref[:, cols]), 0.0)
        acc = acc + _dot((hid * hid).astype(BF16), w2_ref[cols, :])
    y = y1 + mod(5) * acc
    if split_out:
        @pl.when(is_prompt)
        def _():
            y_refs[0][...] = y

        @pl.when(jnp.logical_not(is_prompt))
        def _():
            y_refs[1][...] = y
    else:
        y_refs[0][...] = y


def _post(xs, o_p, o_s, mod, layer, g_mlp, w_o, w_fc1, w_fc2, split_out, side_jobs):
    prompt_spec = pl.BlockSpec((ROW_TILE, D_MODEL), lambda i: (_prompt_tile(i), 0))
    sample_spec = pl.BlockSpec((ROW_TILE, D_MODEL), lambda i: (_sample_tile(i), 0))
    row_spec = pl.BlockSpec((ROW_TILE, D_MODEL), lambda i: (i, 0))
    split_in = len(xs) == 2
    if split_out:
        out_specs = [prompt_spec, sample_spec]
        out_shape = [jax.ShapeDtypeStruct((N_PROMPT_ROWS, D_MODEL), F32),
                     jax.ShapeDtypeStruct((N_SAMPLE_ROWS, D_MODEL), F32)]
    else:
        out_specs = [row_spec]
        out_shape = [jax.ShapeDtypeStruct((N_ROWS, D_MODEL), F32)]
    return _pallas_call_with_side_jobs(
        functools.partial(_post_kernel, split_in, split_out), side_jobs,
        grid=(N_ROW_TILES,),
        in_specs=([prompt_spec, sample_spec] if split_in else [row_spec]) + [
            prompt_spec, sample_spec,
            _resident((N_COND, 6 * D_MODEL)),
            _layer_slice((1, D_MODEL), layer),
            _resident((D_MODEL, D_MODEL)),
            _resident((D_MODEL, D_FF)),
            _resident((D_FF, D_MODEL)),
        ],
        out_specs=out_specs,
        out_shape=out_shape,
        args=[*xs, o_p, o_s, mod, g_mlp, w_o, w_fc1, w_fc2],
        name="wo_mlp",
    )


def _rope_tables():
    n_freq = HEAD_DIM // 4
    freqs = ROPE_BASE ** (-jnp.arange(n_freq, dtype=F32) / n_freq)
    t = jnp.arange(DEC_SEQ)
    ang_row = (t // GRID_W).astype(F32)[:, None] * freqs
    ang_col = (t % GRID_W).astype(F32)[:, None] * freqs
    cos64 = jnp.concatenate([jnp.cos(ang_row)] * 2 + [jnp.cos(ang_col)] * 2, axis=-1)
    sin64 = jnp.concatenate([-jnp.sin(ang_row), jnp.sin(ang_row),
                             -jnp.sin(ang_col), jnp.sin(ang_col)], axis=-1)
    return jnp.tile(cos64, (1, 2)), jnp.tile(sin64, (1, 2))


def _pair_gain(g):
    return jnp.tile(g.reshape(1, HEAD_DIM), (1, LANES // HEAD_DIM))


def _head_major_cache(cache):
    n_req, n_layers = cache.shape[:2]
    return jnp.transpose(cache, (0, 1, 3, 4, 2)).reshape(n_req, n_layers, D_MODEL, PAST_LEN)


def kernel(x_prompt, x_sample, cache_diff_k, cache_diff_v, cache_na_k, cache_na_v, c, c_ctx, w_ada, b_ada, norm_mix_g, norm_mlp_g, w_fc1, w_fc2, w_qkv_diff, w_o_diff, q_norm_diff_g, k_norm_diff_g, lambda_q1, lambda_k1, lambda_q2, lambda_k2, subln_g, w_qkv_na, w_o_na, q_norm_na_g, k_norm_na_g, rel_bias_na):
    cond = jnp.concatenate([c_ctx[None, :], c, jnp.zeros((N_COND - 1 - DEC_BATCH, D_MODEL), F32)], axis=0)
    b_ada = b_ada.reshape(DEPTH, 1, 6 * D_MODEL)
    mod = _modulation(cond, w_ada, b_ada, 0)
    cos_tab, sin_tab = _rope_tables()
    na_bias = _padded_rel_bias(rel_bias_na)
    cdk = cache_diff_k.reshape(DEC_BATCH, DEPTH // 2, PAST_LEN * N_LANE_BLOCKS, LANES)
    cdv = cache_diff_v.reshape(DEC_BATCH, DEPTH // 2, PAST_LEN * N_LANE_BLOCKS, LANES)
    cnk_t = _head_major_cache(cache_na_k)
    cnv_t = _head_major_cache(cache_na_v)
    g_mix = norm_mix_g.reshape(DEPTH, 1, D_MODEL)
    g_mlp = norm_mlp_g.reshape(DEPTH, 1, D_MODEL)
    w_qkv_f32 = (w_qkv_diff, w_qkv_na)
    w_o_f32 = (w_o_diff, w_o_na)
    q_gain = (q_norm_diff_g, q_norm_na_g)
    k_gain = (k_norm_diff_g, k_norm_na_g)

    xs = [x_prompt.reshape(N_PROMPT_ROWS, D_MODEL), x_sample.reshape(N_SAMPLE_ROWS, D_MODEL)]
    new_kv = [None, None]
    w_qkv = w_qkv_diff[0].astype(BF16)
    w_o = w1 = w2 = None
    n_qkv_tiles = N_PROMPT_TILES
    for l in range(DEPTH):
        i = l // 2
        kind = l % 2
        is_diff = kind == 0
        first = l == 0
        last = l == DEPTH - 1
        x_prompt_src, x_sample_src = (xs[0], xs[1]) if len(xs) == 2 else (xs[0], xs[0])
        sample_first_tile = 0 if len(xs) == 2 else N_PROMPT_TILES
        gains = (_pair_gain(q_gain[kind][i]), _pair_gain(k_gain[kind][i]))
        lambda_init = 0.8 - 0.6 * math.exp(-0.3 * l)
        lam_params = [p[i if is_diff else 0].reshape(1, HEAD_DIM)
                      for p in (lambda_q1, lambda_k1, lambda_q2, lambda_k2)]
        gsub = subln_g[i if is_diff else 0].reshape(1, LANES)
        o_p, new_k, new_v, *cast = _ctx_layer(
            x_prompt_src, mod, l, g_mix, w_qkv, i, *gains, lam_params, gsub, is_diff, lambda_init,
            new_kv[kind],
            [_cast_job(w_fc1, l, 1, n_qkv_tiles), _cast_job(w_o_f32[kind], i, 0, n_qkv_tiles)] if first else [])
        w1, w_o = cast if first else (w1, w_o)
        new_kv[kind] = (new_k, new_v)
        qs, ks, vs, *cast = _qkv_latent(
            x_sample_src, sample_first_tile, mod, l, g_mix, w_qkv, *gains, (cos_tab, sin_tab), is_diff,
            [_cast_job(w_fc2, l, 0, n_qkv_tiles)] if first else [])
        w2 = cast[0] if first else w2
        if is_diff:
            o_s = _diff_latent_attention(qs, ks, vs, cdk, cdv, i, lam_params, gsub, lambda_init)
        else:
            o_s = _na_latent_attention(qs, ks, vs, cnk_t, cnv_t, i, na_bias)
        nxt, nxt_kind, nxt_i = l + 1, (l + 1) % 2, (l + 1) // 2
        next_jobs = [] if last else [_cast_job(w_qkv_f32[nxt_kind], nxt_i, 0, N_ROW_TILES),
                                     _cast_job(w_o_f32[nxt_kind], nxt_i, 0, N_ROW_TILES),
                                     _cast_job(w_fc1, nxt, 1, N_ROW_TILES),
                                     _cast_job(w_fc2, nxt, 0, N_ROW_TILES),
                                     _ada_job(cond, w_ada, b_ada, nxt, N_ROW_TILES)]
        outs = list(_post(xs, o_p, o_s, mod, l, g_mlp, w_o, w1, w2, last, next_jobs))
        if last:
            xp, xs = outs
        else:
            *xs, w_qkv, w_o, w1, w2, mod = outs

    n_kind = DEPTH // 2
    diff_k, diff_v = (a.reshape(BATCH, n_kind, SEQ, N_LANE_BLOCKS, LANES) for a in new_kv[0])
    na_k, na_v = (jnp.transpose(a.reshape(BATCH, n_kind, NA_HEADS, HEAD_DIM, SEQ), (0, 1, 4, 2, 3))
                  for a in new_kv[1])
    return (xp.reshape(BATCH, SEQ, D_MODEL), xs.reshape(DEC_BATCH, DEC_SEQ, D_MODEL),
            diff_k, diff_v, na_k, na_v)
```

```python
import functools
import math
from typing import Callable, NamedTuple

import jax
import jax.numpy as jnp
from jax import lax
from jax.experimental import pallas as pl
from jax.experimental.pallas import tpu as pltpu

D_MODEL = 1024
BATCH = 16
SEQ = 256
DEPTH = 4
DEC_BATCH = 4
DEC_SEQ = 1024
PAST_LEN = 256
GRID_W = 64
GRID_ROWS = DEC_SEQ // GRID_W
HEAD_DIM = 64
D_FF = 4 * D_MODEL
NA_HEADS = 16
NA_KH = 8
NA_KW = 16
ROPE_BASE = 10000.0
EPS = 1e-6
NEG_INF = -1e30
LOG2_E = math.log2(math.e)
SCORE_SCALE = HEAD_DIM ** -0.5 * LOG2_E

LANES = 128
SUBLANES = 8
N_LANE_BLOCKS = D_MODEL // LANES
N_PROMPT_ROWS = BATCH * SEQ
N_SAMPLE_ROWS = DEC_BATCH * DEC_SEQ
N_ROWS = N_PROMPT_ROWS + N_SAMPLE_ROWS
ROW_TILE = 512
N_ROW_TILES = N_ROWS // ROW_TILE
N_PROMPT_TILES = N_PROMPT_ROWS // ROW_TILE
TILES_PER_SAMPLE = DEC_SEQ // ROW_TILE
REQ_PER_TILE = ROW_TILE // SEQ
N_COND = 8
VMEM_LIMIT = 56 * 1024 * 1024

F32 = jnp.float32
BF16 = jnp.bfloat16
NT_DIMS = (((1,), (1,)), ((), ()))


def _dot(a, b):
    return jnp.dot(a, b, preferred_element_type=F32)


def _dot_nt(a, b):
    return lax.dot_general(a, b, NT_DIMS, preferred_element_type=F32)


def _params(n_axes):
    return pltpu.CompilerParams(dimension_semantics=("arbitrary",) * n_axes,
                                vmem_limit_bytes=VMEM_LIMIT)


def _resident(shape):
    return pl.BlockSpec(shape, lambda *_: (0,) * len(shape), pipeline_mode=pl.Buffered(1))


def _layer_slice(shape, layer):
    return pl.BlockSpec((None,) + shape, lambda *_: (layer,) + (0,) * len(shape),
                        pipeline_mode=pl.Buffered(1))


class _SideJob(NamedTuple):
    args: list
    in_specs: list
    out_spec: pl.BlockSpec
    out_shape: jax.ShapeDtypeStruct
    body: Callable


def _cast_body(src_ref, dst_ref):
    dst_ref[...] = src_ref[...].astype(BF16)


def _cast_job(weights, layer, split_axis, n_steps):
    _, rows, cols = weights.shape
    if split_axis == 0:
        band = (rows // n_steps, cols)
        src_spec = pl.BlockSpec((None,) + band, lambda i: (layer, i, 0))
        dst_spec = pl.BlockSpec(band, lambda i: (i, 0))
    else:
        band = (rows, cols // n_steps)
        src_spec = pl.BlockSpec((None,) + band, lambda i: (layer, 0, i))
        dst_spec = pl.BlockSpec(band, lambda i: (0, i))
    return _SideJob([weights], [src_spec], dst_spec, jax.ShapeDtypeStruct((rows, cols), BF16), _cast_body)


def _pallas_call_with_side_jobs(kernel_fn, jobs, *, grid, in_specs, out_specs, out_shape, args, name,
                                input_output_aliases=None):
    n_in, n_out = len(in_specs), len(out_specs)
    n_job_in = [len(job.args) for job in jobs]

    def kernel_and_jobs(*refs):
        ins, refs = refs[:n_in], refs[n_in:]
        job_ins, refs = refs[:sum(n_job_in)], refs[sum(n_job_in):]
        outs, refs = refs[:n_out], refs[n_out:]
        job_outs, scratch = refs[:len(jobs)], refs[len(jobs):]
        for job, n, out_ref in zip(jobs, n_job_in, job_outs):
            job.body(*job_ins[:n], out_ref)
            job_ins = job_ins[n:]
        kernel_fn(*ins, *outs, *scratch)

    return pl.pallas_call(
        kernel_and_jobs,
        grid=grid,
        in_specs=list(in_specs) + [spec for job in jobs for spec in job.in_specs],
        out_specs=list(out_specs) + [job.out_spec for job in jobs],
        out_shape=list(out_shape) + [job.out_shape for job in jobs],
        input_output_aliases=input_output_aliases or {},
        compiler_params=_params(1),
        name=name,
    )(*args, *(arg for job in jobs for arg in job.args))


def _prompt_tile(i):
    return jnp.minimum(i, N_PROMPT_TILES - 1)


def _sample_tile(i):
    return jnp.maximum(i - N_PROMPT_TILES, 0)


def _cond_row(i):
    return jnp.where(i < N_PROMPT_TILES, 0, 1 + (i - N_PROMPT_TILES) // TILES_PER_SAMPLE)


def _low_half_mask(shape):
    return lax.broadcasted_iota(jnp.int32, shape, len(shape) - 1) < HEAD_DIM


def _ada_kernel(cond_ref, w_ref, b_ref, out_ref):
    cond = cond_ref[...]
    act = (cond * jax.nn.sigmoid(cond)).astype(BF16)
    out_ref[...] = _dot(act, w_ref[...].astype(BF16)) + b_ref[...]


def _ada_specs(layer, n_steps):
    tn = 6 * D_MODEL // n_steps
    in_specs = [pl.BlockSpec((N_COND, D_MODEL), lambda j: (0, 0)),
                pl.BlockSpec((None, D_MODEL, tn), lambda j: (layer, 0, j)),
                pl.BlockSpec((None, 1, tn), lambda j: (layer, 0, j))]
    return in_specs, pl.BlockSpec((N_COND, tn), lambda j: (0, j)), jax.ShapeDtypeStruct((N_COND, 6 * D_MODEL), F32)


def _modulation(cond, w_ada, b_ada, layer):
    n_steps = 4
    in_specs, out_spec, out_shape = _ada_specs(layer, n_steps)
    return pl.pallas_call(_ada_kernel, grid=(n_steps,), in_specs=in_specs, out_specs=out_spec,
                          out_shape=out_shape, compiler_params=_params(1),
                          name="ada_modulation")(cond, w_ada, b_ada)


def _ada_job(cond, w_ada, b_ada, layer, n_steps):
    in_specs, out_spec, out_shape = _ada_specs(layer, n_steps)
    return _SideJob([cond, w_ada, b_ada], in_specs, out_spec, out_shape, _ada_kernel)


def _group_rms(x, gain):
    low = _low_half_mask((1, LANES))
    cols = []
    for c in range(N_LANE_BLOCKS):
        xb = x[:, c * LANES:(c + 1) * LANES]
        sq = xb * xb
        s_lo = jnp.sum(jnp.where(low, sq, 0.0), axis=-1, keepdims=True)
        s_hi = jnp.sum(jnp.where(low, 0.0, sq), axis=-1, keepdims=True)
        ms = jnp.where(low, s_lo, s_hi) * (1.0 / HEAD_DIM)
        cols.append(xb * lax.rsqrt(ms + EPS) * gain)
    return cols


def _rope(cols, cos, sin_signed):
    lane = lax.broadcasted_iota(jnp.int32, (1, LANES), 1)
    first = (lane % 32) < 16
    out = []
    for xb in cols:
        partner = jnp.where(first, pltpu.roll(xb, LANES - 16, 1), pltpu.roll(xb, 16, 1))
        out.append(xb * cos + partner * sin_signed)
    return out


def _project_qkv(x_ref, mod_ref, g_ref, w_ref, qg_ref, kg_ref, r):
    x = x_ref[...]
    ms = jnp.mean(x * x, axis=-1, keepdims=True)
    xn = x * lax.rsqrt(ms + EPS) * g_ref[...]
    shift = mod_ref[pl.ds(r, 1), 0:D_MODEL]
    scale = mod_ref[pl.ds(r, 1), D_MODEL:2 * D_MODEL]
    h = (xn * (1.0 + scale) + shift).astype(BF16)
    qkv = _dot(h, w_ref[...])
    qn = _group_rms(qkv[:, 0:D_MODEL], qg_ref[...])
    kn = _group_rms(qkv[:, D_MODEL:2 * D_MODEL], kg_ref[...])
    return qn, kn, qkv[:, 2 * D_MODEL:3 * D_MODEL]


def _lane_block(c):
    return slice(c * LANES, (c + 1) * LANES)


def _qkv_latent_kernel(is_diff, x_ref, mod_ref, g_ref, w_ref, qg_ref, kg_ref, cos_ref, sin_ref,
                       q_ref, k_ref, v_ref):
    r = 1 + pl.program_id(0) // TILES_PER_SAMPLE
    qn, kn, v = _project_qkv(x_ref, mod_ref, g_ref, w_ref, qg_ref, kg_ref, r)
    v_ref[...] = v.astype(BF16)
    if is_diff:
        cos = cos_ref[...]
        sin = sin_ref[...]
        qn = _rope(qn, cos, sin)
        kn = _rope(kn, cos, sin)
    for c in range(N_LANE_BLOCKS):
        q_ref[:, _lane_block(c)] = (qn[c] * SCORE_SCALE).astype(BF16)
        k_ref[:, _lane_block(c)] = kn[c].astype(BF16)


def _ctx_layer_kernel(is_diff, lambda_init, has_prev, *refs):
    if has_prev:
        refs = refs[2:]
    (x_ref, mod_ref, g_ref, w_ref, qg_ref, kg_ref, lq1_ref, lk1_ref, lq2_ref, lk2_ref, gsub_ref,
     o_ref, kf_ref, vf_ref) = refs
    qn, kn, v = _project_qkv(x_ref, mod_ref, g_ref, w_ref, qg_ref, kg_ref, 0)
    for c in range(N_LANE_BLOCKS):
        for bb in range(REQ_PER_TILE):
            rows = slice(bb * SEQ, (bb + 1) * SEQ)
            if is_diff:
                dst = pl.ds(c, SEQ, stride=N_LANE_BLOCKS)
                kf_ref[bb, dst, :] = kn[c][rows, :]
                vf_ref[bb, dst, :] = v[rows, _lane_block(c)]
            else:
                kf_ref[bb, _lane_block(c), :] = kn[c][rows, :].T
                vf_ref[bb, _lane_block(c), :] = v[rows, _lane_block(c)].T
    q = [(qn[c] * SCORE_SCALE).astype(BF16) for c in range(N_LANE_BLOCKS)]
    k = [kn[c].astype(BF16) for c in range(N_LANE_BLOCKS)]
    vb = [v[:, _lane_block(c)].astype(BF16) for c in range(N_LANE_BLOCKS)]
    if is_diff:
        lam = _diff_lambda(lq1_ref, lk1_ref, lq2_ref, lk2_ref, lambda_init)
        combine = _diff_combine(lam, gsub_ref[...], lambda_init)
    else:
        combine = _head_pair_combine

    def block(b):
        bb, c = divmod(b, N_LANE_BLOCKS)
        return slice(bb * SEQ, (bb + 1) * SEQ), c

    def load_q(b):
        rows, c = block(b)
        return q[c][rows, :]

    def scores(b, q2):
        rows, c = block(b)
        return [_dot_nt(q2, k[c][rows, :])]

    def values(b):
        rows, c = block(b)
        return [_times(vb[c][rows, :])]

    def store(b, o):
        rows, c = block(b)
        o_ref[rows, _lane_block(c)] = o.astype(BF16)

    _attend_pairs(REQ_PER_TILE * N_LANE_BLOCKS, load_q, scores, values, combine, store)


def _qkv_latent(x, first_tile, mod, layer, g_mix, w_qkv, q_gain, k_gain, rope_tabs, is_diff, side_jobs):
    n_tiles = N_ROW_TILES - N_PROMPT_TILES
    row_spec = pl.BlockSpec((ROW_TILE, D_MODEL), lambda i: (i, 0))
    rope_spec = pl.BlockSpec((ROW_TILE, LANES), lambda i: (i % TILES_PER_SAMPLE, 0))
    return _pallas_call_with_side_jobs(
        functools.partial(_qkv_latent_kernel, is_diff), side_jobs,
        grid=(n_tiles,),
        in_specs=[pl.BlockSpec((ROW_TILE, D_MODEL), lambda i: (first_tile + i, 0)),
                  _resident((N_COND, 6 * D_MODEL)),
                  _layer_slice((1, D_MODEL), layer),
                  _resident((D_MODEL, 3 * D_MODEL)),
                  _resident((1, LANES)), _resident((1, LANES)), rope_spec, rope_spec],
        out_specs=[row_spec] * 3,
        out_shape=[jax.ShapeDtypeStruct((N_SAMPLE_ROWS, D_MODEL), BF16)] * 3,
        args=[x, mod, g_mix, w_qkv, q_gain, k_gain, *rope_tabs],
        name="qkv_latent",
    )


def _ctx_layer(x, mod, layer, g_mix, w_qkv, kind_layer, q_gain, k_gain, lam_params, gsub, is_diff,
               lambda_init, prev, side_jobs):
    row_spec = pl.BlockSpec((ROW_TILE, D_MODEL), lambda i: (i, 0))
    if is_diff:
        cache_rows, cache_cols = SEQ * N_LANE_BLOCKS, LANES
    else:
        cache_rows, cache_cols = D_MODEL, SEQ
    cache_spec = pl.BlockSpec((REQ_PER_TILE, None, cache_rows, cache_cols), lambda i: (i, kind_layer, 0, 0))
    cache_shape = jax.ShapeDtypeStruct((BATCH, DEPTH // 2, cache_rows, cache_cols), F32)
    in_specs = ([row_spec, _resident((N_COND, 6 * D_MODEL)), _layer_slice((1, D_MODEL), layer),
                 _resident((D_MODEL, 3 * D_MODEL)), _resident((1, LANES)), _resident((1, LANES))]
                + [_resident((1, HEAD_DIM))] * 4 + [_resident((1, LANES))])
    args = [x, mod, g_mix, w_qkv, q_gain, k_gain, *lam_params, gsub]
    aliases = {}
    if prev is not None:
        in_specs = [pl.BlockSpec(memory_space=pl.ANY)] * 2 + in_specs
        args = list(prev) + args
        aliases = {0: 1, 1: 2}
    return _pallas_call_with_side_jobs(
        functools.partial(_ctx_layer_kernel, is_diff, lambda_init, prev is not None), side_jobs,
        grid=(N_PROMPT_TILES,),
        in_specs=in_specs,
        out_specs=[row_spec, cache_spec, cache_spec],
        out_shape=[jax.ShapeDtypeStruct((N_PROMPT_ROWS, D_MODEL), BF16), cache_shape, cache_shape],
        args=args,
        input_output_aliases=aliases,
        name="ctx_layer",
    )


def _softmax_pv(score_parts, value_parts):
    m = None
    for s in score_parts:
        pm = jnp.max(s, axis=-1, keepdims=True)
        m = pm if m is None else jnp.maximum(m, pm)
    acc = None
    for s, times_v in zip(score_parts, value_parts):
        pv = times_v(jnp.exp2(s - m).astype(BF16))
        acc = pv if acc is None else acc + pv
    return acc[:, :LANES] * (1.0 / acc[:, LANES:])


def _times(v):
    v1 = jnp.concatenate([v, jnp.ones_like(v)], axis=1)
    return lambda w: _dot(w, v1)


def _times_t(v_t):
    v1_t = jnp.concatenate([v_t, jnp.ones_like(v_t)], axis=0)
    return lambda w: _dot_nt(w, v1_t)


def _diff_lambda(lq1_ref, lk1_ref, lq2_ref, lk2_ref, lambda_init):
    e1 = jnp.exp(jnp.sum(lq1_ref[...] * lk1_ref[...], axis=-1, keepdims=True))
    e2 = jnp.exp(jnp.sum(lq2_ref[...] * lk2_ref[...], axis=-1, keepdims=True))
    return e1 - e2 + lambda_init


def _attend_pairs(n_blocks, load_q, scores, values, combine, store):
    def block_scores(b):
        q = load_q(b)
        low = _low_half_mask(q.shape)
        zero = jnp.zeros_like(q)
        return scores(b, jnp.concatenate([jnp.where(low, q, zero), jnp.where(low, zero, q)], axis=0))

    s = block_scores(0)
    for b in range(n_blocks):
        s_next = block_scores(b + 1) if b + 1 < n_blocks else None
        o = _softmax_pv(s, values(b))
        s = s_next
        rows = o.shape[0] // 2
        store(b, combine(o[:rows], o[rows:]))


def _diff_combine(lam, gsub, lambda_init):
    def combine(o1, o2):
        o = o1 - lam * o2
        ms = jnp.mean(o * o, axis=-1, keepdims=True)
        return o * lax.rsqrt(ms + EPS) * gsub * (1.0 - lambda_init)
    return combine


def _head_pair_combine(o_lo, o_hi):
    return jnp.where(_low_half_mask(o_lo.shape), o_lo, o_hi)


Q_CHUNK = 512
LATENT_HEADS_PER_STEP = 2


def _diff_latent_kernel(lambda_init, q_ref, k_ref, v_ref, ck_ref, cv_ref,
                        lq1_ref, lk1_ref, lq2_ref, lk2_ref, gsub_ref, o_ref):
    lam = _diff_lambda(lq1_ref, lk1_ref, lq2_ref, lk2_ref, lambda_init)
    n_chunks = DEC_SEQ // Q_CHUNK
    keys, values = [], []
    for hh in range(LATENT_HEADS_PER_STEP):
        cols = slice(hh * LANES, (hh + 1) * LANES)
        head = pl.program_id(1) * LATENT_HEADS_PER_STEP + hh
        head_rows = pl.ds(head, PAST_LEN, stride=N_LANE_BLOCKS)
        keys.append([k_ref[:, cols], ck_ref[head_rows, :].astype(BF16)])
        values.append([_times(v_ref[:, cols]), _times(cv_ref[head_rows, :].astype(BF16))])

    def block(b):
        hh, c = divmod(b, n_chunks)
        return slice(c * Q_CHUNK, (c + 1) * Q_CHUNK), slice(hh * LANES, (hh + 1) * LANES)

    def store(b, o):
        o_ref[block(b)] = o.astype(BF16)

    _attend_pairs(LATENT_HEADS_PER_STEP * n_chunks,
                  lambda b: q_ref[block(b)],
                  lambda b, q2: [_dot_nt(q2, kpart) for kpart in keys[b // n_chunks]],
                  lambda b: values[b // n_chunks],
                  _diff_combine(lam, gsub_ref[...], lambda_init), store)


def _diff_latent_attention(q, k, v, cache_k, cache_v, layer_i, lam_params, gsub, lambda_init):
    blk = pl.BlockSpec((DEC_SEQ, LATENT_HEADS_PER_STEP * LANES), lambda b, h: (b, h))
    cache = pl.BlockSpec((None, None, PAST_LEN * N_LANE_BLOCKS, LANES), lambda b, h: (b, layer_i, 0, 0))
    small = [_resident((1, HEAD_DIM))] * 4 + [_resident((1, LANES))]
    return pl.pallas_call(
        functools.partial(_diff_latent_kernel, lambda_init),
        grid=(DEC_BATCH, N_LANE_BLOCKS // LATENT_HEADS_PER_STEP),
        in_specs=[blk, blk, blk, cache, cache] + small,
        out_specs=blk,
        out_shape=jax.ShapeDtypeStruct((N_SAMPLE_ROWS, D_MODEL), BF16),
        compiler_params=_params(2),
        name="diff_latent_attention",
    )(q, k, v, cache_k, cache_v, *lam_params, gsub)


NA_Q_ROWS = 8
NA_K_ROWS = 12
NA_Q_TOK = NA_Q_ROWS * GRID_W
NA_K_TOK = NA_K_ROWS * GRID_W
NA_GROUPS = GRID_ROWS // NA_Q_ROWS
NA_K_STEP = (GRID_ROWS - NA_K_ROWS) // (NA_GROUPS - 1)


N_DR = 2 * NA_KH - 1
N_DC = 2 * NA_KW - 1
N_DR_PAD = 16


def _expand_na_bias(rb_ref, tile_ref, bias_ref):
    qc = lax.broadcasted_iota(jnp.int32, (GRID_W, LANES), 0)
    lane = lax.broadcasted_iota(jnp.int32, (GRID_W, LANES), 1)
    kc = lane % GRID_W
    cs = jnp.clip(qc - NA_KW // 2, 0, GRID_W - NA_KW)
    col_ok = (kc >= cs) & (kc < cs + NA_KW)
    low = lane < GRID_W
    neg = jnp.full((GRID_W, LANES), NEG_INF, F32)
    for head in range(2 * LATENT_HEADS_PER_STEP):
        for dr in range(N_DR):
            row = jnp.broadcast_to(rb_ref[head, dr:dr + 1, :], (GRID_W, LANES))
            shifted = pltpu.roll(row, LANES - (NA_KW - 1), 1, stride=1, stride_axis=0)
            tile_ref[dr] = jnp.where(col_ok, shifted * LOG2_E, neg)
        for g in range(NA_GROUPS):
            for rq in range(NA_Q_ROWS):
                r = g * NA_Q_ROWS + rq
                rs = min(max(r - NA_KH // 2, 0), GRID_ROWS - NA_KH)

                def tile(kr, r=r, rs=rs):
                    if rs <= kr < rs + NA_KH:
                        return tile_ref[kr - r + NA_KH - 1]
                    return neg

                for p in range(NA_K_ROWS // 2):
                    kr = g * NA_K_STEP + 2 * p
                    bias_ref[head, g, rq * GRID_W:(rq + 1) * GRID_W, p * LANES:(p + 1) * LANES] = (
                        jnp.where(low, tile(kr), tile(kr + 1)))


def _na_latent_kernel(q_ref, k_ref, v_ref, ckt_ref, cvt_ref, rb_ref, o_ref, tile_ref, bias_ref):
    @pl.when(pl.program_id(1) == 0)
    def _():
        _expand_na_bias(rb_ref, tile_ref, bias_ref)

    ck_t = [ckt_ref[hh * LANES:(hh + 1) * LANES, :].astype(BF16) for hh in range(LATENT_HEADS_PER_STEP)]
    cv_t = [cvt_ref[hh * LANES:(hh + 1) * LANES, :].astype(BF16) for hh in range(LATENT_HEADS_PER_STEP)]

    def block(b):
        return divmod(b, NA_GROUPS)

    def cols(hh):
        return slice(hh * LANES, (hh + 1) * LANES)

    def qrows(g):
        return slice(g * NA_Q_TOK, (g + 1) * NA_Q_TOK)

    def krows(g):
        return slice(g * NA_K_STEP * GRID_W, g * NA_K_STEP * GRID_W + NA_K_TOK)

    def load_q(b):
        hh, g = block(b)
        return q_ref[qrows(g), cols(hh)]

    def scores(b, q2):
        hh, g = block(b)
        bias = jnp.concatenate([bias_ref[2 * hh, g], bias_ref[2 * hh + 1, g]], axis=0)
        return [_dot_nt(q2, k_ref[krows(g), cols(hh)]) + bias, _dot(q2, ck_t[hh])]

    def values(b):
        hh, g = block(b)
        return [_times(v_ref[krows(g), cols(hh)]), _times_t(cv_t[hh])]

    def store(b, o):
        hh, g = block(b)
        o_ref[qrows(g), cols(hh)] = o.astype(BF16)

    _attend_pairs(LATENT_HEADS_PER_STEP * NA_GROUPS, load_q, scores, values, _head_pair_combine, store)


def _na_latent_attention(q, k, v, cache_k_t, cache_v_t, layer_i, rel_bias_rows):
    width = LATENT_HEADS_PER_STEP * LANES
    blk = pl.BlockSpec((DEC_SEQ, width), lambda p, b: (b, p))
    cache = pl.BlockSpec((None, None, width, PAST_LEN), lambda p, b: (b, layer_i, p, 0))
    rb_spec = pl.BlockSpec((None, 2 * LATENT_HEADS_PER_STEP, N_DR_PAD, LANES), lambda p, b: (layer_i, p, 0, 0))
    return pl.pallas_call(
        _na_latent_kernel,
        grid=(N_LANE_BLOCKS // LATENT_HEADS_PER_STEP, DEC_BATCH),
        in_specs=[blk, blk, blk, cache, cache, rb_spec],
        out_specs=blk,
        out_shape=jax.ShapeDtypeStruct((N_SAMPLE_ROWS, D_MODEL), BF16),
        scratch_shapes=[pltpu.VMEM((N_DR, GRID_W, LANES), F32),
                        pltpu.VMEM((2 * LATENT_HEADS_PER_STEP, NA_GROUPS, NA_Q_TOK, NA_K_TOK), F32)],
        compiler_params=_params(2),
        name="na_latent_attention",
    )(q, k, v, cache_k_t, cache_v_t, rel_bias_rows)


def _padded_rel_bias(rel_bias_na):
    half = jnp.pad(rel_bias_na, ((0, 0), (0, 0), (0, N_DR_PAD - N_DR), (0, GRID_W - N_DC)))
    return jnp.concatenate([half, half], axis=-1)


FF_CHUNK = 1024


def _post_kernel(split_in, split_out, *refs):
    n_x = 2 if split_in else 1
    x_refs = refs[:n_x]
    op_ref, os_ref, mod_ref, g_ref, wo_ref, w1_ref, w2_ref = refs[n_x:n_x + 7]
    y_refs = refs[n_x + 7:]
    i = pl.program_id(0)
    r = _cond_row(i)
    is_prompt = i < N_PROMPT_TILES

    def mod(j):
        return mod_ref[pl.ds(r, 1), j * D_MODEL:(j + 1) * D_MODEL]

    x = jnp.where(is_prompt, x_refs[0][...], x_refs[1][...]) if split_in else x_refs[0][...]
    o = jnp.where(is_prompt, op_ref[...], os_ref[...])
    y1 = x + mod(2) * _dot(o, wo_ref[...])
    ms = jnp.mean(y1 * y1, axis=-1, keepdims=True)
    h = (y1 * lax.rsqrt(ms + EPS) * g_ref[...] * (1.0 + mod(4)) + mod(3)).astype(BF16)
    acc = jnp.zeros((ROW_TILE, D_MODEL), F32)
    for c in range(D_FF // FF_CHUNK):
        cols = slice(c * FF_CHUNK, (c + 1) * FF_CHUNK)
        hid = jnp.maximum(_dot(h, w1_ref[:, cols]), 0.0)
        acc = acc + _dot((hid * hid).astype(BF16), w2_ref[cols, :])
    y = y1 + mod(5) * acc
    if split_out:
        @pl.when(is_prompt)
        def _():
            y_refs[0][...] = y

        @pl.when(jnp.logical_not(is_prompt))
        def _():
            y_refs[1][...] = y
    else:
        y_refs[0][...] = y


def _post(xs, o_p, o_s, mod, layer, g_mlp, w_o, w_fc1, w_fc2, split_out, side_jobs):
    prompt_spec = pl.BlockSpec((ROW_TILE, D_MODEL), lambda i: (_prompt_tile(i), 0))
    sample_spec = pl.BlockSpec((ROW_TILE, D_MODEL), lambda i: (_sample_tile(i), 0))
    row_spec = pl.BlockSpec((ROW_TILE, D_MODEL), lambda i: (i, 0))
    split_in = len(xs) == 2
    if split_out:
        out_specs = [prompt_spec, sample_spec]
        out_shape = [jax.ShapeDtypeStruct((N_PROMPT_ROWS, D_MODEL), F32),
                     jax.ShapeDtypeStruct((N_SAMPLE_ROWS, D_MODEL), F32)]
    else:
        out_specs = [row_spec]
        out_shape = [jax.ShapeDtypeStruct((N_ROWS, D_MODEL), F32)]
    return _pallas_call_with_side_jobs(
        functools.partial(_post_kernel, split_in, split_out), side_jobs,
        grid=(N_ROW_TILES,),
        in_specs=([prompt_spec, sample_spec] if split_in else [row_spec]) + [
            prompt_spec, sample_spec,
            _resident((N_COND, 6 * D_MODEL)),
            _layer_slice((1, D_MODEL), layer),
            _resident((D_MODEL, D_MODEL)),
            _resident((D_MODEL, D_FF)),
            _resident((D_FF, D_MODEL)),
        ],
        out_specs=out_specs,
        out_shape=out_shape,
        args=[*xs, o_p, o_s, mod, g_mlp, w_o, w_fc1, w_fc2],
        name="wo_mlp",
    )


def _rope_tables():
    n_freq = HEAD_DIM // 4
    freqs = ROPE_BASE ** (-jnp.arange(n_freq, dtype=F32) / n_freq)
    t = jnp.arange(DEC_SEQ)
    ang_row = (t // GRID_W).astype(F32)[:, None] * freqs
    ang_col = (t % GRID_W).astype(F32)[:, None] * freqs
    cos64 = jnp.concatenate([jnp.cos(ang_row)] * 2 + [jnp.cos(ang_col)] * 2, axis=-1)
    sin64 = jnp.concatenate([-jnp.sin(ang_row), jnp.sin(ang_row),
                             -jnp.sin(ang_col), jnp.sin(ang_col)], axis=-1)
    return jnp.tile(cos64, (1, 2)), jnp.tile(sin64, (1, 2))


def _pair_gain(g):
    return jnp.tile(g.reshape(1, HEAD_DIM), (1, LANES // HEAD_DIM))


def _head_major_cache(cache):
    n_req, n_layers = cache.shape[:2]
    return jnp.transpose(cache, (0, 1, 3, 4, 2)).reshape(n_req, n_layers, D_MODEL, PAST_LEN)


def kernel(x_prompt, x_sample, cache_diff_k, cache_diff_v, cache_na_k, cache_na_v, c, c_ctx, w_ada, b_ada, norm_mix_g, norm_mlp_g, w_fc1, w_fc2, w_qkv_diff, w_o_diff, q_norm_diff_g, k_norm_diff_g, lambda_q1, lambda_k1, lambda_q2, lambda_k2, subln_g, w_qkv_na, w_o_na, q_norm_na_g, k_norm_na_g, rel_bias_na):
    cond = jnp.concatenate([c_ctx[None, :], c, jnp.zeros((N_COND - 1 - DEC_BATCH, D_MODEL), F32)], axis=0)
    b_ada = b_ada.reshape(DEPTH, 1, 6 * D_MODEL)
    mod = _modulation(cond, w_ada, b_ada, 0)
    cos_tab, sin_tab = _rope_tables()
    na_bias = _padded_rel_bias(rel_bias_na)
    cdk = cache_diff_k.reshape(DEC_BATCH, DEPTH // 2, PAST_LEN * N_LANE_BLOCKS, LANES)
    cdv = cache_diff_v.reshape(DEC_BATCH, DEPTH // 2, PAST_LEN * N_LANE_BLOCKS, LANES)
    cnk_t = _head_major_cache(cache_na_k)
    cnv_t = _head_major_cache(cache_na_v)
    g_mix = norm_mix_g.reshape(DEPTH, 1, D_MODEL)
    g_mlp = norm_mlp_g.reshape(DEPTH, 1, D_MODEL)
    w_qkv_f32 = (w_qkv_diff, w_qkv_na)
    w_o_f32 = (w_o_diff, w_o_na)
    q_gain = (q_norm_diff_g, q_norm_na_g)
    k_gain = (k_norm_diff_g, k_norm_na_g)

    xs = [x_prompt.reshape(N_PROMPT_ROWS, D_MODEL), x_sample.reshape(N_SAMPLE_ROWS, D_MODEL)]
    new_kv = [None, None]
    w_qkv = w_qkv_diff[0].astype(BF16)
    w_o = w1 = w2 = None
    n_qkv_tiles = N_PROMPT_TILES
    for l in range(DEPTH):
        i = l // 2
        kind = l % 2
        is_diff = kind == 0
        first = l == 0
        last = l == DEPTH - 1
        x_prompt_src, x_sample_src = (xs[0], xs[1]) if len(xs) == 2 else (xs[0], xs[0])
        sample_first_tile = 0 if len(xs) == 2 else N_PROMPT_TILES
        gains = (_pair_gain(q_gain[kind][i]), _pair_gain(k_gain[kind][i]))
        lambda_init = 0.8 - 0.6 * math.exp(-0.3 * l)
        lam_params = [p[i if is_diff else 0].reshape(1, HEAD_DIM)
                      for p in (lambda_q1, lambda_k1, lambda_q2, lambda_k2)]
        gsub = subln_g[i if is_diff else 0].reshape(1, LANES)
        o_p, new_k, new_v, *cast = _ctx_layer(
            x_prompt_src, mod, l, g_mix, w_qkv, i, *gains, lam_params, gsub, is_diff, lambda_init,
            new_kv[kind],
            [_cast_job(w_fc1, l, 1, n_qkv_tiles), _cast_job(w_o_f32[kind], i, 0, n_qkv_tiles)] if first else [])
        w1, w_o = cast if first else (w1, w_o)
        new_kv[kind] = (new_k, new_v)
        qs, ks, vs, *cast = _qkv_latent(
            x_sample_src, sample_first_tile, mod, l, g_mix, w_qkv, *gains, (cos_tab, sin_tab), is_diff,
            [_cast_job(w_fc2, l, 0, n_qkv_tiles)] if first else [])
        w2 = cast[0] if first else w2
        if is_diff:
            o_s = _diff_latent_attention(qs, ks, vs, cdk, cdv, i, lam_params, gsub, lambda_init)
        else:
            o_s = _na_latent_attention(qs, ks, vs, cnk_t, cnv_t, i, na_bias)
        nxt, nxt_kind, nxt_i = l + 1, (l + 1) % 2, (l + 1) // 2
        next_jobs = [] if last else [_cast_job(w_qkv_f32[nxt_kind], nxt_i, 0, N_ROW_TILES),
                                     _cast_job(w_o_f32[nxt_kind], nxt_i, 0, N_ROW_TILES),
                                     _cast_job(w_fc1, nxt, 1, N_ROW_TILES),
                                     _cast_job(w_fc2, nxt, 0, N_ROW_TILES),
                                     _ada_job(cond, w_ada, b_ada, nxt, N_ROW_TILES)]
        outs = list(_post(xs, o_p, o_s, mod, l, g_mlp, w_o, w1, w2, last, next_jobs))
        if last:
            xp, xs = outs
        else:
            *xs, w_qkv, w_o, w1, w2, mod = outs

    n_kind = DEPTH // 2
    diff_k, diff_v = (a.reshape(BATCH, n_kind, SEQ, N_LANE_BLOCKS, LANES) for a in new_kv[0])
    na_k, na_v = (jnp.transpose(a.reshape(BATCH, n_kind, NA_HEADS, HEAD_DIM, SEQ), (0, 1, 4, 2, 3))
                  for a in new_kv[1])
    return (xp.reshape(BATCH, SEQ, D_MODEL), xs.reshape(DEC_BATCH, DEC_SEQ, D_MODEL),
            diff_k, diff_v, na_k, na_v)
```

```python
import functools
import math
from typing import Callable, NamedTuple

import jax
import jax.numpy as jnp
from jax import lax
from jax.experimental import pallas as pl
from jax.experimental.pallas import tpu as pltpu

D_MODEL = 1024
BATCH = 16
SEQ = 256
DEPTH = 4
DEC_BATCH = 4
DEC_SEQ = 1024
PAST_LEN = 256
GRID_W = 64
GRID_ROWS = DEC_SEQ // GRID_W
HEAD_DIM = 64
D_FF = 4 * D_MODEL
NA_HEADS = 16
NA_KH = 8
NA_KW = 16
ROPE_BASE = 10000.0
EPS = 1e-6
NEG_INF = -1e30
LOG2_E = math.log2(math.e)
SCORE_SCALE = HEAD_DIM ** -0.5 * LOG2_E

LANES = 128
SUBLANES = 8
N_LANE_BLOCKS = D_MODEL // LANES
N_PROMPT_ROWS = BATCH * SEQ
N_SAMPLE_ROWS = DEC_BATCH * DEC_SEQ
N_ROWS = N_PROMPT_ROWS + N_SAMPLE_ROWS
ROW_TILE = 512
N_ROW_TILES = N_ROWS // ROW_TILE
N_PROMPT_TILES = N_PROMPT_ROWS // ROW_TILE
TILES_PER_SAMPLE = DEC_SEQ // ROW_TILE
REQ_PER_TILE = ROW_TILE // SEQ
N_COND = 8
VMEM_LIMIT = 56 * 1024 * 1024

F32 = jnp.float32
BF16 = jnp.bfloat16
NT_DIMS = (((1,), (1,)), ((), ()))


def _dot(a, b):
    return jnp.dot(a, b, preferred_element_type=F32)


def _dot_nt(a, b):
    return lax.dot_general(a, b, NT_DIMS, preferred_element_type=F32)


def _params(n_axes):
    return pltpu.CompilerParams(dimension_semantics=("arbitrary",) * n_axes,
                                vmem_limit_bytes=VMEM_LIMIT)


def _resident(shape):
    return pl.BlockSpec(shape, lambda *_: (0,) * len(shape), pipeline_mode=pl.Buffered(1))


def _layer_slice(shape, layer):
    return pl.BlockSpec((None,) + shape, lambda *_: (layer,) + (0,) * len(shape),
                        pipeline_mode=pl.Buffered(1))


class _SideJob(NamedTuple):
    args: list
    in_specs: list
    out_spec: pl.BlockSpec
    out_shape: jax.ShapeDtypeStruct
    body: Callable


def _cast_body(src_ref, dst_ref):
    dst_ref[...] = src_ref[...].astype(BF16)


def _cast_job(weights, layer, split_axis, n_steps):
    _, rows, cols = weights.shape
    if split_axis == 0:
        band = (rows // n_steps, cols)
        src_spec = pl.BlockSpec((None,) + band, lambda i: (layer, i, 0))
        dst_spec = pl.BlockSpec(band, lambda i: (i, 0))
    else:
        band = (rows, cols // n_steps)
        src_spec = pl.BlockSpec((None,) + band, lambda i: (layer, 0, i))
        dst_spec = pl.BlockSpec(band, lambda i: (0, i))
    return _SideJob([weights], [src_spec], dst_spec, jax.ShapeDtypeStruct((rows, cols), BF16), _cast_body)


def _pallas_call_with_side_jobs(kernel_fn, jobs, *, grid, in_specs, out_specs, out_shape, args, name,
                                input_output_aliases=None):
    n_in, n_out = len(in_specs), len(out_specs)
    n_job_in = [len(job.args) for job in jobs]

    def kernel_and_jobs(*refs):
        ins, refs = refs[:n_in], refs[n_in:]
        job_ins, refs = refs[:sum(n_job_in)], refs[sum(n_job_in):]
        outs, refs = refs[:n_out], refs[n_out:]
        job_outs, scratch = refs[:len(jobs)], refs[len(jobs):]
        for job, n, out_ref in zip(jobs, n_job_in, job_outs):
            job.body(*job_ins[:n], out_ref)
            job_ins = job_ins[n:]
        kernel_fn(*ins, *outs, *scratch)

    return pl.pallas_call(
        kernel_and_jobs,
        grid=grid,
        in_specs=list(in_specs) + [spec for job in jobs for spec in job.in_specs],
        out_specs=list(out_specs) + [job.out_spec for job in jobs],
        out_shape=list(out_shape) + [job.out_shape for job in jobs],
        input_output_aliases=input_output_aliases or {},
        compiler_params=_params(1),
        name=name,
    )(*args, *(arg for job in jobs for arg in job.args))


def _prompt_tile(i):
    return jnp.minimum(i, N_PROMPT_TILES - 1)


def _sample_tile(i):
    return jnp.maximum(i - N_PROMPT_TILES, 0)


def _cond_row(i):
    return jnp.where(i < N_PROMPT_TILES, 0, 1 + (i - N_PROMPT_TILES) // TILES_PER_SAMPLE)


def _low_half_mask(shape):
    return lax.broadcasted_iota(jnp.int32, shape, len(shape) - 1) < HEAD_DIM


def _ada_kernel(cond_ref, w_ref, b_ref, out_ref):
    cond = cond_ref[...]
    act = (cond * jax.nn.sigmoid(cond)).astype(BF16)
    out_ref[...] = _dot(act, w_ref[...].astype(BF16)) + b_ref[...]


def _ada_specs(layer, n_steps):
    tn = 6 * D_MODEL // n_steps
    in_specs = [pl.BlockSpec((N_COND, D_MODEL), lambda j: (0, 0)),
                pl.BlockSpec((None, D_MODEL, tn), lambda j: (layer, 0, j)),
                pl.BlockSpec((None, 1, tn), lambda j: (layer, 0, j))]
    return in_specs, pl.BlockSpec((N_COND, tn), lambda j: (0, j)), jax.ShapeDtypeStruct((N_COND, 6 * D_MODEL), F32)


def _modulation(cond, w_ada, b_ada, layer):
    n_steps = 4
    in_specs, out_spec, out_shape = _ada_specs(layer, n_steps)
    return pl.pallas_call(_ada_kernel, grid=(n_steps,), in_specs=in_specs, out_specs=out_spec,
                          out_shape=out_shape, compiler_params=_params(1),
                          name="ada_modulation")(cond, w_ada, b_ada)


def _ada_job(cond, w_ada, b_ada, layer, n_steps):
    in_specs, out_spec, out_shape = _ada_specs(layer, n_steps)
    return _SideJob([cond, w_ada, b_ada], in_specs, out_spec, out_shape, _ada_kernel)


def _group_rms(x, gain):
    low = _low_half_mask((1, LANES))
    cols = []
    for c in range(N_LANE_BLOCKS):
        xb = x[:, c * LANES:(c + 1) * LANES]
        sq = xb * xb
        s_lo = jnp.sum(jnp.where(low, sq, 0.0), axis=-1, keepdims=True)
        s_hi = jnp.sum(jnp.where(low, 0.0, sq), axis=-1, keepdims=True)
        ms = jnp.where(low, s_lo, s_hi) * (1.0 / HEAD_DIM)
        cols.append(xb * lax.rsqrt(ms + EPS) * gain)
    return cols


def _rope(cols, cos, sin_signed):
    lane = lax.broadcasted_iota(jnp.int32, (1, LANES), 1)
    first = (lane % 32) < 16
    out = []
    for xb in cols:
        partner = jnp.where(first, pltpu.roll(xb, LANES - 16, 1), pltpu.roll(xb, 16, 1))
        out.append(xb * cos + partner * sin_signed)
    return out


def _project_qkv(x_ref, mod_ref, g_ref, w_ref, qg_ref, kg_ref, r):
    x = x_ref[...]
    ms = jnp.mean(x * x, axis=-1, keepdims=True)
    xn = x * lax.rsqrt(ms + EPS) * g_ref[...]
    shift = mod_ref[pl.ds(r, 1), 0:D_MODEL]
    scale = mod_ref[pl.ds(r, 1), D_MODEL:2 * D_MODEL]
    h = (xn * (1.0 + scale) + shift).astype(BF16)
    qkv = _dot(h, w_ref[...])
    qn = _group_rms(qkv[:, 0:D_MODEL], qg_ref[...])
    kn = _group_rms(qkv[:, D_MODEL:2 * D_MODEL], kg_ref[...])
    return qn, kn, qkv[:, 2 * D_MODEL:3 * D_MODEL]


def _lane_block(c):
    return slice(c * LANES, (c + 1) * LANES)


def _qkv_latent_kernel(is_diff, x_ref, mod_ref, g_ref, w_ref, qg_ref, kg_ref, cos_ref, sin_ref,
                       q_ref, k_ref, v_ref):
    r = 1 + pl.program_id(0) // TILES_PER_SAMPLE
    qn, kn, v = _project_qkv(x_ref, mod_ref, g_ref, w_ref, qg_ref, kg_ref, r)
    v_ref[...] = v.astype(BF16)
    if is_diff:
        cos = cos_ref[...]
        sin = sin_ref[...]
        qn = _rope(qn, cos, sin)
        kn = _rope(kn, cos, sin)
    for c in range(N_LANE_BLOCKS):
        q_ref[:, _lane_block(c)] = (qn[c] * SCORE_SCALE).astype(BF16)
        k_ref[:, _lane_block(c)] = kn[c].astype(BF16)


def _ctx_layer_kernel(is_diff, lambda_init, has_prev, *refs):
    if has_prev:
        refs = refs[2:]
    (x_ref, mod_ref, g_ref, w_ref, qg_ref, kg_ref, lq1_ref, lk1_ref, lq2_ref, lk2_ref, gsub_ref,
     o_ref, kf_ref, vf_ref) = refs
    qn, kn, v = _project_qkv(x_ref, mod_ref, g_ref, w_ref, qg_ref, kg_ref, 0)
    for c in range(N_LANE_BLOCKS):
        for bb in range(REQ_PER_TILE):
            rows = slice(bb * SEQ, (bb + 1) * SEQ)
            if is_diff:
                dst = pl.ds(c, SEQ, stride=N_LANE_BLOCKS)
                kf_ref[bb, dst, :] = kn[c][rows, :]
                vf_ref[bb, dst, :] = v[rows, _lane_block(c)]
            else:
                kf_ref[bb, _lane_block(c), :] = kn[c][rows, :].T
                vf_ref[bb, _lane_block(c), :] = v[rows, _lane_block(c)].T
    q = [(qn[c] * SCORE_SCALE).astype(BF16) for c in range(N_LANE_BLOCKS)]
    k = [kn[c].astype(BF16) for c in range(N_LANE_BLOCKS)]
    vb = [v[:, _lane_block(c)].astype(BF16) for c in range(N_LANE_BLOCKS)]
    if is_diff:
        lam = _diff_lambda(lq1_ref, lk1_ref, lq2_ref, lk2_ref, lambda_init)
        combine = _diff_combine(lam, gsub_ref[...], lambda_init)
    else:
        combine = _head_pair_combine

    def block(b):
        bb, c = divmod(b, N_LANE_BLOCKS)
        return slice(bb * SEQ, (bb + 1) * SEQ), c

    def load_q(b):
        rows, c = block(b)
        return q[c][rows, :]

    def scores(b, q2):
        rows, c = block(b)
        return [_dot_nt(q2, k[c][rows, :])]

    def values(b):
        rows, c = block(b)
        return [_times(vb[c][rows, :])]

    def store(b, o):
        rows, c = block(b)
        o_ref[rows, _lane_block(c)] = o.astype(BF16)

    _attend_pairs(REQ_PER_TILE * N_LANE_BLOCKS, load_q, scores, values, combine, store)


def _qkv_latent(x, first_tile, mod, layer, g_mix, w_qkv, q_gain, k_gain, rope_tabs, is_diff, side_jobs):
    n_tiles = N_ROW_TILES - N_PROMPT_TILES
    row_spec = pl.BlockSpec((ROW_TILE, D_MODEL), lambda i: (i, 0))
    rope_spec = pl.BlockSpec((ROW_TILE, LANES), lambda i: (i % TILES_PER_SAMPLE, 0))
    return _pallas_call_with_side_jobs(
        functools.partial(_qkv_latent_kernel, is_diff), side_jobs,
        grid=(n_tiles,),
        in_specs=[pl.BlockSpec((ROW_TILE, D_MODEL), lambda i: (first_tile + i, 0)),
                  _resident((N_COND, 6 * D_MODEL)),
                  _layer_slice((1, D_MODEL), layer),
                  _resident((D_MODEL, 3 * D_MODEL)),
                  _resident((1, LANES)), _resident((1, LANES)), rope_spec, rope_spec],
        out_specs=[row_spec] * 3,
        out_shape=[jax.ShapeDtypeStruct((N_SAMPLE_ROWS, D_MODEL), BF16)] * 3,
        args=[x, mod, g_mix, w_qkv, q_gain, k_gain, *rope_tabs],
        name="qkv_latent",
    )


def _ctx_layer(x, mod, layer, g_mix, w_qkv, kind_layer, q_gain, k_gain, lam_params, gsub, is_diff,
               lambda_init, prev, side_jobs):
    row_spec = pl.BlockSpec((ROW_TILE, D_MODEL), lambda i: (i, 0))
    if is_diff:
        cache_rows, cache_cols = SEQ * N_LANE_BLOCKS, LANES
    else:
        cache_rows, cache_cols = D_MODEL, SEQ
    cache_spec = pl.BlockSpec((REQ_PER_TILE, None, cache_rows, cache_cols), lambda i: (i, kind_layer, 0, 0))
    cache_shape = jax.ShapeDtypeStruct((BATCH, DEPTH // 2, cache_rows, cache_cols), F32)
    in_specs = ([row_spec, _resident((N_COND, 6 * D_MODEL)), _layer_slice((1, D_MODEL), layer),
                 _resident((D_MODEL, 3 * D_MODEL)), _resident((1, LANES)), _resident((1, LANES))]
                + [_resident((1, HEAD_DIM))] * 4 + [_resident((1, LANES))])
    args = [x, mod, g_mix, w_qkv, q_gain, k_gain, *lam_params, gsub]
    aliases = {}
    if prev is not None:
        in_specs = [pl.BlockSpec(memory_space=pl.ANY)] * 2 + in_specs
        args = list(prev) + args
        aliases = {0: 1, 1: 2}
    return _pallas_call_with_side_jobs(
        functools.partial(_ctx_layer_kernel, is_diff, lambda_init, prev is not None), side_jobs,
        grid=(N_PROMPT_TILES,),
        in_specs=in_specs,
        out_specs=[row_spec, cache_spec, cache_spec],
        out_shape=[jax.ShapeDtypeStruct((N_PROMPT_ROWS, D_MODEL), BF16), cache_shape, cache_shape],
        args=args,
        input_output_aliases=aliases,
        name="ctx_layer",
    )


def _softmax_pv(score_parts, value_parts):
    m = None
    for s in score_parts:
        pm = jnp.max(s, axis=-1, keepdims=True)
        m = pm if m is None else jnp.maximum(m, pm)
    acc = None
    for s, times_v in zip(score_parts, value_parts):
        pv = times_v(jnp.exp2(s - m).astype(BF16))
        acc = pv if acc is None else acc + pv
    return acc[:, :LANES] * (1.0 / acc[:, LANES:])


def _times(v):
    v1 = jnp.concatenate([v, jnp.ones_like(v)], axis=1)
    return lambda w: _dot(w, v1)


def _times_t(v_t):
    v1_t = jnp.concatenate([v_t, jnp.ones_like(v_t)], axis=0)
    return lambda w: _dot_nt(w, v1_t)


def _diff_lambda(lq1_ref, lk1_ref, lq2_ref, lk2_ref, lambda_init):
    e1 = jnp.exp(jnp.sum(lq1_ref[...] * lk1_ref[...], axis=-1, keepdims=True))
    e2 = jnp.exp(jnp.sum(lq2_ref[...] * lk2_ref[...], axis=-1, keepdims=True))
    return e1 - e2 + lambda_init


def _attend_pairs(n_blocks, load_q, scores, values, combine, store):
    def block_scores(b):
        q = load_q(b)
        low = _low_half_mask(q.shape)
        zero = jnp.zeros_like(q)
        return scores(b, jnp.concatenate([jnp.where(low, q, zero), jnp.where(low, zero, q)], axis=0))

    s = block_scores(0)
    for b in range(n_blocks):
        s_next = block_scores(b + 1) if b + 1 < n_blocks else None
        o = _softmax_pv(s, values(b))
        s = s_next
        rows = o.shape[0] // 2
        store(b, combine(o[:rows], o[rows:]))


def _diff_combine(lam, gsub, lambda_init):
    def combine(o1, o2):
        o = o1 - lam * o2
        ms = jnp.mean(o * o, axis=-1, keepdims=True)
        return o * lax.rsqrt(ms + EPS) * gsub * (1.0 - lambda_init)
    return combine


def _head_pair_combine(o_lo, o_hi):
    return jnp.where(_low_half_mask(o_lo.shape), o_lo, o_hi)


Q_CHUNK = 512
LATENT_HEADS_PER_STEP = 2


def _diff_latent_kernel(lambda_init, q_ref, k_ref, v_ref, ck_ref, cv_ref,
                        lq1_ref, lk1_ref, lq2_ref, lk2_ref, gsub_ref, o_ref):
    lam = _diff_lambda(lq1_ref, lk1_ref, lq2_ref, lk2_ref, lambda_init)
    n_chunks = DEC_SEQ // Q_CHUNK
    keys, values = [], []
    for hh in range(LATENT_HEADS_PER_STEP):
        cols = slice(hh * LANES, (hh + 1) * LANES)
        head = pl.program_id(1) * LATENT_HEADS_PER_STEP + hh
        head_rows = pl.ds(head, PAST_LEN, stride=N_LANE_BLOCKS)
        keys.append([k_ref[:, cols], ck_ref[head_rows, :].astype(BF16)])
        values.append([_times(v_ref[:, cols]), _times(cv_ref[head_rows, :].astype(BF16))])

    def block(b):
        hh, c = divmod(b, n_chunks)
        return slice(c * Q_CHUNK, (c + 1) * Q_CHUNK), slice(hh * LANES, (hh + 1) * LANES)

    def store(b, o):
        o_ref[block(b)] = o.astype(BF16)

    _attend_pairs(LATENT_HEADS_PER_STEP * n_chunks,
                  lambda b: q_ref[block(b)],
                  lambda b, q2: [_dot_nt(q2, kpart) for kpart in keys[b // n_chunks]],
                  lambda b: values[b // n_chunks],
                  _diff_combine(lam, gsub_ref[...], lambda_init), store)


def _diff_latent_attention(q, k, v, cache_k, cache_v, layer_i, lam_params, gsub, lambda_init):
    blk = pl.BlockSpec((DEC_SEQ, LATENT_HEADS_PER_STEP * LANES), lambda b, h: (b, h))
    cache = pl.BlockSpec((None, None, PAST_LEN * N_LANE_BLOCKS, LANES), lambda b, h: (b, layer_i, 0, 0))
    small = [_resident((1, HEAD_DIM))] * 4 + [_resident((1, LANES))]
    return pl.pallas_call(
        functools.partial(_diff_latent_kernel, lambda_init),
        grid=(DEC_BATCH, N_LANE_BLOCKS // LATENT_HEADS_PER_STEP),
        in_specs=[blk, blk, blk, cache, cache] + small,
        out_specs=blk,
        out_shape=jax.ShapeDtypeStruct((N_SAMPLE_ROWS, D_MODEL), BF16),
        compiler_params=_params(2),
        name="diff_latent_attention",
    )(q, k, v, cache_k, cache_v, *lam_params, gsub)


class _NaGroup(NamedTuple):
    q_row: int
    n_q: int
    k_row: int
    n_k: int


def _window_start(r):
    return min(max(r - NA_KH // 2, 0), GRID_ROWS - NA_KH)


def _na_groups(rows_per_group):
    groups = []
    for q_row in range(0, GRID_ROWS, rows_per_group):
        first = _window_start(q_row)
        last = _window_start(q_row + rows_per_group - 1) + NA_KH
        k_row = first - first % 2
        n_k = last - k_row
        n_k += n_k % 2
        groups.append(_NaGroup(q_row, rows_per_group, min(k_row, GRID_ROWS - n_k), n_k))
    return groups


NA_GROUPS = _na_groups(4)
N_DR = 2 * NA_KH - 1
N_DC = 2 * NA_KW - 1
N_DR_PAD = 16


def _expand_na_bias(rb_ref, tile_ref, bias_refs):
    qc = lax.broadcasted_iota(jnp.int32, (GRID_W, LANES), 0)
    lane = lax.broadcasted_iota(jnp.int32, (GRID_W, LANES), 1)
    kc = lane % GRID_W
    cs = jnp.clip(qc - NA_KW // 2, 0, GRID_W - NA_KW)
    col_ok = (kc >= cs) & (kc < cs + NA_KW)
    low = lane < GRID_W
    neg = jnp.full((GRID_W, LANES), NEG_INF, F32)
    for head in range(2 * LATENT_HEADS_PER_STEP):
        for dr in range(N_DR):
            row = jnp.broadcast_to(rb_ref[head, dr:dr + 1, :], (GRID_W, LANES))
            shifted = pltpu.roll(row, LANES - (NA_KW - 1), 1, stride=1, stride_axis=0)
            tile_ref[dr] = jnp.where(col_ok, shifted * LOG2_E, neg)
        for grp, bias_ref in zip(NA_GROUPS, bias_refs):
            for rq in range(grp.n_q):
                r = grp.q_row + rq
                rs = _window_start(r)

                def tile(kr, r=r, rs=rs):
                    if rs <= kr < rs + NA_KH:
                        return tile_ref[kr - r + NA_KH - 1]
                    return neg

                for p in range(grp.n_k // 2):
                    kr = grp.k_row + 2 * p
                    bias_ref[head, rq * GRID_W:(rq + 1) * GRID_W, p * LANES:(p + 1) * LANES] = (
                        jnp.where(low, tile(kr), tile(kr + 1)))


def _na_latent_kernel(q_ref, k_ref, v_ref, ckt_ref, cvt_ref, rb_ref, o_ref, tile_ref, *bias_refs):
    @pl.when(pl.program_id(1) == 0)
    def _():
        _expand_na_bias(rb_ref, tile_ref, bias_refs)

    ck_t = [ckt_ref[hh * LANES:(hh + 1) * LANES, :].astype(BF16) for hh in range(LATENT_HEADS_PER_STEP)]
    cv_t = [cvt_ref[hh * LANES:(hh + 1) * LANES, :].astype(BF16) for hh in range(LATENT_HEADS_PER_STEP)]

    def block(b):
        return divmod(b, len(NA_GROUPS))

    def cols(hh):
        return slice(hh * LANES, (hh + 1) * LANES)

    def qrows(g):
        return slice(NA_GROUPS[g].q_row * GRID_W, (NA_GROUPS[g].q_row + NA_GROUPS[g].n_q) * GRID_W)

    def krows(g):
        return slice(NA_GROUPS[g].k_row * GRID_W, (NA_GROUPS[g].k_row + NA_GROUPS[g].n_k) * GRID_W)

    def load_q(b):
        hh, g = block(b)
        return q_ref[qrows(g), cols(hh)]

    def scores(b, q2):
        hh, g = block(b)
        bias = jnp.concatenate([bias_refs[g][2 * hh], bias_refs[g][2 * hh + 1]], axis=0)
        return [_dot_nt(q2, k_ref[krows(g), cols(hh)]) + bias, _dot(q2, ck_t[hh])]

    def values(b):
        hh, g = block(b)
        return [_times(v_ref[krows(g), cols(hh)]), _times_t(cv_t[hh])]

    def store(b, o):
        hh, g = block(b)
        o_ref[qrows(g), cols(hh)] = o.astype(BF16)

    _attend_pairs(LATENT_HEADS_PER_STEP * len(NA_GROUPS), load_q, scores, values, _head_pair_combine, store)


def _na_latent_attention(q, k, v, cache_k_t, cache_v_t, layer_i, rel_bias_rows):
    width = LATENT_HEADS_PER_STEP * LANES
    blk = pl.BlockSpec((DEC_SEQ, width), lambda p, b: (b, p))
    cache = pl.BlockSpec((None, None, width, PAST_LEN), lambda p, b: (b, layer_i, p, 0))
    rb_spec = pl.BlockSpec((None, 2 * LATENT_HEADS_PER_STEP, N_DR_PAD, LANES), lambda p, b: (layer_i, p, 0, 0))
    return pl.pallas_call(
        _na_latent_kernel,
        grid=(N_LANE_BLOCKS // LATENT_HEADS_PER_STEP, DEC_BATCH),
        in_specs=[blk, blk, blk, cache, cache, rb_spec],
        out_specs=blk,
        out_shape=jax.ShapeDtypeStruct((N_SAMPLE_ROWS, D_MODEL), BF16),
        scratch_shapes=[pltpu.VMEM((N_DR, GRID_W, LANES), F32)] + [
            pltpu.VMEM((2 * LATENT_HEADS_PER_STEP, grp.n_q * GRID_W, grp.n_k * GRID_W), F32)
            for grp in NA_GROUPS],
        compiler_params=_params(2),
        name="na_latent_attention",
    )(q, k, v, cache_k_t, cache_v_t, rel_bias_rows)


def _padded_rel_bias(rel_bias_na):
    half = jnp.pad(rel_bias_na, ((0, 0), (0, 0), (0, N_DR_PAD - N_DR), (0, GRID_W - N_DC)))
    return jnp.concatenate([half, half], axis=-1)


FF_CHUNK = 1024


def _post_kernel(split_in, split_out, *refs):
    n_x = 2 if split_in else 1
    x_refs = refs[:n_x]
    op_ref, os_ref, mod_ref, g_ref, wo_ref, w1_ref, w2_ref = refs[n_x:n_x + 7]
    y_refs = refs[n_x + 7:]
    i = pl.program_id(0)
    r = _cond_row(i)
    is_prompt = i < N_PROMPT_TILES

    def mod(j):
        return mod_ref[pl.ds(r, 1), j * D_MODEL:(j + 1) * D_MODEL]

    x = jnp.where(is_prompt, x_refs[0][...], x_refs[1][...]) if split_in else x_refs[0][...]
    o = jnp.where(is_prompt, op_ref[...], os_ref[...])
    y1 = x + mod(2) * _dot(o, wo_ref[...])
    ms = jnp.mean(y1 * y1, axis=-1, keepdims=True)
    h = (y1 * lax.rsqrt(ms + EPS) * g_ref[...] * (1.0 + mod(4)) + mod(3)).astype(BF16)
    acc = jnp.zeros((ROW_TILE, D_MODEL), F32)
    for c in range(D_FF // FF_CHUNK):
        cols = slice(c * FF_CHUNK, (c + 1) * FF_CHUNK)
        hid = jnp.maximum(_dot(h, w1_ref[:, cols]), 0.0)
        acc = acc + _dot((hid * hid).astype(BF16), w2_ref[cols, :])
    y = y1 + mod(5) * acc
    if split_out:
        @pl.when(is_prompt)
        def _():
            y_refs[0][...] = y

        @pl.when(jnp.logical_not(is_prompt))
        def _():
            y_refs[1][...] = y
    else:
        y_refs[0][...] = y


def _post(xs, o_p, o_s, mod, layer, g_mlp, w_o, w_fc1, w_fc2, split_out, side_jobs):
    prompt_spec = pl.BlockSpec((ROW_TILE, D_MODEL), lambda i: (_prompt_tile(i), 0))
    sample_spec = pl.BlockSpec((ROW_TILE, D_MODEL), lambda i: (_sample_tile(i), 0))
    row_spec = pl.BlockSpec((ROW_TILE, D_MODEL), lambda i: (i, 0))
    split_in = len(xs) == 2
    if split_out:
        out_specs = [prompt_spec, sample_spec]
        out_shape = [jax.ShapeDtypeStruct((N_PROMPT_ROWS, D_MODEL), F32),
                     jax.ShapeDtypeStruct((N_SAMPLE_ROWS, D_MODEL), F32)]
    else:
        out_specs = [row_spec]
        out_shape = [jax.ShapeDtypeStruct((N_ROWS, D_MODEL), F32)]
    return _pallas_call_with_side_jobs(
        functools.partial(_post_kernel, split_in, split_out), side_jobs,
        grid=(N_ROW_TILES,),
        in_specs=([prompt_spec, sample_spec] if split_in else [row_spec]) + [
            prompt_spec, sample_spec,
            _resident((N_COND, 6 * D_MODEL)),
            _layer_slice((1, D_MODEL), layer),
            _resident((D_MODEL, D_MODEL)),
            _resident((D_MODEL, D_FF)),
            _resident((D_FF, D_MODEL)),
        ],
        out_specs=out_specs,
        out_shape=out_shape,
        args=[*xs, o_p, o_s, mod, g_mlp, w_o, w_fc1, w_fc2],
        name="wo_mlp",
    )


def _rope_tables():
    n_freq = HEAD_DIM // 4
    freqs = ROPE_BASE ** (-jnp.arange(n_freq, dtype=F32) / n_freq)
    t = jnp.arange(DEC_SEQ)
    ang_row = (t // GRID_W).astype(F32)[:, None] * freqs
    ang_col = (t % GRID_W).astype(F32)[:, None] * freqs
    cos64 = jnp.concatenate([jnp.cos(ang_row)] * 2 + [jnp.cos(ang_col)] * 2, axis=-1)
    sin64 = jnp.concatenate([-jnp.sin(ang_row), jnp.sin(ang_row),
                             -jnp.sin(ang_col), jnp.sin(ang_col)], axis=-1)
    return jnp.tile(cos64, (1, 2)), jnp.tile(sin64, (1, 2))


def _pair_gain(g):
    return jnp.tile(g.reshape(1, HEAD_DIM), (1, LANES // HEAD_DIM))


def _head_major_cache(cache):
    n_req, n_layers = cache.shape[:2]
    return jnp.transpose(cache, (0, 1, 3, 4, 2)).reshape(n_req, n_layers, D_MODEL, PAST_LEN)


def kernel(x_prompt, x_sample, cache_diff_k, cache_diff_v, cache_na_k, cache_na_v, c, c_ctx, w_ada, b_ada, norm_mix_g, norm_mlp_g, w_fc1, w_fc2, w_qkv_diff, w_o_diff, q_norm_diff_g, k_norm_diff_g, lambda_q1, lambda_k1, lambda_q2, lambda_k2, subln_g, w_qkv_na, w_o_na, q_norm_na_g, k_norm_na_g, rel_bias_na):
    cond = jnp.concatenate([c_ctx[None, :], c, jnp.zeros((N_COND - 1 - DEC_BATCH, D_MODEL), F32)], axis=0)
    b_ada = b_ada.reshape(DEPTH, 1, 6 * D_MODEL)
    mod = _modulation(cond, w_ada, b_ada, 0)
    cos_tab, sin_tab = _rope_tables()
    na_bias = _padded_rel_bias(rel_bias_na)
    cdk = cache_diff_k.reshape(DEC_BATCH, DEPTH // 2, PAST_LEN * N_LANE_BLOCKS, LANES)
    cdv = cache_diff_v.reshape(DEC_BATCH, DEPTH // 2, PAST_LEN * N_LANE_BLOCKS, LANES)
    cnk_t = _head_major_cache(cache_na_k)
    cnv_t = _head_major_cache(cache_na_v)
    g_mix = norm_mix_g.reshape(DEPTH, 1, D_MODEL)
    g_mlp = norm_mlp_g.reshape(DEPTH, 1, D_MODEL)
    w_qkv_f32 = (w_qkv_diff, w_qkv_na)
    w_o_f32 = (w_o_diff, w_o_na)
    q_gain = (q_norm_diff_g, q_norm_na_g)
    k_gain = (k_norm_diff_g, k_norm_na_g)

    xs = [x_prompt.reshape(N_PROMPT_ROWS, D_MODEL), x_sample.reshape(N_SAMPLE_ROWS, D_MODEL)]
    new_kv = [None, None]
    w_qkv = w_qkv_diff[0].astype(BF16)
    w_o = w1 = w2 = None
    n_qkv_tiles = N_PROMPT_TILES
    for l in range(DEPTH):
        i = l // 2
        kind = l % 2
        is_diff = kind == 0
        first = l == 0
        last = l == DEPTH - 1
        x_prompt_src, x_sample_src = (xs[0], xs[1]) if len(xs) == 2 else (xs[0], xs[0])
        sample_first_tile = 0 if len(xs) == 2 else N_PROMPT_TILES
        gains = (_pair_gain(q_gain[kind][i]), _pair_gain(k_gain[kind][i]))
        lambda_init = 0.8 - 0.6 * math.exp(-0.3 * l)
        lam_params = [p[i if is_diff else 0].reshape(1, HEAD_DIM)
                      for p in (lambda_q1, lambda_k1, lambda_q2, lambda_k2)]
        gsub = subln_g[i if is_diff else 0].reshape(1, LANES)
        o_p, new_k, new_v, *cast = _ctx_layer(
            x_prompt_src, mod, l, g_mix, w_qkv, i, *gains, lam_params, gsub, is_diff, lambda_init,
            new_kv[kind],
            [_cast_job(w_fc1, l, 1, n_qkv_tiles), _cast_job(w_o_f32[kind], i, 0, n_qkv_tiles)] if first else [])
        w1, w_o = cast if first else (w1, w_o)
        new_kv[kind] = (new_k, new_v)
        qs, ks, vs, *cast = _qkv_latent(
            x_sample_src, sample_first_tile, mod, l, g_mix, w_qkv, *gains, (cos_tab, sin_tab), is_diff,
            [_cast_job(w_fc2, l, 0, n_qkv_tiles)] if first else [])
        w2 = cast[0] if first else w2
        if is_diff:
            o_s = _diff_latent_attention(qs, ks, vs, cdk, cdv, i, lam_params, gsub, lambda_init)
        else:
            o_s = _na_latent_attention(qs, ks, vs, cnk_t, cnv_t, i, na_bias)
        nxt, nxt_kind, nxt_i = l + 1, (l + 1) % 2, (l + 1) // 2
        next_jobs = [] if last else [_cast_job(w_qkv_f32[nxt_kind], nxt_i, 0, N_ROW_TILES),
                                     _cast_job(w_o_f32[nxt_kind], nxt_i, 0, N_ROW_TILES),
                                     _cast_job(w_fc1, nxt, 1, N_ROW_TILES),
                                     _cast_job(w_fc2, nxt, 0, N_ROW_TILES),
                                     _ada_job(cond, w_ada, b_ada, nxt, N_ROW_TILES)]
        outs = list(_post(xs, o_p, o_s, mod, l, g_mlp, w_o, w1, w2, last, next_jobs))
        if last:
            xp, xs = outs
        else:
            *xs, w_qkv, w_o, w1, w2, mod = outs

    n_kind = DEPTH // 2
    diff_k, diff_v = (a.reshape(BATCH, n_kind, SEQ, N_LANE_BLOCKS, LANES) for a in new_kv[0])
    na_k, na_v = (jnp.transpose(a.reshape(BATCH, n_kind, NA_HEADS, HEAD_DIM, SEQ), (0, 1, 4, 2, 3))
                  for a in new_kv[1])
    return (xp.reshape(BATCH, SEQ, D_MODEL), xs.reshape(DEC_BATCH, DEC_SEQ, D_MODEL),
            diff_k, diff_v, na_k, na_v)
```

```python
import functools
import math
from typing import Callable, NamedTuple

import jax
import jax.numpy as jnp
from jax import lax
from jax.experimental import pallas as pl
from jax.experimental.pallas import tpu as pltpu

D_MODEL = 1024
BATCH = 16
SEQ = 256
DEPTH = 4
DEC_BATCH = 4
DEC_SEQ = 1024
PAST_LEN = 256
GRID_W = 64
GRID_ROWS = DEC_SEQ // GRID_W
HEAD_DIM = 64
D_FF = 4 * D_MODEL
NA_HEADS = 16
NA_KH = 8
NA_KW = 16
ROPE_BASE = 10000.0
EPS = 1e-6
NEG_INF = -1e30
LOG2_E = math.log2(math.e)
SCORE_SCALE = HEAD_DIM ** -0.5 * LOG2_E

LANES = 128
N_LANE_BLOCKS = D_MODEL // LANES
N_PROMPT_ROWS = BATCH * SEQ
N_SAMPLE_ROWS = DEC_BATCH * DEC_SEQ
N_ROWS = N_PROMPT_ROWS + N_SAMPLE_ROWS
ROW_TILE = 512
N_ROW_TILES = N_ROWS // ROW_TILE
N_PROMPT_TILES = N_PROMPT_ROWS // ROW_TILE
N_LATENT_TILES = N_SAMPLE_ROWS // ROW_TILE
TILES_PER_SAMPLE = DEC_SEQ // ROW_TILE
REQ_PER_TILE = ROW_TILE // SEQ
N_COND = 8
V7X_VMEM_BYTES = 64 * 1024 * 1024
VMEM_LIMIT = V7X_VMEM_BYTES // 8 * 7

F32 = jnp.float32
BF16 = jnp.bfloat16
NT_DIMS = (((1,), (1,)), ((), ()))


def _dot(a, b):
    return jnp.dot(a, b, preferred_element_type=F32)


def _dot_nt(a, b):
    return lax.dot_general(a, b, NT_DIMS, preferred_element_type=F32)


def _params(n_axes):
    return pltpu.CompilerParams(dimension_semantics=("arbitrary",) * n_axes,
                                vmem_limit_bytes=VMEM_LIMIT)


def _resident(shape):
    return pl.BlockSpec(shape, lambda *_: (0,) * len(shape), pipeline_mode=pl.Buffered(1))


def _layer_slice(shape, layer):
    return pl.BlockSpec((None,) + shape, lambda *_: (layer,) + (0,) * len(shape),
                        pipeline_mode=pl.Buffered(1))


class _SideJob(NamedTuple):
    args: list
    in_specs: list
    out_spec: pl.BlockSpec
    out_shape: jax.ShapeDtypeStruct
    body: Callable


def _cast_body(src_ref, dst_ref):
    dst_ref[...] = src_ref[...].astype(BF16)


def _cast_job(weights, layer, split_axis, n_steps):
    _, rows, cols = weights.shape
    if split_axis == 0:
        band = (rows // n_steps, cols)
        src_spec = pl.BlockSpec((None,) + band, lambda i: (layer, i, 0))
        dst_spec = pl.BlockSpec(band, lambda i: (i, 0))
    else:
        band = (rows, cols // n_steps)
        src_spec = pl.BlockSpec((None,) + band, lambda i: (layer, 0, i))
        dst_spec = pl.BlockSpec(band, lambda i: (0, i))
    return _SideJob([weights], [src_spec], dst_spec, jax.ShapeDtypeStruct((rows, cols), BF16), _cast_body)


def _pallas_call_with_side_jobs(kernel_fn, jobs, *, grid, in_specs, out_specs, out_shape, args, name,
                                input_output_aliases=None):
    n_in, n_out = len(in_specs), len(out_specs)
    n_job_in = [len(job.args) for job in jobs]

    def kernel_and_jobs(*refs):
        ins, refs = refs[:n_in], refs[n_in:]
        job_ins, refs = refs[:sum(n_job_in)], refs[sum(n_job_in):]
        outs, refs = refs[:n_out], refs[n_out:]
        job_outs, scratch = refs[:len(jobs)], refs[len(jobs):]
        for job, n, out_ref in zip(jobs, n_job_in, job_outs):
            job.body(*job_ins[:n], out_ref)
            job_ins = job_ins[n:]
        kernel_fn(*ins, *outs, *scratch)

    return pl.pallas_call(
        kernel_and_jobs,
        grid=grid,
        in_specs=list(in_specs) + [spec for job in jobs for spec in job.in_specs],
        out_specs=list(out_specs) + [job.out_spec for job in jobs],
        out_shape=list(out_shape) + [job.out_shape for job in jobs],
        input_output_aliases=input_output_aliases or {},
        compiler_params=_params(1),
        name=name,
    )(*args, *(arg for job in jobs for arg in job.args))


def _prompt_tile(i):
    return jnp.minimum(i, N_PROMPT_TILES - 1)


def _sample_tile(i):
    return jnp.maximum(i - N_PROMPT_TILES, 0)


def _cond_row(i):
    return jnp.where(i < N_PROMPT_TILES, 0, 1 + (i - N_PROMPT_TILES) // TILES_PER_SAMPLE)


def _low_half_mask(shape):
    return lax.broadcasted_iota(jnp.int32, shape, len(shape) - 1) < HEAD_DIM


def _ada_kernel(cond_ref, w_ref, b_ref, out_ref):
    cond = cond_ref[...]
    act = (cond * jax.nn.sigmoid(cond)).astype(BF16)
    out_ref[...] = _dot(act, w_ref[...].astype(BF16)) + b_ref[...]


def _ada_specs(layer, n_steps):
    tn = 6 * D_MODEL // n_steps
    in_specs = [pl.BlockSpec((N_COND, D_MODEL), lambda j: (0, 0)),
                pl.BlockSpec((None, D_MODEL, tn), lambda j: (layer, 0, j)),
                pl.BlockSpec((None, 1, tn), lambda j: (layer, 0, j))]
    return in_specs, pl.BlockSpec((N_COND, tn), lambda j: (0, j)), jax.ShapeDtypeStruct((N_COND, 6 * D_MODEL), F32)


def _modulation(cond, w_ada, b_ada, layer):
    n_steps = 8
    in_specs, out_spec, out_shape = _ada_specs(layer, n_steps)
    return pl.pallas_call(_ada_kernel, grid=(n_steps,), in_specs=in_specs, out_specs=out_spec,
                          out_shape=out_shape, compiler_params=_params(1),
                          name="ada_modulation")(cond, w_ada, b_ada)


def _ada_job(cond, w_ada, b_ada, layer, n_steps):
    in_specs, out_spec, out_shape = _ada_specs(layer, n_steps)
    return _SideJob([cond, w_ada, b_ada], in_specs, out_spec, out_shape, _ada_kernel)


def _group_rms(x, gain):
    low = _low_half_mask((1, LANES))
    cols = []
    for c in range(N_LANE_BLOCKS):
        xb = x[:, c * LANES:(c + 1) * LANES]
        sq = xb * xb
        s_lo = jnp.sum(jnp.where(low, sq, 0.0), axis=-1, keepdims=True)
        s_hi = jnp.sum(jnp.where(low, 0.0, sq), axis=-1, keepdims=True)
        ms = jnp.where(low, s_lo, s_hi) * (1.0 / HEAD_DIM)
        cols.append(xb * lax.rsqrt(ms + EPS) * gain)
    return cols


def _rope(cols, cos, sin_signed):
    lane = lax.broadcasted_iota(jnp.int32, (1, LANES), 1)
    first = (lane % 32) < 16
    out = []
    for xb in cols:
        partner = jnp.where(first, pltpu.roll(xb, LANES - 16, 1), pltpu.roll(xb, 16, 1))
        out.append(xb * cos + partner * sin_signed)
    return out


def _project_qkv(x_ref, mod_ref, g_ref, w_ref, qg_ref, kg_ref, r):
    x = x_ref[...]
    ms = jnp.mean(x * x, axis=-1, keepdims=True)
    xn = x * lax.rsqrt(ms + EPS) * g_ref[...]
    shift = mod_ref[pl.ds(r, 1), 0:D_MODEL]
    scale = mod_ref[pl.ds(r, 1), D_MODEL:2 * D_MODEL]
    h = (xn * (1.0 + scale) + shift).astype(BF16)
    qkv = _dot(h, w_ref[...])
    qn = _group_rms(qkv[:, 0:D_MODEL], qg_ref[...])
    kn = _group_rms(qkv[:, D_MODEL:2 * D_MODEL], kg_ref[...])
    return qn, kn, qkv[:, 2 * D_MODEL:3 * D_MODEL]


def _lane_block(c):
    return slice(c * LANES, (c + 1) * LANES)


def _qkv_latent_kernel(is_diff, x_ref, mod_ref, g_ref, w_ref, qg_ref, kg_ref, cos_ref, sin_ref,
                       q_ref, k_ref, v_ref):
    r = 1 + pl.program_id(0) // TILES_PER_SAMPLE
    qn, kn, v = _project_qkv(x_ref, mod_ref, g_ref, w_ref, qg_ref, kg_ref, r)
    v_ref[...] = v.astype(BF16)
    if is_diff:
        cos = cos_ref[...]
        sin = sin_ref[...]
        qn = _rope(qn, cos, sin)
        kn = _rope(kn, cos, sin)
    for c in range(N_LANE_BLOCKS):
        q_ref[:, _lane_block(c)] = (qn[c] * SCORE_SCALE).astype(BF16)
        k_ref[:, _lane_block(c)] = kn[c].astype(BF16)


def _ctx_layer_kernel(is_diff, lambda_init, has_prev, *refs):
    if has_prev:
        refs = refs[2:]
    (x_ref, mod_ref, g_ref, w_ref, qg_ref, kg_ref, lq1_ref, lk1_ref, lq2_ref, lk2_ref, gsub_ref,
     o_ref, kf_ref, vf_ref) = refs
    qn, kn, v = _project_qkv(x_ref, mod_ref, g_ref, w_ref, qg_ref, kg_ref, 0)
    for c in range(N_LANE_BLOCKS):
        for bb in range(REQ_PER_TILE):
            rows = slice(bb * SEQ, (bb + 1) * SEQ)
            if is_diff:
                dst = pl.ds(c, SEQ, stride=N_LANE_BLOCKS)
                kf_ref[bb, dst, :] = kn[c][rows, :]
                vf_ref[bb, dst, :] = v[rows, _lane_block(c)]
            else:
                kf_ref[bb, _lane_block(c), :] = kn[c][rows, :].T
                vf_ref[bb, _lane_block(c), :] = v[rows, _lane_block(c)].T
    q = [(qn[c] * SCORE_SCALE).astype(BF16) for c in range(N_LANE_BLOCKS)]
    k = [kn[c].astype(BF16) for c in range(N_LANE_BLOCKS)]
    vb = [v[:, _lane_block(c)].astype(BF16) for c in range(N_LANE_BLOCKS)]
    if is_diff:
        lam = _diff_lambda(lq1_ref, lk1_ref, lq2_ref, lk2_ref, lambda_init)
        combine = _diff_combine(lam, gsub_ref[...], lambda_init)
    else:
        combine = _head_pair_combine

    def block(b):
        bb, c = divmod(b, N_LANE_BLOCKS)
        return slice(bb * SEQ, (bb + 1) * SEQ), c

    def load_q(b):
        rows, c = block(b)
        return q[c][rows, :]

    def scores(b, q2):
        rows, c = block(b)
        return [_dot_nt(q2, k[c][rows, :])]

    def values(b):
        rows, c = block(b)
        return [_times(vb[c][rows, :])]

    def store(b, o):
        rows, c = block(b)
        o_ref[rows, _lane_block(c)] = o.astype(BF16)

    _attend_pairs(REQ_PER_TILE * N_LANE_BLOCKS, load_q, scores, values, combine, store)


def _qkv_latent(x, first_tile, mod, layer, g_mix, w_qkv, q_gain, k_gain, rope_tabs, is_diff, side_jobs):
    row_spec = pl.BlockSpec((ROW_TILE, D_MODEL), lambda i: (i, 0))
    rope_spec = pl.BlockSpec((ROW_TILE, LANES), lambda i: (i % TILES_PER_SAMPLE, 0))
    return _pallas_call_with_side_jobs(
        functools.partial(_qkv_latent_kernel, is_diff), side_jobs,
        grid=(N_LATENT_TILES,),
        in_specs=[pl.BlockSpec((ROW_TILE, D_MODEL), lambda i: (first_tile + i, 0)),
                  _resident((N_COND, 6 * D_MODEL)),
                  _layer_slice((1, D_MODEL), layer),
                  _resident((D_MODEL, 3 * D_MODEL)),
                  _resident((1, LANES)), _resident((1, LANES)), rope_spec, rope_spec],
        out_specs=[row_spec] * 3,
        out_shape=[jax.ShapeDtypeStruct((N_SAMPLE_ROWS, D_MODEL), BF16)] * 3,
        args=[x, mod, g_mix, w_qkv, q_gain, k_gain, *rope_tabs],
        name="qkv_latent",
    )


def _ctx_layer(x, mod, layer, g_mix, w_qkv, kind_layer, q_gain, k_gain, lam_params, gsub, is_diff,
               lambda_init, prev, side_jobs):
    row_spec = pl.BlockSpec((ROW_TILE, D_MODEL), lambda i: (i, 0))
    if is_diff:
        cache_rows, cache_cols = SEQ * N_LANE_BLOCKS, LANES
    else:
        cache_rows, cache_cols = D_MODEL, SEQ
    cache_spec = pl.BlockSpec((REQ_PER_TILE, None, cache_rows, cache_cols), lambda i: (i, kind_layer, 0, 0))
    cache_shape = jax.ShapeDtypeStruct((BATCH, DEPTH // 2, cache_rows, cache_cols), F32)
    in_specs = ([row_spec, _resident((N_COND, 6 * D_MODEL)), _layer_slice((1, D_MODEL), layer),
                 _resident((D_MODEL, 3 * D_MODEL)), _resident((1, LANES)), _resident((1, LANES))]
                + [_resident((1, HEAD_DIM))] * 4 + [_resident((1, LANES))])
    args = [x, mod, g_mix, w_qkv, q_gain, k_gain, *lam_params, gsub]
    aliases = {}
    if prev is not None:
        in_specs = [pl.BlockSpec(memory_space=pl.ANY)] * 2 + in_specs
        args = list(prev) + args
        aliases = {0: 1, 1: 2}
    return _pallas_call_with_side_jobs(
        functools.partial(_ctx_layer_kernel, is_diff, lambda_init, prev is not None), side_jobs,
        grid=(N_PROMPT_TILES,),
        in_specs=in_specs,
        out_specs=[row_spec, cache_spec, cache_spec],
        out_shape=[jax.ShapeDtypeStruct((N_PROMPT_ROWS, D_MODEL), BF16), cache_shape, cache_shape],
        args=args,
        input_output_aliases=aliases,
        name="ctx_layer",
    )


def _softmax_pv(score_parts, value_parts):
    m = None
    for s in score_parts:
        pm = jnp.max(s, axis=-1, keepdims=True)
        m = pm if m is None else jnp.maximum(m, pm)
    acc = None
    for s, times_v in zip(score_parts, value_parts):
        pv = times_v(jnp.exp2(s - m).astype(BF16))
        acc = pv if acc is None else acc + pv
    return acc[:, :LANES] * (1.0 / acc[:, LANES:])


def _times(v):
    v1 = jnp.concatenate([v, jnp.ones_like(v)], axis=1)
    return lambda w: _dot(w, v1)


def _times_t(v_t):
    v1_t = jnp.concatenate([v_t, jnp.ones_like(v_t)], axis=0)
    return lambda w: _dot_nt(w, v1_t)


def _diff_lambda(lq1_ref, lk1_ref, lq2_ref, lk2_ref, lambda_init):
    e1 = jnp.exp(jnp.sum(lq1_ref[...] * lk1_ref[...], axis=-1, keepdims=True))
    e2 = jnp.exp(jnp.sum(lq2_ref[...] * lk2_ref[...], axis=-1, keepdims=True))
    return e1 - e2 + lambda_init


def _attend_pairs(n_blocks, load_q, scores, values, combine, store):
    def block_scores(b):
        q = load_q(b)
        low = _low_half_mask(q.shape)
        zero = jnp.zeros_like(q)
        return scores(b, jnp.concatenate([jnp.where(low, q, zero), jnp.where(low, zero, q)], axis=0))

    s = block_scores(0)
    for b in range(n_blocks):
        s_next = block_scores(b + 1) if b + 1 < n_blocks else None
        o = _softmax_pv(s, values(b))
        s = s_next
        rows = o.shape[0] // 2
        store(b, combine(o[:rows], o[rows:]))


def _diff_combine(lam, gsub, lambda_init):
    def combine(o1, o2):
        o = o1 - lam * o2
        ms = jnp.mean(o * o, axis=-1, keepdims=True)
        return o * lax.rsqrt(ms + EPS) * gsub * (1.0 - lambda_init)
    return combine


def _head_pair_combine(o_lo, o_hi):
    return jnp.where(_low_half_mask(o_lo.shape), o_lo, o_hi)


Q_CHUNK = 512
LATENT_HEADS_PER_STEP = 2


def _diff_latent_kernel(lambda_init, q_ref, k_ref, v_ref, ck_ref, cv_ref,
                        lq1_ref, lk1_ref, lq2_ref, lk2_ref, gsub_ref, o_ref):
    lam = _diff_lambda(lq1_ref, lk1_ref, lq2_ref, lk2_ref, lambda_init)
    n_chunks = DEC_SEQ // Q_CHUNK
    keys, values = [], []
    for hh in range(LATENT_HEADS_PER_STEP):
        cols = slice(hh * LANES, (hh + 1) * LANES)
        head = pl.program_id(1) * LATENT_HEADS_PER_STEP + hh
        head_rows = pl.ds(head, PAST_LEN, stride=N_LANE_BLOCKS)
        keys.append([k_ref[:, cols], ck_ref[head_rows, :].astype(BF16)])
        values.append([_times(v_ref[:, cols]), _times(cv_ref[head_rows, :].astype(BF16))])

    def block(b):
        hh, c = divmod(b, n_chunks)
        return slice(c * Q_CHUNK, (c + 1) * Q_CHUNK), slice(hh * LANES, (hh + 1) * LANES)

    def store(b, o):
        o_ref[block(b)] = o.astype(BF16)

    _attend_pairs(LATENT_HEADS_PER_STEP * n_chunks,
                  lambda b: q_ref[block(b)],
                  lambda b, q2: [_dot_nt(q2, kpart) for kpart in keys[b // n_chunks]],
                  lambda b: values[b // n_chunks],
                  _diff_combine(lam, gsub_ref[...], lambda_init), store)


def _diff_latent_attention(q, k, v, cache_k, cache_v, layer_i, lam_params, gsub, lambda_init):
    blk = pl.BlockSpec((DEC_SEQ, LATENT_HEADS_PER_STEP * LANES), lambda b, h: (b, h))
    cache = pl.BlockSpec((None, None, PAST_LEN * N_LANE_BLOCKS, LANES), lambda b, h: (b, layer_i, 0, 0))
    small = [_resident((1, HEAD_DIM))] * 4 + [_resident((1, LANES))]
    return pl.pallas_call(
        functools.partial(_diff_latent_kernel, lambda_init),
        grid=(DEC_BATCH, N_LANE_BLOCKS // LATENT_HEADS_PER_STEP),
        in_specs=[blk, blk, blk, cache, cache] + small,
        out_specs=blk,
        out_shape=jax.ShapeDtypeStruct((N_SAMPLE_ROWS, D_MODEL), BF16),
        compiler_params=_params(2),
        name="diff_latent_attention",
    )(q, k, v, cache_k, cache_v, *lam_params, gsub)


class _NaGroup(NamedTuple):
    q_row: int
    n_q: int
    k_row: int
    n_k: int


def _window_start(r):
    return min(max(r - NA_KH // 2, 0), GRID_ROWS - NA_KH)


def _na_groups(rows_per_group):
    groups = []
    for q_row in range(0, GRID_ROWS, rows_per_group):
        first = _window_start(q_row)
        last = _window_start(q_row + rows_per_group - 1) + NA_KH
        k_row = first - first % 2
        n_k = last - k_row
        n_k += n_k % 2
        groups.append(_NaGroup(q_row, rows_per_group, min(k_row, GRID_ROWS - n_k), n_k))
    return groups


NA_GROUPS = _na_groups(4)
N_DR = 2 * NA_KH - 1
N_DC = 2 * NA_KW - 1
N_DR_PAD = 16


def _expand_na_bias(rb_ref, tile_ref, bias_refs):
    qc = lax.broadcasted_iota(jnp.int32, (GRID_W, LANES), 0)
    lane = lax.broadcasted_iota(jnp.int32, (GRID_W, LANES), 1)
    kc = lane % GRID_W
    cs = jnp.clip(qc - NA_KW // 2, 0, GRID_W - NA_KW)
    col_ok = (kc >= cs) & (kc < cs + NA_KW)
    low = lane < GRID_W
    neg = jnp.full((GRID_W, LANES), NEG_INF, F32)
    for head in range(2 * LATENT_HEADS_PER_STEP):
        for dr in range(N_DR):
            row = jnp.broadcast_to(rb_ref[head, dr:dr + 1, :], (GRID_W, LANES))
            shifted = pltpu.roll(row, LANES - (NA_KW - 1), 1, stride=1, stride_axis=0)
            tile_ref[dr] = jnp.where(col_ok, shifted * LOG2_E, neg)
        for grp, bias_ref in zip(NA_GROUPS, bias_refs):
            for rq in range(grp.n_q):
                r = grp.q_row + rq
                rs = _window_start(r)

                def tile(kr, r=r, rs=rs):
                    if rs <= kr < rs + NA_KH:
                        return tile_ref[kr - r + NA_KH - 1]
                    return neg

                for p in range(grp.n_k // 2):
                    kr = grp.k_row + 2 * p
                    bias_ref[head, rq * GRID_W:(rq + 1) * GRID_W, p * LANES:(p + 1) * LANES] = (
                        jnp.where(low, tile(kr), tile(kr + 1)))


def _na_latent_kernel(q_ref, k_ref, v_ref, ckt_ref, cvt_ref, rb_ref, o_ref, tile_ref, *bias_refs):
    @pl.when(pl.program_id(1) == 0)
    def _():
        _expand_na_bias(rb_ref, tile_ref, bias_refs)

    ck_t = [ckt_ref[hh * LANES:(hh + 1) * LANES, :].astype(BF16) for hh in range(LATENT_HEADS_PER_STEP)]
    cv_t = [cvt_ref[hh * LANES:(hh + 1) * LANES, :].astype(BF16) for hh in range(LATENT_HEADS_PER_STEP)]

    def block(b):
        return divmod(b, len(NA_GROUPS))

    def cols(hh):
        return slice(hh * LANES, (hh + 1) * LANES)

    def qrows(g):
        return slice(NA_GROUPS[g].q_row * GRID_W, (NA_GROUPS[g].q_row + NA_GROUPS[g].n_q) * GRID_W)

    def krows(g):
        return slice(NA_GROUPS[g].k_row * GRID_W, (NA_GROUPS[g].k_row + NA_GROUPS[g].n_k) * GRID_W)

    def load_q(b):
        hh, g = block(b)
        return q_ref[qrows(g), cols(hh)]

    def scores(b, q2):
        hh, g = block(b)
        bias = jnp.concatenate([bias_refs[g][2 * hh], bias_refs[g][2 * hh + 1]], axis=0)
        return [_dot_nt(q2, k_ref[krows(g), cols(hh)]) + bias, _dot(q2, ck_t[hh])]

    def values(b):
        hh, g = block(b)
        return [_times(v_ref[krows(g), cols(hh)]), _times_t(cv_t[hh])]

    def store(b, o):
        hh, g = block(b)
        o_ref[qrows(g), cols(hh)] = o.astype(BF16)

    _attend_pairs(LATENT_HEADS_PER_STEP * len(NA_GROUPS), load_q, scores, values, _head_pair_combine, store)


def _na_latent_attention(q, k, v, cache_k_t, cache_v_t, layer_i, rel_bias_rows):
    width = LATENT_HEADS_PER_STEP * LANES
    blk = pl.BlockSpec((DEC_SEQ, width), lambda p, b: (b, p))
    cache = pl.BlockSpec((None, None, width, PAST_LEN), lambda p, b: (b, layer_i, p, 0))
    rb_spec = pl.BlockSpec((None, 2 * LATENT_HEADS_PER_STEP, N_DR_PAD, LANES), lambda p, b: (layer_i, p, 0, 0))
    return pl.pallas_call(
        _na_latent_kernel,
        grid=(N_LANE_BLOCKS // LATENT_HEADS_PER_STEP, DEC_BATCH),
        in_specs=[blk, blk, blk, cache, cache, rb_spec],
        out_specs=blk,
        out_shape=jax.ShapeDtypeStruct((N_SAMPLE_ROWS, D_MODEL), BF16),
        scratch_shapes=[pltpu.VMEM((N_DR, GRID_W, LANES), F32)] + [
            pltpu.VMEM((2 * LATENT_HEADS_PER_STEP, grp.n_q * GRID_W, grp.n_k * GRID_W), F32)
            for grp in NA_GROUPS],
        compiler_params=_params(2),
        name="na_latent_attention",
    )(q, k, v, cache_k_t, cache_v_t, rel_bias_rows)


def _padded_rel_bias(rel_bias_na):
    half = jnp.pad(rel_bias_na, ((0, 0), (0, 0), (0, N_DR_PAD - N_DR), (0, GRID_W - N_DC)))
    return jnp.concatenate([half, half], axis=-1)


FF_CHUNK = 1024


def _post_kernel(split_in, split_out, *refs):
    n_x = 2 if split_in else 1
    x_refs = refs[:n_x]
    op_ref, os_ref, mod_ref, g_ref, wo_ref, w1_ref, w2_ref = refs[n_x:n_x + 7]
    y_refs = refs[n_x + 7:]
    i = pl.program_id(0)
    r = _cond_row(i)
    is_prompt = i < N_PROMPT_TILES

    def mod(j):
        return mod_ref[pl.ds(r, 1), j * D_MODEL:(j + 1) * D_MODEL]

    x = jnp.where(is_prompt, x_refs[0][...], x_refs[1][...]) if split_in else x_refs[0][...]
    o = jnp.where(is_prompt, op_ref[...], os_ref[...])
    y1 = x + mod(2) * _dot(o, wo_ref[...])
    ms = jnp.mean(y1 * y1, axis=-1, keepdims=True)
    h = (y1 * lax.rsqrt(ms + EPS) * g_ref[...] * (1.0 + mod(4)) + mod(3)).astype(BF16)
    acc = jnp.zeros((ROW_TILE, D_MODEL), F32)
    for c in range(D_FF // FF_CHUNK):
        cols = slice(c * FF_CHUNK, (c + 1) * FF_CHUNK)
        hid = jnp.maximum(_dot(h, w1_ref[:, cols]), 0.0)
        acc = acc + _dot((hid * hid).astype(BF16), w2_ref[cols, :])
    y = y1 + mod(5) * acc
    if split_out:
        @pl.when(is_prompt)
        def _():
            y_refs[0][...] = y

        @pl.when(jnp.logical_not(is_prompt))
        def _():
            y_refs[1][...] = y
    else:
        y_refs[0][...] = y


def _post(xs, o_p, o_s, mod, layer, g_mlp, w_o, w_fc1, w_fc2, split_out, side_jobs):
    prompt_spec = pl.BlockSpec((ROW_TILE, D_MODEL), lambda i: (_prompt_tile(i), 0))
    sample_spec = pl.BlockSpec((ROW_TILE, D_MODEL), lambda i: (_sample_tile(i), 0))
    row_spec = pl.BlockSpec((ROW_TILE, D_MODEL), lambda i: (i, 0))
    split_in = len(xs) == 2
    if split_out:
        out_specs = [prompt_spec, sample_spec]
        out_shape = [jax.ShapeDtypeStruct((N_PROMPT_ROWS, D_MODEL), F32),
                     jax.ShapeDtypeStruct((N_SAMPLE_ROWS, D_MODEL), F32)]
    else:
        out_specs = [row_spec]
        out_shape = [jax.ShapeDtypeStruct((N_ROWS, D_MODEL), F32)]
    return _pallas_call_with_side_jobs(
        functools.partial(_post_kernel, split_in, split_out), side_jobs,
        grid=(N_ROW_TILES,),
        in_specs=([prompt_spec, sample_spec] if split_in else [row_spec]) + [
            prompt_spec, sample_spec,
            _resident((N_COND, 6 * D_MODEL)),
            _layer_slice((1, D_MODEL), layer),
            _resident((D_MODEL, D_MODEL)),
            _resident((D_MODEL, D_FF)),
            _resident((D_FF, D_MODEL)),
        ],
        out_specs=out_specs,
        out_shape=out_shape,
        args=[*xs, o_p, o_s, mod, g_mlp, w_o, w_fc1, w_fc2],
        name="wo_mlp",
    )


def _rope_tables():
    n_freq = HEAD_DIM // 4
    freqs = ROPE_BASE ** (-jnp.arange(n_freq, dtype=F32) / n_freq)
    t = jnp.arange(DEC_SEQ)
    ang_row = (t // GRID_W).astype(F32)[:, None] * freqs
    ang_col = (t % GRID_W).astype(F32)[:, None] * freqs
    cos64 = jnp.concatenate([jnp.cos(ang_row)] * 2 + [jnp.cos(ang_col)] * 2, axis=-1)
    sin64 = jnp.concatenate([-jnp.sin(ang_row), jnp.sin(ang_row),
                             -jnp.sin(ang_col), jnp.sin(ang_col)], axis=-1)
    return jnp.tile(cos64, (1, 2)), jnp.tile(sin64, (1, 2))


def _pair_gain(g):
    return jnp.tile(g.reshape(1, HEAD_DIM), (1, LANES // HEAD_DIM))


def _head_major_cache(cache):
    n_req, n_layers = cache.shape[:2]
    return jnp.transpose(cache, (0, 1, 3, 4, 2)).reshape(n_req, n_layers, D_MODEL, PAST_LEN)


def kernel(x_prompt, x_sample, cache_diff_k, cache_diff_v, cache_na_k, cache_na_v, c, c_ctx, w_ada, b_ada, norm_mix_g, norm_mlp_g, w_fc1, w_fc2, w_qkv_diff, w_o_diff, q_norm_diff_g, k_norm_diff_g, lambda_q1, lambda_k1, lambda_q2, lambda_k2, subln_g, w_qkv_na, w_o_na, q_norm_na_g, k_norm_na_g, rel_bias_na):
    cond = jnp.concatenate([c_ctx[None, :], c, jnp.zeros((N_COND - 1 - DEC_BATCH, D_MODEL), F32)], axis=0)
    b_ada = b_ada.reshape(DEPTH, 1, 6 * D_MODEL)
    mod = _modulation(cond, w_ada, b_ada, 0)
    cos_tab, sin_tab = _rope_tables()
    na_bias = _padded_rel_bias(rel_bias_na)
    cdk = cache_diff_k.reshape(DEC_BATCH, DEPTH // 2, PAST_LEN * N_LANE_BLOCKS, LANES)
    cdv = cache_diff_v.reshape(DEC_BATCH, DEPTH // 2, PAST_LEN * N_LANE_BLOCKS, LANES)
    cnk_t = _head_major_cache(cache_na_k)
    cnv_t = _head_major_cache(cache_na_v)
    g_mix = norm_mix_g.reshape(DEPTH, 1, D_MODEL)
    g_mlp = norm_mlp_g.reshape(DEPTH, 1, D_MODEL)
    w_qkv_f32 = (w_qkv_diff, w_qkv_na)
    w_o_f32 = (w_o_diff, w_o_na)
    q_gain = (q_norm_diff_g, q_norm_na_g)
    k_gain = (k_norm_diff_g, k_norm_na_g)

    xs = [x_prompt.reshape(N_PROMPT_ROWS, D_MODEL), x_sample.reshape(N_SAMPLE_ROWS, D_MODEL)]
    new_kv = [None, None]
    w_qkv = w_qkv_diff[0].astype(BF16)
    w_o = w1 = w2 = None
    for l in range(DEPTH):
        i = l // 2
        kind = l % 2
        is_diff = kind == 0
        first = l == 0
        last = l == DEPTH - 1
        x_prompt_src, x_sample_src = (xs[0], xs[1]) if len(xs) == 2 else (xs[0], xs[0])
        sample_first_tile = 0 if len(xs) == 2 else N_PROMPT_TILES
        gains = (_pair_gain(q_gain[kind][i]), _pair_gain(k_gain[kind][i]))
        lambda_init = 0.8 - 0.6 * math.exp(-0.3 * l)
        lam_params = [p[i if is_diff else 0].reshape(1, HEAD_DIM)
                      for p in (lambda_q1, lambda_k1, lambda_q2, lambda_k2)]
        gsub = subln_g[i if is_diff else 0].reshape(1, LANES)
        o_p, new_k, new_v, *cast = _ctx_layer(
            x_prompt_src, mod, l, g_mix, w_qkv, i, *gains, lam_params, gsub, is_diff, lambda_init,
            new_kv[kind],
            [_cast_job(w_fc1, l, 1, N_PROMPT_TILES), _cast_job(w_o_f32[kind], i, 0, N_PROMPT_TILES)]
            if first else [])
        w1, w_o = cast if first else (w1, w_o)
        new_kv[kind] = (new_k, new_v)
        qs, ks, vs, *cast = _qkv_latent(
            x_sample_src, sample_first_tile, mod, l, g_mix, w_qkv, *gains, (cos_tab, sin_tab), is_diff,
            [_cast_job(w_fc2, l, 0, N_LATENT_TILES)] if first else [])
        w2 = cast[0] if first else w2
        if is_diff:
            o_s = _diff_latent_attention(qs, ks, vs, cdk, cdv, i, lam_params, gsub, lambda_init)
        else:
            o_s = _na_latent_attention(qs, ks, vs, cnk_t, cnv_t, i, na_bias)
        nxt, nxt_kind, nxt_i = l + 1, (l + 1) % 2, (l + 1) // 2
        next_jobs = [] if last else [_cast_job(w_qkv_f32[nxt_kind], nxt_i, 0, N_ROW_TILES),
                                     _cast_job(w_o_f32[nxt_kind], nxt_i, 0, N_ROW_TILES),
                                     _cast_job(w_fc1, nxt, 1, N_ROW_TILES),
                                     _cast_job(w_fc2, nxt, 0, N_ROW_TILES),
                                     _ada_job(cond, w_ada, b_ada, nxt, N_ROW_TILES)]
        outs = list(_post(xs, o_p, o_s, mod, l, g_mlp, w_o, w1, w2, last, next_jobs))
        if last:
            xp, xs = outs
        else:
            *xs, w_qkv, w_o, w1, w2, mod = outs

    n_kind = DEPTH // 2
    diff_k, diff_v = (a.reshape(BATCH, n_kind, SEQ, N_LANE_BLOCKS, LANES) for a in new_kv[0])
    na_k, na_v = (jnp.transpose(a.reshape(BATCH, n_kind, NA_HEADS, HEAD_DIM, SEQ), (0, 1, 4, 2, 3))
                  for a in new_kv[1])
    return (xp.reshape(BATCH, SEQ, D_MODEL), xs.reshape(DEC_BATCH, DEC_SEQ, D_MODEL),
            diff_k, diff_v, na_k, na_v)
```

```python
import functools
import math
from typing import Callable, NamedTuple

import jax
import jax.numpy as jnp
from jax import lax
from jax.experimental import pallas as pl
from jax.experimental.pallas import tpu as pltpu

D_MODEL = 1024
BATCH = 16
SEQ = 256
DEPTH = 4
DEC_BATCH = 4
DEC_SEQ = 1024
PAST_LEN = 256
GRID_W = 64
GRID_ROWS = DEC_SEQ // GRID_W
HEAD_DIM = 64
D_FF = 4 * D_MODEL
NA_HEADS = 16
NA_KH = 8
NA_KW = 16
ROPE_BASE = 10000.0
EPS = 1e-6
NEG_INF = -1e30
LOG2_E = math.log2(math.e)
SCORE_SCALE = HEAD_DIM ** -0.5 * LOG2_E

LANES = 128
N_LANE_BLOCKS = D_MODEL // LANES
N_PROMPT_ROWS = BATCH * SEQ
N_SAMPLE_ROWS = DEC_BATCH * DEC_SEQ
N_ROWS = N_PROMPT_ROWS + N_SAMPLE_ROWS
ROW_TILE = 512
N_ROW_TILES = N_ROWS // ROW_TILE
N_PROMPT_TILES = N_PROMPT_ROWS // ROW_TILE
N_LATENT_TILES = N_SAMPLE_ROWS // ROW_TILE
TILES_PER_SAMPLE = DEC_SEQ // ROW_TILE
REQ_PER_TILE = ROW_TILE // SEQ
N_COND = 8
V7X_VMEM_BYTES = 64 * 1024 * 1024
VMEM_LIMIT = V7X_VMEM_BYTES // 8 * 7

F32 = jnp.float32
BF16 = jnp.bfloat16
NT_DIMS = (((1,), (1,)), ((), ()))


def _dot(a, b):
    return jnp.dot(a, b, preferred_element_type=F32)


def _dot_nt(a, b):
    return lax.dot_general(a, b, NT_DIMS, preferred_element_type=F32)


def _params(n_axes):
    return pltpu.CompilerParams(dimension_semantics=("arbitrary",) * n_axes,
                                vmem_limit_bytes=VMEM_LIMIT)


def _resident(shape):
    return pl.BlockSpec(shape, lambda *_: (0,) * len(shape), pipeline_mode=pl.Buffered(1))


def _layer_slice(shape, layer):
    return pl.BlockSpec((None,) + shape, lambda *_: (layer,) + (0,) * len(shape),
                        pipeline_mode=pl.Buffered(1))


class _SideJob(NamedTuple):
    args: list
    in_specs: list
    out_spec: pl.BlockSpec
    out_shape: jax.ShapeDtypeStruct
    body: Callable


def _cast_body(src_ref, dst_ref):
    dst_ref[...] = src_ref[...].astype(BF16)


def _step_number(grid):
    def number(*idx):
        step = idx[0]
        for size, i in zip(grid[1:], idx[1:]):
            step = step * size + i
        return step
    return number


def _cast_job(weights, layer, split_axis, grid):
    _, rows, cols = weights.shape
    n_steps, step = math.prod(grid), _step_number(grid)
    if split_axis == 0:
        band = (rows // n_steps, cols)
        src_spec = pl.BlockSpec((None,) + band, lambda *idx: (layer, step(*idx), 0))
        dst_spec = pl.BlockSpec(band, lambda *idx: (step(*idx), 0))
    else:
        band = (rows, cols // n_steps)
        src_spec = pl.BlockSpec((None,) + band, lambda *idx: (layer, 0, step(*idx)))
        dst_spec = pl.BlockSpec(band, lambda *idx: (0, step(*idx)))
    return _SideJob([weights], [src_spec], dst_spec, jax.ShapeDtypeStruct((rows, cols), BF16), _cast_body)


def _pallas_call_with_side_jobs(kernel_fn, jobs, *, grid, in_specs, out_specs, out_shape, args, name,
                                input_output_aliases=None, scratch_shapes=()):
    n_in, n_out = len(in_specs), len(out_specs)
    n_job_in = [len(job.args) for job in jobs]

    def kernel_and_jobs(*refs):
        ins, refs = refs[:n_in], refs[n_in:]
        job_ins, refs = refs[:sum(n_job_in)], refs[sum(n_job_in):]
        outs, refs = refs[:n_out], refs[n_out:]
        job_outs, scratch = refs[:len(jobs)], refs[len(jobs):]
        for job, n, out_ref in zip(jobs, n_job_in, job_outs):
            job.body(*job_ins[:n], out_ref)
            job_ins = job_ins[n:]
        kernel_fn(*ins, *outs, *scratch)

    return pl.pallas_call(
        kernel_and_jobs,
        grid=grid,
        in_specs=list(in_specs) + [spec for job in jobs for spec in job.in_specs],
        out_specs=list(out_specs) + [job.out_spec for job in jobs],
        out_shape=list(out_shape) + [job.out_shape for job in jobs],
        input_output_aliases=input_output_aliases or {},
        scratch_shapes=list(scratch_shapes),
        compiler_params=_params(len(grid)),
        name=name,
    )(*args, *(arg for job in jobs for arg in job.args))


def _prompt_tile(i):
    return jnp.minimum(i, N_PROMPT_TILES - 1)


def _sample_tile(i):
    return jnp.maximum(i - N_PROMPT_TILES, 0)


def _cond_row(i):
    return jnp.where(i < N_PROMPT_TILES, 0, 1 + (i - N_PROMPT_TILES) // TILES_PER_SAMPLE)


def _low_half_mask(shape):
    return lax.broadcasted_iota(jnp.int32, shape, len(shape) - 1) < HEAD_DIM


def _ada_kernel(cond_ref, w_ref, b_ref, out_ref):
    cond = cond_ref[...]
    act = (cond * jax.nn.sigmoid(cond)).astype(BF16)
    out_ref[...] = _dot(act, w_ref[...].astype(BF16)) + b_ref[...]


def _ada_specs(layer, n_steps):
    tn = 6 * D_MODEL // n_steps
    in_specs = [pl.BlockSpec((N_COND, D_MODEL), lambda j: (0, 0)),
                pl.BlockSpec((None, D_MODEL, tn), lambda j: (layer, 0, j)),
                pl.BlockSpec((None, 1, tn), lambda j: (layer, 0, j))]
    return in_specs, pl.BlockSpec((N_COND, tn), lambda j: (0, j)), jax.ShapeDtypeStruct((N_COND, 6 * D_MODEL), F32)


def _modulation(cond, w_ada, b_ada, layer):
    n_steps = 4
    in_specs, out_spec, out_shape = _ada_specs(layer, n_steps)
    return pl.pallas_call(_ada_kernel, grid=(n_steps,), in_specs=in_specs, out_specs=out_spec,
                          out_shape=out_shape, compiler_params=_params(1),
                          name="ada_modulation")(cond, w_ada, b_ada)


def _ada_job(cond, w_ada, b_ada, layer, grid):
    (n_steps,) = grid
    in_specs, out_spec, out_shape = _ada_specs(layer, n_steps)
    return _SideJob([cond, w_ada, b_ada], in_specs, out_spec, out_shape, _ada_kernel)


def _group_rms(x, gain):
    low = _low_half_mask((1, LANES))
    cols = []
    for c in range(N_LANE_BLOCKS):
        xb = x[:, c * LANES:(c + 1) * LANES]
        sq = xb * xb
        s_lo = jnp.sum(jnp.where(low, sq, 0.0), axis=-1, keepdims=True)
        s_hi = jnp.sum(jnp.where(low, 0.0, sq), axis=-1, keepdims=True)
        ms = jnp.where(low, s_lo, s_hi) * (1.0 / HEAD_DIM)
        cols.append(xb * lax.rsqrt(ms + EPS) * gain)
    return cols


def _rope(cols, cos, sin_signed):
    lane = lax.broadcasted_iota(jnp.int32, (1, LANES), 1)
    first = (lane % 32) < 16
    out = []
    for xb in cols:
        partner = jnp.where(first, pltpu.roll(xb, LANES - 16, 1), pltpu.roll(xb, 16, 1))
        out.append(xb * cos + partner * sin_signed)
    return out


def _project_qkv(x_ref, mod_ref, g_ref, w_ref, qg_ref, kg_ref, r):
    x = x_ref[...]
    ms = jnp.mean(x * x, axis=-1, keepdims=True)
    xn = x * lax.rsqrt(ms + EPS) * g_ref[...]
    shift = mod_ref[pl.ds(r, 1), 0:D_MODEL]
    scale = mod_ref[pl.ds(r, 1), D_MODEL:2 * D_MODEL]
    h = (xn * (1.0 + scale) + shift).astype(BF16)
    qkv = _dot(h, w_ref[...])
    qn = _group_rms(qkv[:, 0:D_MODEL], qg_ref[...])
    kn = _group_rms(qkv[:, D_MODEL:2 * D_MODEL], kg_ref[...])
    return qn, kn, qkv[:, 2 * D_MODEL:3 * D_MODEL]


def _lane_block(c):
    return slice(c * LANES, (c + 1) * LANES)


def _qkv_latent_kernel(is_diff, x_ref, mod_ref, g_ref, w_ref, qg_ref, kg_ref, cos_ref, sin_ref,
                       q_ref, k_ref, v_ref):
    r = 1 + pl.program_id(0) // TILES_PER_SAMPLE
    qn, kn, v = _project_qkv(x_ref, mod_ref, g_ref, w_ref, qg_ref, kg_ref, r)
    v_ref[...] = v.astype(BF16)
    if is_diff:
        cos = cos_ref[...]
        sin = sin_ref[...]
        qn = _rope(qn, cos, sin)
        kn = _rope(kn, cos, sin)
    for c in range(N_LANE_BLOCKS):
        q_ref[:, _lane_block(c)] = (qn[c] * SCORE_SCALE).astype(BF16)
        k_ref[:, _lane_block(c)] = kn[c].astype(BF16)


def _ctx_layer_kernel(is_diff, lambda_init, has_prev, *refs):
    if has_prev:
        refs = refs[2:]
    (x_ref, mod_ref, g_ref, w_ref, qg_ref, kg_ref, lq1_ref, lk1_ref, lq2_ref, lk2_ref, gsub_ref,
     o_ref, kf_ref, vf_ref) = refs
    qn, kn, v = _project_qkv(x_ref, mod_ref, g_ref, w_ref, qg_ref, kg_ref, 0)
    for c in range(N_LANE_BLOCKS):
        for bb in range(REQ_PER_TILE):
            rows = slice(bb * SEQ, (bb + 1) * SEQ)
            if is_diff:
                dst = pl.ds(c, SEQ, stride=N_LANE_BLOCKS)
                kf_ref[bb, dst, :] = kn[c][rows, :]
                vf_ref[bb, dst, :] = v[rows, _lane_block(c)]
            else:
                kf_ref[bb, _lane_block(c), :] = kn[c][rows, :].T
                vf_ref[bb, _lane_block(c), :] = v[rows, _lane_block(c)].T
    q = [(qn[c] * SCORE_SCALE).astype(BF16) for c in range(N_LANE_BLOCKS)]
    k = [kn[c].astype(BF16) for c in range(N_LANE_BLOCKS)]
    vb = [v[:, _lane_block(c)].astype(BF16) for c in range(N_LANE_BLOCKS)]
    if is_diff:
        lam = _diff_lambda(lq1_ref, lk1_ref, lq2_ref, lk2_ref, lambda_init)
        combine = _diff_combine(lam, gsub_ref[...], lambda_init)
    else:
        combine = _head_pair_combine

    def block(b):
        bb, c = divmod(b, N_LANE_BLOCKS)
        return slice(bb * SEQ, (bb + 1) * SEQ), c

    def load_q(b):
        rows, c = block(b)
        return q[c][rows, :]

    def scores(b, q2):
        rows, c = block(b)
        return [_dot_nt(q2, k[c][rows, :])]

    def values(b):
        rows, c = block(b)
        return [_times(vb[c][rows, :])]

    def store(b, o):
        rows, c = block(b)
        o_ref[rows, _lane_block(c)] = o.astype(BF16)

    _attend_pairs(REQ_PER_TILE * N_LANE_BLOCKS, load_q, scores, values, combine, store)


def _qkv_latent(x, first_tile, mod, layer, g_mix, w_qkv, q_gain, k_gain, rope_tabs, is_diff, side_jobs):
    row_spec = pl.BlockSpec((ROW_TILE, D_MODEL), lambda i: (i, 0))
    rope_spec = pl.BlockSpec((ROW_TILE, LANES), lambda i: (i % TILES_PER_SAMPLE, 0))
    return _pallas_call_with_side_jobs(
        functools.partial(_qkv_latent_kernel, is_diff), side_jobs,
        grid=(N_LATENT_TILES,),
        in_specs=[pl.BlockSpec((ROW_TILE, D_MODEL), lambda i: (first_tile + i, 0)),
                  _resident((N_COND, 6 * D_MODEL)),
                  _layer_slice((1, D_MODEL), layer),
                  _resident((D_MODEL, 3 * D_MODEL)),
                  _resident((1, LANES)), _resident((1, LANES)), rope_spec, rope_spec],
        out_specs=[row_spec] * 3,
        out_shape=[jax.ShapeDtypeStruct((N_SAMPLE_ROWS, D_MODEL), BF16)] * 3,
        args=[x, mod, g_mix, w_qkv, q_gain, k_gain, *rope_tabs],
        name="qkv_latent",
    )


def _ctx_layer(x, mod, layer, g_mix, w_qkv, kind_layer, q_gain, k_gain, lam_params, gsub, is_diff,
               lambda_init, prev, side_jobs):
    row_spec = pl.BlockSpec((ROW_TILE, D_MODEL), lambda i: (i, 0))
    if is_diff:
        cache_rows, cache_cols = SEQ * N_LANE_BLOCKS, LANES
    else:
        cache_rows, cache_cols = D_MODEL, SEQ
    cache_spec = pl.BlockSpec((REQ_PER_TILE, None, cache_rows, cache_cols), lambda i: (i, kind_layer, 0, 0))
    cache_shape = jax.ShapeDtypeStruct((BATCH, DEPTH // 2, cache_rows, cache_cols), F32)
    in_specs = ([row_spec, _resident((N_COND, 6 * D_MODEL)), _layer_slice((1, D_MODEL), layer),
                 _resident((D_MODEL, 3 * D_MODEL)), _resident((1, LANES)), _resident((1, LANES))]
                + [_resident((1, HEAD_DIM))] * 4 + [_resident((1, LANES))])
    args = [x, mod, g_mix, w_qkv, q_gain, k_gain, *lam_params, gsub]
    aliases = {}
    if prev is not None:
        in_specs = [pl.BlockSpec(memory_space=pl.ANY)] * 2 + in_specs
        args = list(prev) + args
        aliases = {0: 1, 1: 2}
    return _pallas_call_with_side_jobs(
        functools.partial(_ctx_layer_kernel, is_diff, lambda_init, prev is not None), side_jobs,
        grid=(N_PROMPT_TILES,),
        in_specs=in_specs,
        out_specs=[row_spec, cache_spec, cache_spec],
        out_shape=[jax.ShapeDtypeStruct((N_PROMPT_ROWS, D_MODEL), BF16), cache_shape, cache_shape],
        args=args,
        input_output_aliases=aliases,
        name="ctx_layer",
    )


def _softmax_pv(score_parts, value_parts):
    m = None
    for s in score_parts:
        pm = jnp.max(s, axis=-1, keepdims=True)
        m = pm if m is None else jnp.maximum(m, pm)
    acc = None
    for s, times_v in zip(score_parts, value_parts):
        pv = times_v(jnp.exp2(s - m).astype(BF16))
        acc = pv if acc is None else acc + pv
    return acc[:, :LANES] * (1.0 / acc[:, LANES:])


def _times(v):
    v1 = jnp.concatenate([v, jnp.ones_like(v)], axis=1)
    return lambda w: _dot(w, v1)


def _times_t(v_t):
    v1_t = jnp.concatenate([v_t, jnp.ones_like(v_t)], axis=0)
    return lambda w: _dot_nt(w, v1_t)


def _diff_lambda(lq1_ref, lk1_ref, lq2_ref, lk2_ref, lambda_init):
    e1 = jnp.exp(jnp.sum(lq1_ref[...] * lk1_ref[...], axis=-1, keepdims=True))
    e2 = jnp.exp(jnp.sum(lq2_ref[...] * lk2_ref[...], axis=-1, keepdims=True))
    return e1 - e2 + lambda_init


def _attend_pairs(n_blocks, load_q, scores, values, combine, store):
    def block_scores(b):
        q = load_q(b)
        low = _low_half_mask(q.shape)
        zero = jnp.zeros_like(q)
        return scores(b, jnp.concatenate([jnp.where(low, q, zero), jnp.where(low, zero, q)], axis=0))

    s = block_scores(0)
    for b in range(n_blocks):
        s_next = block_scores(b + 1) if b + 1 < n_blocks else None
        o = _softmax_pv(s, values(b))
        s = s_next
        rows = o.shape[0] // 2
        store(b, combine(o[:rows], o[rows:]))


def _diff_combine(lam, gsub, lambda_init):
    def combine(o1, o2):
        o = o1 - lam * o2
        ms = jnp.mean(o * o, axis=-1, keepdims=True)
        return o * lax.rsqrt(ms + EPS) * gsub * (1.0 - lambda_init)
    return combine


def _head_pair_combine(o_lo, o_hi):
    return jnp.where(_low_half_mask(o_lo.shape), o_lo, o_hi)


Q_CHUNK = 512
LATENT_HEADS_PER_STEP = 2


def _diff_latent_kernel(lambda_init, q_ref, k_ref, v_ref, ck_ref, cv_ref,
                        lq1_ref, lk1_ref, lq2_ref, lk2_ref, gsub_ref, o_ref):
    lam = _diff_lambda(lq1_ref, lk1_ref, lq2_ref, lk2_ref, lambda_init)
    n_chunks = DEC_SEQ // Q_CHUNK
    keys, values = [], []
    for hh in range(LATENT_HEADS_PER_STEP):
        cols = slice(hh * LANES, (hh + 1) * LANES)
        head = pl.program_id(1) * LATENT_HEADS_PER_STEP + hh
        head_rows = pl.ds(head, PAST_LEN, stride=N_LANE_BLOCKS)
        keys.append([k_ref[:, cols], ck_ref[head_rows, :].astype(BF16)])
        values.append([_times(v_ref[:, cols]), _times(cv_ref[head_rows, :].astype(BF16))])

    def block(b):
        hh, c = divmod(b, n_chunks)
        return slice(c * Q_CHUNK, (c + 1) * Q_CHUNK), slice(hh * LANES, (hh + 1) * LANES)

    def store(b, o):
        o_ref[block(b)] = o.astype(BF16)

    _attend_pairs(LATENT_HEADS_PER_STEP * n_chunks,
                  lambda b: q_ref[block(b)],
                  lambda b, q2: [_dot_nt(q2, kpart) for kpart in keys[b // n_chunks]],
                  lambda b: values[b // n_chunks],
                  _diff_combine(lam, gsub_ref[...], lambda_init), store)


LATENT_ATTENTION_GRID = (DEC_BATCH, N_LANE_BLOCKS // LATENT_HEADS_PER_STEP)


def _diff_latent_attention(q, k, v, cache_k, cache_v, layer_i, lam_params, gsub, lambda_init, side_jobs):
    blk = pl.BlockSpec((DEC_SEQ, LATENT_HEADS_PER_STEP * LANES), lambda b, h: (b, h))
    cache = pl.BlockSpec((None, None, PAST_LEN * N_LANE_BLOCKS, LANES), lambda b, h: (b, layer_i, 0, 0))
    small = [_resident((1, HEAD_DIM))] * 4 + [_resident((1, LANES))]
    return _pallas_call_with_side_jobs(
        functools.partial(_diff_latent_kernel, lambda_init), side_jobs,
        grid=LATENT_ATTENTION_GRID,
        in_specs=[blk, blk, blk, cache, cache] + small,
        out_specs=[blk],
        out_shape=[jax.ShapeDtypeStruct((N_SAMPLE_ROWS, D_MODEL), BF16)],
        args=[q, k, v, cache_k, cache_v, *lam_params, gsub],
        name="diff_latent_attention",
    )


class _NaGroup(NamedTuple):
    q_row: int
    n_q: int
    k_row: int
    n_k: int


def _window_start(r):
    return min(max(r - NA_KH // 2, 0), GRID_ROWS - NA_KH)


def _na_groups(rows_per_group):
    groups = []
    for q_row in range(0, GRID_ROWS, rows_per_group):
        first = _window_start(q_row)
        last = _window_start(q_row + rows_per_group - 1) + NA_KH
        k_row = first - first % 2
        n_k = last - k_row
        n_k += n_k % 2
        groups.append(_NaGroup(q_row, rows_per_group, min(k_row, GRID_ROWS - n_k), n_k))
    return groups


NA_GROUPS = _na_groups(4)
N_DR = 2 * NA_KH - 1
N_DC = 2 * NA_KW - 1
N_DR_PAD = 16


def _expand_na_bias(rb_ref, tile_ref, bias_refs):
    qc = lax.broadcasted_iota(jnp.int32, (GRID_W, LANES), 0)
    lane = lax.broadcasted_iota(jnp.int32, (GRID_W, LANES), 1)
    kc = lane % GRID_W
    cs = jnp.clip(qc - NA_KW // 2, 0, GRID_W - NA_KW)
    col_ok = (kc >= cs) & (kc < cs + NA_KW)
    low = lane < GRID_W
    neg = jnp.full((GRID_W, LANES), NEG_INF, F32)
    for head in range(2 * LATENT_HEADS_PER_STEP):
        for dr in range(N_DR):
            row = jnp.broadcast_to(rb_ref[head, dr:dr + 1, :], (GRID_W, LANES))
            shifted = pltpu.roll(row, LANES - (NA_KW - 1), 1, stride=1, stride_axis=0)
            tile_ref[dr] = jnp.where(col_ok, shifted * LOG2_E, neg)
        for grp, bias_ref in zip(NA_GROUPS, bias_refs):
            for rq in range(grp.n_q):
                r = grp.q_row + rq
                rs = _window_start(r)

                def tile(kr, r=r, rs=rs):
                    if rs <= kr < rs + NA_KH:
                        return tile_ref[kr - r + NA_KH - 1]
                    return neg

                for p in range(grp.n_k // 2):
                    kr = grp.k_row + 2 * p
                    bias_ref[head, rq * GRID_W:(rq + 1) * GRID_W, p * LANES:(p + 1) * LANES] = (
                        jnp.where(low, tile(kr), tile(kr + 1)))


def _na_latent_kernel(q_ref, k_ref, v_ref, ckt_ref, cvt_ref, rb_ref, o_ref, tile_ref, *bias_refs):
    @pl.when(pl.program_id(1) == 0)
    def _():
        _expand_na_bias(rb_ref, tile_ref, bias_refs)

    ck_t = [ckt_ref[hh * LANES:(hh + 1) * LANES, :].astype(BF16) for hh in range(LATENT_HEADS_PER_STEP)]
    cv_t = [cvt_ref[hh * LANES:(hh + 1) * LANES, :].astype(BF16) for hh in range(LATENT_HEADS_PER_STEP)]

    def block(b):
        return divmod(b, len(NA_GROUPS))

    def cols(hh):
        return slice(hh * LANES, (hh + 1) * LANES)

    def qrows(g):
        return slice(NA_GROUPS[g].q_row * GRID_W, (NA_GROUPS[g].q_row + NA_GROUPS[g].n_q) * GRID_W)

    def krows(g):
        return slice(NA_GROUPS[g].k_row * GRID_W, (NA_GROUPS[g].k_row + NA_GROUPS[g].n_k) * GRID_W)

    def load_q(b):
        hh, g = block(b)
        return q_ref[qrows(g), cols(hh)]

    def scores(b, q2):
        hh, g = block(b)
        bias = jnp.concatenate([bias_refs[g][2 * hh], bias_refs[g][2 * hh + 1]], axis=0)
        return [_dot_nt(q2, k_ref[krows(g), cols(hh)]) + bias, _dot(q2, ck_t[hh])]

    def values(b):
        hh, g = block(b)
        return [_times(v_ref[krows(g), cols(hh)]), _times_t(cv_t[hh])]

    def store(b, o):
        hh, g = block(b)
        o_ref[qrows(g), cols(hh)] = o.astype(BF16)

    _attend_pairs(LATENT_HEADS_PER_STEP * len(NA_GROUPS), load_q, scores, values, _head_pair_combine, store)


def _na_latent_attention(q, k, v, cache_k_t, cache_v_t, layer_i, rel_bias_rows, side_jobs):
    width = LATENT_HEADS_PER_STEP * LANES
    blk = pl.BlockSpec((DEC_SEQ, width), lambda p, b: (b, p))
    cache = pl.BlockSpec((None, None, width, PAST_LEN), lambda p, b: (b, layer_i, p, 0))
    rb_spec = pl.BlockSpec((None, 2 * LATENT_HEADS_PER_STEP, N_DR_PAD, LANES), lambda p, b: (layer_i, p, 0, 0))
    return _pallas_call_with_side_jobs(
        _na_latent_kernel, side_jobs,
        grid=LATENT_ATTENTION_GRID[::-1],
        in_specs=[blk, blk, blk, cache, cache, rb_spec],
        out_specs=[blk],
        out_shape=[jax.ShapeDtypeStruct((N_SAMPLE_ROWS, D_MODEL), BF16)],
        args=[q, k, v, cache_k_t, cache_v_t, rel_bias_rows],
        scratch_shapes=[pltpu.VMEM((N_DR, GRID_W, LANES), F32)] + [
            pltpu.VMEM((2 * LATENT_HEADS_PER_STEP, grp.n_q * GRID_W, grp.n_k * GRID_W), F32)
            for grp in NA_GROUPS],
        name="na_latent_attention",
    )


def _padded_rel_bias(rel_bias_na):
    half = jnp.pad(rel_bias_na, ((0, 0), (0, 0), (0, N_DR_PAD - N_DR), (0, GRID_W - N_DC)))
    return jnp.concatenate([half, half], axis=-1)


FF_CHUNK = 1024


def _post_kernel(split_in, split_out, *refs):
    n_x = 2 if split_in else 1
    x_refs = refs[:n_x]
    op_ref, os_ref, mod_ref, g_ref, wo_ref, w1_ref, w2_ref = refs[n_x:n_x + 7]
    y_refs = refs[n_x + 7:]
    i = pl.program_id(0)
    r = _cond_row(i)
    is_prompt = i < N_PROMPT_TILES

    def mod(j):
        return mod_ref[pl.ds(r, 1), j * D_MODEL:(j + 1) * D_MODEL]

    x = jnp.where(is_prompt, x_refs[0][...], x_refs[1][...]) if split_in else x_refs[0][...]
    o = jnp.where(is_prompt, op_ref[...], os_ref[...])
    y1 = x + mod(2) * _dot(o, wo_ref[...])
    ms = jnp.mean(y1 * y1, axis=-1, keepdims=True)
    h = (y1 * lax.rsqrt(ms + EPS) * g_ref[...] * (1.0 + mod(4)) + mod(3)).astype(BF16)
    acc = jnp.zeros((ROW_TILE, D_MODEL), F32)
    for c in range(D_FF // FF_CHUNK):
        cols = slice(c * FF_CHUNK, (c + 1) * FF_CHUNK)
        hid = jnp.maximum(_dot(h, w1_ref[:, cols]), 0.0)
        acc = acc + _dot((hid * hid).astype(BF16), w2_ref[cols, :])
    y = y1 + mod(5) * acc
    if split_out:
        @pl.when(is_prompt)
        def _():
            y_refs[0][...] = y

        @pl.when(jnp.logical_not(is_prompt))
        def _():
            y_refs[1][...] = y
    else:
        y_refs[0][...] = y


def _post(xs, o_p, o_s, mod, layer, g_mlp, w_o, w_fc1, w_fc2, split_out, side_jobs):
    prompt_spec = pl.BlockSpec((ROW_TILE, D_MODEL), lambda i: (_prompt_tile(i), 0))
    sample_spec = pl.BlockSpec((ROW_TILE, D_MODEL), lambda i: (_sample_tile(i), 0))
    row_spec = pl.BlockSpec((ROW_TILE, D_MODEL), lambda i: (i, 0))
    split_in = len(xs) == 2
    if split_out:
        out_specs = [prompt_spec, sample_spec]
        out_shape = [jax.ShapeDtypeStruct((N_PROMPT_ROWS, D_MODEL), F32),
                     jax.ShapeDtypeStruct((N_SAMPLE_ROWS, D_MODEL), F32)]
    else:
        out_specs = [row_spec]
        out_shape = [jax.ShapeDtypeStruct((N_ROWS, D_MODEL), F32)]
    return _pallas_call_with_side_jobs(
        functools.partial(_post_kernel, split_in, split_out), side_jobs,
        grid=(N_ROW_TILES,),
        in_specs=([prompt_spec, sample_spec] if split_in else [row_spec]) + [
            prompt_spec, sample_spec,
            _resident((N_COND, 6 * D_MODEL)),
            _layer_slice((1, D_MODEL), layer),
            _resident((D_MODEL, D_MODEL)),
            _resident((D_MODEL, D_FF)),
            _resident((D_FF, D_MODEL)),
        ],
        out_specs=out_specs,
        out_shape=out_shape,
        args=[*xs, o_p, o_s, mod, g_mlp, w_o, w_fc1, w_fc2],
        name="wo_mlp",
    )


def _rope_tables():
    n_freq = HEAD_DIM // 4
    freqs = ROPE_BASE ** (-jnp.arange(n_freq, dtype=F32) / n_freq)
    t = jnp.arange(DEC_SEQ)
    ang_row = (t // GRID_W).astype(F32)[:, None] * freqs
    ang_col = (t % GRID_W).astype(F32)[:, None] * freqs
    cos64 = jnp.concatenate([jnp.cos(ang_row)] * 2 + [jnp.cos(ang_col)] * 2, axis=-1)
    sin64 = jnp.concatenate([-jnp.sin(ang_row), jnp.sin(ang_row),
                             -jnp.sin(ang_col), jnp.sin(ang_col)], axis=-1)
    return jnp.tile(cos64, (1, 2)), jnp.tile(sin64, (1, 2))


def _pair_gain(g):
    return jnp.tile(g.reshape(1, HEAD_DIM), (1, LANES // HEAD_DIM))


def _head_major_cache(cache):
    n_req, n_layers = cache.shape[:2]
    return jnp.transpose(cache, (0, 1, 3, 4, 2)).reshape(n_req, n_layers, D_MODEL, PAST_LEN)


def kernel(x_prompt, x_sample, cache_diff_k, cache_diff_v, cache_na_k, cache_na_v, c, c_ctx, w_ada, b_ada, norm_mix_g, norm_mlp_g, w_fc1, w_fc2, w_qkv_diff, w_o_diff, q_norm_diff_g, k_norm_diff_g, lambda_q1, lambda_k1, lambda_q2, lambda_k2, subln_g, w_qkv_na, w_o_na, q_norm_na_g, k_norm_na_g, rel_bias_na):
    cond = jnp.concatenate([c_ctx[None, :], c, jnp.zeros((N_COND - 1 - DEC_BATCH, D_MODEL), F32)], axis=0)
    b_ada = b_ada.reshape(DEPTH, 1, 6 * D_MODEL)
    mod = _modulation(cond, w_ada, b_ada, 0)
    cos_tab, sin_tab = _rope_tables()
    na_bias = _padded_rel_bias(rel_bias_na)
    cdk = cache_diff_k.reshape(DEC_BATCH, DEPTH // 2, PAST_LEN * N_LANE_BLOCKS, LANES)
    cdv = cache_diff_v.reshape(DEC_BATCH, DEPTH // 2, PAST_LEN * N_LANE_BLOCKS, LANES)
    cnk_t = _head_major_cache(cache_na_k)
    cnv_t = _head_major_cache(cache_na_v)
    g_mix = norm_mix_g.reshape(DEPTH, 1, D_MODEL)
    g_mlp = norm_mlp_g.reshape(DEPTH, 1, D_MODEL)
    w_qkv_f32 = (w_qkv_diff, w_qkv_na)
    w_o_f32 = (w_o_diff, w_o_na)
    q_gain = (q_norm_diff_g, q_norm_na_g)
    k_gain = (k_norm_diff_g, k_norm_na_g)

    xs = [x_prompt.reshape(N_PROMPT_ROWS, D_MODEL), x_sample.reshape(N_SAMPLE_ROWS, D_MODEL)]
    new_kv = [None, None]
    w_qkv = w_qkv_diff[0].astype(BF16)
    w_o = None
    for l in range(DEPTH):
        i = l // 2
        kind = l % 2
        is_diff = kind == 0
        first = l == 0
        last = l == DEPTH - 1
        x_prompt_src, x_sample_src = (xs[0], xs[1]) if len(xs) == 2 else (xs[0], xs[0])
        sample_first_tile = 0 if len(xs) == 2 else N_PROMPT_TILES
        gains = (_pair_gain(q_gain[kind][i]), _pair_gain(k_gain[kind][i]))
        lambda_init = 0.8 - 0.6 * math.exp(-0.3 * l)
        lam_params = [p[i if is_diff else 0].reshape(1, HEAD_DIM)
                      for p in (lambda_q1, lambda_k1, lambda_q2, lambda_k2)]
        gsub = subln_g[i if is_diff else 0].reshape(1, LANES)
        o_p, new_k, new_v, *cast = _ctx_layer(
            x_prompt_src, mod, l, g_mix, w_qkv, i, *gains, lam_params, gsub, is_diff, lambda_init,
            new_kv[kind], [_cast_job(w_o_f32[kind], i, 0, (N_PROMPT_TILES,))] if first else [])
        w_o = cast[0] if first else w_o
        new_kv[kind] = (new_k, new_v)
        qs, ks, vs = _qkv_latent(
            x_sample_src, sample_first_tile, mod, l, g_mix, w_qkv, *gains, (cos_tab, sin_tab), is_diff, [])
        if is_diff:
            mlp_jobs = [_cast_job(w_fc1, l, 1, LATENT_ATTENTION_GRID), _cast_job(w_fc2, l, 0, LATENT_ATTENTION_GRID)]
            o_s, w1, w2 = _diff_latent_attention(qs, ks, vs, cdk, cdv, i, lam_params, gsub, lambda_init,
                                                 mlp_jobs)
        else:
            na_grid = LATENT_ATTENTION_GRID[::-1]
            mlp_jobs = [_cast_job(w_fc1, l, 1, na_grid), _cast_job(w_fc2, l, 0, na_grid)]
            o_s, w1, w2 = _na_latent_attention(qs, ks, vs, cnk_t, cnv_t, i, na_bias, mlp_jobs)
        nxt, nxt_kind, nxt_i = l + 1, (l + 1) % 2, (l + 1) // 2
        post_grid = (N_ROW_TILES,)
        next_jobs = [] if last else [_cast_job(w_qkv_f32[nxt_kind], nxt_i, 0, post_grid),
                                     _cast_job(w_o_f32[nxt_kind], nxt_i, 0, post_grid),
                                     _ada_job(cond, w_ada, b_ada, nxt, post_grid)]
        outs = list(_post(xs, o_p, o_s, mod, l, g_mlp, w_o, w1, w2, last, next_jobs))
        if last:
            xp, xs = outs
        else:
            *xs, w_qkv, w_o, mod = outs

    n_kind = DEPTH // 2
    diff_k, diff_v = (a.reshape(BATCH, n_kind, SEQ, N_LANE_BLOCKS, LANES) for a in new_kv[0])
    na_k, na_v = (jnp.transpose(a.reshape(BATCH, n_kind, NA_HEADS, HEAD_DIM, SEQ), (0, 1, 4, 2, 3))
                  for a in new_kv[1])
    return (xp.reshape(BATCH, SEQ, D_MODEL), xs.reshape(DEC_BATCH, DEC_SEQ, D_MODEL),
            diff_k, diff_v, na_k, na_v)
```

```python
import functools
import math
from typing import Callable, NamedTuple

import jax
import jax.numpy as jnp
import numpy as np
from jax import lax
from jax.experimental import pallas as pl
from jax.experimental.pallas import tpu as pltpu

D_MODEL = 1024
BATCH = 16
SEQ = 256
DEPTH = 4
DEC_BATCH = 4
DEC_SEQ = 1024
PAST_LEN = 256
GRID_W = 64
GRID_ROWS = DEC_SEQ // GRID_W
HEAD_DIM = 64
D_FF = 4 * D_MODEL
NA_HEADS = 16
NA_KH = 8
NA_KW = 16
ROPE_BASE = 10000.0
EPS = 1e-6
NEG_INF = -1e30
LOG2_E = math.log2(math.e)
SCORE_SCALE = HEAD_DIM ** -0.5 * LOG2_E

LANES = 128
N_LANE_BLOCKS = D_MODEL // LANES
N_PROMPT_ROWS = BATCH * SEQ
N_SAMPLE_ROWS = DEC_BATCH * DEC_SEQ
N_ROWS = N_PROMPT_ROWS + N_SAMPLE_ROWS
ROW_TILE = 512
N_ROW_TILES = N_ROWS // ROW_TILE
N_PROMPT_TILES = N_PROMPT_ROWS // ROW_TILE
N_LATENT_TILES = N_SAMPLE_ROWS // ROW_TILE
TILES_PER_SAMPLE = DEC_SEQ // ROW_TILE
REQ_PER_TILE = ROW_TILE // SEQ
N_COND = 8
V7X_VMEM_BYTES = 64 * 1024 * 1024
VMEM_LIMIT = V7X_VMEM_BYTES // 8 * 7

F32 = jnp.float32
BF16 = jnp.bfloat16
NT_DIMS = (((1,), (1,)), ((), ()))


def _dot(a, b):
    return jnp.dot(a, b, preferred_element_type=F32)


def _dot_nt(a, b):
    return lax.dot_general(a, b, NT_DIMS, preferred_element_type=F32)


def _params(n_axes):
    return pltpu.CompilerParams(dimension_semantics=("arbitrary",) * n_axes,
                                vmem_limit_bytes=VMEM_LIMIT)


def _resident(shape):
    return pl.BlockSpec(shape, lambda *_: (0,) * len(shape), pipeline_mode=pl.Buffered(1))


class _SideJob(NamedTuple):
    args: list
    in_specs: list
    out_spec: pl.BlockSpec
    out_shape: jax.ShapeDtypeStruct
    body: Callable


def _cast_body(src_ref, dst_ref):
    dst_ref[...] = src_ref[...].astype(BF16)


def _step_number(grid):
    def number(*idx):
        step = idx[0]
        for size, i in zip(grid[1:], idx[1:]):
            step = step * size + i
        return step
    return number


def _cast_job(weights, layer, split_axis, grid):
    _, rows, cols = weights.shape
    n_steps, step = math.prod(grid), _step_number(grid)
    if split_axis == 0:
        band = (rows // n_steps, cols)
        src_spec = pl.BlockSpec((None,) + band, lambda *idx: (layer, step(*idx), 0))
        dst_spec = pl.BlockSpec(band, lambda *idx: (step(*idx), 0))
    else:
        band = (rows, cols // n_steps)
        src_spec = pl.BlockSpec((None,) + band, lambda *idx: (layer, 0, step(*idx)))
        dst_spec = pl.BlockSpec(band, lambda *idx: (0, step(*idx)))
    return _SideJob([weights], [src_spec], dst_spec, jax.ShapeDtypeStruct((rows, cols), BF16), _cast_body)


def _pallas_call_with_side_jobs(kernel_fn, jobs, *, grid, in_specs, out_specs, out_shape, args, name,
                                input_output_aliases=None, scratch_shapes=()):
    n_in, n_out = len(in_specs), len(out_specs)
    n_job_in = [len(job.args) for job in jobs]

    def kernel_and_jobs(*refs):
        ins, refs = refs[:n_in], refs[n_in:]
        job_ins, refs = refs[:sum(n_job_in)], refs[sum(n_job_in):]
        outs, refs = refs[:n_out], refs[n_out:]
        job_outs, scratch = refs[:len(jobs)], refs[len(jobs):]
        for job, n, out_ref in zip(jobs, n_job_in, job_outs):
            job.body(*job_ins[:n], out_ref)
            job_ins = job_ins[n:]
        kernel_fn(*ins, *outs, *scratch)

    return pl.pallas_call(
        kernel_and_jobs,
        grid=grid,
        in_specs=list(in_specs) + [spec for job in jobs for spec in job.in_specs],
        out_specs=list(out_specs) + [job.out_spec for job in jobs],
        out_shape=list(out_shape) + [job.out_shape for job in jobs],
        input_output_aliases=input_output_aliases or {},
        scratch_shapes=list(scratch_shapes),
        compiler_params=_params(len(grid)),
        name=name,
    )(*args, *(arg for job in jobs for arg in job.args))


def _prompt_tile(i):
    return jnp.minimum(i, N_PROMPT_TILES - 1)


def _sample_tile(i):
    return jnp.maximum(i - N_PROMPT_TILES, 0)


def _cond_row(i):
    return jnp.where(i < N_PROMPT_TILES, 0, 1 + (i - N_PROMPT_TILES) // TILES_PER_SAMPLE)


def _low_half_mask(shape):
    return lax.broadcasted_iota(jnp.int32, shape, len(shape) - 1) < HEAD_DIM


def _ada_kernel(layer, cond_ref, w_ref, b_ref, out_ref):
    cond = cond_ref[...]
    act = (cond * jax.nn.sigmoid(cond)).astype(BF16)
    out_ref[...] = _dot(act, w_ref[...].astype(BF16)) + b_ref[layer:layer + 1, :]


def _ada_specs(layer, n_steps):
    tn = 6 * D_MODEL // n_steps
    in_specs = [pl.BlockSpec((N_COND, D_MODEL), lambda j: (0, 0)),
                pl.BlockSpec((None, D_MODEL, tn), lambda j: (layer, 0, j)),
                pl.BlockSpec((DEPTH, tn), lambda j: (0, j))]
    return in_specs, pl.BlockSpec((N_COND, tn), lambda j: (0, j)), jax.ShapeDtypeStruct((N_COND, 6 * D_MODEL), F32)


def _modulation(cond, w_ada, b_ada, layer):
    n_steps = 4
    in_specs, out_spec, out_shape = _ada_specs(layer, n_steps)
    return pl.pallas_call(functools.partial(_ada_kernel, layer), grid=(n_steps,), in_specs=in_specs,
                          out_specs=out_spec, out_shape=out_shape, compiler_params=_params(1),
                          name="ada_modulation")(cond, w_ada, b_ada)


def _ada_job(cond, w_ada, b_ada, layer, grid):
    (n_steps,) = grid
    in_specs, out_spec, out_shape = _ada_specs(layer, n_steps)
    return _SideJob([cond, w_ada, b_ada], in_specs, out_spec, out_shape, functools.partial(_ada_kernel, layer))


def _group_rms(x, gain):
    low = _low_half_mask((1, LANES))
    cols = []
    for c in range(N_LANE_BLOCKS):
        xb = x[:, c * LANES:(c + 1) * LANES]
        sq = xb * xb
        s_lo = jnp.sum(jnp.where(low, sq, 0.0), axis=-1, keepdims=True)
        s_hi = jnp.sum(jnp.where(low, 0.0, sq), axis=-1, keepdims=True)
        ms = jnp.where(low, s_lo, s_hi) * (1.0 / HEAD_DIM)
        cols.append(xb * lax.rsqrt(ms + EPS) * gain)
    return cols


def _rope(cols, cos, sin_signed):
    lane = lax.broadcasted_iota(jnp.int32, (1, LANES), 1)
    first = (lane % 32) < 16
    out = []
    for xb in cols:
        partner = jnp.where(first, pltpu.roll(xb, LANES - 16, 1), pltpu.roll(xb, 16, 1))
        out.append(xb * cos + partner * sin_signed)
    return out


def _project_qkv(layer, x_ref, mod_ref, g_ref, w_ref, qg_ref, kg_ref, r):
    x = x_ref[...]
    ms = jnp.mean(x * x, axis=-1, keepdims=True)
    xn = x * lax.rsqrt(ms + EPS) * g_ref[layer:layer + 1, :]
    shift = mod_ref[pl.ds(r, 1), 0:D_MODEL]
    scale = mod_ref[pl.ds(r, 1), D_MODEL:2 * D_MODEL]
    h = (xn * (1.0 + scale) + shift).astype(BF16)
    qkv = _dot(h, w_ref[...])
    qn = _group_rms(qkv[:, 0:D_MODEL], qg_ref[...])
    kn = _group_rms(qkv[:, D_MODEL:2 * D_MODEL], kg_ref[...])
    return qn, kn, qkv[:, 2 * D_MODEL:3 * D_MODEL]


def _lane_block(c):
    return slice(c * LANES, (c + 1) * LANES)


def _qkv_latent_kernel(is_diff, layer, x_ref, mod_ref, g_ref, w_ref, qg_ref, kg_ref, cos_ref, sin_ref,
                       q_ref, k_ref, v_ref):
    r = 1 + pl.program_id(0) // TILES_PER_SAMPLE
    qn, kn, v = _project_qkv(layer, x_ref, mod_ref, g_ref, w_ref, qg_ref, kg_ref, r)
    v_ref[...] = v.astype(BF16)
    if is_diff:
        cos = cos_ref[...]
        sin = sin_ref[...]
        qn = _rope(qn, cos, sin)
        kn = _rope(kn, cos, sin)
    for c in range(N_LANE_BLOCKS):
        q_ref[:, _lane_block(c)] = (qn[c] * SCORE_SCALE).astype(BF16)
        k_ref[:, _lane_block(c)] = kn[c].astype(BF16)


def _ctx_layer_kernel(is_diff, layer, lambda_init, has_prev, *refs):
    if has_prev:
        refs = refs[2:]
    (x_ref, mod_ref, g_ref, w_ref, qg_ref, kg_ref, lq1_ref, lk1_ref, lq2_ref, lk2_ref, gsub_ref,
     o_ref, kf_ref, vf_ref) = refs
    qn, kn, v = _project_qkv(layer, x_ref, mod_ref, g_ref, w_ref, qg_ref, kg_ref, 0)
    for c in range(N_LANE_BLOCKS):
        for bb in range(REQ_PER_TILE):
            rows = slice(bb * SEQ, (bb + 1) * SEQ)
            if is_diff:
                dst = pl.ds(c, SEQ, stride=N_LANE_BLOCKS)
                kf_ref[bb, dst, :] = kn[c][rows, :]
                vf_ref[bb, dst, :] = v[rows, _lane_block(c)]
            else:
                kf_ref[bb, _lane_block(c), :] = kn[c][rows, :].T
                vf_ref[bb, _lane_block(c), :] = v[rows, _lane_block(c)].T
    q = [(qn[c] * SCORE_SCALE).astype(BF16) for c in range(N_LANE_BLOCKS)]
    k = [kn[c].astype(BF16) for c in range(N_LANE_BLOCKS)]
    vb = [v[:, _lane_block(c)].astype(BF16) for c in range(N_LANE_BLOCKS)]
    if is_diff:
        lam = _diff_lambda(lq1_ref, lk1_ref, lq2_ref, lk2_ref, lambda_init)
        combine = _diff_combine(lam, gsub_ref[...], lambda_init)
    else:
        combine = _head_pair_combine

    def block(b):
        bb, c = divmod(b, N_LANE_BLOCKS)
        return slice(bb * SEQ, (bb + 1) * SEQ), c

    def load_q(b):
        rows, c = block(b)
        return q[c][rows, :]

    def scores(b, q2):
        rows, c = block(b)
        return [_dot_nt(q2, k[c][rows, :])]

    def values(b):
        rows, c = block(b)
        return [_times(vb[c][rows, :])]

    def store(b, o):
        rows, c = block(b)
        o_ref[rows, _lane_block(c)] = o.astype(BF16)

    _attend_pairs(REQ_PER_TILE * N_LANE_BLOCKS, load_q, scores, values, combine, store)


def _qkv_latent(x, first_tile, mod, layer, g_mix, w_qkv, q_gain, k_gain, rope_tabs, is_diff, side_jobs):
    row_spec = pl.BlockSpec((ROW_TILE, D_MODEL), lambda i: (i, 0))
    rope_spec = pl.BlockSpec((ROW_TILE, LANES), lambda i: (i % TILES_PER_SAMPLE, 0))
    return _pallas_call_with_side_jobs(
        functools.partial(_qkv_latent_kernel, is_diff, layer), side_jobs,
        grid=(N_LATENT_TILES,),
        in_specs=[pl.BlockSpec((ROW_TILE, D_MODEL), lambda i: (first_tile + i, 0)),
                  _resident((N_COND, 6 * D_MODEL)),
                  _resident((DEPTH, D_MODEL)),
                  _resident((D_MODEL, 3 * D_MODEL)),
                  _resident((1, LANES)), _resident((1, LANES)), rope_spec, rope_spec],
        out_specs=[row_spec] * 3,
        out_shape=[jax.ShapeDtypeStruct((N_SAMPLE_ROWS, D_MODEL), BF16)] * 3,
        args=[x, mod, g_mix, w_qkv, q_gain, k_gain, *rope_tabs],
        name="qkv_latent",
    )


def _ctx_layer(x, mod, layer, g_mix, w_qkv, kind_layer, q_gain, k_gain, lam_params, gsub, is_diff,
               lambda_init, prev, side_jobs):
    row_spec = pl.BlockSpec((ROW_TILE, D_MODEL), lambda i: (i, 0))
    if is_diff:
        cache_rows, cache_cols = SEQ * N_LANE_BLOCKS, LANES
    else:
        cache_rows, cache_cols = D_MODEL, SEQ
    cache_spec = pl.BlockSpec((REQ_PER_TILE, None, cache_rows, cache_cols), lambda i: (i, kind_layer, 0, 0))
    cache_shape = jax.ShapeDtypeStruct((BATCH, DEPTH // 2, cache_rows, cache_cols), F32)
    in_specs = ([row_spec, _resident((N_COND, 6 * D_MODEL)), _resident((DEPTH, D_MODEL)),
                 _resident((D_MODEL, 3 * D_MODEL)), _resident((1, LANES)), _resident((1, LANES))]
                + [_resident((1, HEAD_DIM))] * 4 + [_resident((1, LANES))])
    args = [x, mod, g_mix, w_qkv, q_gain, k_gain, *lam_params, gsub]
    aliases = {}
    if prev is not None:
        in_specs = [pl.BlockSpec(memory_space=pl.ANY)] * 2 + in_specs
        args = list(prev) + args
        aliases = {0: 1, 1: 2}
    return _pallas_call_with_side_jobs(
        functools.partial(_ctx_layer_kernel, is_diff, layer, lambda_init, prev is not None), side_jobs,
        grid=(N_PROMPT_TILES,),
        in_specs=in_specs,
        out_specs=[row_spec, cache_spec, cache_spec],
        out_shape=[jax.ShapeDtypeStruct((N_PROMPT_ROWS, D_MODEL), BF16), cache_shape, cache_shape],
        args=args,
        input_output_aliases=aliases,
        name="ctx_layer",
    )


def _softmax_pv(score_parts, value_parts):
    m = None
    for s in score_parts:
        pm = jnp.max(s, axis=-1, keepdims=True)
        m = pm if m is None else jnp.maximum(m, pm)
    acc = None
    for s, times_v in zip(score_parts, value_parts):
        pv = times_v(jnp.exp2(s - m).astype(BF16))
        acc = pv if acc is None else acc + pv
    return acc[:, :LANES] * (1.0 / acc[:, LANES:])


def _times(v):
    v1 = jnp.concatenate([v, jnp.ones_like(v)], axis=1)
    return lambda w: _dot(w, v1)


def _times_t(v_t):
    v1_t = jnp.concatenate([v_t, jnp.ones_like(v_t)], axis=0)
    return lambda w: _dot_nt(w, v1_t)


def _diff_lambda(lq1_ref, lk1_ref, lq2_ref, lk2_ref, lambda_init):
    e1 = jnp.exp(jnp.sum(lq1_ref[...] * lk1_ref[...], axis=-1, keepdims=True))
    e2 = jnp.exp(jnp.sum(lq2_ref[...] * lk2_ref[...], axis=-1, keepdims=True))
    return e1 - e2 + lambda_init


def _attend_pairs(n_blocks, load_q, scores, values, combine, store):
    def block_scores(b):
        q = load_q(b)
        low = _low_half_mask(q.shape)
        zero = jnp.zeros_like(q)
        return scores(b, jnp.concatenate([jnp.where(low, q, zero), jnp.where(low, zero, q)], axis=0))

    s = block_scores(0)
    for b in range(n_blocks):
        s_next = block_scores(b + 1) if b + 1 < n_blocks else None
        o = _softmax_pv(s, values(b))
        s = s_next
        rows = o.shape[0] // 2
        store(b, combine(o[:rows], o[rows:]))


def _diff_combine(lam, gsub, lambda_init):
    def combine(o1, o2):
        o = o1 - lam * o2
        ms = jnp.mean(o * o, axis=-1, keepdims=True)
        return o * lax.rsqrt(ms + EPS) * gsub * (1.0 - lambda_init)
    return combine


def _head_pair_combine(o_lo, o_hi):
    return jnp.where(_low_half_mask(o_lo.shape), o_lo, o_hi)


Q_CHUNK = 512
LATENT_HEADS_PER_STEP = 2


def _diff_latent_kernel(lambda_init, q_ref, k_ref, v_ref, ck_ref, cv_ref,
                        lq1_ref, lk1_ref, lq2_ref, lk2_ref, gsub_ref, o_ref):
    lam = _diff_lambda(lq1_ref, lk1_ref, lq2_ref, lk2_ref, lambda_init)
    n_chunks = DEC_SEQ // Q_CHUNK
    keys, values = [], []
    for hh in range(LATENT_HEADS_PER_STEP):
        cols = slice(hh * LANES, (hh + 1) * LANES)
        head = pl.program_id(1) * LATENT_HEADS_PER_STEP + hh
        head_rows = pl.ds(head, PAST_LEN, stride=N_LANE_BLOCKS)
        keys.append([k_ref[:, cols], ck_ref[head_rows, :].astype(BF16)])
        values.append([_times(v_ref[:, cols]), _times(cv_ref[head_rows, :].astype(BF16))])

    def block(b):
        hh, c = divmod(b, n_chunks)
        return slice(c * Q_CHUNK, (c + 1) * Q_CHUNK), slice(hh * LANES, (hh + 1) * LANES)

    def store(b, o):
        o_ref[block(b)] = o.astype(BF16)

    _attend_pairs(LATENT_HEADS_PER_STEP * n_chunks,
                  lambda b: q_ref[block(b)],
                  lambda b, q2: [_dot_nt(q2, kpart) for kpart in keys[b // n_chunks]],
                  lambda b: values[b // n_chunks],
                  _diff_combine(lam, gsub_ref[...], lambda_init), store)


LATENT_ATTENTION_GRID = (DEC_BATCH, N_LANE_BLOCKS // LATENT_HEADS_PER_STEP)


def _diff_latent_attention(q, k, v, cache_k, cache_v, layer_i, lam_params, gsub, lambda_init, side_jobs):
    blk = pl.BlockSpec((DEC_SEQ, LATENT_HEADS_PER_STEP * LANES), lambda b, h: (b, h))
    cache = pl.BlockSpec((None, None, PAST_LEN * N_LANE_BLOCKS, LANES), lambda b, h: (b, layer_i, 0, 0))
    small = [_resident((1, HEAD_DIM))] * 4 + [_resident((1, LANES))]
    return _pallas_call_with_side_jobs(
        functools.partial(_diff_latent_kernel, lambda_init), side_jobs,
        grid=LATENT_ATTENTION_GRID,
        in_specs=[blk, blk, blk, cache, cache] + small,
        out_specs=[blk],
        out_shape=[jax.ShapeDtypeStruct((N_SAMPLE_ROWS, D_MODEL), BF16)],
        args=[q, k, v, cache_k, cache_v, *lam_params, gsub],
        name="diff_latent_attention",
    )


class _NaGroup(NamedTuple):
    q_row: int
    n_q: int
    k_row: int
    n_k: int


def _window_start(r):
    return min(max(r - NA_KH // 2, 0), GRID_ROWS - NA_KH)


def _na_groups(rows_per_group):
    groups = []
    for q_row in range(0, GRID_ROWS, rows_per_group):
        first = _window_start(q_row)
        last = _window_start(q_row + rows_per_group - 1) + NA_KH
        k_row = first - first % 2
        n_k = last - k_row
        n_k += n_k % 2
        groups.append(_NaGroup(q_row, rows_per_group, min(k_row, GRID_ROWS - n_k), n_k))
    return groups


NA_GROUPS = _na_groups(4)
N_DR = 2 * NA_KH - 1
N_DC = 2 * NA_KW - 1
N_DR_PAD = 16


def _expand_na_bias(rb_ref, tile_ref, bias_refs):
    qc = lax.broadcasted_iota(jnp.int32, (GRID_W, LANES), 0)
    lane = lax.broadcasted_iota(jnp.int32, (GRID_W, LANES), 1)
    kc = lane % GRID_W
    cs = jnp.clip(qc - NA_KW // 2, 0, GRID_W - NA_KW)
    col_ok = (kc >= cs) & (kc < cs + NA_KW)
    low = lane < GRID_W
    neg = jnp.full((GRID_W, LANES), NEG_INF, F32)
    for head in range(2 * LATENT_HEADS_PER_STEP):
        for dr in range(N_DR):
            row = jnp.broadcast_to(rb_ref[head, dr:dr + 1, :], (GRID_W, LANES))
            shifted = pltpu.roll(row, LANES - (NA_KW - 1), 1, stride=1, stride_axis=0)
            tile_ref[dr] = jnp.where(col_ok, shifted * LOG2_E, neg)
        for grp, bias_ref in zip(NA_GROUPS, bias_refs):
            for rq in range(grp.n_q):
                r = grp.q_row + rq
                rs = _window_start(r)

                def tile(kr, r=r, rs=rs):
                    if rs <= kr < rs + NA_KH:
                        return tile_ref[kr - r + NA_KH - 1]
                    return neg

                for p in range(grp.n_k // 2):
                    kr = grp.k_row + 2 * p
                    bias_ref[head, rq * GRID_W:(rq + 1) * GRID_W, p * LANES:(p + 1) * LANES] = (
                        jnp.where(low, tile(kr), tile(kr + 1)))


def _na_latent_kernel(q_ref, k_ref, v_ref, ckt_ref, cvt_ref, rb_ref, o_ref, tile_ref, *bias_refs):
    @pl.when(pl.program_id(1) == 0)
    def _():
        _expand_na_bias(rb_ref, tile_ref, bias_refs)

    ck_t = [ckt_ref[hh * LANES:(hh + 1) * LANES, :].astype(BF16) for hh in range(LATENT_HEADS_PER_STEP)]
    cv_t = [cvt_ref[hh * LANES:(hh + 1) * LANES, :].astype(BF16) for hh in range(LATENT_HEADS_PER_STEP)]

    def block(b):
        return divmod(b, len(NA_GROUPS))

    def cols(hh):
        return slice(hh * LANES, (hh + 1) * LANES)

    def qrows(g):
        return slice(NA_GROUPS[g].q_row * GRID_W, (NA_GROUPS[g].q_row + NA_GROUPS[g].n_q) * GRID_W)

    def krows(g):
        return slice(NA_GROUPS[g].k_row * GRID_W, (NA_GROUPS[g].k_row + NA_GROUPS[g].n_k) * GRID_W)

    def load_q(b):
        hh, g = block(b)
        return q_ref[qrows(g), cols(hh)]

    def scores(b, q2):
        hh, g = block(b)
        bias = jnp.concatenate([bias_refs[g][2 * hh], bias_refs[g][2 * hh + 1]], axis=0)
        return [_dot_nt(q2, k_ref[krows(g), cols(hh)]) + bias, _dot(q2, ck_t[hh])]

    def values(b):
        hh, g = block(b)
        return [_times(v_ref[krows(g), cols(hh)]), _times_t(cv_t[hh])]

    def store(b, o):
        hh, g = block(b)
        o_ref[qrows(g), cols(hh)] = o.astype(BF16)

    _attend_pairs(LATENT_HEADS_PER_STEP * len(NA_GROUPS), load_q, scores, values, _head_pair_combine, store)


def _na_latent_attention(q, k, v, cache_k_t, cache_v_t, layer_i, rel_bias_rows, side_jobs):
    width = LATENT_HEADS_PER_STEP * LANES
    blk = pl.BlockSpec((DEC_SEQ, width), lambda p, b: (b, p))
    cache = pl.BlockSpec((None, None, width, PAST_LEN), lambda p, b: (b, layer_i, p, 0))
    rb_spec = pl.BlockSpec((None, 2 * LATENT_HEADS_PER_STEP, N_DR_PAD, LANES), lambda p, b: (layer_i, p, 0, 0))
    return _pallas_call_with_side_jobs(
        _na_latent_kernel, side_jobs,
        grid=LATENT_ATTENTION_GRID[::-1],
        in_specs=[blk, blk, blk, cache, cache, rb_spec],
        out_specs=[blk],
        out_shape=[jax.ShapeDtypeStruct((N_SAMPLE_ROWS, D_MODEL), BF16)],
        args=[q, k, v, cache_k_t, cache_v_t, rel_bias_rows],
        scratch_shapes=[pltpu.VMEM((N_DR, GRID_W, LANES), F32)] + [
            pltpu.VMEM((2 * LATENT_HEADS_PER_STEP, grp.n_q * GRID_W, grp.n_k * GRID_W), F32)
            for grp in NA_GROUPS],
        name="na_latent_attention",
    )


def _padded_rel_bias(rel_bias_na):
    half = jnp.pad(rel_bias_na, ((0, 0), (0, 0), (0, N_DR_PAD - N_DR), (0, GRID_W - N_DC)))
    return jnp.concatenate([half, half], axis=-1)


FF_CHUNK = 1024


def _post_kernel(split_in, split_out, layer, *refs):
    n_x = 2 if split_in else 1
    x_refs = refs[:n_x]
    op_ref, os_ref, mod_ref, g_ref, wo_ref, w1_ref, w2_ref = refs[n_x:n_x + 7]
    y_refs = refs[n_x + 7:]
    i = pl.program_id(0)
    r = _cond_row(i)
    is_prompt = i < N_PROMPT_TILES

    def mod(j):
        return mod_ref[pl.ds(r, 1), j * D_MODEL:(j + 1) * D_MODEL]

    x = jnp.where(is_prompt, x_refs[0][...], x_refs[1][...]) if split_in else x_refs[0][...]
    o = jnp.where(is_prompt, op_ref[...], os_ref[...])
    y1 = x + mod(2) * _dot(o, wo_ref[...])
    ms = jnp.mean(y1 * y1, axis=-1, keepdims=True)
    h = (y1 * lax.rsqrt(ms + EPS) * g_ref[layer:layer + 1, :] * (1.0 + mod(4)) + mod(3)).astype(BF16)
    acc = jnp.zeros((ROW_TILE, D_MODEL), F32)
    for c in range(D_FF // FF_CHUNK):
        cols = slice(c * FF_CHUNK, (c + 1) * FF_CHUNK)
        hid = jnp.maximum(_dot(h, w1_ref[:, cols]), 0.0)
        acc = acc + _dot((hid * hid).astype(BF16), w2_ref[cols, :])
    y = y1 + mod(5) * acc
    if split_out:
        @pl.when(is_prompt)
        def _():
            y_refs[0][...] = y

        @pl.when(jnp.logical_not(is_prompt))
        def _():
            y_refs[1][...] = y
    else:
        y_refs[0][...] = y


def _post(xs, o_p, o_s, mod, layer, g_mlp, w_o, w_fc1, w_fc2, split_out, side_jobs):
    prompt_spec = pl.BlockSpec((ROW_TILE, D_MODEL), lambda i: (_prompt_tile(i), 0))
    sample_spec = pl.BlockSpec((ROW_TILE, D_MODEL), lambda i: (_sample_tile(i), 0))
    row_spec = pl.BlockSpec((ROW_TILE, D_MODEL), lambda i: (i, 0))
    split_in = len(xs) == 2
    if split_out:
        out_specs = [prompt_spec, sample_spec]
        out_shape = [jax.ShapeDtypeStruct((N_PROMPT_ROWS, D_MODEL), F32),
                     jax.ShapeDtypeStruct((N_SAMPLE_ROWS, D_MODEL), F32)]
    else:
        out_specs = [row_spec]
        out_shape = [jax.ShapeDtypeStruct((N_ROWS, D_MODEL), F32)]
    return _pallas_call_with_side_jobs(
        functools.partial(_post_kernel, split_in, split_out, layer), side_jobs,
        grid=(N_ROW_TILES,),
        in_specs=([prompt_spec, sample_spec] if split_in else [row_spec]) + [
            prompt_spec, sample_spec,
            _resident((N_COND, 6 * D_MODEL)),
            _resident((DEPTH, D_MODEL)),
            _resident((D_MODEL, D_MODEL)),
            _resident((D_MODEL, D_FF)),
            _resident((D_FF, D_MODEL)),
        ],
        out_specs=out_specs,
        out_shape=out_shape,
        args=[*xs, o_p, o_s, mod, g_mlp, w_o, w_fc1, w_fc2],
        name="wo_mlp",
    )


def _rope_tables():
    n_freq = HEAD_DIM // 4
    freqs = ROPE_BASE ** (-np.arange(n_freq, dtype=np.float64) / n_freq)
    t = np.arange(DEC_SEQ)
    ang_row = (t // GRID_W)[:, None] * freqs
    ang_col = (t % GRID_W)[:, None] * freqs
    cos64 = np.concatenate([np.cos(ang_row)] * 2 + [np.cos(ang_col)] * 2, axis=-1)
    sin64 = np.concatenate([-np.sin(ang_row), np.sin(ang_row), -np.sin(ang_col), np.sin(ang_col)], axis=-1)
    return (jnp.asarray(np.tile(cos64, (1, 2)), dtype=F32), jnp.asarray(np.tile(sin64, (1, 2)), dtype=F32))


def _pair_gain(g):
    return jnp.tile(g.reshape(1, HEAD_DIM), (1, LANES // HEAD_DIM))


def _head_major_cache(cache):
    n_req, n_layers = cache.shape[:2]
    return jnp.transpose(cache, (0, 1, 3, 4, 2)).reshape(n_req, n_layers, D_MODEL, PAST_LEN)


def kernel(x_prompt, x_sample, cache_diff_k, cache_diff_v, cache_na_k, cache_na_v, c, c_ctx, w_ada, b_ada, norm_mix_g, norm_mlp_g, w_fc1, w_fc2, w_qkv_diff, w_o_diff, q_norm_diff_g, k_norm_diff_g, lambda_q1, lambda_k1, lambda_q2, lambda_k2, subln_g, w_qkv_na, w_o_na, q_norm_na_g, k_norm_na_g, rel_bias_na):
    cond = jnp.concatenate([c_ctx[None, :], c, jnp.zeros((N_COND - 1 - DEC_BATCH, D_MODEL), F32)], axis=0)
    mod = _modulation(cond, w_ada, b_ada, 0)
    cos_tab, sin_tab = _rope_tables()
    na_bias = _padded_rel_bias(rel_bias_na)
    cdk = cache_diff_k.reshape(DEC_BATCH, DEPTH // 2, PAST_LEN * N_LANE_BLOCKS, LANES)
    cdv = cache_diff_v.reshape(DEC_BATCH, DEPTH // 2, PAST_LEN * N_LANE_BLOCKS, LANES)
    cnk_t = _head_major_cache(cache_na_k)
    cnv_t = _head_major_cache(cache_na_v)
    g_mix, g_mlp = norm_mix_g, norm_mlp_g
    w_qkv_f32 = (w_qkv_diff, w_qkv_na)
    w_o_f32 = (w_o_diff, w_o_na)
    q_gain = (q_norm_diff_g, q_norm_na_g)
    k_gain = (k_norm_diff_g, k_norm_na_g)

    xs = [x_prompt.reshape(N_PROMPT_ROWS, D_MODEL), x_sample.reshape(N_SAMPLE_ROWS, D_MODEL)]
    new_kv = [None, None]
    w_qkv = w_qkv_diff[0].astype(BF16)
    w_o = None
    for l in range(DEPTH):
        i = l // 2
        kind = l % 2
        is_diff = kind == 0
        first = l == 0
        last = l == DEPTH - 1
        x_prompt_src, x_sample_src = (xs[0], xs[1]) if len(xs) == 2 else (xs[0], xs[0])
        sample_first_tile = 0 if len(xs) == 2 else N_PROMPT_TILES
        gains = (_pair_gain(q_gain[kind][i]), _pair_gain(k_gain[kind][i]))
        lambda_init = 0.8 - 0.6 * math.exp(-0.3 * l)
        lam_params = [p[i if is_diff else 0].reshape(1, HEAD_DIM)
                      for p in (lambda_q1, lambda_k1, lambda_q2, lambda_k2)]
        gsub = subln_g[i if is_diff else 0].reshape(1, LANES)
        o_p, new_k, new_v, *cast = _ctx_layer(
            x_prompt_src, mod, l, g_mix, w_qkv, i, *gains, lam_params, gsub, is_diff, lambda_init,
            new_kv[kind], [_cast_job(w_o_f32[kind], i, 0, (N_PROMPT_TILES,))] if first else [])
        w_o = cast[0] if first else w_o
        new_kv[kind] = (new_k, new_v)
        qs, ks, vs = _qkv_latent(
            x_sample_src, sample_first_tile, mod, l, g_mix, w_qkv, *gains, (cos_tab, sin_tab), is_diff, [])
        if is_diff:
            mlp_jobs = [_cast_job(w_fc1, l, 1, LATENT_ATTENTION_GRID), _cast_job(w_fc2, l, 0, LATENT_ATTENTION_GRID)]
            o_s, w1, w2 = _diff_latent_attention(qs, ks, vs, cdk, cdv, i, lam_params, gsub, lambda_init,
                                                 mlp_jobs)
        else:
            na_grid = LATENT_ATTENTION_GRID[::-1]
            mlp_jobs = [_cast_job(w_fc1, l, 1, na_grid), _cast_job(w_fc2, l, 0, na_grid)]
            o_s, w1, w2 = _na_latent_attention(qs, ks, vs, cnk_t, cnv_t, i, na_bias, mlp_jobs)
        nxt, nxt_kind, nxt_i = l + 1, (l + 1) % 2, (l + 1) // 2
        post_grid = (N_ROW_TILES,)
        next_jobs = [] if last else [_cast_job(w_qkv_f32[nxt_kind], nxt_i, 0, post_grid),
                                     _cast_job(w_o_f32[nxt_kind], nxt_i, 0, post_grid),
                                     _ada_job(cond, w_ada, b_ada, nxt, post_grid)]
        outs = list(_post(xs, o_p, o_s, mod, l, g_mlp, w_o, w1, w2, last, next_jobs))
        if last:
            xp, xs = outs
        else:
            *xs, w_qkv, w_o, mod = outs

    n_kind = DEPTH // 2
    diff_k, diff_v = (a.reshape(BATCH, n_kind, SEQ, N_LANE_BLOCKS, LANES) for a in new_kv[0])
    na_k, na_v = (jnp.transpose(a.reshape(BATCH, n_kind, NA_HEADS, HEAD_DIM, SEQ), (0, 1, 4, 2, 3))
                  for a in new_kv[1])
    return (xp.reshape(BATCH, SEQ, D_MODEL), xs.reshape(DEC_BATCH, DEC_SEQ, D_MODEL),
            diff_k, diff_v, na_k, na_v)
```

```python
import functools
import math
from typing import Callable, NamedTuple

import jax
import jax.numpy as jnp
import numpy as np
from jax import lax
from jax.experimental import pallas as pl
from jax.experimental.pallas import tpu as pltpu

D_MODEL = 1024
BATCH = 16
SEQ = 256
DEPTH = 4
DEC_BATCH = 4
DEC_SEQ = 1024
PAST_LEN = 256
GRID_W = 64
GRID_ROWS = DEC_SEQ // GRID_W
HEAD_DIM = 64
D_FF = 4 * D_MODEL
NA_HEADS = 16
NA_KH = 8
NA_KW = 16
ROPE_BASE = 10000.0
EPS = 1e-6
NEG_INF = -1e30
LOG2_E = math.log2(math.e)
SCORE_SCALE = HEAD_DIM ** -0.5 * LOG2_E

LANES = 128
N_LANE_BLOCKS = D_MODEL // LANES
N_PROMPT_ROWS = BATCH * SEQ
N_SAMPLE_ROWS = DEC_BATCH * DEC_SEQ
N_ROWS = N_PROMPT_ROWS + N_SAMPLE_ROWS
ROW_TILE = 512
N_ROW_TILES = N_ROWS // ROW_TILE
N_PROMPT_TILES = N_PROMPT_ROWS // ROW_TILE
N_LATENT_TILES = N_SAMPLE_ROWS // ROW_TILE
TILES_PER_SAMPLE = DEC_SEQ // ROW_TILE
REQ_PER_TILE = ROW_TILE // SEQ
N_COND = 8
V7X_VMEM_BYTES = 64 * 1024 * 1024
VMEM_LIMIT = V7X_VMEM_BYTES // 8 * 7

F32 = jnp.float32
BF16 = jnp.bfloat16
NT_DIMS = (((1,), (1,)), ((), ()))


def _dot(a, b):
    return jnp.dot(a, b, preferred_element_type=F32)


def _dot_nt(a, b):
    return lax.dot_general(a, b, NT_DIMS, preferred_element_type=F32)


def _params(n_axes):
    return pltpu.CompilerParams(dimension_semantics=("arbitrary",) * n_axes,
                                vmem_limit_bytes=VMEM_LIMIT)


def _resident(shape):
    return pl.BlockSpec(shape, lambda *_: (0,) * len(shape), pipeline_mode=pl.Buffered(1))


class _SideJob(NamedTuple):
    args: list
    in_specs: list
    out_spec: pl.BlockSpec
    out_shape: jax.ShapeDtypeStruct
    body: Callable


def _cast_body(src_ref, dst_ref):
    dst_ref[...] = src_ref[...].astype(BF16)


def _step_number(grid):
    def number(*idx):
        step = idx[0]
        for size, i in zip(grid[1:], idx[1:]):
            step = step * size + i
        return step
    return number


def _cast_job(weights, layer, split_axis, grid):
    _, rows, cols = weights.shape
    n_steps, step = math.prod(grid), _step_number(grid)
    if split_axis == 0:
        band = (rows // n_steps, cols)
        src_spec = pl.BlockSpec((None,) + band, lambda *idx: (layer, step(*idx), 0))
        dst_spec = pl.BlockSpec(band, lambda *idx: (step(*idx), 0))
    else:
        band = (rows, cols // n_steps)
        src_spec = pl.BlockSpec((None,) + band, lambda *idx: (layer, 0, step(*idx)))
        dst_spec = pl.BlockSpec(band, lambda *idx: (0, step(*idx)))
    return _SideJob([weights], [src_spec], dst_spec, jax.ShapeDtypeStruct((rows, cols), BF16), _cast_body)


def _pallas_call_with_side_jobs(kernel_fn, jobs, *, grid, in_specs, out_specs, out_shape, args, name,
                                input_output_aliases=None, scratch_shapes=()):
    n_in, n_out = len(in_specs), len(out_specs)
    n_job_in = [len(job.args) for job in jobs]

    def kernel_and_jobs(*refs):
        ins, refs = refs[:n_in], refs[n_in:]
        job_ins, refs = refs[:sum(n_job_in)], refs[sum(n_job_in):]
        outs, refs = refs[:n_out], refs[n_out:]
        job_outs, scratch = refs[:len(jobs)], refs[len(jobs):]
        for job, n, out_ref in zip(jobs, n_job_in, job_outs):
            job.body(*job_ins[:n], out_ref)
            job_ins = job_ins[n:]
        kernel_fn(*ins, *outs, *scratch)

    return pl.pallas_call(
        kernel_and_jobs,
        grid=grid,
        in_specs=list(in_specs) + [spec for job in jobs for spec in job.in_specs],
        out_specs=list(out_specs) + [job.out_spec for job in jobs],
        out_shape=list(out_shape) + [job.out_shape for job in jobs],
        input_output_aliases=input_output_aliases or {},
        scratch_shapes=list(scratch_shapes),
        compiler_params=_params(len(grid)),
        name=name,
    )(*args, *(arg for job in jobs for arg in job.args))


def _prompt_tile(i):
    return jnp.minimum(i, N_PROMPT_TILES - 1)


def _sample_tile(i):
    return jnp.maximum(i - N_PROMPT_TILES, 0)


def _cond_row(i):
    return jnp.where(i < N_PROMPT_TILES, 0, 1 + (i - N_PROMPT_TILES) // TILES_PER_SAMPLE)


def _low_half_mask(shape):
    return lax.broadcasted_iota(jnp.int32, shape, len(shape) - 1) < HEAD_DIM


def _ada_kernel(layer, cond_ref, w_ref, b_ref, out_ref):
    cond = cond_ref[...]
    act = (cond * jax.nn.sigmoid(cond)).astype(BF16)
    out_ref[...] = _dot(act, w_ref[...].astype(BF16)) + b_ref[layer:layer + 1, :]


def _ada_specs(layer, n_steps):
    tn = 6 * D_MODEL // n_steps
    in_specs = [pl.BlockSpec((N_COND, D_MODEL), lambda j: (0, 0)),
                pl.BlockSpec((None, D_MODEL, tn), lambda j: (layer, 0, j)),
                pl.BlockSpec((DEPTH, tn), lambda j: (0, j))]
    return in_specs, pl.BlockSpec((N_COND, tn), lambda j: (0, j)), jax.ShapeDtypeStruct((N_COND, 6 * D_MODEL), F32)


ADA_GRID = (4,)


def _modulation(cond, w_ada, b_ada, layer, side_jobs):
    in_specs, out_spec, out_shape = _ada_specs(layer, *ADA_GRID)
    return _pallas_call_with_side_jobs(
        functools.partial(_ada_kernel, layer), side_jobs, grid=ADA_GRID, in_specs=in_specs,
        out_specs=[out_spec], out_shape=[out_shape], args=[cond, w_ada, b_ada], name="ada_modulation")


def _ada_job(cond, w_ada, b_ada, layer, grid):
    (n_steps,) = grid
    in_specs, out_spec, out_shape = _ada_specs(layer, n_steps)
    return _SideJob([cond, w_ada, b_ada], in_specs, out_spec, out_shape, functools.partial(_ada_kernel, layer))


def _group_rms(x, gain):
    low = _low_half_mask((1, LANES))
    cols = []
    for c in range(N_LANE_BLOCKS):
        xb = x[:, c * LANES:(c + 1) * LANES]
        sq = xb * xb
        s_lo = jnp.sum(jnp.where(low, sq, 0.0), axis=-1, keepdims=True)
        s_hi = jnp.sum(jnp.where(low, 0.0, sq), axis=-1, keepdims=True)
        ms = jnp.where(low, s_lo, s_hi) * (1.0 / HEAD_DIM)
        cols.append(xb * lax.rsqrt(ms + EPS) * gain)
    return cols


def _rope(cols, cos, sin_signed):
    lane = lax.broadcasted_iota(jnp.int32, (1, LANES), 1)
    first = (lane % 32) < 16
    out = []
    for xb in cols:
        partner = jnp.where(first, pltpu.roll(xb, LANES - 16, 1), pltpu.roll(xb, 16, 1))
        out.append(xb * cos + partner * sin_signed)
    return out


def _project_qkv(layer, x_ref, mod_ref, g_ref, w_ref, qg_ref, kg_ref, r):
    x = x_ref[...]
    ms = jnp.mean(x * x, axis=-1, keepdims=True)
    xn = x * lax.rsqrt(ms + EPS) * g_ref[layer:layer + 1, :]
    shift = mod_ref[pl.ds(r, 1), 0:D_MODEL]
    scale = mod_ref[pl.ds(r, 1), D_MODEL:2 * D_MODEL]
    h = (xn * (1.0 + scale) + shift).astype(BF16)
    qkv = _dot(h, w_ref[...])
    qn = _group_rms(qkv[:, 0:D_MODEL], qg_ref[...])
    kn = _group_rms(qkv[:, D_MODEL:2 * D_MODEL], kg_ref[...])
    return qn, kn, qkv[:, 2 * D_MODEL:3 * D_MODEL]


def _lane_block(c):
    return slice(c * LANES, (c + 1) * LANES)


def _qkv_latent_kernel(is_diff, layer, x_ref, mod_ref, g_ref, w_ref, qg_ref, kg_ref, cos_ref, sin_ref,
                       q_ref, k_ref, v_ref):
    r = 1 + pl.program_id(0) // TILES_PER_SAMPLE
    qn, kn, v = _project_qkv(layer, x_ref, mod_ref, g_ref, w_ref, qg_ref, kg_ref, r)
    v_ref[...] = v.astype(BF16)
    if is_diff:
        cos = cos_ref[...]
        sin = sin_ref[...]
        qn = _rope(qn, cos, sin)
        kn = _rope(kn, cos, sin)
    for c in range(N_LANE_BLOCKS):
        q_ref[:, _lane_block(c)] = (qn[c] * SCORE_SCALE).astype(BF16)
        k_ref[:, _lane_block(c)] = kn[c].astype(BF16)


def _ctx_layer_kernel(is_diff, layer, lambda_init, has_prev, *refs):
    if has_prev:
        refs = refs[2:]
    (x_ref, mod_ref, g_ref, w_ref, qg_ref, kg_ref, lq1_ref, lk1_ref, lq2_ref, lk2_ref, gsub_ref,
     o_ref, kf_ref, vf_ref) = refs
    qn, kn, v = _project_qkv(layer, x_ref, mod_ref, g_ref, w_ref, qg_ref, kg_ref, 0)
    for c in range(N_LANE_BLOCKS):
        for bb in range(REQ_PER_TILE):
            rows = slice(bb * SEQ, (bb + 1) * SEQ)
            if is_diff:
                dst = pl.ds(c, SEQ, stride=N_LANE_BLOCKS)
                kf_ref[bb, dst, :] = kn[c][rows, :]
                vf_ref[bb, dst, :] = v[rows, _lane_block(c)]
            else:
                kf_ref[bb, _lane_block(c), :] = kn[c][rows, :].T
                vf_ref[bb, _lane_block(c), :] = v[rows, _lane_block(c)].T
    q = [(qn[c] * SCORE_SCALE).astype(BF16) for c in range(N_LANE_BLOCKS)]
    k = [kn[c].astype(BF16) for c in range(N_LANE_BLOCKS)]
    vb = [v[:, _lane_block(c)].astype(BF16) for c in range(N_LANE_BLOCKS)]
    if is_diff:
        lam = _diff_lambda(lq1_ref, lk1_ref, lq2_ref, lk2_ref, lambda_init)
        combine = _diff_combine(lam, gsub_ref[...], lambda_init)
    else:
        combine = _head_pair_combine

    def block(b):
        bb, c = divmod(b, N_LANE_BLOCKS)
        return slice(bb * SEQ, (bb + 1) * SEQ), c

    def load_q(b):
        rows, c = block(b)
        return q[c][rows, :]

    def scores(b, q2):
        rows, c = block(b)
        return [_dot_nt(q2, k[c][rows, :])]

    def values(b):
        rows, c = block(b)
        return [_times(vb[c][rows, :])]

    def store(b, o):
        rows, c = block(b)
        o_ref[rows, _lane_block(c)] = o.astype(BF16)

    _attend_pairs(REQ_PER_TILE * N_LANE_BLOCKS, load_q, scores, values, combine, store)


def _qkv_latent(x, first_tile, mod, layer, g_mix, w_qkv, q_gain, k_gain, rope_tabs, is_diff, side_jobs):
    row_spec = pl.BlockSpec((ROW_TILE, D_MODEL), lambda i: (i, 0))
    rope_spec = pl.BlockSpec((ROW_TILE, LANES), lambda i: (i % TILES_PER_SAMPLE, 0))
    return _pallas_call_with_side_jobs(
        functools.partial(_qkv_latent_kernel, is_diff, layer), side_jobs,
        grid=(N_LATENT_TILES,),
        in_specs=[pl.BlockSpec((ROW_TILE, D_MODEL), lambda i: (first_tile + i, 0)),
                  _resident((N_COND, 6 * D_MODEL)),
                  _resident((DEPTH, D_MODEL)),
                  _resident((D_MODEL, 3 * D_MODEL)),
                  _resident((1, LANES)), _resident((1, LANES)), rope_spec, rope_spec],
        out_specs=[row_spec] * 3,
        out_shape=[jax.ShapeDtypeStruct((N_SAMPLE_ROWS, D_MODEL), BF16)] * 3,
        args=[x, mod, g_mix, w_qkv, q_gain, k_gain, *rope_tabs],
        name="qkv_latent",
    )


def _ctx_layer(x, mod, layer, g_mix, w_qkv, kind_layer, q_gain, k_gain, lam_params, gsub, is_diff,
               lambda_init, prev, side_jobs):
    row_spec = pl.BlockSpec((ROW_TILE, D_MODEL), lambda i: (i, 0))
    if is_diff:
        cache_rows, cache_cols = SEQ * N_LANE_BLOCKS, LANES
    else:
        cache_rows, cache_cols = D_MODEL, SEQ
    cache_spec = pl.BlockSpec((REQ_PER_TILE, None, cache_rows, cache_cols), lambda i: (i, kind_layer, 0, 0))
    cache_shape = jax.ShapeDtypeStruct((BATCH, DEPTH // 2, cache_rows, cache_cols), F32)
    in_specs = ([row_spec, _resident((N_COND, 6 * D_MODEL)), _resident((DEPTH, D_MODEL)),
                 _resident((D_MODEL, 3 * D_MODEL)), _resident((1, LANES)), _resident((1, LANES))]
                + [_resident((1, HEAD_DIM))] * 4 + [_resident((1, LANES))])
    args = [x, mod, g_mix, w_qkv, q_gain, k_gain, *lam_params, gsub]
    aliases = {}
    if prev is not None:
        in_specs = [pl.BlockSpec(memory_space=pl.ANY)] * 2 + in_specs
        args = list(prev) + args
        aliases = {0: 1, 1: 2}
    return _pallas_call_with_side_jobs(
        functools.partial(_ctx_layer_kernel, is_diff, layer, lambda_init, prev is not None), side_jobs,
        grid=(N_PROMPT_TILES,),
        in_specs=in_specs,
        out_specs=[row_spec, cache_spec, cache_spec],
        out_shape=[jax.ShapeDtypeStruct((N_PROMPT_ROWS, D_MODEL), BF16), cache_shape, cache_shape],
        args=args,
        input_output_aliases=aliases,
        name="ctx_layer",
    )


def _softmax_pv(score_parts, value_parts):
    m = None
    for s in score_parts:
        pm = jnp.max(s, axis=-1, keepdims=True)
        m = pm if m is None else jnp.maximum(m, pm)
    acc = None
    for s, times_v in zip(score_parts, value_parts):
        pv = times_v(jnp.exp2(s - m).astype(BF16))
        acc = pv if acc is None else acc + pv
    return acc[:, :LANES] * (1.0 / acc[:, LANES:])


def _times(v):
    v1 = jnp.concatenate([v, jnp.ones_like(v)], axis=1)
    return lambda w: _dot(w, v1)


def _times_t(v_t):
    v1_t = jnp.concatenate([v_t, jnp.ones_like(v_t)], axis=0)
    return lambda w: _dot_nt(w, v1_t)


def _diff_lambda(lq1_ref, lk1_ref, lq2_ref, lk2_ref, lambda_init):
    e1 = jnp.exp(jnp.sum(lq1_ref[...] * lk1_ref[...], axis=-1, keepdims=True))
    e2 = jnp.exp(jnp.sum(lq2_ref[...] * lk2_ref[...], axis=-1, keepdims=True))
    return e1 - e2 + lambda_init


def _attend_pairs(n_blocks, load_q, scores, values, combine, store):
    def block_scores(b):
        q = load_q(b)
        low = _low_half_mask(q.shape)
        zero = jnp.zeros_like(q)
        return scores(b, jnp.concatenate([jnp.where(low, q, zero), jnp.where(low, zero, q)], axis=0))

    s = block_scores(0)
    for b in range(n_blocks):
        s_next = block_scores(b + 1) if b + 1 < n_blocks else None
        o = _softmax_pv(s, values(b))
        s = s_next
        rows = o.shape[0] // 2
        store(b, combine(o[:rows], o[rows:]))


def _diff_combine(lam, gsub, lambda_init):
    def combine(o1, o2):
        o = o1 - lam * o2
        ms = jnp.mean(o * o, axis=-1, keepdims=True)
        return o * lax.rsqrt(ms + EPS) * gsub * (1.0 - lambda_init)
    return combine


def _head_pair_combine(o_lo, o_hi):
    return jnp.where(_low_half_mask(o_lo.shape), o_lo, o_hi)


Q_CHUNK = 512
LATENT_HEADS_PER_STEP = 2


def _diff_latent_kernel(lambda_init, q_ref, k_ref, v_ref, ck_ref, cv_ref,
                        lq1_ref, lk1_ref, lq2_ref, lk2_ref, gsub_ref, o_ref):
    lam = _diff_lambda(lq1_ref, lk1_ref, lq2_ref, lk2_ref, lambda_init)
    n_chunks = DEC_SEQ // Q_CHUNK
    keys, values = [], []
    for hh in range(LATENT_HEADS_PER_STEP):
        cols = slice(hh * LANES, (hh + 1) * LANES)
        head = pl.program_id(1) * LATENT_HEADS_PER_STEP + hh
        head_rows = pl.ds(head, PAST_LEN, stride=N_LANE_BLOCKS)
        keys.append([k_ref[:, cols], ck_ref[head_rows, :].astype(BF16)])
        values.append([_times(v_ref[:, cols]), _times(cv_ref[head_rows, :].astype(BF16))])

    def block(b):
        hh, c = divmod(b, n_chunks)
        return slice(c * Q_CHUNK, (c + 1) * Q_CHUNK), slice(hh * LANES, (hh + 1) * LANES)

    def store(b, o):
        o_ref[block(b)] = o.astype(BF16)

    _attend_pairs(LATENT_HEADS_PER_STEP * n_chunks,
                  lambda b: q_ref[block(b)],
                  lambda b, q2: [_dot_nt(q2, kpart) for kpart in keys[b // n_chunks]],
                  lambda b: values[b // n_chunks],
                  _diff_combine(lam, gsub_ref[...], lambda_init), store)


LATENT_ATTENTION_GRID = (DEC_BATCH, N_LANE_BLOCKS // LATENT_HEADS_PER_STEP)


def _diff_latent_attention(q, k, v, cache_k, cache_v, layer_i, lam_params, gsub, lambda_init, side_jobs):
    blk = pl.BlockSpec((DEC_SEQ, LATENT_HEADS_PER_STEP * LANES), lambda b, h: (b, h))
    cache = pl.BlockSpec((None, None, PAST_LEN * N_LANE_BLOCKS, LANES), lambda b, h: (b, layer_i, 0, 0))
    small = [_resident((1, HEAD_DIM))] * 4 + [_resident((1, LANES))]
    return _pallas_call_with_side_jobs(
        functools.partial(_diff_latent_kernel, lambda_init), side_jobs,
        grid=LATENT_ATTENTION_GRID,
        in_specs=[blk, blk, blk, cache, cache] + small,
        out_specs=[blk],
        out_shape=[jax.ShapeDtypeStruct((N_SAMPLE_ROWS, D_MODEL), BF16)],
        args=[q, k, v, cache_k, cache_v, *lam_params, gsub],
        name="diff_latent_attention",
    )


class _NaGroup(NamedTuple):
    q_row: int
    n_q: int
    k_row: int
    n_k: int


def _window_start(r):
    return min(max(r - NA_KH // 2, 0), GRID_ROWS - NA_KH)


def _na_groups(rows_per_group):
    groups = []
    for q_row in range(0, GRID_ROWS, rows_per_group):
        first = _window_start(q_row)
        last = _window_start(q_row + rows_per_group - 1) + NA_KH
        k_row = first - first % 2
        n_k = last - k_row
        n_k += n_k % 2
        groups.append(_NaGroup(q_row, rows_per_group, min(k_row, GRID_ROWS - n_k), n_k))
    return groups


NA_GROUPS = _na_groups(4)
N_DR = 2 * NA_KH - 1
N_DC = 2 * NA_KW - 1
N_DR_PAD = 16


def _expand_na_bias(rb_ref, tile_ref, bias_refs):
    qc = lax.broadcasted_iota(jnp.int32, (GRID_W, LANES), 0)
    lane = lax.broadcasted_iota(jnp.int32, (GRID_W, LANES), 1)
    kc = lane % GRID_W
    cs = jnp.clip(qc - NA_KW // 2, 0, GRID_W - NA_KW)
    col_ok = (kc >= cs) & (kc < cs + NA_KW)
    low = lane < GRID_W
    neg = jnp.full((GRID_W, LANES), NEG_INF, F32)
    for head in range(2 * LATENT_HEADS_PER_STEP):
        for dr in range(N_DR):
            row = jnp.broadcast_to(rb_ref[head, dr:dr + 1, :], (GRID_W, LANES))
            shifted = pltpu.roll(row, LANES - (NA_KW - 1), 1, stride=1, stride_axis=0)
            tile_ref[dr] = jnp.where(col_ok, shifted * LOG2_E, neg)
        for grp, bias_ref in zip(NA_GROUPS, bias_refs):
            for rq in range(grp.n_q):
                r = grp.q_row + rq
                rs = _window_start(r)

                def tile(kr, r=r, rs=rs):
                    if rs <= kr < rs + NA_KH:
                        return tile_ref[kr - r + NA_KH - 1]
                    return neg

                for p in range(grp.n_k // 2):
                    kr = grp.k_row + 2 * p
                    bias_ref[head, rq * GRID_W:(rq + 1) * GRID_W, p * LANES:(p + 1) * LANES] = (
                        jnp.where(low, tile(kr), tile(kr + 1)))


def _na_latent_kernel(q_ref, k_ref, v_ref, ckt_ref, cvt_ref, rb_ref, o_ref, tile_ref, *bias_refs):
    @pl.when(pl.program_id(1) == 0)
    def _():
        _expand_na_bias(rb_ref, tile_ref, bias_refs)

    ck_t = [ckt_ref[hh * LANES:(hh + 1) * LANES, :].astype(BF16) for hh in range(LATENT_HEADS_PER_STEP)]
    cv_t = [cvt_ref[hh * LANES:(hh + 1) * LANES, :].astype(BF16) for hh in range(LATENT_HEADS_PER_STEP)]

    def block(b):
        return divmod(b, len(NA_GROUPS))

    def cols(hh):
        return slice(hh * LANES, (hh + 1) * LANES)

    def qrows(g):
        return slice(NA_GROUPS[g].q_row * GRID_W, (NA_GROUPS[g].q_row + NA_GROUPS[g].n_q) * GRID_W)

    def krows(g):
        return slice(NA_GROUPS[g].k_row * GRID_W, (NA_GROUPS[g].k_row + NA_GROUPS[g].n_k) * GRID_W)

    def load_q(b):
        hh, g = block(b)
        return q_ref[qrows(g), cols(hh)]

    def scores(b, q2):
        hh, g = block(b)
        bias = jnp.concatenate([bias_refs[g][2 * hh], bias_refs[g][2 * hh + 1]], axis=0)
        return [_dot_nt(q2, k_ref[krows(g), cols(hh)]) + bias, _dot(q2, ck_t[hh])]

    def values(b):
        hh, g = block(b)
        return [_times(v_ref[krows(g), cols(hh)]), _times_t(cv_t[hh])]

    def store(b, o):
        hh, g = block(b)
        o_ref[qrows(g), cols(hh)] = o.astype(BF16)

    _attend_pairs(LATENT_HEADS_PER_STEP * len(NA_GROUPS), load_q, scores, values, _head_pair_combine, store)


def _na_latent_attention(q, k, v, cache_k_t, cache_v_t, layer_i, rel_bias_rows, side_jobs):
    width = LATENT_HEADS_PER_STEP * LANES
    blk = pl.BlockSpec((DEC_SEQ, width), lambda p, b: (b, p))
    cache = pl.BlockSpec((None, None, width, PAST_LEN), lambda p, b: (b, layer_i, p, 0))
    rb_spec = pl.BlockSpec((None, 2 * LATENT_HEADS_PER_STEP, N_DR_PAD, LANES), lambda p, b: (layer_i, p, 0, 0))
    return _pallas_call_with_side_jobs(
        _na_latent_kernel, side_jobs,
        grid=LATENT_ATTENTION_GRID[::-1],
        in_specs=[blk, blk, blk, cache, cache, rb_spec],
        out_specs=[blk],
        out_shape=[jax.ShapeDtypeStruct((N_SAMPLE_ROWS, D_MODEL), BF16)],
        args=[q, k, v, cache_k_t, cache_v_t, rel_bias_rows],
        scratch_shapes=[pltpu.VMEM((N_DR, GRID_W, LANES), F32)] + [
            pltpu.VMEM((2 * LATENT_HEADS_PER_STEP, grp.n_q * GRID_W, grp.n_k * GRID_W), F32)
            for grp in NA_GROUPS],
        name="na_latent_attention",
    )


def _padded_rel_bias(rel_bias_na):
    half = jnp.pad(rel_bias_na, ((0, 0), (0, 0), (0, N_DR_PAD - N_DR), (0, GRID_W - N_DC)))
    return jnp.concatenate([half, half], axis=-1)


FF_CHUNK = 1024


def _post_kernel(split_in, split_out, layer, *refs):
    n_x = 2 if split_in else 1
    x_refs = refs[:n_x]
    op_ref, os_ref, mod_ref, g_ref, wo_ref, w1_ref, w2_ref = refs[n_x:n_x + 7]
    y_refs = refs[n_x + 7:]
    i = pl.program_id(0)
    r = _cond_row(i)
    is_prompt = i < N_PROMPT_TILES

    def mod(j):
        return mod_ref[pl.ds(r, 1), j * D_MODEL:(j + 1) * D_MODEL]

    x = jnp.where(is_prompt, x_refs[0][...], x_refs[1][...]) if split_in else x_refs[0][...]
    o = jnp.where(is_prompt, op_ref[...], os_ref[...])
    y1 = x + mod(2) * _dot(o, wo_ref[...])
    ms = jnp.mean(y1 * y1, axis=-1, keepdims=True)
    h = (y1 * lax.rsqrt(ms + EPS) * g_ref[layer:layer + 1, :] * (1.0 + mod(4)) + mod(3)).astype(BF16)
    acc = jnp.zeros((ROW_TILE, D_MODEL), F32)
    for c in range(D_FF // FF_CHUNK):
        cols = slice(c * FF_CHUNK, (c + 1) * FF_CHUNK)
        hid = jnp.maximum(_dot(h, w1_ref[:, cols]), 0.0)
        acc = acc + _dot((hid * hid).astype(BF16), w2_ref[cols, :])
    y = y1 + mod(5) * acc
    if split_out:
        @pl.when(is_prompt)
        def _():
            y_refs[0][...] = y

        @pl.when(jnp.logical_not(is_prompt))
        def _():
            y_refs[1][...] = y
    else:
        y_refs[0][...] = y


def _post(xs, o_p, o_s, mod, layer, g_mlp, w_o, w_fc1, w_fc2, split_out, side_jobs):
    prompt_spec = pl.BlockSpec((ROW_TILE, D_MODEL), lambda i: (_prompt_tile(i), 0))
    sample_spec = pl.BlockSpec((ROW_TILE, D_MODEL), lambda i: (_sample_tile(i), 0))
    row_spec = pl.BlockSpec((ROW_TILE, D_MODEL), lambda i: (i, 0))
    split_in = len(xs) == 2
    if split_out:
        out_specs = [prompt_spec, sample_spec]
        out_shape = [jax.ShapeDtypeStruct((N_PROMPT_ROWS, D_MODEL), F32),
                     jax.ShapeDtypeStruct((N_SAMPLE_ROWS, D_MODEL), F32)]
    else:
        out_specs = [row_spec]
        out_shape = [jax.ShapeDtypeStruct((N_ROWS, D_MODEL), F32)]
    return _pallas_call_with_side_jobs(
        functools.partial(_post_kernel, split_in, split_out, layer), side_jobs,
        grid=(N_ROW_TILES,),
        in_specs=([prompt_spec, sample_spec] if split_in else [row_spec]) + [
            prompt_spec, sample_spec,
            _resident((N_COND, 6 * D_MODEL)),
            _resident((DEPTH, D_MODEL)),
            _resident((D_MODEL, D_MODEL)),
            _resident((D_MODEL, D_FF)),
            _resident((D_FF, D_MODEL)),
        ],
        out_specs=out_specs,
        out_shape=out_shape,
        args=[*xs, o_p, o_s, mod, g_mlp, w_o, w_fc1, w_fc2],
        name="wo_mlp",
    )


def _rope_tables():
    n_freq = HEAD_DIM // 4
    freqs = ROPE_BASE ** (-np.arange(n_freq, dtype=np.float64) / n_freq)
    t = np.arange(DEC_SEQ)
    ang_row = (t // GRID_W)[:, None] * freqs
    ang_col = (t % GRID_W)[:, None] * freqs
    cos64 = np.concatenate([np.cos(ang_row)] * 2 + [np.cos(ang_col)] * 2, axis=-1)
    sin64 = np.concatenate([-np.sin(ang_row), np.sin(ang_row), -np.sin(ang_col), np.sin(ang_col)], axis=-1)
    return (jnp.asarray(np.tile(cos64, (1, 2)), dtype=F32), jnp.asarray(np.tile(sin64, (1, 2)), dtype=F32))


def _pair_gain(g):
    return jnp.tile(g.reshape(1, HEAD_DIM), (1, LANES // HEAD_DIM))


def _head_major_cache(cache):
    n_req, n_layers = cache.shape[:2]
    return jnp.transpose(cache, (0, 1, 3, 4, 2)).reshape(n_req, n_layers, D_MODEL, PAST_LEN)


def kernel(x_prompt, x_sample, cache_diff_k, cache_diff_v, cache_na_k, cache_na_v, c, c_ctx, w_ada, b_ada, norm_mix_g, norm_mlp_g, w_fc1, w_fc2, w_qkv_diff, w_o_diff, q_norm_diff_g, k_norm_diff_g, lambda_q1, lambda_k1, lambda_q2, lambda_k2, subln_g, w_qkv_na, w_o_na, q_norm_na_g, k_norm_na_g, rel_bias_na):
    cond = jnp.concatenate([c_ctx[None, :], c, jnp.zeros((N_COND - 1 - DEC_BATCH, D_MODEL), F32)], axis=0)
    mod, w_qkv = _modulation(cond, w_ada, b_ada, 0, [_cast_job(w_qkv_diff, 0, 0, ADA_GRID)])
    cos_tab, sin_tab = _rope_tables()
    na_bias = _padded_rel_bias(rel_bias_na)
    cdk = cache_diff_k.reshape(DEC_BATCH, DEPTH // 2, PAST_LEN * N_LANE_BLOCKS, LANES)
    cdv = cache_diff_v.reshape(DEC_BATCH, DEPTH // 2, PAST_LEN * N_LANE_BLOCKS, LANES)
    cnk_t = _head_major_cache(cache_na_k)
    cnv_t = _head_major_cache(cache_na_v)
    g_mix, g_mlp = norm_mix_g, norm_mlp_g
    w_qkv_f32 = (w_qkv_diff, w_qkv_na)
    w_o_f32 = (w_o_diff, w_o_na)
    q_gain = (q_norm_diff_g, q_norm_na_g)
    k_gain = (k_norm_diff_g, k_norm_na_g)

    xs = [x_prompt.reshape(N_PROMPT_ROWS, D_MODEL), x_sample.reshape(N_SAMPLE_ROWS, D_MODEL)]
    new_kv = [None, None]
    w_o = None
    for l in range(DEPTH):
        i = l // 2
        kind = l % 2
        is_diff = kind == 0
        first = l == 0
        last = l == DEPTH - 1
        x_prompt_src, x_sample_src = (xs[0], xs[1]) if len(xs) == 2 else (xs[0], xs[0])
        sample_first_tile = 0 if len(xs) == 2 else N_PROMPT_TILES
        gains = (_pair_gain(q_gain[kind][i]), _pair_gain(k_gain[kind][i]))
        lambda_init = 0.8 - 0.6 * math.exp(-0.3 * l)
        lam_params = [p[i if is_diff else 0].reshape(1, HEAD_DIM)
                      for p in (lambda_q1, lambda_k1, lambda_q2, lambda_k2)]
        gsub = subln_g[i if is_diff else 0].reshape(1, LANES)
        o_p, new_k, new_v, *cast = _ctx_layer(
            x_prompt_src, mod, l, g_mix, w_qkv, i, *gains, lam_params, gsub, is_diff, lambda_init,
            new_kv[kind], [_cast_job(w_o_f32[kind], i, 0, (N_PROMPT_TILES,))] if first else [])
        w_o = cast[0] if first else w_o
        new_kv[kind] = (new_k, new_v)
        qs, ks, vs = _qkv_latent(
            x_sample_src, sample_first_tile, mod, l, g_mix, w_qkv, *gains, (cos_tab, sin_tab), is_diff, [])
        if is_diff:
            mlp_jobs = [_cast_job(w_fc1, l, 1, LATENT_ATTENTION_GRID), _cast_job(w_fc2, l, 0, LATENT_ATTENTION_GRID)]
            o_s, w1, w2 = _diff_latent_attention(qs, ks, vs, cdk, cdv, i, lam_params, gsub, lambda_init,
                                                 mlp_jobs)
        else:
            na_grid = LATENT_ATTENTION_GRID[::-1]
            mlp_jobs = [_cast_job(w_fc1, l, 1, na_grid), _cast_job(w_fc2, l, 0, na_grid)]
            o_s, w1, w2 = _na_latent_attention(qs, ks, vs, cnk_t, cnv_t, i, na_bias, mlp_jobs)
        nxt, nxt_kind, nxt_i = l + 1, (l + 1) % 2, (l + 1) // 2
        post_grid = (N_ROW_TILES,)
        next_jobs = [] if last else [_cast_job(w_qkv_f32[nxt_kind], nxt_i, 0, post_grid),
                                     _cast_job(w_o_f32[nxt_kind], nxt_i, 0, post_grid),
                                     _ada_job(cond, w_ada, b_ada, nxt, post_grid)]
        outs = list(_post(xs, o_p, o_s, mod, l, g_mlp, w_o, w1, w2, last, next_jobs))
        if last:
            xp, xs = outs
        else:
            *xs, w_qkv, w_o, mod = outs

    n_kind = DEPTH // 2
    diff_k, diff_v = (a.reshape(BATCH, n_kind, SEQ, N_LANE_BLOCKS, LANES) for a in new_kv[0])
    na_k, na_v = (jnp.transpose(a.reshape(BATCH, n_kind, NA_HEADS, HEAD_DIM, SEQ), (0, 1, 4, 2, 3))
                  for a in new_kv[1])
    return (xp.reshape(BATCH, SEQ, D_MODEL), xs.reshape(DEC_BATCH, DEC_SEQ, D_MODEL),
            diff_k, diff_v, na_k, na_v)
```

```python
import functools
import math
from typing import Callable, NamedTuple

import jax
import jax.numpy as jnp
import numpy as np
from jax import lax
from jax.experimental import pallas as pl
from jax.experimental.pallas import tpu as pltpu

D_MODEL = 1024
BATCH = 16
SEQ = 256
DEPTH = 4
DEC_BATCH = 4
DEC_SEQ = 1024
PAST_LEN = 256
GRID_W = 64
GRID_ROWS = DEC_SEQ // GRID_W
HEAD_DIM = 64
D_FF = 4 * D_MODEL
NA_HEADS = 16
NA_KH = 8
NA_KW = 16
ROPE_BASE = 10000.0
EPS = 1e-6
NEG_INF = -1e30
LOG2_E = math.log2(math.e)
SCORE_SCALE = HEAD_DIM ** -0.5 * LOG2_E

LANES = 128
N_LANE_BLOCKS = D_MODEL // LANES
N_PROMPT_ROWS = BATCH * SEQ
N_SAMPLE_ROWS = DEC_BATCH * DEC_SEQ
N_ROWS = N_PROMPT_ROWS + N_SAMPLE_ROWS
ROW_TILE = 512
N_ROW_TILES = N_ROWS // ROW_TILE
N_PROMPT_TILES = N_PROMPT_ROWS // ROW_TILE
N_LATENT_TILES = N_SAMPLE_ROWS // ROW_TILE
TILES_PER_SAMPLE = DEC_SEQ // ROW_TILE
REQ_PER_TILE = ROW_TILE // SEQ
N_COND = 8
V7X_VMEM_BYTES = 64 * 1024 * 1024
VMEM_LIMIT = V7X_VMEM_BYTES // 8 * 7

F32 = jnp.float32
BF16 = jnp.bfloat16
NT_DIMS = (((1,), (1,)), ((), ()))


def _dot(a, b):
    return jnp.dot(a, b, preferred_element_type=F32)


def _dot_nt(a, b):
    return lax.dot_general(a, b, NT_DIMS, preferred_element_type=F32)


def _params(n_axes):
    return pltpu.CompilerParams(dimension_semantics=("arbitrary",) * n_axes,
                                vmem_limit_bytes=VMEM_LIMIT)


def _resident(shape):
    return pl.BlockSpec(shape, lambda *_: (0,) * len(shape), pipeline_mode=pl.Buffered(1))


class _SideJob(NamedTuple):
    args: list
    in_specs: list
    out_spec: pl.BlockSpec
    out_shape: jax.ShapeDtypeStruct
    body: Callable


def _cast_body(src_ref, dst_ref):
    dst_ref[...] = src_ref[...].astype(BF16)


def _step_number(grid):
    def number(*idx):
        step = idx[0]
        for size, i in zip(grid[1:], idx[1:]):
            step = step * size + i
        return step
    return number


def _cast_job(weights, layer, split_axis, grid):
    _, rows, cols = weights.shape
    n_steps, step = math.prod(grid), _step_number(grid)
    if split_axis == 0:
        band = (rows // n_steps, cols)
        src_spec = pl.BlockSpec((None,) + band, lambda *idx: (layer, step(*idx), 0))
        dst_spec = pl.BlockSpec(band, lambda *idx: (step(*idx), 0))
    else:
        band = (rows, cols // n_steps)
        src_spec = pl.BlockSpec((None,) + band, lambda *idx: (layer, 0, step(*idx)))
        dst_spec = pl.BlockSpec(band, lambda *idx: (0, step(*idx)))
    return _SideJob([weights], [src_spec], dst_spec, jax.ShapeDtypeStruct((rows, cols), BF16), _cast_body)


def _pallas_call_with_side_jobs(kernel_fn, jobs, *, grid, in_specs, out_specs, out_shape, args, name,
                                input_output_aliases=None, scratch_shapes=()):
    n_in, n_out = len(in_specs), len(out_specs)
    n_job_in = [len(job.args) for job in jobs]

    def kernel_and_jobs(*refs):
        ins, refs = refs[:n_in], refs[n_in:]
        job_ins, refs = refs[:sum(n_job_in)], refs[sum(n_job_in):]
        outs, refs = refs[:n_out], refs[n_out:]
        job_outs, scratch = refs[:len(jobs)], refs[len(jobs):]
        for job, n, out_ref in zip(jobs, n_job_in, job_outs):
            job.body(*job_ins[:n], out_ref)
            job_ins = job_ins[n:]
        kernel_fn(*ins, *outs, *scratch)

    return pl.pallas_call(
        kernel_and_jobs,
        grid=grid,
        in_specs=list(in_specs) + [spec for job in jobs for spec in job.in_specs],
        out_specs=list(out_specs) + [job.out_spec for job in jobs],
        out_shape=list(out_shape) + [job.out_shape for job in jobs],
        input_output_aliases=input_output_aliases or {},
        scratch_shapes=list(scratch_shapes),
        compiler_params=_params(len(grid)),
        name=name,
    )(*args, *(arg for job in jobs for arg in job.args))


def _prompt_tile(i):
    return jnp.minimum(i, N_PROMPT_TILES - 1)


def _sample_tile(i):
    return jnp.maximum(i - N_PROMPT_TILES, 0)


def _cond_row(i):
    return jnp.where(i < N_PROMPT_TILES, 0, 1 + (i - N_PROMPT_TILES) // TILES_PER_SAMPLE)


def _low_half_mask(shape):
    return lax.broadcasted_iota(jnp.int32, shape, len(shape) - 1) < HEAD_DIM


def _ada_kernel(layer, cond_ref, w_ref, b_ref, out_ref):
    cond = cond_ref[...]
    act = (cond * jax.nn.sigmoid(cond)).astype(BF16)
    out_ref[...] = _dot(act, w_ref[...].astype(BF16)) + b_ref[layer:layer + 1, :]


def _ada_specs(layer, n_steps):
    tn = 6 * D_MODEL // n_steps
    in_specs = [pl.BlockSpec((N_COND, D_MODEL), lambda j: (0, 0)),
                pl.BlockSpec((None, D_MODEL, tn), lambda j: (layer, 0, j)),
                pl.BlockSpec((DEPTH, tn), lambda j: (0, j))]
    return in_specs, pl.BlockSpec((N_COND, tn), lambda j: (0, j)), jax.ShapeDtypeStruct((N_COND, 6 * D_MODEL), F32)


ADA_GRID = (4,)


def _modulation(cond, w_ada, b_ada, layer, side_jobs):
    in_specs, out_spec, out_shape = _ada_specs(layer, *ADA_GRID)
    return _pallas_call_with_side_jobs(
        functools.partial(_ada_kernel, layer), side_jobs, grid=ADA_GRID, in_specs=in_specs,
        out_specs=[out_spec], out_shape=[out_shape], args=[cond, w_ada, b_ada], name="ada_modulation")


def _ada_job(cond, w_ada, b_ada, layer, grid):
    (n_steps,) = grid
    in_specs, out_spec, out_shape = _ada_specs(layer, n_steps)
    return _SideJob([cond, w_ada, b_ada], in_specs, out_spec, out_shape, functools.partial(_ada_kernel, layer))


def _group_rms(x, gain):
    low = _low_half_mask((1, LANES))
    cols = []
    for c in range(N_LANE_BLOCKS):
        xb = x[:, c * LANES:(c + 1) * LANES]
        sq = xb * xb
        s_lo = jnp.sum(jnp.where(low, sq, 0.0), axis=-1, keepdims=True)
        s_hi = jnp.sum(jnp.where(low, 0.0, sq), axis=-1, keepdims=True)
        ms = jnp.where(low, s_lo, s_hi) * (1.0 / HEAD_DIM)
        cols.append(xb * lax.rsqrt(ms + EPS) * gain)
    return cols


def _rope(cols, cos, sin_signed):
    lane = lax.broadcasted_iota(jnp.int32, (1, LANES), 1)
    first = (lane % 32) < 16
    out = []
    for xb in cols:
        partner = jnp.where(first, pltpu.roll(xb, LANES - 16, 1), pltpu.roll(xb, 16, 1))
        out.append(xb * cos + partner * sin_signed)
    return out


def _qkv_matmul(layer, x_ref, mod_ref, g_ref, w_ref, r, rows=slice(None)):
    x = x_ref[rows, :]
    ms = jnp.mean(x * x, axis=-1, keepdims=True)
    xn = x * lax.rsqrt(ms + EPS) * g_ref[layer:layer + 1, :]
    shift = mod_ref[pl.ds(r, 1), 0:D_MODEL]
    scale = mod_ref[pl.ds(r, 1), D_MODEL:2 * D_MODEL]
    h = (xn * (1.0 + scale) + shift).astype(BF16)
    return _dot(h, w_ref[...])


def _qk_norm(qkv, qg_ref, kg_ref):
    qn = _group_rms(qkv[:, 0:D_MODEL], qg_ref[...])
    kn = _group_rms(qkv[:, D_MODEL:2 * D_MODEL], kg_ref[...])
    return qn, kn, qkv[:, 2 * D_MODEL:3 * D_MODEL]


def _lane_block(c):
    return slice(c * LANES, (c + 1) * LANES)


LATENT_ROW_BANDS = 2


def _qkv_latent_kernel(is_diff, layer, x_ref, mod_ref, g_ref, w_ref, qg_ref, kg_ref, cos_ref, sin_ref,
                       q_ref, k_ref, v_ref):
    r = 1 + pl.program_id(0) // TILES_PER_SAMPLE
    band = ROW_TILE // LATENT_ROW_BANDS
    bands = [slice(a * band, (a + 1) * band) for a in range(LATENT_ROW_BANDS)]
    raw = [_qkv_matmul(layer, x_ref, mod_ref, g_ref, w_ref, r, rows) for rows in bands]
    for rows, qkv in zip(bands, raw):
        qn, kn, v = _qk_norm(qkv, qg_ref, kg_ref)
        v_ref[rows, :] = v.astype(BF16)
        if is_diff:
            cos = cos_ref[rows, :]
            sin = sin_ref[rows, :]
            qn = _rope(qn, cos, sin)
            kn = _rope(kn, cos, sin)
        for c in range(N_LANE_BLOCKS):
            q_ref[rows, _lane_block(c)] = (qn[c] * SCORE_SCALE).astype(BF16)
            k_ref[rows, _lane_block(c)] = kn[c].astype(BF16)


def _ctx_layer_kernel(is_diff, layer, lambda_init, has_prev, *refs):
    if has_prev:
        refs = refs[2:]
    (x_ref, mod_ref, g_ref, w_ref, qg_ref, kg_ref, lq1_ref, lk1_ref, lq2_ref, lk2_ref, gsub_ref,
     o_ref, kf_ref, vf_ref) = refs
    qn, kn, v = _qk_norm(_qkv_matmul(layer, x_ref, mod_ref, g_ref, w_ref, 0), qg_ref, kg_ref)
    for c in range(N_LANE_BLOCKS):
        for bb in range(REQ_PER_TILE):
            rows = slice(bb * SEQ, (bb + 1) * SEQ)
            if is_diff:
                dst = pl.ds(c, SEQ, stride=N_LANE_BLOCKS)
                kf_ref[bb, dst, :] = kn[c][rows, :]
                vf_ref[bb, dst, :] = v[rows, _lane_block(c)]
            else:
                kf_ref[bb, _lane_block(c), :] = kn[c][rows, :].T
                vf_ref[bb, _lane_block(c), :] = v[rows, _lane_block(c)].T
    q = [(qn[c] * SCORE_SCALE).astype(BF16) for c in range(N_LANE_BLOCKS)]
    k = [kn[c].astype(BF16) for c in range(N_LANE_BLOCKS)]
    vb = [v[:, _lane_block(c)].astype(BF16) for c in range(N_LANE_BLOCKS)]
    if is_diff:
        lam = _diff_lambda(lq1_ref, lk1_ref, lq2_ref, lk2_ref, lambda_init)
        combine = _diff_combine(lam, gsub_ref[...], lambda_init)
    else:
        combine = _head_pair_combine

    def block(b):
        bb, c = divmod(b, N_LANE_BLOCKS)
        return slice(bb * SEQ, (bb + 1) * SEQ), c

    def load_q(b):
        rows, c = block(b)
        return q[c][rows, :]

    def scores(b, q2):
        rows, c = block(b)
        return [_dot_nt(q2, k[c][rows, :])]

    def values(b):
        rows, c = block(b)
        return [_times(vb[c][rows, :])]

    def store(b, o):
        rows, c = block(b)
        o_ref[rows, _lane_block(c)] = o.astype(BF16)

    _attend_pairs(REQ_PER_TILE * N_LANE_BLOCKS, load_q, scores, values, combine, store)


def _qkv_latent(x, first_tile, mod, layer, g_mix, w_qkv, q_gain, k_gain, rope_tabs, is_diff, side_jobs):
    row_spec = pl.BlockSpec((ROW_TILE, D_MODEL), lambda i: (i, 0))
    rope_spec = pl.BlockSpec((ROW_TILE, LANES), lambda i: (i % TILES_PER_SAMPLE, 0))
    return _pallas_call_with_side_jobs(
        functools.partial(_qkv_latent_kernel, is_diff, layer), side_jobs,
        grid=(N_LATENT_TILES,),
        in_specs=[pl.BlockSpec((ROW_TILE, D_MODEL), lambda i: (first_tile + i, 0)),
                  _resident((N_COND, 6 * D_MODEL)),
                  _resident((DEPTH, D_MODEL)),
                  _resident((D_MODEL, 3 * D_MODEL)),
                  _resident((1, LANES)), _resident((1, LANES)), rope_spec, rope_spec],
        out_specs=[row_spec] * 3,
        out_shape=[jax.ShapeDtypeStruct((N_SAMPLE_ROWS, D_MODEL), BF16)] * 3,
        args=[x, mod, g_mix, w_qkv, q_gain, k_gain, *rope_tabs],
        name="qkv_latent",
    )


def _ctx_layer(x, mod, layer, g_mix, w_qkv, kind_layer, q_gain, k_gain, lam_params, gsub, is_diff,
               lambda_init, prev, side_jobs):
    row_spec = pl.BlockSpec((ROW_TILE, D_MODEL), lambda i: (i, 0))
    if is_diff:
        cache_rows, cache_cols = SEQ * N_LANE_BLOCKS, LANES
    else:
        cache_rows, cache_cols = D_MODEL, SEQ
    cache_spec = pl.BlockSpec((REQ_PER_TILE, None, cache_rows, cache_cols), lambda i: (i, kind_layer, 0, 0))
    cache_shape = jax.ShapeDtypeStruct((BATCH, DEPTH // 2, cache_rows, cache_cols), F32)
    in_specs = ([row_spec, _resident((N_COND, 6 * D_MODEL)), _resident((DEPTH, D_MODEL)),
                 _resident((D_MODEL, 3 * D_MODEL)), _resident((1, LANES)), _resident((1, LANES))]
                + [_resident((1, HEAD_DIM))] * 4 + [_resident((1, LANES))])
    args = [x, mod, g_mix, w_qkv, q_gain, k_gain, *lam_params, gsub]
    aliases = {}
    if prev is not None:
        in_specs = [pl.BlockSpec(memory_space=pl.ANY)] * 2 + in_specs
        args = list(prev) + args
        aliases = {0: 1, 1: 2}
    return _pallas_call_with_side_jobs(
        functools.partial(_ctx_layer_kernel, is_diff, layer, lambda_init, prev is not None), side_jobs,
        grid=(N_PROMPT_TILES,),
        in_specs=in_specs,
        out_specs=[row_spec, cache_spec, cache_spec],
        out_shape=[jax.ShapeDtypeStruct((N_PROMPT_ROWS, D_MODEL), BF16), cache_shape, cache_shape],
        args=args,
        input_output_aliases=aliases,
        name="ctx_layer",
    )


def _softmax_pv(score_parts, value_parts):
    m = None
    for s in score_parts:
        pm = jnp.max(s, axis=-1, keepdims=True)
        m = pm if m is None else jnp.maximum(m, pm)
    acc = None
    for s, times_v in zip(score_parts, value_parts):
        pv = times_v(jnp.exp2(s - m).astype(BF16))
        acc = pv if acc is None else acc + pv
    return acc[:, :LANES] * (1.0 / acc[:, LANES:])


def _times(v):
    v1 = jnp.concatenate([v, jnp.ones_like(v)], axis=1)
    return lambda w: _dot(w, v1)


def _times_t(v_t):
    v1_t = jnp.concatenate([v_t, jnp.ones_like(v_t)], axis=0)
    return lambda w: _dot_nt(w, v1_t)


def _diff_lambda(lq1_ref, lk1_ref, lq2_ref, lk2_ref, lambda_init):
    e1 = jnp.exp(jnp.sum(lq1_ref[...] * lk1_ref[...], axis=-1, keepdims=True))
    e2 = jnp.exp(jnp.sum(lq2_ref[...] * lk2_ref[...], axis=-1, keepdims=True))
    return e1 - e2 + lambda_init


def _attend_pairs(n_blocks, load_q, scores, values, combine, store):
    def block_scores(b):
        q = load_q(b)
        low = _low_half_mask(q.shape)
        zero = jnp.zeros_like(q)
        return scores(b, jnp.concatenate([jnp.where(low, q, zero), jnp.where(low, zero, q)], axis=0))

    s = block_scores(0)
    for b in range(n_blocks):
        s_next = block_scores(b + 1) if b + 1 < n_blocks else None
        o = _softmax_pv(s, values(b))
        s = s_next
        rows = o.shape[0] // 2
        store(b, combine(o[:rows], o[rows:]))


def _diff_combine(lam, gsub, lambda_init):
    def combine(o1, o2):
        o = o1 - lam * o2
        ms = jnp.mean(o * o, axis=-1, keepdims=True)
        return o * lax.rsqrt(ms + EPS) * gsub * (1.0 - lambda_init)
    return combine


def _head_pair_combine(o_lo, o_hi):
    return jnp.where(_low_half_mask(o_lo.shape), o_lo, o_hi)


Q_CHUNK = 512
LATENT_HEADS_PER_STEP = 2


def _diff_latent_kernel(lambda_init, q_ref, k_ref, v_ref, ck_ref, cv_ref,
                        lq1_ref, lk1_ref, lq2_ref, lk2_ref, gsub_ref, o_ref):
    lam = _diff_lambda(lq1_ref, lk1_ref, lq2_ref, lk2_ref, lambda_init)
    n_chunks = DEC_SEQ // Q_CHUNK
    keys, values = [], []
    for hh in range(LATENT_HEADS_PER_STEP):
        cols = slice(hh * LANES, (hh + 1) * LANES)
        head = pl.program_id(1) * LATENT_HEADS_PER_STEP + hh
        head_rows = pl.ds(head, PAST_LEN, stride=N_LANE_BLOCKS)
        keys.append([k_ref[:, cols], ck_ref[head_rows, :].astype(BF16)])
        values.append([_times(v_ref[:, cols]), _times(cv_ref[head_rows, :].astype(BF16))])

    def block(b):
        hh, c = divmod(b, n_chunks)
        return slice(c * Q_CHUNK, (c + 1) * Q_CHUNK), slice(hh * LANES, (hh + 1) * LANES)

    def store(b, o):
        o_ref[block(b)] = o.astype(BF16)

    _attend_pairs(LATENT_HEADS_PER_STEP * n_chunks,
                  lambda b: q_ref[block(b)],
                  lambda b, q2: [_dot_nt(q2, kpart) for kpart in keys[b // n_chunks]],
                  lambda b: values[b // n_chunks],
                  _diff_combine(lam, gsub_ref[...], lambda_init), store)


LATENT_ATTENTION_GRID = (DEC_BATCH, N_LANE_BLOCKS // LATENT_HEADS_PER_STEP)


def _diff_latent_attention(q, k, v, cache_k, cache_v, layer_i, lam_params, gsub, lambda_init, side_jobs):
    blk = pl.BlockSpec((DEC_SEQ, LATENT_HEADS_PER_STEP * LANES), lambda b, h: (b, h))
    cache = pl.BlockSpec((None, None, PAST_LEN * N_LANE_BLOCKS, LANES), lambda b, h: (b, layer_i, 0, 0))
    small = [_resident((1, HEAD_DIM))] * 4 + [_resident((1, LANES))]
    return _pallas_call_with_side_jobs(
        functools.partial(_diff_latent_kernel, lambda_init), side_jobs,
        grid=LATENT_ATTENTION_GRID,
        in_specs=[blk, blk, blk, cache, cache] + small,
        out_specs=[blk],
        out_shape=[jax.ShapeDtypeStruct((N_SAMPLE_ROWS, D_MODEL), BF16)],
        args=[q, k, v, cache_k, cache_v, *lam_params, gsub],
        name="diff_latent_attention",
    )


class _NaGroup(NamedTuple):
    q_row: int
    n_q: int
    k_row: int
    n_k: int


def _window_start(r):
    return min(max(r - NA_KH // 2, 0), GRID_ROWS - NA_KH)


def _na_groups(rows_per_group):
    groups = []
    for q_row in range(0, GRID_ROWS, rows_per_group):
        first = _window_start(q_row)
        last = _window_start(q_row + rows_per_group - 1) + NA_KH
        k_row = first - first % 2
        n_k = last - k_row
        n_k += n_k % 2
        groups.append(_NaGroup(q_row, rows_per_group, min(k_row, GRID_ROWS - n_k), n_k))
    return groups


NA_GROUPS = _na_groups(2)
N_DR = 2 * NA_KH - 1
N_DC = 2 * NA_KW - 1
N_DR_PAD = 16


def _expand_na_bias(rb_ref, tile_ref, bias_refs):
    qc = lax.broadcasted_iota(jnp.int32, (GRID_W, LANES), 0)
    lane = lax.broadcasted_iota(jnp.int32, (GRID_W, LANES), 1)
    kc = lane % GRID_W
    cs = jnp.clip(qc - NA_KW // 2, 0, GRID_W - NA_KW)
    col_ok = (kc >= cs) & (kc < cs + NA_KW)
    low = lane < GRID_W
    neg = jnp.full((GRID_W, LANES), NEG_INF, F32)
    for head in range(2 * LATENT_HEADS_PER_STEP):
        for dr in range(N_DR):
            row = jnp.broadcast_to(rb_ref[head, dr:dr + 1, :], (GRID_W, LANES))
            shifted = pltpu.roll(row, LANES - (NA_KW - 1), 1, stride=1, stride_axis=0)
            tile_ref[dr] = jnp.where(col_ok, shifted * LOG2_E, neg)
        for grp, bias_ref in zip(NA_GROUPS, bias_refs):
            for rq in range(grp.n_q):
                r = grp.q_row + rq
                rs = _window_start(r)

                def tile(kr, r=r, rs=rs):
                    if rs <= kr < rs + NA_KH:
                        return tile_ref[kr - r + NA_KH - 1]
                    return neg

                for p in range(grp.n_k // 2):
                    kr = grp.k_row + 2 * p
                    bias_ref[head, rq * GRID_W:(rq + 1) * GRID_W, p * LANES:(p + 1) * LANES] = (
                        jnp.where(low, tile(kr), tile(kr + 1)))


def _na_latent_kernel(q_ref, k_ref, v_ref, ckt_ref, cvt_ref, rb_ref, o_ref, tile_ref, *bias_refs):
    @pl.when(pl.program_id(1) == 0)
    def _():
        _expand_na_bias(rb_ref, tile_ref, bias_refs)

    ck_t = [ckt_ref[hh * LANES:(hh + 1) * LANES, :].astype(BF16) for hh in range(LATENT_HEADS_PER_STEP)]
    cv_t = [cvt_ref[hh * LANES:(hh + 1) * LANES, :].astype(BF16) for hh in range(LATENT_HEADS_PER_STEP)]

    def block(b):
        return divmod(b, len(NA_GROUPS))

    def cols(hh):
        return slice(hh * LANES, (hh + 1) * LANES)

    def qrows(g):
        return slice(NA_GROUPS[g].q_row * GRID_W, (NA_GROUPS[g].q_row + NA_GROUPS[g].n_q) * GRID_W)

    def krows(g):
        return slice(NA_GROUPS[g].k_row * GRID_W, (NA_GROUPS[g].k_row + NA_GROUPS[g].n_k) * GRID_W)

    def load_q(b):
        hh, g = block(b)
        return q_ref[qrows(g), cols(hh)]

    def scores(b, q2):
        hh, g = block(b)
        bias = jnp.concatenate([bias_refs[g][2 * hh], bias_refs[g][2 * hh + 1]], axis=0)
        return [_dot_nt(q2, k_ref[krows(g), cols(hh)]) + bias, _dot(q2, ck_t[hh])]

    def values(b):
        hh, g = block(b)
        return [_times(v_ref[krows(g), cols(hh)]), _times_t(cv_t[hh])]

    def store(b, o):
        hh, g = block(b)
        o_ref[qrows(g), cols(hh)] = o.astype(BF16)

    _attend_pairs(LATENT_HEADS_PER_STEP * len(NA_GROUPS), load_q, scores, values, _head_pair_combine, store)


def _na_latent_attention(q, k, v, cache_k_t, cache_v_t, layer_i, rel_bias_rows, side_jobs):
    width = LATENT_HEADS_PER_STEP * LANES
    blk = pl.BlockSpec((DEC_SEQ, width), lambda p, b: (b, p))
    cache = pl.BlockSpec((None, None, width, PAST_LEN), lambda p, b: (b, layer_i, p, 0))
    rb_spec = pl.BlockSpec((None, 2 * LATENT_HEADS_PER_STEP, N_DR_PAD, LANES), lambda p, b: (layer_i, p, 0, 0))
    return _pallas_call_with_side_jobs(
        _na_latent_kernel, side_jobs,
        grid=LATENT_ATTENTION_GRID[::-1],
        in_specs=[blk, blk, blk, cache, cache, rb_spec],
        out_specs=[blk],
        out_shape=[jax.ShapeDtypeStruct((N_SAMPLE_ROWS, D_MODEL), BF16)],
        args=[q, k, v, cache_k_t, cache_v_t, rel_bias_rows],
        scratch_shapes=[pltpu.VMEM((N_DR, GRID_W, LANES), F32)] + [
            pltpu.VMEM((2 * LATENT_HEADS_PER_STEP, grp.n_q * GRID_W, grp.n_k * GRID_W), F32)
            for grp in NA_GROUPS],
        name="na_latent_attention",
    )


def _padded_rel_bias(rel_bias_na):
    half = jnp.pad(rel_bias_na, ((0, 0), (0, 0), (0, N_DR_PAD - N_DR), (0, GRID_W - N_DC)))
    return jnp.concatenate([half, half], axis=-1)


FF_CHUNK = 1024


def _post_kernel(split_in, split_out, layer, *refs):
    n_x = 2 if split_in else 1
    x_refs = refs[:n_x]
    op_ref, os_ref, mod_ref, g_ref, wo_ref, w1_ref, w2_ref = refs[n_x:n_x + 7]
    y_refs = refs[n_x + 7:]
    i = pl.program_id(0)
    r = _cond_row(i)
    is_prompt = i < N_PROMPT_TILES

    def mod(j):
        return mod_ref[pl.ds(r, 1), j * D_MODEL:(j + 1) * D_MODEL]

    x = jnp.where(is_prompt, x_refs[0][...], x_refs[1][...]) if split_in else x_refs[0][...]
    o = jnp.where(is_prompt, op_ref[...], os_ref[...])
    y1 = x + mod(2) * _dot(o, wo_ref[...])
    ms = jnp.mean(y1 * y1, axis=-1, keepdims=True)
    h = (y1 * lax.rsqrt(ms + EPS) * g_ref[layer:layer + 1, :] * (1.0 + mod(4)) + mod(3)).astype(BF16)
    acc = jnp.zeros((ROW_TILE, D_MODEL), F32)
    for c in range(D_FF // FF_CHUNK):
        cols = slice(c * FF_CHUNK, (c + 1) * FF_CHUNK)
        hid = jnp.maximum(_dot(h, w1_ref[:, cols]), 0.0)
        acc = acc + _dot((hid * hid).astype(BF16), w2_ref[cols, :])
    y = y1 + mod(5) * acc
    if split_out:
        @pl.when(is_prompt)
        def _():
            y_refs[0][...] = y

        @pl.when(jnp.logical_not(is_prompt))
        def _():
            y_refs[1][...] = y
    else:
        y_refs[0][...] = y


def _post(xs, o_p, o_s, mod, layer, g_mlp, w_o, w_fc1, w_fc2, split_out, side_jobs):
    prompt_spec = pl.BlockSpec((ROW_TILE, D_MODEL), lambda i: (_prompt_tile(i), 0))
    sample_spec = pl.BlockSpec((ROW_TILE, D_MODEL), lambda i: (_sample_tile(i), 0))
    row_spec = pl.BlockSpec((ROW_TILE, D_MODEL), lambda i: (i, 0))
    split_in = len(xs) == 2
    if split_out:
        out_specs = [prompt_spec, sample_spec]
        out_shape = [jax.ShapeDtypeStruct((N_PROMPT_ROWS, D_MODEL), F32),
                     jax.ShapeDtypeStruct((N_SAMPLE_ROWS, D_MODEL), F32)]
    else:
        out_specs = [row_spec]
        out_shape = [jax.ShapeDtypeStruct((N_ROWS, D_MODEL), F32)]
    return _pallas_call_with_side_jobs(
        functools.partial(_post_kernel, split_in, split_out, layer), side_jobs,
        grid=(N_ROW_TILES,),
        in_specs=([prompt_spec, sample_spec] if split_in else [row_spec]) + [
            prompt_spec, sample_spec,
            _resident((N_COND, 6 * D_MODEL)),
            _resident((DEPTH, D_MODEL)),
            _resident((D_MODEL, D_MODEL)),
            _resident((D_MODEL, D_FF)),
            _resident((D_FF, D_MODEL)),
        ],
        out_specs=out_specs,
        out_shape=out_shape,
        args=[*xs, o_p, o_s, mod, g_mlp, w_o, w_fc1, w_fc2],
        name="wo_mlp",
    )


def _rope_tables():
    n_freq = HEAD_DIM // 4
    freqs = ROPE_BASE ** (-np.arange(n_freq, dtype=np.float64) / n_freq)
    t = np.arange(DEC_SEQ)
    ang_row = (t // GRID_W)[:, None] * freqs
    ang_col = (t % GRID_W)[:, None] * freqs
    cos64 = np.concatenate([np.cos(ang_row)] * 2 + [np.cos(ang_col)] * 2, axis=-1)
    sin64 = np.concatenate([-np.sin(ang_row), np.sin(ang_row), -np.sin(ang_col), np.sin(ang_col)], axis=-1)
    return (jnp.asarray(np.tile(cos64, (1, 2)), dtype=F32), jnp.asarray(np.tile(sin64, (1, 2)), dtype=F32))


def _pair_gain(g):
    return jnp.tile(g.reshape(1, HEAD_DIM), (1, LANES // HEAD_DIM))


def _head_major_cache(cache):
    n_req, n_layers = cache.shape[:2]
    return jnp.transpose(cache, (0, 1, 3, 4, 2)).reshape(n_req, n_layers, D_MODEL, PAST_LEN)


def kernel(x_prompt, x_sample, cache_diff_k, cache_diff_v, cache_na_k, cache_na_v, c, c_ctx, w_ada, b_ada, norm_mix_g, norm_mlp_g, w_fc1, w_fc2, w_qkv_diff, w_o_diff, q_norm_diff_g, k_norm_diff_g, lambda_q1, lambda_k1, lambda_q2, lambda_k2, subln_g, w_qkv_na, w_o_na, q_norm_na_g, k_norm_na_g, rel_bias_na):
    cond = jnp.concatenate([c_ctx[None, :], c, jnp.zeros((N_COND - 1 - DEC_BATCH, D_MODEL), F32)], axis=0)
    mod, w_qkv = _modulation(cond, w_ada, b_ada, 0, [_cast_job(w_qkv_diff, 0, 0, ADA_GRID)])
    cos_tab, sin_tab = _rope_tables()
    na_bias = _padded_rel_bias(rel_bias_na)
    cdk = cache_diff_k.reshape(DEC_BATCH, DEPTH // 2, PAST_LEN * N_LANE_BLOCKS, LANES)
    cdv = cache_diff_v.reshape(DEC_BATCH, DEPTH // 2, PAST_LEN * N_LANE_BLOCKS, LANES)
    cnk_t = _head_major_cache(cache_na_k)
    cnv_t = _head_major_cache(cache_na_v)
    g_mix, g_mlp = norm_mix_g, norm_mlp_g
    w_qkv_f32 = (w_qkv_diff, w_qkv_na)
    w_o_f32 = (w_o_diff, w_o_na)
    q_gain = (q_norm_diff_g, q_norm_na_g)
    k_gain = (k_norm_diff_g, k_norm_na_g)

    xs = [x_prompt.reshape(N_PROMPT_ROWS, D_MODEL), x_sample.reshape(N_SAMPLE_ROWS, D_MODEL)]
    new_kv = [None, None]
    w_o = None
    for l in range(DEPTH):
        i = l // 2
        kind = l % 2
        is_diff = kind == 0
        first = l == 0
        last = l == DEPTH - 1
        x_prompt_src, x_sample_src = (xs[0], xs[1]) if len(xs) == 2 else (xs[0], xs[0])
        sample_first_tile = 0 if len(xs) == 2 else N_PROMPT_TILES
        gains = (_pair_gain(q_gain[kind][i]), _pair_gain(k_gain[kind][i]))
        lambda_init = 0.8 - 0.6 * math.exp(-0.3 * l)
        lam_params = [p[i if is_diff else 0].reshape(1, HEAD_DIM)
                      for p in (lambda_q1, lambda_k1, lambda_q2, lambda_k2)]
        gsub = subln_g[i if is_diff else 0].reshape(1, LANES)
        o_p, new_k, new_v, *cast = _ctx_layer(
            x_prompt_src, mod, l, g_mix, w_qkv, i, *gains, lam_params, gsub, is_diff, lambda_init,
            new_kv[kind], [_cast_job(w_o_f32[kind], i, 0, (N_PROMPT_TILES,))] if first else [])
        w_o = cast[0] if first else w_o
        new_kv[kind] = (new_k, new_v)
        qs, ks, vs = _qkv_latent(
            x_sample_src, sample_first_tile, mod, l, g_mix, w_qkv, *gains, (cos_tab, sin_tab), is_diff, [])
        if is_diff:
            mlp_jobs = [_cast_job(w_fc1, l, 1, LATENT_ATTENTION_GRID), _cast_job(w_fc2, l, 0, LATENT_ATTENTION_GRID)]
            o_s, w1, w2 = _diff_latent_attention(qs, ks, vs, cdk, cdv, i, lam_params, gsub, lambda_init,
                                                 mlp_jobs)
        else:
            na_grid = LATENT_ATTENTION_GRID[::-1]
            mlp_jobs = [_cast_job(w_fc1, l, 1, na_grid), _cast_job(w_fc2, l, 0, na_grid)]
            o_s, w1, w2 = _na_latent_attention(qs, ks, vs, cnk_t, cnv_t, i, na_bias, mlp_jobs)
        nxt, nxt_kind, nxt_i = l + 1, (l + 1) % 2, (l + 1) // 2
        post_grid = (N_ROW_TILES,)
        next_jobs = [] if last else [_cast_job(w_qkv_f32[nxt_kind], nxt_i, 0, post_grid),
                                     _cast_job(w_o_f32[nxt_kind], nxt_i, 0, post_grid),
                                     _ada_job(cond, w_ada, b_ada, nxt, post_grid)]
        outs = list(_post(xs, o_p, o_s, mod, l, g_mlp, w_o, w1, w2, last, next_jobs))
        if last:
            xp, xs = outs
        else:
            *xs, w_qkv, w_o, mod = outs

    n_kind = DEPTH // 2
    diff_k, diff_v = (a.reshape(BATCH, n_kind, SEQ, N_LANE_BLOCKS, LANES) for a in new_kv[0])
    na_k, na_v = (jnp.transpose(a.reshape(BATCH, n_kind, NA_HEADS, HEAD_DIM, SEQ), (0, 1, 4, 2, 3))
                  for a in new_kv[1])
    return (xp.reshape(BATCH, SEQ, D_MODEL), xs.reshape(DEC_BATCH, DEC_SEQ, D_MODEL),
            diff_k, diff_v, na_k, na_v)
```

```python
import functools
import math
from typing import Callable, NamedTuple

import jax
import jax.numpy as jnp
import numpy as np
from jax import lax
from jax.experimental import pallas as pl
from jax.experimental.pallas import tpu as pltpu

D_MODEL = 1024
BATCH = 16
SEQ = 256
DEPTH = 4
DEC_BATCH = 4
DEC_SEQ = 1024
PAST_LEN = 256
GRID_W = 64
GRID_ROWS = DEC_SEQ // GRID_W
HEAD_DIM = 64
D_FF = 4 * D_MODEL
NA_HEADS = 16
NA_KH = 8
NA_KW = 16
ROPE_BASE = 10000.0
EPS = 1e-6
NEG_INF = -1e30
LOG2_E = math.log2(math.e)
SCORE_SCALE = HEAD_DIM ** -0.5 * LOG2_E

LANES = 128
N_LANE_BLOCKS = D_MODEL // LANES
N_PROMPT_ROWS = BATCH * SEQ
N_SAMPLE_ROWS = DEC_BATCH * DEC_SEQ
N_ROWS = N_PROMPT_ROWS + N_SAMPLE_ROWS
ROW_TILE = 512
N_ROW_TILES = N_ROWS // ROW_TILE
N_PROMPT_TILES = N_PROMPT_ROWS // ROW_TILE
N_LATENT_TILES = N_SAMPLE_ROWS // ROW_TILE
TILES_PER_SAMPLE = DEC_SEQ // ROW_TILE
REQ_PER_TILE = ROW_TILE // SEQ
N_COND = 8
V7X_VMEM_BYTES = 64 * 1024 * 1024
VMEM_LIMIT = V7X_VMEM_BYTES // 8 * 7

F32 = jnp.float32
BF16 = jnp.bfloat16
NT_DIMS = (((1,), (1,)), ((), ()))


def _dot(a, b):
    return jnp.dot(a, b, preferred_element_type=F32)


def _dot_nt(a, b):
    return lax.dot_general(a, b, NT_DIMS, preferred_element_type=F32)


def _params(n_axes):
    return pltpu.CompilerParams(dimension_semantics=("arbitrary",) * n_axes,
                                vmem_limit_bytes=VMEM_LIMIT)


def _resident(shape):
    return pl.BlockSpec(shape, lambda *_: (0,) * len(shape), pipeline_mode=pl.Buffered(1))


class _SideJob(NamedTuple):
    args: list
    in_specs: list
    out_spec: pl.BlockSpec
    out_shape: jax.ShapeDtypeStruct
    body: Callable


def _cast_body(src_ref, dst_ref):
    dst_ref[...] = src_ref[...].astype(BF16)


def _step_number(grid):
    def number(*idx):
        step = idx[0]
        for size, i in zip(grid[1:], idx[1:]):
            step = step * size + i
        return step
    return number


def _cast_job(weights, layer, split_axis, grid):
    _, rows, cols = weights.shape
    n_steps, step = math.prod(grid), _step_number(grid)
    if split_axis == 0:
        band = (rows // n_steps, cols)
        src_spec = pl.BlockSpec((None,) + band, lambda *idx: (layer, step(*idx), 0))
        dst_spec = pl.BlockSpec(band, lambda *idx: (step(*idx), 0))
    else:
        band = (rows, cols // n_steps)
        src_spec = pl.BlockSpec((None,) + band, lambda *idx: (layer, 0, step(*idx)))
        dst_spec = pl.BlockSpec(band, lambda *idx: (0, step(*idx)))
    return _SideJob([weights], [src_spec], dst_spec, jax.ShapeDtypeStruct((rows, cols), BF16), _cast_body)


def _pallas_call_with_side_jobs(kernel_fn, jobs, *, grid, in_specs, out_specs, out_shape, args, name,
                                input_output_aliases=None, scratch_shapes=()):
    n_in, n_out = len(in_specs), len(out_specs)
    n_job_in = [len(job.args) for job in jobs]

    def kernel_and_jobs(*refs):
        ins, refs = refs[:n_in], refs[n_in:]
        job_ins, refs = refs[:sum(n_job_in)], refs[sum(n_job_in):]
        outs, refs = refs[:n_out], refs[n_out:]
        job_outs, scratch = refs[:len(jobs)], refs[len(jobs):]
        for job, n, out_ref in zip(jobs, n_job_in, job_outs):
            job.body(*job_ins[:n], out_ref)
            job_ins = job_ins[n:]
        kernel_fn(*ins, *outs, *scratch)

    return pl.pallas_call(
        kernel_and_jobs,
        grid=grid,
        in_specs=list(in_specs) + [spec for job in jobs for spec in job.in_specs],
        out_specs=list(out_specs) + [job.out_spec for job in jobs],
        out_shape=list(out_shape) + [job.out_shape for job in jobs],
        input_output_aliases=input_output_aliases or {},
        scratch_shapes=list(scratch_shapes),
        compiler_params=_params(len(grid)),
        name=name,
    )(*args, *(arg for job in jobs for arg in job.args))


def _prompt_tile(i):
    return jnp.minimum(i, N_PROMPT_TILES - 1)


def _sample_tile(i):
    return jnp.maximum(i - N_PROMPT_TILES, 0)


def _cond_row(i):
    return jnp.where(i < N_PROMPT_TILES, 0, 1 + (i - N_PROMPT_TILES) // TILES_PER_SAMPLE)


def _low_half_mask(shape):
    return lax.broadcasted_iota(jnp.int32, shape, len(shape) - 1) < HEAD_DIM


def _ada_kernel(layer, cond_ref, w_ref, b_ref, out_ref):
    cond = cond_ref[...]
    act = (cond * jax.nn.sigmoid(cond)).astype(BF16)
    out_ref[...] = _dot(act, w_ref[...].astype(BF16)) + b_ref[layer:layer + 1, :]


def _ada_specs(layer, n_steps):
    tn = 6 * D_MODEL // n_steps
    in_specs = [pl.BlockSpec((N_COND, D_MODEL), lambda j: (0, 0)),
                pl.BlockSpec((None, D_MODEL, tn), lambda j: (layer, 0, j)),
                pl.BlockSpec((DEPTH, tn), lambda j: (0, j))]
    return in_specs, pl.BlockSpec((N_COND, tn), lambda j: (0, j)), jax.ShapeDtypeStruct((N_COND, 6 * D_MODEL), F32)


ADA_GRID = (4,)


def _modulation(cond, w_ada, b_ada, layer, side_jobs):
    in_specs, out_spec, out_shape = _ada_specs(layer, *ADA_GRID)
    return _pallas_call_with_side_jobs(
        functools.partial(_ada_kernel, layer), side_jobs, grid=ADA_GRID, in_specs=in_specs,
        out_specs=[out_spec], out_shape=[out_shape], args=[cond, w_ada, b_ada], name="ada_modulation")


def _ada_job(cond, w_ada, b_ada, layer, grid):
    (n_steps,) = grid
    in_specs, out_spec, out_shape = _ada_specs(layer, n_steps)
    return _SideJob([cond, w_ada, b_ada], in_specs, out_spec, out_shape, functools.partial(_ada_kernel, layer))


def _group_rms(x, gain):
    low = _low_half_mask((1, LANES))
    cols = []
    for c in range(N_LANE_BLOCKS):
        xb = x[:, c * LANES:(c + 1) * LANES]
        sq = xb * xb
        s_lo = jnp.sum(jnp.where(low, sq, 0.0), axis=-1, keepdims=True)
        s_hi = jnp.sum(jnp.where(low, 0.0, sq), axis=-1, keepdims=True)
        ms = jnp.where(low, s_lo, s_hi) * (1.0 / HEAD_DIM)
        cols.append(xb * lax.rsqrt(ms + EPS) * gain)
    return cols


def _rope(cols, cos, sin_signed):
    lane = lax.broadcasted_iota(jnp.int32, (1, LANES), 1)
    first = (lane % 32) < 16
    out = []
    for xb in cols:
        partner = jnp.where(first, pltpu.roll(xb, LANES - 16, 1), pltpu.roll(xb, 16, 1))
        out.append(xb * cos + partner * sin_signed)
    return out


def _qkv_matmul(layer, x_ref, mod_ref, g_ref, w_ref, r, rows=slice(None)):
    x = x_ref[rows, :]
    ms = jnp.mean(x * x, axis=-1, keepdims=True)
    xn = x * lax.rsqrt(ms + EPS) * g_ref[layer:layer + 1, :]
    shift = mod_ref[pl.ds(r, 1), 0:D_MODEL]
    scale = mod_ref[pl.ds(r, 1), D_MODEL:2 * D_MODEL]
    h = (xn * (1.0 + scale) + shift).astype(BF16)
    return _dot(h, w_ref[...])


def _qk_norm(qkv, qg_ref, kg_ref):
    qn = _group_rms(qkv[:, 0:D_MODEL], qg_ref[...])
    kn = _group_rms(qkv[:, D_MODEL:2 * D_MODEL], kg_ref[...])
    return qn, kn, qkv[:, 2 * D_MODEL:3 * D_MODEL]


def _lane_block(c):
    return slice(c * LANES, (c + 1) * LANES)


LATENT_ROW_BANDS = 2


def _qkv_latent_kernel(is_diff, layer, x_ref, mod_ref, g_ref, w_ref, qg_ref, kg_ref, cos_ref, sin_ref,
                       q_ref, k_ref, v_ref):
    r = 1 + pl.program_id(0) // TILES_PER_SAMPLE
    band = ROW_TILE // LATENT_ROW_BANDS
    bands = [slice(a * band, (a + 1) * band) for a in range(LATENT_ROW_BANDS)]
    raw = [_qkv_matmul(layer, x_ref, mod_ref, g_ref, w_ref, r, rows) for rows in bands]
    for rows, qkv in zip(bands, raw):
        qn, kn, v = _qk_norm(qkv, qg_ref, kg_ref)
        v_ref[rows, :] = v.astype(BF16)
        if is_diff:
            cos = cos_ref[rows, :]
            sin = sin_ref[rows, :]
            qn = _rope(qn, cos, sin)
            kn = _rope(kn, cos, sin)
        for c in range(N_LANE_BLOCKS):
            q_ref[rows, _lane_block(c)] = (qn[c] * SCORE_SCALE).astype(BF16)
            k_ref[rows, _lane_block(c)] = kn[c].astype(BF16)


def _ctx_layer_kernel(is_diff, layer, lambda_init, has_prev, *refs):
    if has_prev:
        refs = refs[2:]
    (x_ref, mod_ref, g_ref, w_ref, qg_ref, kg_ref, lq1_ref, lk1_ref, lq2_ref, lk2_ref, gsub_ref,
     o_ref, kf_ref, vf_ref) = refs
    qn, kn, v = _qk_norm(_qkv_matmul(layer, x_ref, mod_ref, g_ref, w_ref, 0), qg_ref, kg_ref)
    for c in range(N_LANE_BLOCKS):
        for bb in range(REQ_PER_TILE):
            rows = slice(bb * SEQ, (bb + 1) * SEQ)
            if is_diff:
                dst = pl.ds(c, SEQ, stride=N_LANE_BLOCKS)
                kf_ref[bb, dst, :] = kn[c][rows, :]
                vf_ref[bb, dst, :] = v[rows, _lane_block(c)]
            else:
                kf_ref[bb, _lane_block(c), :] = kn[c][rows, :].T
                vf_ref[bb, _lane_block(c), :] = v[rows, _lane_block(c)].T
    q = [(qn[c] * SCORE_SCALE).astype(BF16) for c in range(N_LANE_BLOCKS)]
    k = [kn[c].astype(BF16) for c in range(N_LANE_BLOCKS)]
    vb = [v[:, _lane_block(c)].astype(BF16) for c in range(N_LANE_BLOCKS)]
    if is_diff:
        lam = _diff_lambda(lq1_ref, lk1_ref, lq2_ref, lk2_ref, lambda_init)
        combine = _diff_combine(lam, gsub_ref[...], lambda_init)
    else:
        combine = _head_pair_combine

    def block(b):
        bb, c = divmod(b, N_LANE_BLOCKS)
        return slice(bb * SEQ, (bb + 1) * SEQ), c

    def load_q(b):
        rows, c = block(b)
        return q[c][rows, :]

    def scores(b, q2):
        rows, c = block(b)
        return [_dot_nt(q2, k[c][rows, :])]

    def values(b):
        rows, c = block(b)
        return [_times(vb[c][rows, :])]

    def store(b, o):
        rows, c = block(b)
        o_ref[rows, _lane_block(c)] = o.astype(BF16)

    _attend_pairs(REQ_PER_TILE * N_LANE_BLOCKS, load_q, scores, values, combine, store)


def _qkv_latent(x, first_tile, mod, layer, g_mix, w_qkv, q_gain, k_gain, rope_tabs, is_diff, side_jobs):
    row_spec = pl.BlockSpec((ROW_TILE, D_MODEL), lambda i: (i, 0))
    rope_spec = pl.BlockSpec((ROW_TILE, LANES), lambda i: (i % TILES_PER_SAMPLE, 0))
    return _pallas_call_with_side_jobs(
        functools.partial(_qkv_latent_kernel, is_diff, layer), side_jobs,
        grid=(N_LATENT_TILES,),
        in_specs=[pl.BlockSpec((ROW_TILE, D_MODEL), lambda i: (first_tile + i, 0)),
                  _resident((N_COND, 6 * D_MODEL)),
                  _resident((DEPTH, D_MODEL)),
                  _resident((D_MODEL, 3 * D_MODEL)),
                  _resident((1, LANES)), _resident((1, LANES)), rope_spec, rope_spec],
        out_specs=[row_spec] * 3,
        out_shape=[jax.ShapeDtypeStruct((N_SAMPLE_ROWS, D_MODEL), BF16)] * 3,
        args=[x, mod, g_mix, w_qkv, q_gain, k_gain, *rope_tabs],
        name="qkv_latent",
    )


def _ctx_layer(x, mod, layer, g_mix, w_qkv, kind_layer, q_gain, k_gain, lam_params, gsub, is_diff,
               lambda_init, prev, side_jobs):
    row_spec = pl.BlockSpec((ROW_TILE, D_MODEL), lambda i: (i, 0))
    if is_diff:
        cache_rows, cache_cols = SEQ * N_LANE_BLOCKS, LANES
    else:
        cache_rows, cache_cols = D_MODEL, SEQ
    cache_spec = pl.BlockSpec((REQ_PER_TILE, None, cache_rows, cache_cols), lambda i: (i, kind_layer, 0, 0))
    cache_shape = jax.ShapeDtypeStruct((BATCH, DEPTH // 2, cache_rows, cache_cols), F32)
    in_specs = ([row_spec, _resident((N_COND, 6 * D_MODEL)), _resident((DEPTH, D_MODEL)),
                 _resident((D_MODEL, 3 * D_MODEL)), _resident((1, LANES)), _resident((1, LANES))]
                + [_resident((1, HEAD_DIM))] * 4 + [_resident((1, LANES))])
    args = [x, mod, g_mix, w_qkv, q_gain, k_gain, *lam_params, gsub]
    aliases = {}
    if prev is not None:
        in_specs = [pl.BlockSpec(memory_space=pl.ANY)] * 2 + in_specs
        args = list(prev) + args
        aliases = {0: 1, 1: 2}
    return _pallas_call_with_side_jobs(
        functools.partial(_ctx_layer_kernel, is_diff, layer, lambda_init, prev is not None), side_jobs,
        grid=(N_PROMPT_TILES,),
        in_specs=in_specs,
        out_specs=[row_spec, cache_spec, cache_spec],
        out_shape=[jax.ShapeDtypeStruct((N_PROMPT_ROWS, D_MODEL), BF16), cache_shape, cache_shape],
        args=args,
        input_output_aliases=aliases,
        name="ctx_layer",
    )


def _softmax_pv(score_parts, value_parts):
    m = None
    for s in score_parts:
        pm = jnp.max(s, axis=-1, keepdims=True)
        m = pm if m is None else jnp.maximum(m, pm)
    acc = None
    for s, times_v in zip(score_parts, value_parts):
        pv = times_v(jnp.exp2(s - m).astype(BF16))
        acc = pv if acc is None else acc + pv
    return acc[:, :LANES] * (1.0 / acc[:, LANES:])


def _times(v):
    v1 = jnp.concatenate([v, jnp.ones_like(v)], axis=1)
    return lambda w: _dot(w, v1)


def _times_t(v_t):
    v1_t = jnp.concatenate([v_t, jnp.ones_like(v_t)], axis=0)
    return lambda w: _dot_nt(w, v1_t)


def _diff_lambda(lq1_ref, lk1_ref, lq2_ref, lk2_ref, lambda_init):
    e1 = jnp.exp(jnp.sum(lq1_ref[...] * lk1_ref[...], axis=-1, keepdims=True))
    e2 = jnp.exp(jnp.sum(lq2_ref[...] * lk2_ref[...], axis=-1, keepdims=True))
    return e1 - e2 + lambda_init


def _attend_pairs(n_blocks, load_q, scores, values, combine, store):
    def block_scores(b):
        q = load_q(b)
        low = _low_half_mask(q.shape)
        zero = jnp.zeros_like(q)
        return scores(b, jnp.concatenate([jnp.where(low, q, zero), jnp.where(low, zero, q)], axis=0))

    s = block_scores(0)
    for b in range(n_blocks):
        s_next = block_scores(b + 1) if b + 1 < n_blocks else None
        o = _softmax_pv(s, values(b))
        s = s_next
        rows = o.shape[0] // 2
        store(b, combine(o[:rows], o[rows:]))


def _diff_combine(lam, gsub, lambda_init):
    def combine(o1, o2):
        o = o1 - lam * o2
        ms = jnp.mean(o * o, axis=-1, keepdims=True)
        return o * lax.rsqrt(ms + EPS) * gsub * (1.0 - lambda_init)
    return combine


def _head_pair_combine(o_lo, o_hi):
    return jnp.where(_low_half_mask(o_lo.shape), o_lo, o_hi)


Q_CHUNK = 512
LATENT_HEADS_PER_STEP = 2


def _diff_latent_kernel(lambda_init, q_ref, k_ref, v_ref, ck_ref, cv_ref,
                        lq1_ref, lk1_ref, lq2_ref, lk2_ref, gsub_ref, o_ref):
    lam = _diff_lambda(lq1_ref, lk1_ref, lq2_ref, lk2_ref, lambda_init)
    n_chunks = DEC_SEQ // Q_CHUNK
    keys, values = [], []
    for hh in range(LATENT_HEADS_PER_STEP):
        cols = slice(hh * LANES, (hh + 1) * LANES)
        head = pl.program_id(1) * LATENT_HEADS_PER_STEP + hh
        head_rows = pl.ds(head, PAST_LEN, stride=N_LANE_BLOCKS)
        keys.append([k_ref[:, cols], ck_ref[head_rows, :].astype(BF16)])
        values.append([_times(v_ref[:, cols]), _times(cv_ref[head_rows, :].astype(BF16))])

    def block(b):
        hh, c = divmod(b, n_chunks)
        return slice(c * Q_CHUNK, (c + 1) * Q_CHUNK), slice(hh * LANES, (hh + 1) * LANES)

    def store(b, o):
        o_ref[block(b)] = o.astype(BF16)

    _attend_pairs(LATENT_HEADS_PER_STEP * n_chunks,
                  lambda b: q_ref[block(b)],
                  lambda b, q2: [_dot_nt(q2, kpart) for kpart in keys[b // n_chunks]],
                  lambda b: values[b // n_chunks],
                  _diff_combine(lam, gsub_ref[...], lambda_init), store)


LATENT_ATTENTION_GRID = (DEC_BATCH, N_LANE_BLOCKS // LATENT_HEADS_PER_STEP)


def _diff_latent_attention(q, k, v, cache_k, cache_v, layer_i, lam_params, gsub, lambda_init, side_jobs):
    blk = pl.BlockSpec((DEC_SEQ, LATENT_HEADS_PER_STEP * LANES), lambda b, h: (b, h))
    cache = pl.BlockSpec((None, None, PAST_LEN * N_LANE_BLOCKS, LANES), lambda b, h: (b, layer_i, 0, 0))
    small = [_resident((1, HEAD_DIM))] * 4 + [_resident((1, LANES))]
    return _pallas_call_with_side_jobs(
        functools.partial(_diff_latent_kernel, lambda_init), side_jobs,
        grid=LATENT_ATTENTION_GRID,
        in_specs=[blk, blk, blk, cache, cache] + small,
        out_specs=[blk],
        out_shape=[jax.ShapeDtypeStruct((N_SAMPLE_ROWS, D_MODEL), BF16)],
        args=[q, k, v, cache_k, cache_v, *lam_params, gsub],
        name="diff_latent_attention",
    )


class _NaGroup(NamedTuple):
    q_row: int
    n_q: int
    k_row: int
    n_k: int


def _window_start(r):
    return min(max(r - NA_KH // 2, 0), GRID_ROWS - NA_KH)


def _na_groups(rows_per_group):
    groups = []
    for q_row in range(0, GRID_ROWS, rows_per_group):
        first = _window_start(q_row)
        last = _window_start(q_row + rows_per_group - 1) + NA_KH
        k_row = first - first % 2
        n_k = last - k_row
        n_k += n_k % 2
        groups.append(_NaGroup(q_row, rows_per_group, min(k_row, GRID_ROWS - n_k), n_k))
    return groups


NA_GROUPS = _na_groups(4)
N_DR = 2 * NA_KH - 1
N_DC = 2 * NA_KW - 1
N_DR_PAD = 16


def _expand_na_bias(rb_ref, tile_ref, bias_refs):
    qc = lax.broadcasted_iota(jnp.int32, (GRID_W, LANES), 0)
    lane = lax.broadcasted_iota(jnp.int32, (GRID_W, LANES), 1)
    kc = lane % GRID_W
    cs = jnp.clip(qc - NA_KW // 2, 0, GRID_W - NA_KW)
    col_ok = (kc >= cs) & (kc < cs + NA_KW)
    low = lane < GRID_W
    neg = jnp.full((GRID_W, LANES), NEG_INF, F32)
    for head in range(2 * LATENT_HEADS_PER_STEP):
        for dr in range(N_DR):
            row = jnp.broadcast_to(rb_ref[head, dr:dr + 1, :], (GRID_W, LANES))
            shifted = pltpu.roll(row, LANES - (NA_KW - 1), 1, stride=1, stride_axis=0)
            tile_ref[dr] = jnp.where(col_ok, shifted * LOG2_E, neg)
        for grp, bias_ref in zip(NA_GROUPS, bias_refs):
            for rq in range(grp.n_q):
                r = grp.q_row + rq
                rs = _window_start(r)

                def tile(kr, r=r, rs=rs):
                    if rs <= kr < rs + NA_KH:
                        return tile_ref[kr - r + NA_KH - 1]
                    return neg

                for p in range(grp.n_k // 2):
                    kr = grp.k_row + 2 * p
                    bias_ref[head, rq * GRID_W:(rq + 1) * GRID_W, p * LANES:(p + 1) * LANES] = (
                        jnp.where(low, tile(kr), tile(kr + 1)))


def _na_latent_kernel(q_ref, k_ref, v_ref, ckt_ref, cvt_ref, rb_ref, o_ref, tile_ref, *bias_refs):
    @pl.when(pl.program_id(1) == 0)
    def _():
        _expand_na_bias(rb_ref, tile_ref, bias_refs)

    ck_t = [ckt_ref[hh * LANES:(hh + 1) * LANES, :].astype(BF16) for hh in range(LATENT_HEADS_PER_STEP)]
    cv_t = [cvt_ref[hh * LANES:(hh + 1) * LANES, :].astype(BF16) for hh in range(LATENT_HEADS_PER_STEP)]

    def block(b):
        return divmod(b, len(NA_GROUPS))

    def cols(hh):
        return slice(hh * LANES, (hh + 1) * LANES)

    def qrows(g):
        return slice(NA_GROUPS[g].q_row * GRID_W, (NA_GROUPS[g].q_row + NA_GROUPS[g].n_q) * GRID_W)

    def krows(g):
        return slice(NA_GROUPS[g].k_row * GRID_W, (NA_GROUPS[g].k_row + NA_GROUPS[g].n_k) * GRID_W)

    def load_q(b):
        hh, g = block(b)
        return q_ref[qrows(g), cols(hh)]

    def scores(b, q2):
        hh, g = block(b)
        bias = jnp.concatenate([bias_refs[g][2 * hh], bias_refs[g][2 * hh + 1]], axis=0)
        return [_dot_nt(q2, k_ref[krows(g), cols(hh)]) + bias, _dot(q2, ck_t[hh])]

    def values(b):
        hh, g = block(b)
        return [_times(v_ref[krows(g), cols(hh)]), _times_t(cv_t[hh])]

    def store(b, o):
        hh, g = block(b)
        o_ref[qrows(g), cols(hh)] = o.astype(BF16)

    _attend_pairs(LATENT_HEADS_PER_STEP * len(NA_GROUPS), load_q, scores, values, _head_pair_combine, store)


def _na_latent_attention(q, k, v, cache_k_t, cache_v_t, layer_i, rel_bias_rows, side_jobs):
    width = LATENT_HEADS_PER_STEP * LANES
    blk = pl.BlockSpec((DEC_SEQ, width), lambda p, b: (b, p))
    cache = pl.BlockSpec((None, None, width, PAST_LEN), lambda p, b: (b, layer_i, p, 0))
    rb_spec = pl.BlockSpec((None, 2 * LATENT_HEADS_PER_STEP, N_DR_PAD, LANES), lambda p, b: (layer_i, p, 0, 0))
    return _pallas_call_with_side_jobs(
        _na_latent_kernel, side_jobs,
        grid=LATENT_ATTENTION_GRID[::-1],
        in_specs=[blk, blk, blk, cache, cache, rb_spec],
        out_specs=[blk],
        out_shape=[jax.ShapeDtypeStruct((N_SAMPLE_ROWS, D_MODEL), BF16)],
        args=[q, k, v, cache_k_t, cache_v_t, rel_bias_rows],
        scratch_shapes=[pltpu.VMEM((N_DR, GRID_W, LANES), F32)] + [
            pltpu.VMEM((2 * LATENT_HEADS_PER_STEP, grp.n_q * GRID_W, grp.n_k * GRID_W), F32)
            for grp in NA_GROUPS],
        name="na_latent_attention",
    )


def _padded_rel_bias(rel_bias_na):
    half = jnp.pad(rel_bias_na, ((0, 0), (0, 0), (0, N_DR_PAD - N_DR), (0, GRID_W - N_DC)))
    return jnp.concatenate([half, half], axis=-1)


FF_CHUNK = 1024


def _post_kernel(split_in, split_out, layer, *refs):
    n_x = 2 if split_in else 1
    x_refs = refs[:n_x]
    op_ref, os_ref, mod_ref, g_ref, wo_ref, w1_ref, w2_ref = refs[n_x:n_x + 7]
    y_refs = refs[n_x + 7:]
    i = pl.program_id(0)
    r = _cond_row(i)
    is_prompt = i < N_PROMPT_TILES

    def mod(j):
        return mod_ref[pl.ds(r, 1), j * D_MODEL:(j + 1) * D_MODEL]

    x = jnp.where(is_prompt, x_refs[0][...], x_refs[1][...]) if split_in else x_refs[0][...]
    o = jnp.where(is_prompt, op_ref[...], os_ref[...])
    y1 = x + mod(2) * _dot(o, wo_ref[...])
    ms = jnp.mean(y1 * y1, axis=-1, keepdims=True)
    h = (y1 * lax.rsqrt(ms + EPS) * g_ref[layer:layer + 1, :] * (1.0 + mod(4)) + mod(3)).astype(BF16)
    acc = jnp.zeros((ROW_TILE, D_MODEL), F32)
    for c in range(D_FF // FF_CHUNK):
        cols = slice(c * FF_CHUNK, (c + 1) * FF_CHUNK)
        hid = jnp.maximum(_dot(h, w1_ref[:, cols]), 0.0)
        acc = acc + _dot((hid * hid).astype(BF16), w2_ref[cols, :])
    y = y1 + mod(5) * acc
    if split_out:
        @pl.when(is_prompt)
        def _():
            y_refs[0][...] = y

        @pl.when(jnp.logical_not(is_prompt))
        def _():
            y_refs[1][...] = y
    else:
        y_refs[0][...] = y


def _post(xs, o_p, o_s, mod, layer, g_mlp, w_o, w_fc1, w_fc2, split_out, side_jobs):
    prompt_spec = pl.BlockSpec((ROW_TILE, D_MODEL), lambda i: (_prompt_tile(i), 0))
    sample_spec = pl.BlockSpec((ROW_TILE, D_MODEL), lambda i: (_sample_tile(i), 0))
    row_spec = pl.BlockSpec((ROW_TILE, D_MODEL), lambda i: (i, 0))
    split_in = len(xs) == 2
    if split_out:
        out_specs = [prompt_spec, sample_spec]
        out_shape = [jax.ShapeDtypeStruct((N_PROMPT_ROWS, D_MODEL), F32),
                     jax.ShapeDtypeStruct((N_SAMPLE_ROWS, D_MODEL), F32)]
    else:
        out_specs = [row_spec]
        out_shape = [jax.ShapeDtypeStruct((N_ROWS, D_MODEL), F32)]
    return _pallas_call_with_side_jobs(
        functools.partial(_post_kernel, split_in, split_out, layer), side_jobs,
        grid=(N_ROW_TILES,),
        in_specs=([prompt_spec, sample_spec] if split_in else [row_spec]) + [
            prompt_spec, sample_spec,
            _resident((N_COND, 6 * D_MODEL)),
            _resident((DEPTH, D_MODEL)),
            _resident((D_MODEL, D_MODEL)),
            _resident((D_MODEL, D_FF)),
            _resident((D_FF, D_MODEL)),
        ],
        out_specs=out_specs,
        out_shape=out_shape,
        args=[*xs, o_p, o_s, mod, g_mlp, w_o, w_fc1, w_fc2],
        name="wo_mlp",
    )


def _rope_tables():
    n_freq = HEAD_DIM // 4
    freqs = ROPE_BASE ** (-np.arange(n_freq, dtype=np.float64) / n_freq)
    t = np.arange(DEC_SEQ)
    ang_row = (t // GRID_W)[:, None] * freqs
    ang_col = (t % GRID_W)[:, None] * freqs
    cos64 = np.concatenate([np.cos(ang_row)] * 2 + [np.cos(ang_col)] * 2, axis=-1)
    sin64 = np.concatenate([-np.sin(ang_row), np.sin(ang_row), -np.sin(ang_col), np.sin(ang_col)], axis=-1)
    return (jnp.asarray(np.tile(cos64, (1, 2)), dtype=F32), jnp.asarray(np.tile(sin64, (1, 2)), dtype=F32))


def _pair_gain(g):
    return jnp.tile(g.reshape(1, HEAD_DIM), (1, LANES // HEAD_DIM))


def _head_major_cache(cache):
    n_req, n_layers = cache.shape[:2]
    return jnp.transpose(cache, (0, 1, 3, 4, 2)).reshape(n_req, n_layers, D_MODEL, PAST_LEN)


def kernel(x_prompt, x_sample, cache_diff_k, cache_diff_v, cache_na_k, cache_na_v, c, c_ctx, w_ada, b_ada, norm_mix_g, norm_mlp_g, w_fc1, w_fc2, w_qkv_diff, w_o_diff, q_norm_diff_g, k_norm_diff_g, lambda_q1, lambda_k1, lambda_q2, lambda_k2, subln_g, w_qkv_na, w_o_na, q_norm_na_g, k_norm_na_g, rel_bias_na):
    cond = jnp.concatenate([c_ctx[None, :], c, jnp.zeros((N_COND - 1 - DEC_BATCH, D_MODEL), F32)], axis=0)
    mod, w_qkv = _modulation(cond, w_ada, b_ada, 0, [_cast_job(w_qkv_diff, 0, 0, ADA_GRID)])
    cos_tab, sin_tab = _rope_tables()
    na_bias = _padded_rel_bias(rel_bias_na)
    cdk = cache_diff_k.reshape(DEC_BATCH, DEPTH // 2, PAST_LEN * N_LANE_BLOCKS, LANES)
    cdv = cache_diff_v.reshape(DEC_BATCH, DEPTH // 2, PAST_LEN * N_LANE_BLOCKS, LANES)
    cnk_t = _head_major_cache(cache_na_k)
    cnv_t = _head_major_cache(cache_na_v)
    g_mix, g_mlp = norm_mix_g, norm_mlp_g
    w_qkv_f32 = (w_qkv_diff, w_qkv_na)
    w_o_f32 = (w_o_diff, w_o_na)
    q_gain = (q_norm_diff_g, q_norm_na_g)
    k_gain = (k_norm_diff_g, k_norm_na_g)

    xs = [x_prompt.reshape(N_PROMPT_ROWS, D_MODEL), x_sample.reshape(N_SAMPLE_ROWS, D_MODEL)]
    new_kv = [None, None]
    w_o = None
    for l in range(DEPTH):
        i = l // 2
        kind = l % 2
        is_diff = kind == 0
        first = l == 0
        last = l == DEPTH - 1
        x_prompt_src, x_sample_src = (xs[0], xs[1]) if len(xs) == 2 else (xs[0], xs[0])
        sample_first_tile = 0 if len(xs) == 2 else N_PROMPT_TILES
        gains = (_pair_gain(q_gain[kind][i]), _pair_gain(k_gain[kind][i]))
        lambda_init = 0.8 - 0.6 * math.exp(-0.3 * l)
        lam_params = [p[i if is_diff else 0].reshape(1, HEAD_DIM)
                      for p in (lambda_q1, lambda_k1, lambda_q2, lambda_k2)]
        gsub = subln_g[i if is_diff else 0].reshape(1, LANES)
        o_p, new_k, new_v, *cast = _ctx_layer(
            x_prompt_src, mod, l, g_mix, w_qkv, i, *gains, lam_params, gsub, is_diff, lambda_init,
            new_kv[kind], [_cast_job(w_o_f32[kind], i, 0, (N_PROMPT_TILES,))] if first else [])
        w_o = cast[0] if first else w_o
        new_kv[kind] = (new_k, new_v)
        qs, ks, vs = _qkv_latent(
            x_sample_src, sample_first_tile, mod, l, g_mix, w_qkv, *gains, (cos_tab, sin_tab), is_diff, [])
        if is_diff:
            mlp_jobs = [_cast_job(w_fc1, l, 1, LATENT_ATTENTION_GRID), _cast_job(w_fc2, l, 0, LATENT_ATTENTION_GRID)]
            o_s, w1, w2 = _diff_latent_attention(qs, ks, vs, cdk, cdv, i, lam_params, gsub, lambda_init,
                                                 mlp_jobs)
        else:
            na_grid = LATENT_ATTENTION_GRID[::-1]
            mlp_jobs = [_cast_job(w_fc1, l, 1, na_grid), _cast_job(w_fc2, l, 0, na_grid)]
            o_s, w1, w2 = _na_latent_attention(qs, ks, vs, cnk_t, cnv_t, i, na_bias, mlp_jobs)
        nxt, nxt_kind, nxt_i = l + 1, (l + 1) % 2, (l + 1) // 2
        post_grid = (N_ROW_TILES,)
        next_jobs = [] if last else [_cast_job(w_qkv_f32[nxt_kind], nxt_i, 0, post_grid),
                                     _cast_job(w_o_f32[nxt_kind], nxt_i, 0, post_grid),
                                     _ada_job(cond, w_ada, b_ada, nxt, post_grid)]
        outs = list(_post(xs, o_p, o_s, mod, l, g_mlp, w_o, w1, w2, last, next_jobs))
        if last:
            xp, xs = outs
        else:
            *xs, w_qkv, w_o, mod = outs

    n_kind = DEPTH // 2
    diff_k, diff_v = (a.reshape(BATCH, n_kind, SEQ, N_LANE_BLOCKS, LANES) for a in new_kv[0])
    na_k, na_v = (jnp.transpose(a.reshape(BATCH, n_kind, NA_HEADS, HEAD_DIM, SEQ), (0, 1, 4, 2, 3))
                  for a in new_kv[1])
    return (xp.reshape(BATCH, SEQ, D_MODEL), xs.reshape(DEC_BATCH, DEC_SEQ, D_MODEL),
            diff_k, diff_v, na_k, na_v)
```

```python
import functools
import math
from typing import Callable, NamedTuple

import jax
import jax.numpy as jnp
import numpy as np
from jax import lax
from jax.experimental import pallas as pl
from jax.experimental.pallas import tpu as pltpu

D_MODEL = 1024
BATCH = 16
SEQ = 256
DEPTH = 4
DEC_BATCH = 4
DEC_SEQ = 1024
PAST_LEN = 256
GRID_W = 64
GRID_ROWS = DEC_SEQ // GRID_W
HEAD_DIM = 64
D_FF = 4 * D_MODEL
NA_HEADS = 16
NA_KH = 8
NA_KW = 16
ROPE_BASE = 10000.0
EPS = 1e-6
NEG_INF = -1e30
LOG2_E = math.log2(math.e)
SCORE_SCALE = HEAD_DIM ** -0.5 * LOG2_E

LANES = 128
N_LANE_BLOCKS = D_MODEL // LANES
N_PROMPT_ROWS = BATCH * SEQ
N_SAMPLE_ROWS = DEC_BATCH * DEC_SEQ
N_ROWS = N_PROMPT_ROWS + N_SAMPLE_ROWS
ROW_TILE = 512
N_ROW_TILES = N_ROWS // ROW_TILE
N_PROMPT_TILES = N_PROMPT_ROWS // ROW_TILE
N_LATENT_TILES = N_SAMPLE_ROWS // ROW_TILE
TILES_PER_SAMPLE = DEC_SEQ // ROW_TILE
REQ_PER_TILE = ROW_TILE // SEQ
N_COND = 8
V7X_VMEM_BYTES = 64 * 1024 * 1024
VMEM_LIMIT = V7X_VMEM_BYTES // 8 * 7

F32 = jnp.float32
BF16 = jnp.bfloat16
NT_DIMS = (((1,), (1,)), ((), ()))


def _dot(a, b):
    return jnp.dot(a, b, preferred_element_type=F32)


def _dot_nt(a, b):
    return lax.dot_general(a, b, NT_DIMS, preferred_element_type=F32)


def _params(n_axes):
    return pltpu.CompilerParams(dimension_semantics=("arbitrary",) * n_axes,
                                vmem_limit_bytes=VMEM_LIMIT)


def _resident(shape):
    return pl.BlockSpec(shape, lambda *_: (0,) * len(shape), pipeline_mode=pl.Buffered(1))


class _SideJob(NamedTuple):
    args: list
    in_specs: list
    out_spec: pl.BlockSpec
    out_shape: jax.ShapeDtypeStruct
    body: Callable


def _cast_body(src_ref, dst_ref):
    dst_ref[...] = src_ref[...].astype(BF16)


def _step_number(grid):
    def number(*idx):
        step = idx[0]
        for size, i in zip(grid[1:], idx[1:]):
            step = step * size + i
        return step
    return number


def _cast_job(weights, layer, split_axis, grid):
    _, rows, cols = weights.shape
    n_steps, step = math.prod(grid), _step_number(grid)
    if split_axis == 0:
        band = (rows // n_steps, cols)
        src_spec = pl.BlockSpec((None,) + band, lambda *idx: (layer, step(*idx), 0))
        dst_spec = pl.BlockSpec(band, lambda *idx: (step(*idx), 0))
    else:
        band = (rows, cols // n_steps)
        src_spec = pl.BlockSpec((None,) + band, lambda *idx: (layer, 0, step(*idx)))
        dst_spec = pl.BlockSpec(band, lambda *idx: (0, step(*idx)))
    return _SideJob([weights], [src_spec], dst_spec, jax.ShapeDtypeStruct((rows, cols), BF16), _cast_body)


def _pallas_call_with_side_jobs(kernel_fn, jobs, *, grid, in_specs, out_specs, out_shape, args, name,
                                input_output_aliases=None, scratch_shapes=()):
    n_in, n_out = len(in_specs), len(out_specs)
    n_job_in = [len(job.args) for job in jobs]

    def kernel_and_jobs(*refs):
        ins, refs = refs[:n_in], refs[n_in:]
        job_ins, refs = refs[:sum(n_job_in)], refs[sum(n_job_in):]
        outs, refs = refs[:n_out], refs[n_out:]
        job_outs, scratch = refs[:len(jobs)], refs[len(jobs):]
        for job, n, out_ref in zip(jobs, n_job_in, job_outs):
            job.body(*job_ins[:n], out_ref)
            job_ins = job_ins[n:]
        kernel_fn(*ins, *outs, *scratch)

    return pl.pallas_call(
        kernel_and_jobs,
        grid=grid,
        in_specs=list(in_specs) + [spec for job in jobs for spec in job.in_specs],
        out_specs=list(out_specs) + [job.out_spec for job in jobs],
        out_shape=list(out_shape) + [job.out_shape for job in jobs],
        input_output_aliases=input_output_aliases or {},
        scratch_shapes=list(scratch_shapes),
        compiler_params=_params(len(grid)),
        name=name,
    )(*args, *(arg for job in jobs for arg in job.args))


def _prompt_tile(i):
    return jnp.minimum(i, N_PROMPT_TILES - 1)


def _sample_tile(i):
    return jnp.maximum(i - N_PROMPT_TILES, 0)


def _cond_row(i):
    return jnp.where(i < N_PROMPT_TILES, 0, 1 + (i - N_PROMPT_TILES) // TILES_PER_SAMPLE)


def _low_half_mask(shape):
    return lax.broadcasted_iota(jnp.int32, shape, len(shape) - 1) < HEAD_DIM


def _ada_kernel(layer, cond_ref, w_ref, b_ref, out_ref):
    cond = cond_ref[...]
    act = (cond * jax.nn.sigmoid(cond)).astype(BF16)
    out_ref[...] = _dot(act, w_ref[...].astype(BF16)) + b_ref[layer:layer + 1, :]


def _ada_specs(layer, n_steps):
    tn = 6 * D_MODEL // n_steps
    in_specs = [pl.BlockSpec((N_COND, D_MODEL), lambda j: (0, 0)),
                pl.BlockSpec((None, D_MODEL, tn), lambda j: (layer, 0, j)),
                pl.BlockSpec((DEPTH, tn), lambda j: (0, j))]
    return in_specs, pl.BlockSpec((N_COND, tn), lambda j: (0, j)), jax.ShapeDtypeStruct((N_COND, 6 * D_MODEL), F32)


ADA_GRID = (4,)


def _modulation(cond, w_ada, b_ada, layer, side_jobs):
    in_specs, out_spec, out_shape = _ada_specs(layer, *ADA_GRID)
    return _pallas_call_with_side_jobs(
        functools.partial(_ada_kernel, layer), side_jobs, grid=ADA_GRID, in_specs=in_specs,
        out_specs=[out_spec], out_shape=[out_shape], args=[cond, w_ada, b_ada], name="ada_modulation")


def _ada_job(cond, w_ada, b_ada, layer, grid):
    (n_steps,) = grid
    in_specs, out_spec, out_shape = _ada_specs(layer, n_steps)
    return _SideJob([cond, w_ada, b_ada], in_specs, out_spec, out_shape, functools.partial(_ada_kernel, layer))


def _group_rms(x, gain):
    low = _low_half_mask((1, LANES))
    cols = []
    for c in range(N_LANE_BLOCKS):
        xb = x[:, c * LANES:(c + 1) * LANES]
        sq = xb * xb
        s_lo = jnp.sum(jnp.where(low, sq, 0.0), axis=-1, keepdims=True)
        s_hi = jnp.sum(jnp.where(low, 0.0, sq), axis=-1, keepdims=True)
        ms = jnp.where(low, s_lo, s_hi) * (1.0 / HEAD_DIM)
        cols.append(xb * lax.rsqrt(ms + EPS) * gain)
    return cols


def _rope(cols, cos, sin_signed):
    lane = lax.broadcasted_iota(jnp.int32, (1, LANES), 1)
    first = (lane % 32) < 16
    out = []
    for xb in cols:
        partner = jnp.where(first, pltpu.roll(xb, LANES - 16, 1), pltpu.roll(xb, 16, 1))
        out.append(xb * cos + partner * sin_signed)
    return out


def _qkv_matmul(layer, x_ref, mod_ref, g_ref, w_ref, r, rows=slice(None)):
    x = x_ref[rows, :]
    ms = jnp.mean(x * x, axis=-1, keepdims=True)
    xn = x * lax.rsqrt(ms + EPS) * g_ref[layer:layer + 1, :]
    shift = mod_ref[pl.ds(r, 1), 0:D_MODEL]
    scale = mod_ref[pl.ds(r, 1), D_MODEL:2 * D_MODEL]
    h = (xn * (1.0 + scale) + shift).astype(BF16)
    return _dot(h, w_ref[...])


def _qk_norm(qkv, qg_ref, kg_ref):
    qn = _group_rms(qkv[:, 0:D_MODEL], qg_ref[...])
    kn = _group_rms(qkv[:, D_MODEL:2 * D_MODEL], kg_ref[...])
    return qn, kn, qkv[:, 2 * D_MODEL:3 * D_MODEL]


def _lane_block(c):
    return slice(c * LANES, (c + 1) * LANES)


LATENT_ROW_BANDS = 2


def _qkv_latent_kernel(is_diff, layer, x_ref, mod_ref, g_ref, w_ref, qg_ref, kg_ref, cos_ref, sin_ref,
                       q_ref, k_ref, v_ref):
    r = 1 + pl.program_id(0) // TILES_PER_SAMPLE
    band = ROW_TILE // LATENT_ROW_BANDS
    bands = [slice(a * band, (a + 1) * band) for a in range(LATENT_ROW_BANDS)]
    raw = [_qkv_matmul(layer, x_ref, mod_ref, g_ref, w_ref, r, rows) for rows in bands]
    for rows, qkv in zip(bands, raw):
        qn, kn, v = _qk_norm(qkv, qg_ref, kg_ref)
        v_ref[rows, :] = v.astype(BF16)
        if is_diff:
            cos = cos_ref[rows, :]
            sin = sin_ref[rows, :]
            qn = _rope(qn, cos, sin)
            kn = _rope(kn, cos, sin)
        for c in range(N_LANE_BLOCKS):
            q_ref[rows, _lane_block(c)] = (qn[c] * SCORE_SCALE).astype(BF16)
            k_ref[rows, _lane_block(c)] = kn[c].astype(BF16)


def _ctx_layer_kernel(is_diff, layer, lambda_init, has_prev, *refs):
    if has_prev:
        refs = refs[2:]
    (x_ref, mod_ref, g_ref, w_ref, qg_ref, kg_ref, lq1_ref, lk1_ref, lq2_ref, lk2_ref, gsub_ref,
     o_ref, kf_ref, vf_ref) = refs
    qn, kn, v = _qk_norm(_qkv_matmul(layer, x_ref, mod_ref, g_ref, w_ref, 0), qg_ref, kg_ref)
    for c in range(N_LANE_BLOCKS):
        for bb in range(REQ_PER_TILE):
            rows = slice(bb * SEQ, (bb + 1) * SEQ)
            if is_diff:
                dst = pl.ds(c, SEQ, stride=N_LANE_BLOCKS)
                kf_ref[bb, dst, :] = kn[c][rows, :]
                vf_ref[bb, dst, :] = v[rows, _lane_block(c)]
            else:
                kf_ref[bb, _lane_block(c), :] = kn[c][rows, :].T
                vf_ref[bb, _lane_block(c), :] = v[rows, _lane_block(c)].T
    q = [(qn[c] * SCORE_SCALE).astype(BF16) for c in range(N_LANE_BLOCKS)]
    k = [kn[c].astype(BF16) for c in range(N_LANE_BLOCKS)]
    vb = [v[:, _lane_block(c)].astype(BF16) for c in range(N_LANE_BLOCKS)]
    if is_diff:
        lam = _diff_lambda(lq1_ref, lk1_ref, lq2_ref, lk2_ref, lambda_init)
        combine = _diff_combine(lam, gsub_ref[...], lambda_init)
    else:
        combine = _head_pair_combine

    def block(b):
        bb, c = divmod(b, N_LANE_BLOCKS)
        return slice(bb * SEQ, (bb + 1) * SEQ), c

    def load_q(b):
        rows, c = block(b)
        return q[c][rows, :]

    def scores(b, q2):
        rows, c = block(b)
        return [_dot_nt(q2, k[c][rows, :])]

    def values(b):
        rows, c = block(b)
        return [_times(vb[c][rows, :])]

    def store(b, o):
        rows, c = block(b)
        o_ref[rows, _lane_block(c)] = o.astype(BF16)

    _attend_pairs(REQ_PER_TILE * N_LANE_BLOCKS, load_q, scores, values, combine, store)


def _qkv_latent(x, first_tile, mod, layer, g_mix, w_qkv, q_gain, k_gain, rope_tabs, is_diff, side_jobs):
    row_spec = pl.BlockSpec((ROW_TILE, D_MODEL), lambda i: (i, 0))
    rope_spec = pl.BlockSpec((ROW_TILE, LANES), lambda i: (i % TILES_PER_SAMPLE, 0))
    return _pallas_call_with_side_jobs(
        functools.partial(_qkv_latent_kernel, is_diff, layer), side_jobs,
        grid=(N_LATENT_TILES,),
        in_specs=[pl.BlockSpec((ROW_TILE, D_MODEL), lambda i: (first_tile + i, 0)),
                  _resident((N_COND, 6 * D_MODEL)),
                  _resident((DEPTH, D_MODEL)),
                  _resident((D_MODEL, 3 * D_MODEL)),
                  _resident((1, LANES)), _resident((1, LANES)), rope_spec, rope_spec],
        out_specs=[row_spec] * 3,
        out_shape=[jax.ShapeDtypeStruct((N_SAMPLE_ROWS, D_MODEL), BF16)] * 3,
        args=[x, mod, g_mix, w_qkv, q_gain, k_gain, *rope_tabs],
        name="qkv_latent",
    )


def _ctx_layer(x, mod, layer, g_mix, w_qkv, kind_layer, q_gain, k_gain, lam_params, gsub, is_diff,
               lambda_init, prev, side_jobs):
    row_spec = pl.BlockSpec((ROW_TILE, D_MODEL), lambda i: (i, 0))
    if is_diff:
        cache_rows, cache_cols = SEQ * N_LANE_BLOCKS, LANES
    else:
        cache_rows, cache_cols = D_MODEL, SEQ
    cache_spec = pl.BlockSpec((REQ_PER_TILE, None, cache_rows, cache_cols), lambda i: (i, kind_layer, 0, 0))
    cache_shape = jax.ShapeDtypeStruct((BATCH, DEPTH // 2, cache_rows, cache_cols), F32)
    in_specs = ([row_spec, _resident((N_COND, 6 * D_MODEL)), _resident((DEPTH, D_MODEL)),
                 _resident((D_MODEL, 3 * D_MODEL)), _resident((1, LANES)), _resident((1, LANES))]
                + [_resident((1, HEAD_DIM))] * 4 + [_resident((1, LANES))])
    args = [x, mod, g_mix, w_qkv, q_gain, k_gain, *lam_params, gsub]
    aliases = {}
    if prev is not None:
        in_specs = [pl.BlockSpec(memory_space=pl.ANY)] * 2 + in_specs
        args = list(prev) + args
        aliases = {0: 1, 1: 2}
    return _pallas_call_with_side_jobs(
        functools.partial(_ctx_layer_kernel, is_diff, layer, lambda_init, prev is not None), side_jobs,
        grid=(N_PROMPT_TILES,),
        in_specs=in_specs,
        out_specs=[row_spec, cache_spec, cache_spec],
        out_shape=[jax.ShapeDtypeStruct((N_PROMPT_ROWS, D_MODEL), BF16), cache_shape, cache_shape],
        args=args,
        input_output_aliases=aliases,
        name="ctx_layer",
    )


def _softmax_pv(score_parts, value_parts):
    m = None
    for s in score_parts:
        pm = jnp.max(s, axis=-1, keepdims=True)
        m = pm if m is None else jnp.maximum(m, pm)
    acc = None
    for s, times_v in zip(score_parts, value_parts):
        pv = times_v(jnp.exp2(s - m).astype(BF16))
        acc = pv if acc is None else acc + pv
    return acc[:, :LANES] * (1.0 / acc[:, LANES:])


def _times(v):
    v1 = jnp.concatenate([v, jnp.ones_like(v)], axis=1)
    return lambda w: _dot(w, v1)


def _times_t(v_t):
    v1_t = jnp.concatenate([v_t, jnp.ones_like(v_t)], axis=0)
    return lambda w: _dot_nt(w, v1_t)


def _diff_lambda(lq1_ref, lk1_ref, lq2_ref, lk2_ref, lambda_init):
    e1 = jnp.exp(jnp.sum(lq1_ref[...] * lk1_ref[...], axis=-1, keepdims=True))
    e2 = jnp.exp(jnp.sum(lq2_ref[...] * lk2_ref[...], axis=-1, keepdims=True))
    return e1 - e2 + lambda_init


def _attend_pairs(n_blocks, load_q, scores, values, combine, store):
    def block_scores(b):
        q = load_q(b)
        low = _low_half_mask(q.shape)
        zero = jnp.zeros_like(q)
        return scores(b, jnp.concatenate([jnp.where(low, q, zero), jnp.where(low, zero, q)], axis=0))

    s = block_scores(0)
    for b in range(n_blocks):
        s_next = block_scores(b + 1) if b + 1 < n_blocks else None
        o = _softmax_pv(s, values(b))
        s = s_next
        rows = o.shape[0] // 2
        store(b, combine(o[:rows], o[rows:]))


def _diff_combine(lam, gsub, lambda_init):
    def combine(o1, o2):
        o = o1 - lam * o2
        ms = jnp.mean(o * o, axis=-1, keepdims=True)
        return o * lax.rsqrt(ms + EPS) * gsub * (1.0 - lambda_init)
    return combine


def _head_pair_combine(o_lo, o_hi):
    return jnp.where(_low_half_mask(o_lo.shape), o_lo, o_hi)


Q_CHUNK = 512
LATENT_HEADS_PER_STEP = 2


def _diff_latent_kernel(lambda_init, q_ref, k_ref, v_ref, ck_ref, cv_ref,
                        lq1_ref, lk1_ref, lq2_ref, lk2_ref, gsub_ref, o_ref):
    lam = _diff_lambda(lq1_ref, lk1_ref, lq2_ref, lk2_ref, lambda_init)
    n_chunks = DEC_SEQ // Q_CHUNK
    keys, values = [], []
    for hh in range(LATENT_HEADS_PER_STEP):
        cols = slice(hh * LANES, (hh + 1) * LANES)
        head = pl.program_id(1) * LATENT_HEADS_PER_STEP + hh
        head_rows = pl.ds(head, PAST_LEN, stride=N_LANE_BLOCKS)
        keys.append([k_ref[:, cols], ck_ref[head_rows, :].astype(BF16)])
        values.append([_times(v_ref[:, cols]), _times(cv_ref[head_rows, :].astype(BF16))])

    def block(b):
        hh, c = divmod(b, n_chunks)
        return slice(c * Q_CHUNK, (c + 1) * Q_CHUNK), slice(hh * LANES, (hh + 1) * LANES)

    def store(b, o):
        o_ref[block(b)] = o.astype(BF16)

    _attend_pairs(LATENT_HEADS_PER_STEP * n_chunks,
                  lambda b: q_ref[block(b)],
                  lambda b, q2: [_dot_nt(q2, kpart) for kpart in keys[b // n_chunks]],
                  lambda b: values[b // n_chunks],
                  _diff_combine(lam, gsub_ref[...], lambda_init), store)


LATENT_ATTENTION_GRID = (DEC_BATCH, N_LANE_BLOCKS // LATENT_HEADS_PER_STEP)


def _diff_latent_attention(q, k, v, cache_k, cache_v, layer_i, lam_params, gsub, lambda_init, side_jobs):
    blk = pl.BlockSpec((DEC_SEQ, LATENT_HEADS_PER_STEP * LANES), lambda b, h: (b, h))
    cache = pl.BlockSpec((None, None, PAST_LEN * N_LANE_BLOCKS, LANES), lambda b, h: (b, layer_i, 0, 0))
    small = [_resident((1, HEAD_DIM))] * 4 + [_resident((1, LANES))]
    return _pallas_call_with_side_jobs(
        functools.partial(_diff_latent_kernel, lambda_init), side_jobs,
        grid=LATENT_ATTENTION_GRID,
        in_specs=[blk, blk, blk, cache, cache] + small,
        out_specs=[blk],
        out_shape=[jax.ShapeDtypeStruct((N_SAMPLE_ROWS, D_MODEL), BF16)],
        args=[q, k, v, cache_k, cache_v, *lam_params, gsub],
        name="diff_latent_attention",
    )


class _NaGroup(NamedTuple):
    q_row: int
    n_q: int
    k_row: int
    n_k: int


def _window_start(r):
    return min(max(r - NA_KH // 2, 0), GRID_ROWS - NA_KH)


def _na_groups(rows_per_group):
    groups = []
    for q_row in range(0, GRID_ROWS, rows_per_group):
        first = _window_start(q_row)
        last = _window_start(q_row + rows_per_group - 1) + NA_KH
        k_row = first - first % 2
        n_k = last - k_row
        n_k += n_k % 2
        groups.append(_NaGroup(q_row, rows_per_group, min(k_row, GRID_ROWS - n_k), n_k))
    return groups


NA_GROUPS = _na_groups(4)
N_DR = 2 * NA_KH - 1
N_DC = 2 * NA_KW - 1
N_DR_PAD = 16


def _expand_na_bias(rb_ref, tile_ref, bias_refs):
    qc = lax.broadcasted_iota(jnp.int32, (GRID_W, LANES), 0)
    lane = lax.broadcasted_iota(jnp.int32, (GRID_W, LANES), 1)
    kc = lane % GRID_W
    cs = jnp.clip(qc - NA_KW // 2, 0, GRID_W - NA_KW)
    col_ok = (kc >= cs) & (kc < cs + NA_KW)
    low = lane < GRID_W
    neg = jnp.full((GRID_W, LANES), NEG_INF, F32)
    for head in range(2 * LATENT_HEADS_PER_STEP):
        for dr in range(N_DR):
            row = jnp.broadcast_to(rb_ref[head, dr:dr + 1, :], (GRID_W, LANES))
            shifted = pltpu.roll(row, LANES - (NA_KW - 1), 1, stride=1, stride_axis=0)
            tile_ref[dr] = jnp.where(col_ok, shifted * LOG2_E, neg)
        for grp, bias_ref in zip(NA_GROUPS, bias_refs):
            for rq in range(grp.n_q):
                r = grp.q_row + rq
                rs = _window_start(r)

                def tile(kr, r=r, rs=rs):
                    if rs <= kr < rs + NA_KH:
                        return tile_ref[kr - r + NA_KH - 1]
                    return neg

                for p in range(grp.n_k // 2):
                    kr = grp.k_row + 2 * p
                    bias_ref[head, rq * GRID_W:(rq + 1) * GRID_W, p * LANES:(p + 1) * LANES] = (
                        jnp.where(low, tile(kr), tile(kr + 1)))


def _na_latent_kernel(q_ref, k_ref, v_ref, ckt_ref, cvt_ref, rb_ref, o_ref, tile_ref, *bias_refs):
    @pl.when(pl.program_id(1) == 0)
    def _():
        _expand_na_bias(rb_ref, tile_ref, bias_refs)

    ck_t = [ckt_ref[hh * LANES:(hh + 1) * LANES, :].astype(BF16) for hh in range(LATENT_HEADS_PER_STEP)]
    cv_t = [cvt_ref[hh * LANES:(hh + 1) * LANES, :].astype(BF16) for hh in range(LATENT_HEADS_PER_STEP)]

    def block(b):
        return divmod(b, len(NA_GROUPS))

    def cols(hh):
        return slice(hh * LANES, (hh + 1) * LANES)

    def qrows(g):
        return slice(NA_GROUPS[g].q_row * GRID_W, (NA_GROUPS[g].q_row + NA_GROUPS[g].n_q) * GRID_W)

    def krows(g):
        return slice(NA_GROUPS[g].k_row * GRID_W, (NA_GROUPS[g].k_row + NA_GROUPS[g].n_k) * GRID_W)

    def load_q(b):
        hh, g = block(b)
        return q_ref[qrows(g), cols(hh)]

    def scores(b, q2):
        hh, g = block(b)
        bias = jnp.concatenate([bias_refs[g][2 * hh], bias_refs[g][2 * hh + 1]], axis=0)
        return [_dot_nt(q2, k_ref[krows(g), cols(hh)]) + bias, _dot(q2, ck_t[hh])]

    def values(b):
        hh, g = block(b)
        return [_times(v_ref[krows(g), cols(hh)]), _times_t(cv_t[hh])]

    def store(b, o):
        hh, g = block(b)
        o_ref[qrows(g), cols(hh)] = o.astype(BF16)

    _attend_pairs(LATENT_HEADS_PER_STEP * len(NA_GROUPS), load_q, scores, values, _head_pair_combine, store)


def _na_latent_attention(q, k, v, cache_k_t, cache_v_t, layer_i, rel_bias_rows, side_jobs):
    width = LATENT_HEADS_PER_STEP * LANES
    blk = pl.BlockSpec((DEC_SEQ, width), lambda p, b: (b, p))
    cache = pl.BlockSpec((None, None, width, PAST_LEN), lambda p, b: (b, layer_i, p, 0))
    rb_spec = pl.BlockSpec((None, 2 * LATENT_HEADS_PER_STEP, N_DR_PAD, LANES), lambda p, b: (layer_i, p, 0, 0))
    return _pallas_call_with_side_jobs(
        _na_latent_kernel, side_jobs,
        grid=LATENT_ATTENTION_GRID[::-1],
        in_specs=[blk, blk, blk, cache, cache, rb_spec],
        out_specs=[blk],
        out_shape=[jax.ShapeDtypeStruct((N_SAMPLE_ROWS, D_MODEL), BF16)],
        args=[q, k, v, cache_k_t, cache_v_t, rel_bias_rows],
        scratch_shapes=[pltpu.VMEM((N_DR, GRID_W, LANES), F32)] + [
            pltpu.VMEM((2 * LATENT_HEADS_PER_STEP, grp.n_q * GRID_W, grp.n_k * GRID_W), F32)
            for grp in NA_GROUPS],
        name="na_latent_attention",
    )


def _padded_rel_bias(rel_bias_na):
    half = jnp.pad(rel_bias_na, ((0, 0), (0, 0), (0, N_DR_PAD - N_DR), (0, GRID_W - N_DC)))
    return jnp.concatenate([half, half], axis=-1)


FF_CHUNK = 1024
POST_ROW_BANDS = 2


def _post_kernel(split_in, split_out, layer, *refs):
    n_x = 2 if split_in else 1
    x_refs = refs[:n_x]
    op_ref, os_ref, mod_ref, g_ref, wo_ref, w1_ref, w2_ref = refs[n_x:n_x + 7]
    y_refs = refs[n_x + 7:]
    i = pl.program_id(0)
    r = _cond_row(i)
    is_prompt = i < N_PROMPT_TILES

    def mod(j):
        return mod_ref[pl.ds(r, 1), j * D_MODEL:(j + 1) * D_MODEL]

    band = ROW_TILE // POST_ROW_BANDS
    bands = [slice(a * band, (a + 1) * band) for a in range(POST_ROW_BANDS)]
    y1s, ys = [], []
    for rows in bands:
        x = (jnp.where(is_prompt, x_refs[0][rows, :], x_refs[1][rows, :]) if split_in
             else x_refs[0][rows, :])
        o = jnp.where(is_prompt, op_ref[rows, :], os_ref[rows, :])
        y1s.append(x + mod(2) * _dot(o, wo_ref[...]))
    for rows, y1 in zip(bands, y1s):
        ms = jnp.mean(y1 * y1, axis=-1, keepdims=True)
        h = (y1 * lax.rsqrt(ms + EPS) * g_ref[layer:layer + 1, :] * (1.0 + mod(4)) + mod(3)).astype(BF16)
        acc = jnp.zeros((band, D_MODEL), F32)
        for c in range(D_FF // FF_CHUNK):
            cols = slice(c * FF_CHUNK, (c + 1) * FF_CHUNK)
            hid = jnp.maximum(_dot(h, w1_ref[:, cols]), 0.0)
            acc = acc + _dot((hid * hid).astype(BF16), w2_ref[cols, :])
        ys.append(y1 + mod(5) * acc)
        if not split_out:
            y_refs[0][rows, :] = ys[-1]
    if split_out:
        @pl.when(is_prompt)
        def _():
            for rows, y in zip(bands, ys):
                y_refs[0][rows, :] = y

        @pl.when(jnp.logical_not(is_prompt))
        def _():
            for rows, y in zip(bands, ys):
                y_refs[1][rows, :] = y


def _post(xs, o_p, o_s, mod, layer, g_mlp, w_o, w_fc1, w_fc2, split_out, side_jobs):
    prompt_spec = pl.BlockSpec((ROW_TILE, D_MODEL), lambda i: (_prompt_tile(i), 0))
    sample_spec = pl.BlockSpec((ROW_TILE, D_MODEL), lambda i: (_sample_tile(i), 0))
    row_spec = pl.BlockSpec((ROW_TILE, D_MODEL), lambda i: (i, 0))
    split_in = len(xs) == 2
    if split_out:
        out_specs = [prompt_spec, sample_spec]
        out_shape = [jax.ShapeDtypeStruct((N_PROMPT_ROWS, D_MODEL), F32),
                     jax.ShapeDtypeStruct((N_SAMPLE_ROWS, D_MODEL), F32)]
    else:
        out_specs = [row_spec]
        out_shape = [jax.ShapeDtypeStruct((N_ROWS, D_MODEL), F32)]
    return _pallas_call_with_side_jobs(
        functools.partial(_post_kernel, split_in, split_out, layer), side_jobs,
        grid=(N_ROW_TILES,),
        in_specs=([prompt_spec, sample_spec] if split_in else [row_spec]) + [
            prompt_spec, sample_spec,
            _resident((N_COND, 6 * D_MODEL)),
            _resident((DEPTH, D_MODEL)),
            _resident((D_MODEL, D_MODEL)),
            _resident((D_MODEL, D_FF)),
            _resident((D_FF, D_MODEL)),
        ],
        out_specs=out_specs,
        out_shape=out_shape,
        args=[*xs, o_p, o_s, mod, g_mlp, w_o, w_fc1, w_fc2],
        name="wo_mlp",
    )


def _rope_tables():
    n_freq = HEAD_DIM // 4
    freqs = ROPE_BASE ** (-np.arange(n_freq, dtype=np.float64) / n_freq)
    t = np.arange(DEC_SEQ)
    ang_row = (t // GRID_W)[:, None] * freqs
    ang_col = (t % GRID_W)[:, None] * freqs
    cos64 = np.concatenate([np.cos(ang_row)] * 2 + [np.cos(ang_col)] * 2, axis=-1)
    sin64 = np.concatenate([-np.sin(ang_row), np.sin(ang_row), -np.sin(ang_col), np.sin(ang_col)], axis=-1)
    return (jnp.asarray(np.tile(cos64, (1, 2)), dtype=F32), jnp.asarray(np.tile(sin64, (1, 2)), dtype=F32))


def _pair_gain(g):
    return jnp.tile(g.reshape(1, HEAD_DIM), (1, LANES // HEAD_DIM))


def _head_major_cache(cache):
    n_req, n_layers = cache.shape[:2]
    return jnp.transpose(cache, (0, 1, 3, 4, 2)).reshape(n_req, n_layers, D_MODEL, PAST_LEN)


def kernel(x_prompt, x_sample, cache_diff_k, cache_diff_v, cache_na_k, cache_na_v, c, c_ctx, w_ada, b_ada, norm_mix_g, norm_mlp_g, w_fc1, w_fc2, w_qkv_diff, w_o_diff, q_norm_diff_g, k_norm_diff_g, lambda_q1, lambda_k1, lambda_q2, lambda_k2, subln_g, w_qkv_na, w_o_na, q_norm_na_g, k_norm_na_g, rel_bias_na):
    cond = jnp.concatenate([c_ctx[None, :], c, jnp.zeros((N_COND - 1 - DEC_BATCH, D_MODEL), F32)], axis=0)
    mod, w_qkv = _modulation(cond, w_ada, b_ada, 0, [_cast_job(w_qkv_diff, 0, 0, ADA_GRID)])
    cos_tab, sin_tab = _rope_tables()
    na_bias = _padded_rel_bias(rel_bias_na)
    cdk = cache_diff_k.reshape(DEC_BATCH, DEPTH // 2, PAST_LEN * N_LANE_BLOCKS, LANES)
    cdv = cache_diff_v.reshape(DEC_BATCH, DEPTH // 2, PAST_LEN * N_LANE_BLOCKS, LANES)
    cnk_t = _head_major_cache(cache_na_k)
    cnv_t = _head_major_cache(cache_na_v)
    g_mix, g_mlp = norm_mix_g, norm_mlp_g
    w_qkv_f32 = (w_qkv_diff, w_qkv_na)
    w_o_f32 = (w_o_diff, w_o_na)
    q_gain = (q_norm_diff_g, q_norm_na_g)
    k_gain = (k_norm_diff_g, k_norm_na_g)

    xs = [x_prompt.reshape(N_PROMPT_ROWS, D_MODEL), x_sample.reshape(N_SAMPLE_ROWS, D_MODEL)]
    new_kv = [None, None]
    w_o = None
    for l in range(DEPTH):
        i = l // 2
        kind = l % 2
        is_diff = kind == 0
        first = l == 0
        last = l == DEPTH - 1
        x_prompt_src, x_sample_src = (xs[0], xs[1]) if len(xs) == 2 else (xs[0], xs[0])
        sample_first_tile = 0 if len(xs) == 2 else N_PROMPT_TILES
        gains = (_pair_gain(q_gain[kind][i]), _pair_gain(k_gain[kind][i]))
        lambda_init = 0.8 - 0.6 * math.exp(-0.3 * l)
        lam_params = [p[i if is_diff else 0].reshape(1, HEAD_DIM)
                      for p in (lambda_q1, lambda_k1, lambda_q2, lambda_k2)]
        gsub = subln_g[i if is_diff else 0].reshape(1, LANES)
        o_p, new_k, new_v, *cast = _ctx_layer(
            x_prompt_src, mod, l, g_mix, w_qkv, i, *gains, lam_params, gsub, is_diff, lambda_init,
            new_kv[kind], [_cast_job(w_o_f32[kind], i, 0, (N_PROMPT_TILES,))] if first else [])
        w_o = cast[0] if first else w_o
        new_kv[kind] = (new_k, new_v)
        qs, ks, vs = _qkv_latent(
            x_sample_src, sample_first_tile, mod, l, g_mix, w_qkv, *gains, (cos_tab, sin_tab), is_diff, [])
        if is_diff:
            mlp_jobs = [_cast_job(w_fc1, l, 1, LATENT_ATTENTION_GRID), _cast_job(w_fc2, l, 0, LATENT_ATTENTION_GRID)]
            o_s, w1, w2 = _diff_latent_attention(qs, ks, vs, cdk, cdv, i, lam_params, gsub, lambda_init,
                                                 mlp_jobs)
        else:
            na_grid = LATENT_ATTENTION_GRID[::-1]
            mlp_jobs = [_cast_job(w_fc1, l, 1, na_grid), _cast_job(w_fc2, l, 0, na_grid)]
            o_s, w1, w2 = _na_latent_attention(qs, ks, vs, cnk_t, cnv_t, i, na_bias, mlp_jobs)
        nxt, nxt_kind, nxt_i = l + 1, (l + 1) % 2, (l + 1) // 2
        post_grid = (N_ROW_TILES,)
        next_jobs = [] if last else [_cast_job(w_qkv_f32[nxt_kind], nxt_i, 0, post_grid),
                                     _cast_job(w_o_f32[nxt_kind], nxt_i, 0, post_grid),
                                     _ada_job(cond, w_ada, b_ada, nxt, post_grid)]
        outs = list(_post(xs, o_p, o_s, mod, l, g_mlp, w_o, w1, w2, last, next_jobs))
        if last:
            xp, xs = outs
        else:
            *xs, w_qkv, w_o, mod = outs

    n_kind = DEPTH // 2
    diff_k, diff_v = (a.reshape(BATCH, n_kind, SEQ, N_LANE_BLOCKS, LANES) for a in new_kv[0])
    na_k, na_v = (jnp.transpose(a.reshape(BATCH, n_kind, NA_HEADS, HEAD_DIM, SEQ), (0, 1, 4, 2, 3))
                  for a in new_kv[1])
    return (xp.reshape(BATCH, SEQ, D_MODEL), xs.reshape(DEC_BATCH, DEC_SEQ, D_MODEL),
            diff_k, diff_v, na_k, na_v)
```

```python
import functools
import math
from typing import Callable, NamedTuple

import jax
import jax.numpy as jnp
import numpy as np
from jax import lax
from jax.experimental import pallas as pl
from jax.experimental.pallas import tpu as pltpu

D_MODEL = 1024
BATCH = 16
SEQ = 256
DEPTH = 4
DEC_BATCH = 4
DEC_SEQ = 1024
PAST_LEN = 256
GRID_W = 64
GRID_ROWS = DEC_SEQ // GRID_W
HEAD_DIM = 64
D_FF = 4 * D_MODEL
NA_HEADS = 16
NA_KH = 8
NA_KW = 16
ROPE_BASE = 10000.0
EPS = 1e-6
NEG_INF = -1e30
LOG2_E = math.log2(math.e)
SCORE_SCALE = HEAD_DIM ** -0.5 * LOG2_E

LANES = 128
N_LANE_BLOCKS = D_MODEL // LANES
N_PROMPT_ROWS = BATCH * SEQ
N_SAMPLE_ROWS = DEC_BATCH * DEC_SEQ
N_ROWS = N_PROMPT_ROWS + N_SAMPLE_ROWS
ROW_TILE = 512
N_ROW_TILES = N_ROWS // ROW_TILE
N_PROMPT_TILES = N_PROMPT_ROWS // ROW_TILE
N_LATENT_TILES = N_SAMPLE_ROWS // ROW_TILE
TILES_PER_SAMPLE = DEC_SEQ // ROW_TILE
REQ_PER_TILE = ROW_TILE // SEQ
N_COND = 8
V7X_VMEM_BYTES = 64 * 1024 * 1024
VMEM_LIMIT = V7X_VMEM_BYTES // 8 * 7

F32 = jnp.float32
BF16 = jnp.bfloat16
NT_DIMS = (((1,), (1,)), ((), ()))


def _dot(a, b):
    return jnp.dot(a, b, preferred_element_type=F32)


def _dot_nt(a, b):
    return lax.dot_general(a, b, NT_DIMS, preferred_element_type=F32)


def _params(n_axes):
    return pltpu.CompilerParams(dimension_semantics=("arbitrary",) * n_axes,
                                vmem_limit_bytes=VMEM_LIMIT)


def _resident(shape):
    return pl.BlockSpec(shape, lambda *_: (0,) * len(shape), pipeline_mode=pl.Buffered(1))


class _SideJob(NamedTuple):
    args: list
    in_specs: list
    out_spec: pl.BlockSpec
    out_shape: jax.ShapeDtypeStruct
    body: Callable


def _cast_body(src_ref, dst_ref):
    dst_ref[...] = src_ref[...].astype(BF16)


def _step_number(grid):
    def number(*idx):
        step = idx[0]
        for size, i in zip(grid[1:], idx[1:]):
            step = step * size + i
        return step
    return number


def _cast_job(weights, layer, split_axis, grid):
    _, rows, cols = weights.shape
    n_steps, step = math.prod(grid), _step_number(grid)
    if split_axis == 0:
        band = (rows // n_steps, cols)
        src_spec = pl.BlockSpec((None,) + band, lambda *idx: (layer, step(*idx), 0))
        dst_spec = pl.BlockSpec(band, lambda *idx: (step(*idx), 0))
    else:
        band = (rows, cols // n_steps)
        src_spec = pl.BlockSpec((None,) + band, lambda *idx: (layer, 0, step(*idx)))
        dst_spec = pl.BlockSpec(band, lambda *idx: (0, step(*idx)))
    return _SideJob([weights], [src_spec], dst_spec, jax.ShapeDtypeStruct((rows, cols), BF16), _cast_body)


def _pallas_call_with_side_jobs(kernel_fn, jobs, *, grid, in_specs, out_specs, out_shape, args, name,
                                input_output_aliases=None, scratch_shapes=()):
    n_in, n_out = len(in_specs), len(out_specs)
    n_job_in = [len(job.args) for job in jobs]

    def kernel_and_jobs(*refs):
        ins, refs = refs[:n_in], refs[n_in:]
        job_ins, refs = refs[:sum(n_job_in)], refs[sum(n_job_in):]
        outs, refs = refs[:n_out], refs[n_out:]
        job_outs, scratch = refs[:len(jobs)], refs[len(jobs):]
        for job, n, out_ref in zip(jobs, n_job_in, job_outs):
            job.body(*job_ins[:n], out_ref)
            job_ins = job_ins[n:]
        kernel_fn(*ins, *outs, *scratch)

    return pl.pallas_call(
        kernel_and_jobs,
        grid=grid,
        in_specs=list(in_specs) + [spec for job in jobs for spec in job.in_specs],
        out_specs=list(out_specs) + [job.out_spec for job in jobs],
        out_shape=list(out_shape) + [job.out_shape for job in jobs],
        input_output_aliases=input_output_aliases or {},
        scratch_shapes=list(scratch_shapes),
        compiler_params=_params(len(grid)),
        name=name,
    )(*args, *(arg for job in jobs for arg in job.args))


def _prompt_tile(i):
    return jnp.minimum(i, N_PROMPT_TILES - 1)


def _sample_tile(i):
    return jnp.maximum(i - N_PROMPT_TILES, 0)


def _cond_row(i):
    return jnp.where(i < N_PROMPT_TILES, 0, 1 + (i - N_PROMPT_TILES) // TILES_PER_SAMPLE)


def _low_half_mask(shape):
    return lax.broadcasted_iota(jnp.int32, shape, len(shape) - 1) < HEAD_DIM


def _ada_kernel(layer, cond_ref, w_ref, b_ref, out_ref):
    cond = cond_ref[...]
    act = (cond * jax.nn.sigmoid(cond)).astype(BF16)
    out_ref[...] = _dot(act, w_ref[...].astype(BF16)) + b_ref[layer:layer + 1, :]


def _ada_specs(layer, n_steps):
    tn = 6 * D_MODEL // n_steps
    in_specs = [pl.BlockSpec((N_COND, D_MODEL), lambda j: (0, 0)),
                pl.BlockSpec((None, D_MODEL, tn), lambda j: (layer, 0, j)),
                pl.BlockSpec((DEPTH, tn), lambda j: (0, j))]
    return in_specs, pl.BlockSpec((N_COND, tn), lambda j: (0, j)), jax.ShapeDtypeStruct((N_COND, 6 * D_MODEL), F32)


ADA_GRID = (4,)


def _modulation(cond, w_ada, b_ada, layer, side_jobs):
    in_specs, out_spec, out_shape = _ada_specs(layer, *ADA_GRID)
    return _pallas_call_with_side_jobs(
        functools.partial(_ada_kernel, layer), side_jobs, grid=ADA_GRID, in_specs=in_specs,
        out_specs=[out_spec], out_shape=[out_shape], args=[cond, w_ada, b_ada], name="ada_modulation")


def _ada_job(cond, w_ada, b_ada, layer, grid):
    (n_steps,) = grid
    in_specs, out_spec, out_shape = _ada_specs(layer, n_steps)
    return _SideJob([cond, w_ada, b_ada], in_specs, out_spec, out_shape, functools.partial(_ada_kernel, layer))


def _group_rms(x, gain):
    low = _low_half_mask((1, LANES))
    cols = []
    for c in range(N_LANE_BLOCKS):
        xb = x[:, c * LANES:(c + 1) * LANES]
        sq = xb * xb
        s_lo = jnp.sum(jnp.where(low, sq, 0.0), axis=-1, keepdims=True)
        s_hi = jnp.sum(jnp.where(low, 0.0, sq), axis=-1, keepdims=True)
        ms = jnp.where(low, s_lo, s_hi) * (1.0 / HEAD_DIM)
        cols.append(xb * lax.rsqrt(ms + EPS) * gain)
    return cols


def _rope(cols, cos, sin_signed):
    lane = lax.broadcasted_iota(jnp.int32, (1, LANES), 1)
    first = (lane % 32) < 16
    out = []
    for xb in cols:
        partner = jnp.where(first, pltpu.roll(xb, LANES - 16, 1), pltpu.roll(xb, 16, 1))
        out.append(xb * cos + partner * sin_signed)
    return out


def _qkv_matmul(layer, x_ref, mod_ref, g_ref, w_ref, r, rows=slice(None)):
    x = x_ref[rows, :]
    ms = jnp.mean(x * x, axis=-1, keepdims=True)
    xn = x * lax.rsqrt(ms + EPS) * g_ref[layer:layer + 1, :]
    shift = mod_ref[pl.ds(r, 1), 0:D_MODEL]
    scale = mod_ref[pl.ds(r, 1), D_MODEL:2 * D_MODEL]
    h = (xn * (1.0 + scale) + shift).astype(BF16)
    return _dot(h, w_ref[...])


def _qk_norm(qkv, qg_ref, kg_ref):
    qn = _group_rms(qkv[:, 0:D_MODEL], qg_ref[...])
    kn = _group_rms(qkv[:, D_MODEL:2 * D_MODEL], kg_ref[...])
    return qn, kn, qkv[:, 2 * D_MODEL:3 * D_MODEL]


def _lane_block(c):
    return slice(c * LANES, (c + 1) * LANES)


LATENT_ROW_BANDS = 2


def _qkv_latent_kernel(is_diff, layer, x_ref, mod_ref, g_ref, w_ref, qg_ref, kg_ref, cos_ref, sin_ref,
                       q_ref, k_ref, v_ref):
    r = 1 + pl.program_id(0) // TILES_PER_SAMPLE
    band = ROW_TILE // LATENT_ROW_BANDS
    bands = [slice(a * band, (a + 1) * band) for a in range(LATENT_ROW_BANDS)]
    raw = [_qkv_matmul(layer, x_ref, mod_ref, g_ref, w_ref, r, rows) for rows in bands]
    for rows, qkv in zip(bands, raw):
        qn, kn, v = _qk_norm(qkv, qg_ref, kg_ref)
        v_ref[rows, :] = v.astype(BF16)
        if is_diff:
            cos = cos_ref[rows, :]
            sin = sin_ref[rows, :]
            qn = _rope(qn, cos, sin)
            kn = _rope(kn, cos, sin)
        for c in range(N_LANE_BLOCKS):
            q_ref[rows, _lane_block(c)] = (qn[c] * SCORE_SCALE).astype(BF16)
            k_ref[rows, _lane_block(c)] = kn[c].astype(BF16)


def _ctx_layer_kernel(is_diff, layer, lambda_init, has_prev, *refs):
    if has_prev:
        refs = refs[2:]
    (x_ref, mod_ref, g_ref, w_ref, qg_ref, kg_ref, lq1_ref, lk1_ref, lq2_ref, lk2_ref, gsub_ref,
     o_ref, kf_ref, vf_ref) = refs
    qn, kn, v = _qk_norm(_qkv_matmul(layer, x_ref, mod_ref, g_ref, w_ref, 0), qg_ref, kg_ref)
    for c in range(N_LANE_BLOCKS):
        for bb in range(REQ_PER_TILE):
            rows = slice(bb * SEQ, (bb + 1) * SEQ)
            if is_diff:
                dst = pl.ds(c, SEQ, stride=N_LANE_BLOCKS)
                kf_ref[bb, dst, :] = kn[c][rows, :]
                vf_ref[bb, dst, :] = v[rows, _lane_block(c)]
            else:
                kf_ref[bb, _lane_block(c), :] = kn[c][rows, :].T
                vf_ref[bb, _lane_block(c), :] = v[rows, _lane_block(c)].T
    q = [(qn[c] * SCORE_SCALE).astype(BF16) for c in range(N_LANE_BLOCKS)]
    k = [kn[c].astype(BF16) for c in range(N_LANE_BLOCKS)]
    vb = [v[:, _lane_block(c)].astype(BF16) for c in range(N_LANE_BLOCKS)]
    if is_diff:
        lam = _diff_lambda(lq1_ref, lk1_ref, lq2_ref, lk2_ref, lambda_init)
        combine = _diff_combine(lam, gsub_ref[...], lambda_init)
    else:
        combine = _head_pair_combine

    def block(b):
        bb, c = divmod(b, N_LANE_BLOCKS)
        return slice(bb * SEQ, (bb + 1) * SEQ), c

    def load_q(b):
        rows, c = block(b)
        return q[c][rows, :]

    def scores(b, q2):
        rows, c = block(b)
        return [_dot_nt(q2, k[c][rows, :])]

    def values(b):
        rows, c = block(b)
        return [_times(vb[c][rows, :])]

    def store(b, o):
        rows, c = block(b)
        o_ref[rows, _lane_block(c)] = o.astype(BF16)

    _attend_pairs(REQ_PER_TILE * N_LANE_BLOCKS, load_q, scores, values, combine, store)


def _qkv_latent(x, first_tile, mod, layer, g_mix, w_qkv, q_gain, k_gain, rope_tabs, is_diff, side_jobs):
    row_spec = pl.BlockSpec((ROW_TILE, D_MODEL), lambda i: (i, 0))
    rope_spec = pl.BlockSpec((ROW_TILE, LANES), lambda i: (i % TILES_PER_SAMPLE, 0))
    return _pallas_call_with_side_jobs(
        functools.partial(_qkv_latent_kernel, is_diff, layer), side_jobs,
        grid=(N_LATENT_TILES,),
        in_specs=[pl.BlockSpec((ROW_TILE, D_MODEL), lambda i: (first_tile + i, 0)),
                  _resident((N_COND, 6 * D_MODEL)),
                  _resident((DEPTH, D_MODEL)),
                  _resident((D_MODEL, 3 * D_MODEL)),
                  _resident((1, LANES)), _resident((1, LANES)), rope_spec, rope_spec],
        out_specs=[row_spec] * 3,
        out_shape=[jax.ShapeDtypeStruct((N_SAMPLE_ROWS, D_MODEL), BF16)] * 3,
        args=[x, mod, g_mix, w_qkv, q_gain, k_gain, *rope_tabs],
        name="qkv_latent",
    )


def _ctx_layer(x, mod, layer, g_mix, w_qkv, kind_layer, q_gain, k_gain, lam_params, gsub, is_diff,
               lambda_init, prev, side_jobs):
    row_spec = pl.BlockSpec((ROW_TILE, D_MODEL), lambda i: (i, 0))
    if is_diff:
        cache_rows, cache_cols = SEQ * N_LANE_BLOCKS, LANES
    else:
        cache_rows, cache_cols = D_MODEL, SEQ
    cache_spec = pl.BlockSpec((REQ_PER_TILE, None, cache_rows, cache_cols), lambda i: (i, kind_layer, 0, 0))
    cache_shape = jax.ShapeDtypeStruct((BATCH, DEPTH // 2, cache_rows, cache_cols), F32)
    in_specs = ([row_spec, _resident((N_COND, 6 * D_MODEL)), _resident((DEPTH, D_MODEL)),
                 _resident((D_MODEL, 3 * D_MODEL)), _resident((1, LANES)), _resident((1, LANES))]
                + [_resident((1, HEAD_DIM))] * 4 + [_resident((1, LANES))])
    args = [x, mod, g_mix, w_qkv, q_gain, k_gain, *lam_params, gsub]
    aliases = {}
    if prev is not None:
        in_specs = [pl.BlockSpec(memory_space=pl.ANY)] * 2 + in_specs
        args = list(prev) + args
        aliases = {0: 1, 1: 2}
    return _pallas_call_with_side_jobs(
        functools.partial(_ctx_layer_kernel, is_diff, layer, lambda_init, prev is not None), side_jobs,
        grid=(N_PROMPT_TILES,),
        in_specs=in_specs,
        out_specs=[row_spec, cache_spec, cache_spec],
        out_shape=[jax.ShapeDtypeStruct((N_PROMPT_ROWS, D_MODEL), BF16), cache_shape, cache_shape],
        args=args,
        input_output_aliases=aliases,
        name="ctx_layer",
    )


def _softmax_pv(score_parts, value_parts):
    m = None
    for s in score_parts:
        pm = jnp.max(s, axis=-1, keepdims=True)
        m = pm if m is None else jnp.maximum(m, pm)
    acc = None
    for s, times_v in zip(score_parts, value_parts):
        pv = times_v(jnp.exp2(s - m).astype(BF16))
        acc = pv if acc is None else acc + pv
    return acc[:, :LANES] * (1.0 / acc[:, LANES:])


def _times(v):
    v1 = jnp.concatenate([v, jnp.ones_like(v)], axis=1)
    return lambda w: _dot(w, v1)


def _times_t(v_t):
    v1_t = jnp.concatenate([v_t, jnp.ones_like(v_t)], axis=0)
    return lambda w: _dot_nt(w, v1_t)


def _diff_lambda(lq1_ref, lk1_ref, lq2_ref, lk2_ref, lambda_init):
    e1 = jnp.exp(jnp.sum(lq1_ref[...] * lk1_ref[...], axis=-1, keepdims=True))
    e2 = jnp.exp(jnp.sum(lq2_ref[...] * lk2_ref[...], axis=-1, keepdims=True))
    return e1 - e2 + lambda_init


def _attend_pairs(n_blocks, load_q, scores, values, combine, store):
    def block_scores(b):
        q = load_q(b)
        low = _low_half_mask(q.shape)
        zero = jnp.zeros_like(q)
        return scores(b, jnp.concatenate([jnp.where(low, q, zero), jnp.where(low, zero, q)], axis=0))

    s = block_scores(0)
    for b in range(n_blocks):
        s_next = block_scores(b + 1) if b + 1 < n_blocks else None
        o = _softmax_pv(s, values(b))
        s = s_next
        rows = o.shape[0] // 2
        store(b, combine(o[:rows], o[rows:]))


def _diff_combine(lam, gsub, lambda_init):
    def combine(o1, o2):
        o = o1 - lam * o2
        ms = jnp.mean(o * o, axis=-1, keepdims=True)
        return o * lax.rsqrt(ms + EPS) * gsub * (1.0 - lambda_init)
    return combine


def _head_pair_combine(o_lo, o_hi):
    return jnp.where(_low_half_mask(o_lo.shape), o_lo, o_hi)


Q_CHUNK = 512
LATENT_HEADS_PER_STEP = 2


def _diff_latent_kernel(lambda_init, q_ref, k_ref, v_ref, ck_ref, cv_ref,
                        lq1_ref, lk1_ref, lq2_ref, lk2_ref, gsub_ref, o_ref):
    lam = _diff_lambda(lq1_ref, lk1_ref, lq2_ref, lk2_ref, lambda_init)
    n_chunks = DEC_SEQ // Q_CHUNK
    keys, values = [], []
    for hh in range(LATENT_HEADS_PER_STEP):
        cols = slice(hh * LANES, (hh + 1) * LANES)
        head = pl.program_id(1) * LATENT_HEADS_PER_STEP + hh
        head_rows = pl.ds(head, PAST_LEN, stride=N_LANE_BLOCKS)
        keys.append([jnp.concatenate([k_ref[:, cols], ck_ref[head_rows, :].astype(BF16)], axis=0)])
        values.append([_times(jnp.concatenate([v_ref[:, cols], cv_ref[head_rows, :].astype(BF16)], axis=0))])

    def block(b):
        hh, c = divmod(b, n_chunks)
        return slice(c * Q_CHUNK, (c + 1) * Q_CHUNK), slice(hh * LANES, (hh + 1) * LANES)

    def store(b, o):
        o_ref[block(b)] = o.astype(BF16)

    _attend_pairs(LATENT_HEADS_PER_STEP * n_chunks,
                  lambda b: q_ref[block(b)],
                  lambda b, q2: [_dot_nt(q2, kpart) for kpart in keys[b // n_chunks]],
                  lambda b: values[b // n_chunks],
                  _diff_combine(lam, gsub_ref[...], lambda_init), store)


LATENT_ATTENTION_GRID = (DEC_BATCH, N_LANE_BLOCKS // LATENT_HEADS_PER_STEP)


def _diff_latent_attention(q, k, v, cache_k, cache_v, layer_i, lam_params, gsub, lambda_init, side_jobs):
    blk = pl.BlockSpec((DEC_SEQ, LATENT_HEADS_PER_STEP * LANES), lambda b, h: (b, h))
    cache = pl.BlockSpec((None, None, PAST_LEN * N_LANE_BLOCKS, LANES), lambda b, h: (b, layer_i, 0, 0))
    small = [_resident((1, HEAD_DIM))] * 4 + [_resident((1, LANES))]
    return _pallas_call_with_side_jobs(
        functools.partial(_diff_latent_kernel, lambda_init), side_jobs,
        grid=LATENT_ATTENTION_GRID,
        in_specs=[blk, blk, blk, cache, cache] + small,
        out_specs=[blk],
        out_shape=[jax.ShapeDtypeStruct((N_SAMPLE_ROWS, D_MODEL), BF16)],
        args=[q, k, v, cache_k, cache_v, *lam_params, gsub],
        name="diff_latent_attention",
    )


class _NaGroup(NamedTuple):
    q_row: int
    n_q: int
    k_row: int
    n_k: int


def _window_start(r):
    return min(max(r - NA_KH // 2, 0), GRID_ROWS - NA_KH)


def _na_groups(rows_per_group):
    groups = []
    for q_row in range(0, GRID_ROWS, rows_per_group):
        first = _window_start(q_row)
        last = _window_start(q_row + rows_per_group - 1) + NA_KH
        k_row = first - first % 2
        n_k = last - k_row
        n_k += n_k % 2
        groups.append(_NaGroup(q_row, rows_per_group, min(k_row, GRID_ROWS - n_k), n_k))
    return groups


NA_GROUPS = _na_groups(4)
N_DR = 2 * NA_KH - 1
N_DC = 2 * NA_KW - 1
N_DR_PAD = 16


def _expand_na_bias(rb_ref, tile_ref, bias_refs):
    qc = lax.broadcasted_iota(jnp.int32, (GRID_W, LANES), 0)
    lane = lax.broadcasted_iota(jnp.int32, (GRID_W, LANES), 1)
    kc = lane % GRID_W
    cs = jnp.clip(qc - NA_KW // 2, 0, GRID_W - NA_KW)
    col_ok = (kc >= cs) & (kc < cs + NA_KW)
    low = lane < GRID_W
    neg = jnp.full((GRID_W, LANES), NEG_INF, F32)
    for head in range(2 * LATENT_HEADS_PER_STEP):
        for dr in range(N_DR):
            row = jnp.broadcast_to(rb_ref[head, dr:dr + 1, :], (GRID_W, LANES))
            shifted = pltpu.roll(row, LANES - (NA_KW - 1), 1, stride=1, stride_axis=0)
            tile_ref[dr] = jnp.where(col_ok, shifted * LOG2_E, neg)
        for grp, bias_ref in zip(NA_GROUPS, bias_refs):
            for rq in range(grp.n_q):
                r = grp.q_row + rq
                rs = _window_start(r)

                def tile(kr, r=r, rs=rs):
                    if rs <= kr < rs + NA_KH:
                        return tile_ref[kr - r + NA_KH - 1]
                    return neg

                for p in range(grp.n_k // 2):
                    kr = grp.k_row + 2 * p
                    bias_ref[head, rq * GRID_W:(rq + 1) * GRID_W, p * LANES:(p + 1) * LANES] = (
                        jnp.where(low, tile(kr), tile(kr + 1)))


def _na_latent_kernel(q_ref, k_ref, v_ref, ckt_ref, cvt_ref, rb_ref, o_ref, tile_ref, *bias_refs):
    @pl.when(pl.program_id(1) == 0)
    def _():
        _expand_na_bias(rb_ref, tile_ref, bias_refs)

    ck_t = [ckt_ref[hh * LANES:(hh + 1) * LANES, :].astype(BF16) for hh in range(LATENT_HEADS_PER_STEP)]
    cv_t = [cvt_ref[hh * LANES:(hh + 1) * LANES, :].astype(BF16) for hh in range(LATENT_HEADS_PER_STEP)]

    def block(b):
        return divmod(b, len(NA_GROUPS))

    def cols(hh):
        return slice(hh * LANES, (hh + 1) * LANES)

    def qrows(g):
        return slice(NA_GROUPS[g].q_row * GRID_W, (NA_GROUPS[g].q_row + NA_GROUPS[g].n_q) * GRID_W)

    def krows(g):
        return slice(NA_GROUPS[g].k_row * GRID_W, (NA_GROUPS[g].k_row + NA_GROUPS[g].n_k) * GRID_W)

    def load_q(b):
        hh, g = block(b)
        return q_ref[qrows(g), cols(hh)]

    def scores(b, q2):
        hh, g = block(b)
        bias = jnp.concatenate([bias_refs[g][2 * hh], bias_refs[g][2 * hh + 1]], axis=0)
        return [_dot_nt(q2, k_ref[krows(g), cols(hh)]) + bias, _dot(q2, ck_t[hh])]

    def values(b):
        hh, g = block(b)
        return [_times(v_ref[krows(g), cols(hh)]), _times_t(cv_t[hh])]

    def store(b, o):
        hh, g = block(b)
        o_ref[qrows(g), cols(hh)] = o.astype(BF16)

    _attend_pairs(LATENT_HEADS_PER_STEP * len(NA_GROUPS), load_q, scores, values, _head_pair_combine, store)


def _na_latent_attention(q, k, v, cache_k_t, cache_v_t, layer_i, rel_bias_rows, side_jobs):
    width = LATENT_HEADS_PER_STEP * LANES
    blk = pl.BlockSpec((DEC_SEQ, width), lambda p, b: (b, p))
    cache = pl.BlockSpec((None, None, width, PAST_LEN), lambda p, b: (b, layer_i, p, 0))
    rb_spec = pl.BlockSpec((None, 2 * LATENT_HEADS_PER_STEP, N_DR_PAD, LANES), lambda p, b: (layer_i, p, 0, 0))
    return _pallas_call_with_side_jobs(
        _na_latent_kernel, side_jobs,
        grid=LATENT_ATTENTION_GRID[::-1],
        in_specs=[blk, blk, blk, cache, cache, rb_spec],
        out_specs=[blk],
        out_shape=[jax.ShapeDtypeStruct((N_SAMPLE_ROWS, D_MODEL), BF16)],
        args=[q, k, v, cache_k_t, cache_v_t, rel_bias_rows],
        scratch_shapes=[pltpu.VMEM((N_DR, GRID_W, LANES), F32)] + [
            pltpu.VMEM((2 * LATENT_HEADS_PER_STEP, grp.n_q * GRID_W, grp.n_k * GRID_W), F32)
            for grp in NA_GROUPS],
        name="na_latent_attention",
    )


def _padded_rel_bias(rel_bias_na):
    half = jnp.pad(rel_bias_na, ((0, 0), (0, 0), (0, N_DR_PAD - N_DR), (0, GRID_W - N_DC)))
    return jnp.concatenate([half, half], axis=-1)


FF_CHUNK = 1024
POST_ROW_BANDS = 2


def _post_kernel(split_in, split_out, layer, *refs):
    n_x = 2 if split_in else 1
    x_refs = refs[:n_x]
    op_ref, os_ref, mod_ref, g_ref, wo_ref, w1_ref, w2_ref = refs[n_x:n_x + 7]
    y_refs = refs[n_x + 7:]
    i = pl.program_id(0)
    r = _cond_row(i)
    is_prompt = i < N_PROMPT_TILES

    def mod(j):
        return mod_ref[pl.ds(r, 1), j * D_MODEL:(j + 1) * D_MODEL]

    band = ROW_TILE // POST_ROW_BANDS
    bands = [slice(a * band, (a + 1) * band) for a in range(POST_ROW_BANDS)]
    y1s, ys = [], []
    for rows in bands:
        x = (jnp.where(is_prompt, x_refs[0][rows, :], x_refs[1][rows, :]) if split_in
             else x_refs[0][rows, :])
        o = jnp.where(is_prompt, op_ref[rows, :], os_ref[rows, :])
        y1s.append(x + mod(2) * _dot(o, wo_ref[...]))
    for rows, y1 in zip(bands, y1s):
        ms = jnp.mean(y1 * y1, axis=-1, keepdims=True)
        h = (y1 * lax.rsqrt(ms + EPS) * g_ref[layer:layer + 1, :] * (1.0 + mod(4)) + mod(3)).astype(BF16)
        acc = jnp.zeros((band, D_MODEL), F32)
        for c in range(D_FF // FF_CHUNK):
            cols = slice(c * FF_CHUNK, (c + 1) * FF_CHUNK)
            hid = jnp.maximum(_dot(h, w1_ref[:, cols]), 0.0)
            acc = acc + _dot((hid * hid).astype(BF16), w2_ref[cols, :])
        ys.append(y1 + mod(5) * acc)
        if not split_out:
            y_refs[0][rows, :] = ys[-1]
    if split_out:
        @pl.when(is_prompt)
        def _():
            for rows, y in zip(bands, ys):
                y_refs[0][rows, :] = y

        @pl.when(jnp.logical_not(is_prompt))
        def _():
            for rows, y in zip(bands, ys):
                y_refs[1][rows, :] = y


def _post(xs, o_p, o_s, mod, layer, g_mlp, w_o, w_fc1, w_fc2, split_out, side_jobs):
    prompt_spec = pl.BlockSpec((ROW_TILE, D_MODEL), lambda i: (_prompt_tile(i), 0))
    sample_spec = pl.BlockSpec((ROW_TILE, D_MODEL), lambda i: (_sample_tile(i), 0))
    row_spec = pl.BlockSpec((ROW_TILE, D_MODEL), lambda i: (i, 0))
    split_in = len(xs) == 2
    if split_out:
        out_specs = [prompt_spec, sample_spec]
        out_shape = [jax.ShapeDtypeStruct((N_PROMPT_ROWS, D_MODEL), F32),
                     jax.ShapeDtypeStruct((N_SAMPLE_ROWS, D_MODEL), F32)]
    else:
        out_specs = [row_spec]
        out_shape = [jax.ShapeDtypeStruct((N_ROWS, D_MODEL), F32)]
    return _pallas_call_with_side_jobs(
        functools.partial(_post_kernel, split_in, split_out, layer), side_jobs,
        grid=(N_ROW_TILES,),
        in_specs=([prompt_spec, sample_spec] if split_in else [row_spec]) + [
            prompt_spec, sample_spec,
            _resident((N_COND, 6 * D_MODEL)),
            _resident((DEPTH, D_MODEL)),
            _resident((D_MODEL, D_MODEL)),
            _resident((D_MODEL, D_FF)),
            _resident((D_FF, D_MODEL)),
        ],
        out_specs=out_specs,
        out_shape=out_shape,
        args=[*xs, o_p, o_s, mod, g_mlp, w_o, w_fc1, w_fc2],
        name="wo_mlp",
    )


def _rope_tables():
    n_freq = HEAD_DIM // 4
    freqs = ROPE_BASE ** (-np.arange(n_freq, dtype=np.float64) / n_freq)
    t = np.arange(DEC_SEQ)
    ang_row = (t // GRID_W)[:, None] * freqs
    ang_col = (t % GRID_W)[:, None] * freqs
    cos64 = np.concatenate([np.cos(ang_row)] * 2 + [np.cos(ang_col)] * 2, axis=-1)
    sin64 = np.concatenate([-np.sin(ang_row), np.sin(ang_row), -np.sin(ang_col), np.sin(ang_col)], axis=-1)
    return (jnp.asarray(np.tile(cos64, (1, 2)), dtype=F32), jnp.asarray(np.tile(sin64, (1, 2)), dtype=F32))


def _pair_gain(g):
    return jnp.tile(g.reshape(1, HEAD_DIM), (1, LANES // HEAD_DIM))


def _head_major_cache(cache):
    n_req, n_layers = cache.shape[:2]
    return jnp.transpose(cache, (0, 1, 3, 4, 2)).reshape(n_req, n_layers, D_MODEL, PAST_LEN)


def kernel(x_prompt, x_sample, cache_diff_k, cache_diff_v, cache_na_k, cache_na_v, c, c_ctx, w_ada, b_ada, norm_mix_g, norm_mlp_g, w_fc1, w_fc2, w_qkv_diff, w_o_diff, q_norm_diff_g, k_norm_diff_g, lambda_q1, lambda_k1, lambda_q2, lambda_k2, subln_g, w_qkv_na, w_o_na, q_norm_na_g, k_norm_na_g, rel_bias_na):
    cond = jnp.concatenate([c_ctx[None, :], c, jnp.zeros((N_COND - 1 - DEC_BATCH, D_MODEL), F32)], axis=0)
    mod, w_qkv = _modulation(cond, w_ada, b_ada, 0, [_cast_job(w_qkv_diff, 0, 0, ADA_GRID)])
    cos_tab, sin_tab = _rope_tables()
    na_bias = _padded_rel_bias(rel_bias_na)
    cdk = cache_diff_k.reshape(DEC_BATCH, DEPTH // 2, PAST_LEN * N_LANE_BLOCKS, LANES)
    cdv = cache_diff_v.reshape(DEC_BATCH, DEPTH // 2, PAST_LEN * N_LANE_BLOCKS, LANES)
    cnk_t = _head_major_cache(cache_na_k)
    cnv_t = _head_major_cache(cache_na_v)
    g_mix, g_mlp = norm_mix_g, norm_mlp_g
    w_qkv_f32 = (w_qkv_diff, w_qkv_na)
    w_o_f32 = (w_o_diff, w_o_na)
    q_gain = (q_norm_diff_g, q_norm_na_g)
    k_gain = (k_norm_diff_g, k_norm_na_g)

    xs = [x_prompt.reshape(N_PROMPT_ROWS, D_MODEL), x_sample.reshape(N_SAMPLE_ROWS, D_MODEL)]
    new_kv = [None, None]
    w_o = None
    for l in range(DEPTH):
        i = l // 2
        kind = l % 2
        is_diff = kind == 0
        first = l == 0
        last = l == DEPTH - 1
        x_prompt_src, x_sample_src = (xs[0], xs[1]) if len(xs) == 2 else (xs[0], xs[0])
        sample_first_tile = 0 if len(xs) == 2 else N_PROMPT_TILES
        gains = (_pair_gain(q_gain[kind][i]), _pair_gain(k_gain[kind][i]))
        lambda_init = 0.8 - 0.6 * math.exp(-0.3 * l)
        lam_params = [p[i if is_diff else 0].reshape(1, HEAD_DIM)
                      for p in (lambda_q1, lambda_k1, lambda_q2, lambda_k2)]
        gsub = subln_g[i if is_diff else 0].reshape(1, LANES)
        o_p, new_k, new_v, *cast = _ctx_layer(
            x_prompt_src, mod, l, g_mix, w_qkv, i, *gains, lam_params, gsub, is_diff, lambda_init,
            new_kv[kind], [_cast_job(w_o_f32[kind], i, 0, (N_PROMPT_TILES,))] if first else [])
        w_o = cast[0] if first else w_o
        new_kv[kind] = (new_k, new_v)
        qs, ks, vs = _qkv_latent(
            x_sample_src, sample_first_tile, mod, l, g_mix, w_qkv, *gains, (cos_tab, sin_tab), is_diff, [])
        if is_diff:
            mlp_jobs = [_cast_job(w_fc1, l, 1, LATENT_ATTENTION_GRID), _cast_job(w_fc2, l, 0, LATENT_ATTENTION_GRID)]
            o_s, w1, w2 = _diff_latent_attention(qs, ks, vs, cdk, cdv, i, lam_params, gsub, lambda_init,
                                                 mlp_jobs)
        else:
            na_grid = LATENT_ATTENTION_GRID[::-1]
            mlp_jobs = [_cast_job(w_fc1, l, 1, na_grid), _cast_job(w_fc2, l, 0, na_grid)]
            o_s, w1, w2 = _na_latent_attention(qs, ks, vs, cnk_t, cnv_t, i, na_bias, mlp_jobs)
        nxt, nxt_kind, nxt_i = l + 1, (l + 1) % 2, (l + 1) // 2
        post_grid = (N_ROW_TILES,)
        next_jobs = [] if last else [_cast_job(w_qkv_f32[nxt_kind], nxt_i, 0, post_grid),
                                     _cast_job(w_o_f32[nxt_kind], nxt_i, 0, post_grid),
                                     _ada_job(cond, w_ada, b_ada, nxt, post_grid)]
        outs = list(_post(xs, o_p, o_s, mod, l, g_mlp, w_o, w1, w2, last, next_jobs))
        if last:
            xp, xs = outs
        else:
            *xs, w_qkv, w_o, mod = outs

    n_kind = DEPTH // 2
    diff_k, diff_v = (a.reshape(BATCH, n_kind, SEQ, N_LANE_BLOCKS, LANES) for a in new_kv[0])
    na_k, na_v = (jnp.transpose(a.reshape(BATCH, n_kind, NA_HEADS, HEAD_DIM, SEQ), (0, 1, 4, 2, 3))
                  for a in new_kv[1])
    return (xp.reshape(BATCH, SEQ, D_MODEL), xs.reshape(DEC_BATCH, DEC_SEQ, D_MODEL),
            diff_k, diff_v, na_k, na_v)
```
